```python
import jax, jax.numpy as jnp
from jax import lax
import numpy as np

D_MODEL = 1024
BATCH = 16
SEQ = 2048
DEPTH = 2

N_MIXERS = 2
N_CONV_LAYERS = (DEPTH + 1) // 2
N_MLA_LAYERS = DEPTH // 2
MIX_WIDTH = 2 * D_MODEL
MEM_LEN = 256
MEM_HEADS = 4
MEM_HEAD_DIM = 128
MEM_WIDTH = MEM_HEADS * MEM_HEAD_DIM
MAIN_WIDTH = MIX_WIDTH - MEM_WIDTH
CONV_WIDTH = MAIN_WIDTH
CONV_KERNEL = 31
MLA_HEADS = 12
MLA_NOPE = 128
MLA_ROPE = 64
MLA_V = 128
MLA_QK = MLA_NOPE + MLA_ROPE
Q_RANK = 512
KV_RANK = 256
ROPE_THETA = 10000.0
Q_BLOCK = 128
RMS_EPS = 1e-6
LN_EPS = 1e-5
CONV_IN_COLS = 2 * CONV_WIDTH + MEM_WIDTH + MIX_WIDTH
MLA_IN_COLS = Q_RANK + KV_RANK + MLA_ROPE + MEM_WIDTH + MIX_WIDTH

kernel_name = "hybrid_conformer_conv_mla_memxattn"


def rmsnorm(x, g):
    xf = x.astype(jnp.float32)
    y = xf * lax.rsqrt(jnp.mean(xf * xf, axis=-1, keepdims=True) + RMS_EPS)
    return (y * g.astype(jnp.float32)).astype(x.dtype)


def layernorm(x, g, b):
    xf = x.astype(jnp.float32)
    mu = jnp.mean(xf, axis=-1, keepdims=True)
    var = jnp.mean(jnp.square(xf - mu), axis=-1, keepdims=True)
    y = (xf - mu) * lax.rsqrt(var + LN_EPS)
    return (y * g.astype(jnp.float32) + b.astype(jnp.float32)).astype(x.dtype)


def rope_tables(positions):
    inv_freq = 1.0 / (ROPE_THETA ** (jnp.arange(0, MLA_ROPE, 2, dtype=jnp.float32) / MLA_ROPE))
    ang = positions.astype(jnp.float32)[..., None] * inv_freq
    return jnp.cos(ang), jnp.sin(ang)


def apply_rope(x, cos, sin):
    xf = x.astype(jnp.float32)
    x1, x2 = jnp.split(xf, 2, axis=-1)
    out = jnp.concatenate([x1 * cos - x2 * sin, x2 * cos + x1 * sin], axis=-1)
    return out.astype(x.dtype)


def mem_cross_attention(q, mem, mem_g, w_mem_kv):
    b, s = q.shape[0], q.shape[1]
    kv = rmsnorm(mem, mem_g) @ w_mem_kv
    k, v = jnp.split(kv, 2, axis=-1)
    k = k.reshape(b, -1, MEM_HEADS, MEM_HEAD_DIM)
    v = v.reshape(b, -1, MEM_HEADS, MEM_HEAD_DIM)
    qh = q.reshape(b, s, MEM_HEADS, MEM_HEAD_DIM)
    sc = jnp.einsum('bshd,bmhd->bhsm', qh, k).astype(jnp.float32) * (MEM_HEAD_DIM ** -0.5)
    p = jax.nn.softmax(sc, axis=-1).astype(v.dtype)
    o = jnp.einsum('bhsm,bmhd->bshd', p, v)
    return o.reshape(b, s, MEM_WIDTH)


def conv_branch(u, dw, dw_b, ln_g, ln_b):
    a, g = jnp.split(u, 2, axis=-1)
    h = a * jax.nn.sigmoid(g)
    h = lax.conv_general_dilated(
        h, dw[:, None, :], window_strides=(1,), padding=[(CONV_KERNEL - 1, 0)],
        dimension_numbers=('NWC', 'WIO', 'NWC'), feature_group_count=CONV_WIDTH) + dw_b
    h = layernorm(h, ln_g, ln_b)
    return jax.nn.silu(h)


def mla_branch(cq, ckv, kr, cos, sin, q_g, w_uq, kv_g, w_ukv):
    b, s = cq.shape[0], cq.shape[1]
    q = (rmsnorm(cq, q_g) @ w_uq).reshape(b, s, MLA_HEADS, MLA_QK)
    q_nope = q[..., :MLA_NOPE]
    q_rope = apply_rope(q[..., MLA_NOPE:], cos[:, :, None, :], sin[:, :, None, :])
    kv = (rmsnorm(ckv, kv_g) @ w_ukv).reshape(b, s, MLA_HEADS, MLA_NOPE + MLA_V)
    k_nope, v = kv[..., :MLA_NOPE], kv[..., MLA_NOPE:]
    k_rope = apply_rope(kr, cos, sin)
    nb = s // Q_BLOCK
    qn_b = q_nope.reshape(b, nb, Q_BLOCK, MLA_HEADS, MLA_NOPE).transpose(1, 0, 2, 3, 4)
    qr_b = q_rope.reshape(b, nb, Q_BLOCK, MLA_HEADS, MLA_ROPE).transpose(1, 0, 2, 3, 4)
    key_idx = jnp.arange(s)
    scale = MLA_QK ** -0.5

    def block(args):
        qn, qr, i = args
        sc = (jnp.einsum('bqhd,bkhd->bhqk', qn, k_nope)
              + jnp.einsum('bqhd,bkd->bhqk', qr, k_rope)).astype(jnp.float32) * scale
        q_idx = i * Q_BLOCK + jnp.arange(Q_BLOCK)
        mask = key_idx[None, :] <= q_idx[:, None]
        sc = jnp.where(mask, sc, -jnp.inf)
        p = jax.nn.softmax(sc, axis=-1).astype(v.dtype)
        return jnp.einsum('bhqk,bkhd->bqhd', p, v)

    o = lax.map(block, (qn_b, qr_b, jnp.arange(nb)))
    return o.transpose(1, 0, 2, 3, 4).reshape(b, s, MLA_HEADS * MLA_V)


def _fwd_setup_inputs(seed: int = 0) -> dict:
    key = jax.random.key(seed)
    ks = jax.random.split(key, 24)
    nrm = jax.random.normal
    f32 = jnp.float32
    x = nrm(ks[0], (BATCH, SEQ, D_MODEL), f32)
    mem = nrm(ks[1], (BATCH, MEM_LEN, D_MODEL), f32)
    offs = jax.random.randint(ks[2], (BATCH, 1), 0, 1024, dtype=jnp.int32)
    positions = (offs + jnp.arange(SEQ, dtype=jnp.int32)[None, :]).astype(jnp.int32)
    gain = lambda k, shape: 1.0 + 0.02 * nrm(k, shape, f32)
    return {
        "x": x,
        "mem": mem,
        "positions": positions,
        "norm_g": gain(ks[3], (DEPTH, D_MODEL)),
        "mem_norm_g": gain(ks[4], (DEPTH, D_MODEL)),
        "w_mem_kv": nrm(ks[5], (DEPTH, D_MODEL, 2 * MEM_WIDTH), f32) * D_MODEL ** -0.5,
        "w_out": nrm(ks[6], (DEPTH, MIX_WIDTH, D_MODEL), f32) * MIX_WIDTH ** -0.5,
        "conv_w_in": nrm(ks[7], (N_CONV_LAYERS, D_MODEL, CONV_IN_COLS), f32) * D_MODEL ** -0.5,
        "conv_dw": nrm(ks[8], (N_CONV_LAYERS, CONV_KERNEL, CONV_WIDTH), f32) * CONV_KERNEL ** -0.5,
        "conv_dw_b": 0.02 * nrm(ks[9], (N_CONV_LAYERS, CONV_WIDTH), f32),
        "conv_ln_g": gain(ks[10], (N_CONV_LAYERS, CONV_WIDTH)),
        "conv_ln_b": 0.02 * nrm(ks[11], (N_CONV_LAYERS, CONV_WIDTH), f32),
        "mla_w_in": nrm(ks[12], (N_MLA_LAYERS, D_MODEL, MLA_IN_COLS), f32) * D_MODEL ** -0.5,
        "mla_q_norm_g": gain(ks[13], (N_MLA_LAYERS, Q_RANK)),
        "mla_w_uq": nrm(ks[14], (N_MLA_LAYERS, Q_RANK, MLA_HEADS * MLA_QK), f32) * Q_RANK ** -0.5,
        "mla_kv_norm_g": gain(ks[15], (N_MLA_LAYERS, KV_RANK)),
        "mla_w_ukv": nrm(ks[16], (N_MLA_LAYERS, KV_RANK, MLA_HEADS * (MLA_NOPE + MLA_V)), f32) * KV_RANK ** -0.5,
        "final_norm_g": gain(ks[17], (D_MODEL,)),
    }


def _fwd_reference(x, mem, positions, norm_g, mem_norm_g, w_mem_kv, w_out, conv_w_in, conv_dw,
              conv_dw_b, conv_ln_g, conv_ln_b, mla_w_in, mla_q_norm_g, mla_w_uq,
              mla_kv_norm_g, mla_w_ukv, final_norm_g):
    cos, sin = rope_tables(positions)
    h = x
    for i in range(DEPTH):
        j = i // N_MIXERS
        u = rmsnorm(h, norm_g[i])
        if i % N_MIXERS == 0:
            proj = u @ conv_w_in[j]
            u_conv, q_mem, z = jnp.split(proj, [2 * CONV_WIDTH, 2 * CONV_WIDTH + MEM_WIDTH], axis=-1)
            y_main = conv_branch(u_conv, conv_dw[j], conv_dw_b[j], conv_ln_g[j], conv_ln_b[j])
        else:
            proj = u @ mla_w_in[j]
            c1 = Q_RANK
            c2 = c1 + KV_RANK
            c3 = c2 + MLA_ROPE
            c4 = c3 + MEM_WIDTH
            cq, ckv, kr, q_mem, z = jnp.split(proj, [c1, c2, c3, c4], axis=-1)
            y_main = mla_branch(cq, ckv, kr, cos, sin, mla_q_norm_g[j], mla_w_uq[j],
                                mla_kv_norm_g[j], mla_w_ukv[j])
        y_mem = mem_cross_attention(q_mem, mem, mem_norm_g[i], w_mem_kv[i])
        y = jnp.concatenate([y_main, y_mem], axis=-1) * jax.nn.silu(z)
        h = h + y @ w_out[i]
    return rmsnorm(h, final_norm_g)


import jax as _jax
import jax.numpy as _jnp

TWIN_FORMAT = 'train_step'
FWD_PARAMS = ['x', 'mem', 'positions', 'norm_g', 'mem_norm_g', 'w_mem_kv', 'w_out', 'conv_w_in', 'conv_dw', 'conv_dw_b', 'conv_ln_g', 'conv_ln_b', 'mla_w_in', 'mla_q_norm_g', 'mla_w_uq', 'mla_kv_norm_g', 'mla_w_ukv', 'final_norm_g']
TWIN_WEIGHTS = ['norm_g', 'mem_norm_g', 'w_mem_kv', 'w_out', 'conv_w_in', 'conv_dw', 'conv_dw_b', 'conv_ln_g', 'conv_ln_b', 'mla_w_in', 'mla_q_norm_g', 'mla_w_uq', 'mla_kv_norm_g', 'mla_w_ukv', 'final_norm_g']
TWIN_DIFF_INPUT = 'x'
TWIN_INPUTS = ['x', 'mem', 'positions', 'norm_g', 'mem_norm_g', 'w_mem_kv', 'w_out', 'conv_w_in', 'conv_dw', 'conv_dw_b', 'conv_ln_g', 'conv_ln_b', 'mla_w_in', 'mla_q_norm_g', 'mla_w_uq', 'mla_kv_norm_g', 'mla_w_ukv', 'final_norm_g', 'loss_target', 'm_norm_g', 'm_mem_norm_g', 'm_w_mem_kv', 'm_w_out', 'm_conv_w_in', 'm_conv_dw', 'm_conv_dw_b', 'm_conv_ln_g', 'm_conv_ln_b', 'm_mla_w_in', 'm_mla_q_norm_g', 'm_mla_w_uq', 'm_mla_kv_norm_g', 'm_mla_w_ukv', 'm_final_norm_g', 'v_norm_g', 'v_mem_norm_g', 'v_w_mem_kv', 'v_w_out', 'v_conv_w_in', 'v_conv_dw', 'v_conv_dw_b', 'v_conv_ln_g', 'v_conv_ln_b', 'v_mla_w_in', 'v_mla_q_norm_g', 'v_mla_w_uq', 'v_mla_kv_norm_g', 'v_mla_w_ukv', 'v_final_norm_g']
TWIN_OUTPUTS = ['loss', 'grad_x', 'grad_norm_g', 'grad_mem_norm_g', 'grad_w_mem_kv', 'grad_w_out', 'grad_conv_w_in', 'grad_conv_dw', 'grad_conv_dw_b', 'grad_conv_ln_g', 'grad_conv_ln_b', 'grad_mla_w_in', 'grad_mla_q_norm_g', 'grad_mla_w_uq', 'grad_mla_kv_norm_g', 'grad_mla_w_ukv', 'grad_final_norm_g', 'delta_norm_g', 'delta_mem_norm_g', 'delta_w_mem_kv', 'delta_w_out', 'delta_conv_w_in', 'delta_conv_dw', 'delta_conv_dw_b', 'delta_conv_ln_g', 'delta_conv_ln_b', 'delta_mla_w_in', 'delta_mla_q_norm_g', 'delta_mla_w_uq', 'delta_mla_kv_norm_g', 'delta_mla_w_ukv', 'delta_final_norm_g', 'new_m_norm_g', 'new_m_mem_norm_g', 'new_m_w_mem_kv', 'new_m_w_out', 'new_m_conv_w_in', 'new_m_conv_dw', 'new_m_conv_dw_b', 'new_m_conv_ln_g', 'new_m_conv_ln_b', 'new_m_mla_w_in', 'new_m_mla_q_norm_g', 'new_m_mla_w_uq', 'new_m_mla_kv_norm_g', 'new_m_mla_w_ukv', 'new_m_final_norm_g', 'new_v_norm_g', 'new_v_mem_norm_g', 'new_v_w_mem_kv', 'new_v_w_out', 'new_v_conv_w_in', 'new_v_conv_dw', 'new_v_conv_dw_b', 'new_v_conv_ln_g', 'new_v_conv_ln_b', 'new_v_mla_w_in', 'new_v_mla_q_norm_g', 'new_v_mla_w_uq', 'new_v_mla_kv_norm_g', 'new_v_mla_w_ukv', 'new_v_final_norm_g']
TWIN_LEAF_KINDS = {'loss': 'loss', 'grad_x': 'grad_x', 'grad_norm_g': 'grad_w', 'grad_mem_norm_g': 'grad_w', 'grad_w_mem_kv': 'grad_w', 'grad_w_out': 'grad_w', 'grad_conv_w_in': 'grad_w', 'grad_conv_dw': 'grad_w', 'grad_conv_dw_b': 'grad_w', 'grad_conv_ln_g': 'grad_w', 'grad_conv_ln_b': 'grad_w', 'grad_mla_w_in': 'grad_w', 'grad_mla_q_norm_g': 'grad_w', 'grad_mla_w_uq': 'grad_w', 'grad_mla_kv_norm_g': 'grad_w', 'grad_mla_w_ukv': 'grad_w', 'grad_final_norm_g': 'grad_w', 'delta_norm_g': 'delta_w', 'delta_mem_norm_g': 'delta_w', 'delta_w_mem_kv': 'delta_w', 'delta_w_out': 'delta_w', 'delta_conv_w_in': 'delta_w', 'delta_conv_dw': 'delta_w', 'delta_conv_dw_b': 'delta_w', 'delta_conv_ln_g': 'delta_w', 'delta_conv_ln_b': 'delta_w', 'delta_mla_w_in': 'delta_w', 'delta_mla_q_norm_g': 'delta_w', 'delta_mla_w_uq': 'delta_w', 'delta_mla_kv_norm_g': 'delta_w', 'delta_mla_w_ukv': 'delta_w', 'delta_final_norm_g': 'delta_w', 'new_m_norm_g': 'new_m', 'new_m_mem_norm_g': 'new_m', 'new_m_w_mem_kv': 'new_m', 'new_m_w_out': 'new_m', 'new_m_conv_w_in': 'new_m', 'new_m_conv_dw': 'new_m', 'new_m_conv_dw_b': 'new_m', 'new_m_conv_ln_g': 'new_m', 'new_m_conv_ln_b': 'new_m', 'new_m_mla_w_in': 'new_m', 'new_m_mla_q_norm_g': 'new_m', 'new_m_mla_w_uq': 'new_m', 'new_m_mla_kv_norm_g': 'new_m', 'new_m_mla_w_ukv': 'new_m', 'new_m_final_norm_g': 'new_m', 'new_v_norm_g': 'new_v', 'new_v_mem_norm_g': 'new_v', 'new_v_w_mem_kv': 'new_v', 'new_v_w_out': 'new_v', 'new_v_conv_w_in': 'new_v', 'new_v_conv_dw': 'new_v', 'new_v_conv_dw_b': 'new_v', 'new_v_conv_ln_g': 'new_v', 'new_v_conv_ln_b': 'new_v', 'new_v_mla_w_in': 'new_v', 'new_v_mla_q_norm_g': 'new_v', 'new_v_mla_w_uq': 'new_v', 'new_v_mla_kv_norm_g': 'new_v', 'new_v_mla_w_ukv': 'new_v', 'new_v_final_norm_g': 'new_v'}


def _forward(args):
    return _fwd_reference(*[args[k] for k in FWD_PARAMS])


def _output_shape():
    out = _jax.eval_shape(lambda: _forward(_fwd_setup_inputs(0)))
    return out.shape, out.dtype

N_MICROBATCH = 1
ADAM_LR = 0.001
ADAM_B1 = 0.9
ADAM_B2 = 0.999
ADAM_EPS = 1e-08
ADAM_WD = 0.01
ADAM_STEP = 10
PER_EXAMPLE_BATCH_AXIS = {'x': 0, 'mem': 0, 'positions': 0, 'loss_target': 0}
SHARED_INPUTS = []
_WEIGHT_DTYPES = {'norm_g': _jnp.float32, 'mem_norm_g': _jnp.float32, 'w_mem_kv': _jnp.float32, 'w_out': _jnp.float32, 'conv_w_in': _jnp.float32, 'conv_dw': _jnp.float32, 'conv_dw_b': _jnp.float32, 'conv_ln_g': _jnp.float32, 'conv_ln_b': _jnp.float32, 'mla_w_in': _jnp.float32, 'mla_q_norm_g': _jnp.float32, 'mla_w_uq': _jnp.float32, 'mla_kv_norm_g': _jnp.float32, 'mla_w_ukv': _jnp.float32, 'final_norm_g': _jnp.float32}
MOMENT_SCALE = {'norm_g': 6.326992e-02, 'mem_norm_g': 7.949480e-03, 'w_mem_kv': 7.523840e-03, 'w_out': 4.023575e-02, 'conv_w_in': 3.449491e-02, 'conv_dw': 4.514653e-02, 'conv_dw_b': 9.951650e-02, 'conv_ln_g': 5.233356e-02, 'conv_ln_b': 4.694806e-02, 'mla_w_in': 1.960319e-02, 'mla_q_norm_g': 2.238059e-02, 'mla_w_uq': 1.047119e-02, 'mla_kv_norm_g': 4.953361e-02, 'mla_w_ukv': 1.347376e-02, 'final_norm_g': 3.197281e+01}


def _to_microbatches(a, axis):
    t = _jnp.moveaxis(a, axis, 0)
    t = t.reshape((N_MICROBATCH, t.shape[0] // N_MICROBATCH) + t.shape[1:])
    return _jnp.moveaxis(t, 1, axis + 1)


def setup_inputs(seed: int = 0) -> dict:
    inp = _fwd_setup_inputs(seed)
    key = _jax.random.fold_in(_jax.random.key(seed), 7919)
    shape, _ = _output_shape()
    out = dict(inp)
    out["loss_target"] = _jax.random.normal(_jax.random.fold_in(key, 0), shape, _jnp.float32)
    for i, name in enumerate(TWIN_WEIGHTS):
        w = inp[name].astype(_jnp.float32)
        if MOMENT_SCALE is None:
            s = _jnp.sqrt(_jnp.mean(_jnp.square(w)) + 1e-30)
        else:
            s = MOMENT_SCALE[name]
        km, kv = _jax.random.split(_jax.random.fold_in(key, i + 1))
        out[name] = w
        out["m_" + name] = s * _jax.random.normal(km, w.shape, _jnp.float32)
        out["v_" + name] = (s * s) * _jax.random.uniform(kv, w.shape, _jnp.float32, 0.5, 1.5)
    if N_MICROBATCH > 1:
        for name, axis in PER_EXAMPLE_BATCH_AXIS.items():
            out[name] = _to_microbatches(out[name], axis)
    return {'x': out['x'], 'mem': out['mem'], 'positions': out['positions'], 'norm_g': out['norm_g'], 'mem_norm_g': out['mem_norm_g'], 'w_mem_kv': out['w_mem_kv'], 'w_out': out['w_out'], 'conv_w_in': out['conv_w_in'], 'conv_dw': out['conv_dw'], 'conv_dw_b': out['conv_dw_b'], 'conv_ln_g': out['conv_ln_g'], 'conv_ln_b': out['conv_ln_b'], 'mla_w_in': out['mla_w_in'], 'mla_q_norm_g': out['mla_q_norm_g'], 'mla_w_uq': out['mla_w_uq'], 'mla_kv_norm_g': out['mla_kv_norm_g'], 'mla_w_ukv': out['mla_w_ukv'], 'final_norm_g': out['final_norm_g'], 'loss_target': out['loss_target'], 'm_norm_g': out['m_norm_g'], 'm_mem_norm_g': out['m_mem_norm_g'], 'm_w_mem_kv': out['m_w_mem_kv'], 'm_w_out': out['m_w_out'], 'm_conv_w_in': out['m_conv_w_in'], 'm_conv_dw': out['m_conv_dw'], 'm_conv_dw_b': out['m_conv_dw_b'], 'm_conv_ln_g': out['m_conv_ln_g'], 'm_conv_ln_b': out['m_conv_ln_b'], 'm_mla_w_in': out['m_mla_w_in'], 'm_mla_q_norm_g': out['m_mla_q_norm_g'], 'm_mla_w_uq': out['m_mla_w_uq'], 'm_mla_kv_norm_g': out['m_mla_kv_norm_g'], 'm_mla_w_ukv': out['m_mla_w_ukv'], 'm_final_norm_g': out['m_final_norm_g'], 'v_norm_g': out['v_norm_g'], 'v_mem_norm_g': out['v_mem_norm_g'], 'v_w_mem_kv': out['v_w_mem_kv'], 'v_w_out': out['v_w_out'], 'v_conv_w_in': out['v_conv_w_in'], 'v_conv_dw': out['v_conv_dw'], 'v_conv_dw_b': out['v_conv_dw_b'], 'v_conv_ln_g': out['v_conv_ln_g'], 'v_conv_ln_b': out['v_conv_ln_b'], 'v_mla_w_in': out['v_mla_w_in'], 'v_mla_q_norm_g': out['v_mla_q_norm_g'], 'v_mla_w_uq': out['v_mla_w_uq'], 'v_mla_kv_norm_g': out['v_mla_kv_norm_g'], 'v_mla_w_ukv': out['v_mla_w_ukv'], 'v_final_norm_g': out['v_final_norm_g']}


def _loss(weights, diff, rest, loss_target):
    with _jax.named_scope("forward"):
        args = {**rest, TWIN_DIFF_INPUT: diff, **{k: w.astype(_WEIGHT_DTYPES[k]) for k, w in weights.items()}}
        y = _forward(args)
    with _jax.named_scope("loss_head"):
        err = _jnp.square(y.astype(_jnp.float32) - loss_target)
        return 0.5 * _jnp.sum(_jnp.mean(err, axis=-1)) if err.ndim else 0.5 * err


def _adamw(w, g, m, v):
    m = ADAM_B1 * m + (1.0 - ADAM_B1) * g
    v = ADAM_B2 * v + (1.0 - ADAM_B2) * _jnp.square(g)
    m_hat = m / (1.0 - ADAM_B1 ** ADAM_STEP)
    v_hat = v / (1.0 - ADAM_B2 ** ADAM_STEP)
    delta = -ADAM_LR * (m_hat / (_jnp.sqrt(v_hat) + ADAM_EPS) + ADAM_WD * w)
    return delta, m, v


def reference(x, mem, positions, norm_g, mem_norm_g, w_mem_kv, w_out, conv_w_in, conv_dw, conv_dw_b, conv_ln_g, conv_ln_b, mla_w_in, mla_q_norm_g, mla_w_uq, mla_kv_norm_g, mla_w_ukv, final_norm_g, loss_target, m_norm_g, m_mem_norm_g, m_w_mem_kv, m_w_out, m_conv_w_in, m_conv_dw, m_conv_dw_b, m_conv_ln_g, m_conv_ln_b, m_mla_w_in, m_mla_q_norm_g, m_mla_w_uq, m_mla_kv_norm_g, m_mla_w_ukv, m_final_norm_g, v_norm_g, v_mem_norm_g, v_w_mem_kv, v_w_out, v_conv_w_in, v_conv_dw, v_conv_dw_b, v_conv_ln_g, v_conv_ln_b, v_mla_w_in, v_mla_q_norm_g, v_mla_w_uq, v_mla_kv_norm_g, v_mla_w_ukv, v_final_norm_g):
    given = dict(x=x, mem=mem, positions=positions, norm_g=norm_g, mem_norm_g=mem_norm_g, w_mem_kv=w_mem_kv, w_out=w_out, conv_w_in=conv_w_in, conv_dw=conv_dw, conv_dw_b=conv_dw_b, conv_ln_g=conv_ln_g, conv_ln_b=conv_ln_b, mla_w_in=mla_w_in, mla_q_norm_g=mla_q_norm_g, mla_w_uq=mla_w_uq, mla_kv_norm_g=mla_kv_norm_g, mla_w_ukv=mla_w_ukv, final_norm_g=final_norm_g, loss_target=loss_target, m_norm_g=m_norm_g, m_mem_norm_g=m_mem_norm_g, m_w_mem_kv=m_w_mem_kv, m_w_out=m_w_out, m_conv_w_in=m_conv_w_in, m_conv_dw=m_conv_dw, m_conv_dw_b=m_conv_dw_b, m_conv_ln_g=m_conv_ln_g, m_conv_ln_b=m_conv_ln_b, m_mla_w_in=m_mla_w_in, m_mla_q_norm_g=m_mla_q_norm_g, m_mla_w_uq=m_mla_w_uq, m_mla_kv_norm_g=m_mla_kv_norm_g, m_mla_w_ukv=m_mla_w_ukv, m_final_norm_g=m_final_norm_g, v_norm_g=v_norm_g, v_mem_norm_g=v_mem_norm_g, v_w_mem_kv=v_w_mem_kv, v_w_out=v_w_out, v_conv_w_in=v_conv_w_in, v_conv_dw=v_conv_dw, v_conv_dw_b=v_conv_dw_b, v_conv_ln_g=v_conv_ln_g, v_conv_ln_b=v_conv_ln_b, v_mla_w_in=v_mla_w_in, v_mla_q_norm_g=v_mla_q_norm_g, v_mla_w_uq=v_mla_w_uq, v_mla_kv_norm_g=v_mla_kv_norm_g, v_mla_w_ukv=v_mla_w_ukv, v_final_norm_g=v_final_norm_g)
    weights = {n: given[n] for n in TWIN_WEIGHTS}
    shared = {n: given[n] for n in SHARED_INPUTS}
    per_example = {n: given[n] for n in ['x', 'mem', 'positions']}
    grad_fn = _jax.value_and_grad(_loss, argnums=(0, 1))

    def one_microbatch(ex, loss_target):
        ex = dict(ex)
        diff = ex.pop(TWIN_DIFF_INPUT)
        return grad_fn(weights, diff, {**shared, **ex}, loss_target)

    if N_MICROBATCH == 1:
        loss, (grad_w, grad_x) = one_microbatch(per_example, given["loss_target"])
    else:
        def body(carry, xs):
            loss_sum, grad_sum = carry
            l_k, (gw_k, gx_k) = one_microbatch(xs[0], xs[1])
            with _jax.named_scope("update"):
                return (loss_sum + l_k, _jax.tree.map(_jnp.add, grad_sum, gw_k)), gx_k

        init = (_jnp.zeros((), _jnp.float32), _jax.tree.map(_jnp.zeros_like, weights))
        (loss, grad_w), grad_x = _jax.lax.scan(body, init, (per_example, given["loss_target"]))
    with _jax.named_scope("update"):
        delta_w, new_m, new_v = {}, {}, {}
        for n in TWIN_WEIGHTS:
            delta_w[n], new_m[n], new_v[n] = _adamw(weights[n], grad_w[n], given["m_" + n], given["v_" + n])
    return (loss, grad_x, *[grad_w[n] for n in TWIN_WEIGHTS], *[delta_w[n] for n in TWIN_WEIGHTS],
            *[new_m[n] for n in TWIN_WEIGHTS], *[new_v[n] for n in TWIN_WEIGHTS])
```

```python
import functools

import jax
import jax.numpy as jnp
from jax import lax
from jax.experimental import pallas as pl
from jax.experimental.pallas import tpu as pltpu

F32 = jnp.float32
BF16 = jnp.bfloat16
MESH = pl.DeviceIdType.MESH

D_MODEL = 1024
MIX = 2048
MAIN = 1536
MEMW = 512
MEM_HEADS = 4
HEAD = 128
CONV_K = 31
CONV_PAD = 32
MLA_HEADS = 12
ROPE = 64
QK_PAD = 256
Q_RANK = 512
KV_RANK = 256
ROPE_THETA = 10000.0
RMS_EPS = 1e-6
LN_EPS = 1e-5
MEM_SCALE = HEAD ** -0.5
MLA_SCALE = (HEAD + ROPE) ** -0.5
NEG = -1e30

P0_COLS = 5632
P0_A, P0_G, P0_QM = 2048, 3584, 5120
P1_COLS = 3456
P1_CQ, P1_QM, P1_CKV, P1_KR = 2048, 2560, 3072, 3328

ADAM_LR = 0.001
ADAM_B1 = 0.9
ADAM_B2 = 0.999
ADAM_EPS = 1e-08
ADAM_WD = 0.01
ADAM_STEP = 10

VMEM_LIMIT = 56 * 1024 * 1024

PACK_ORDER = ("w_mem_kv", "w_out", "conv_w_in", "mla_w_in", "mla_w_uq", "mla_w_ukv")
SHARD_SHAPE = {"w_mem_kv": (2, 256, 1024), "w_out": (2, 512, 1024), "conv_w_in": (1024, 1408),
               "mla_w_in": (1024, 848), "mla_w_uq": (512, 576), "mla_w_ukv": (256, 768)}
PACK_ROWS = 4352
PACK_TILE = 128


def _cparams(sem=None):
    return pltpu.CompilerParams(dimension_semantics=sem, vmem_limit_bytes=VMEM_LIMIT)


def _tile(n, pref):
    if n <= pref:
        return n
    t = (pref // 128) * 128
    while t > 128 and n % t:
        t -= 128
    assert n % t == 0, (n, pref)
    return t


def _mm_kernel(a_ref, b_ref, o_ref, acc_ref, *, nk, ta, tb):
    k = pl.program_id(2)
    a = a_ref[...].astype(BF16)
    b = b_ref[...].astype(BF16)
    dn = (((0 if ta else 1,), (1 if tb else 0,)), ((), ()))
    p = lax.dot_general(a, b, dn, preferred_element_type=F32)

    @pl.when(k == 0)
    def _():
        acc_ref[...] = p

    @pl.when(k > 0)
    def _():
        acc_ref[...] += p

    @pl.when(k == nk - 1)
    def _():
        o_ref[...] = acc_ref[...].astype(o_ref.dtype)


def _matmul(a, b, *, name, ta=False, tb=False, out_dtype=F32, tm=512, tn=512, tk=1024):
    m, kd = (a.shape[1], a.shape[0]) if ta else a.shape
    n = b.shape[0] if tb else b.shape[1]
    assert kd == (b.shape[1] if tb else b.shape[0])
    tm, tn, tk = _tile(m, tm), _tile(n, tn), _tile(kd, tk)
    nk = kd // tk
    a_spec = (pl.BlockSpec((tk, tm), lambda i, j, k: (k, i)) if ta
              else pl.BlockSpec((tm, tk), lambda i, j, k: (i, k)))
    b_spec = (pl.BlockSpec((tn, tk), lambda i, j, k: (j, k)) if tb
              else pl.BlockSpec((tk, tn), lambda i, j, k: (k, j)))
    return pl.pallas_call(
        functools.partial(_mm_kernel, nk=nk, ta=ta, tb=tb),
        name=name,
        grid=(m // tm, n // tn, nk),
        in_specs=[a_spec, b_spec],
        out_specs=pl.BlockSpec((tm, tn), lambda i, j, k: (i, j)),
        out_shape=jax.ShapeDtypeStruct((m, n), out_dtype),
        scratch_shapes=[pltpu.VMEM((tm, tn), F32)],
        compiler_params=_cparams(("parallel", "parallel", "arbitrary")),
    )(a, b)


def _rms_fwd_kernel(h_ref, g_ref, o_ref):
    h = h_ref[...]
    rstd = lax.rsqrt(jnp.mean(h * h, axis=-1, keepdims=True) + RMS_EPS)
    o_ref[...] = (h * rstd * g_ref[...]).astype(o_ref.dtype)


def _rms_fwd(h, g, *, name, width=None, col_block=0, tm=512):
    t = h.shape[0]
    width = width or h.shape[1]
    tm = _tile(t, tm)
    return pl.pallas_call(
        functools.partial(_rms_fwd_kernel),
        name=name,
        grid=(t // tm,),
        in_specs=[pl.BlockSpec((tm, width), lambda i: (i, col_block)),
                  pl.BlockSpec((1, width), lambda i: (0, 0))],
        out_specs=pl.BlockSpec((tm, width), lambda i: (i, 0)),
        out_shape=jax.ShapeDtypeStruct((t, width), BF16),
        compiler_params=_cparams(("parallel",)),
    )(h, g.reshape(1, width))


def _rms_bwd_math(h, g, du):
    rstd = lax.rsqrt(jnp.mean(h * h, axis=-1, keepdims=True) + RMS_EPS)
    dug = du * g
    dh = rstd * dug - h * (rstd * rstd * rstd) * jnp.mean(dug * h, axis=-1, keepdims=True)
    dg = jnp.sum(du * h * rstd, axis=0, keepdims=True)
    return dh, dg


def _rms_bwd_kernel(*refs, has_res):
    if has_res:
        h_ref, g_ref, du_ref, res_ref, dh_ref, dg_ref = refs
    else:
        h_ref, g_ref, du_ref, dh_ref, dg_ref = refs
    dh, dg = _rms_bwd_math(h_ref[...], g_ref[...], du_ref[...].astype(F32))
    if has_res:
        dh = dh + res_ref[...]
    dh_ref[...] = dh.astype(dh_ref.dtype)

    @pl.when(pl.program_id(0) == 0)
    def _():
        dg_ref[...] = dg

    @pl.when(pl.program_id(0) > 0)
    def _():
        dg_ref[...] += dg


def _rms_bwd(h, g, du, res=None, *, name, width=None, col_block=0, out_dtype=F32, tm=512):
    t = h.shape[0]
    width = width or h.shape[1]
    tm = _tile(t, tm)
    row = pl.BlockSpec((tm, width), lambda i: (i, 0))
    in_specs = [pl.BlockSpec((tm, width), lambda i: (i, col_block)),
                pl.BlockSpec((1, width), lambda i: (0, 0)), row]
    args = [h, g.reshape(1, width), du]
    if res is not None:
        in_specs.append(row)
        args.append(res)
    return pl.pallas_call(
        functools.partial(_rms_bwd_kernel, has_res=res is not None),
        name=name,
        grid=(t // tm,),
        in_specs=in_specs,
        out_specs=[row, pl.BlockSpec((1, width), lambda i: (0, 0))],
        out_shape=[jax.ShapeDtypeStruct((t, width), out_dtype), jax.ShapeDtypeStruct((1, width), F32)],
        compiler_params=_cparams(("arbitrary",)),
    )(*args)


def _final_kernel(h_ref, g_ref, t_ref, dh_ref, dg_ref, loss_ref):
    h = h_ref[...]
    g = g_ref[...]
    rstd = lax.rsqrt(jnp.mean(h * h, axis=-1, keepdims=True) + RMS_EPS)
    e = h * rstd * g - t_ref[...]
    part = 0.5 * jnp.sum(jnp.mean(e * e, axis=-1, keepdims=True), axis=0, keepdims=True)
    dh, dg = _rms_bwd_math(h, g, e * (1.0 / D_MODEL))
    dh_ref[...] = dh
    part = jnp.broadcast_to(part, loss_ref.shape)

    @pl.when(pl.program_id(0) == 0)
    def _():
        dg_ref[...] = dg
        loss_ref[...] = part

    @pl.when(pl.program_id(0) > 0)
    def _():
        dg_ref[...] += dg
        loss_ref[...] += part


def _final_loss(h, g, target, *, tm=512):
    t, d = h.shape
    tm = _tile(t, tm)
    row = pl.BlockSpec((tm, d), lambda i: (i, 0))
    return pl.pallas_call(
        functools.partial(_final_kernel),
        name="final_loss",
        grid=(t // tm,),
        in_specs=[row, pl.BlockSpec((1, d), lambda i: (0, 0)), row],
        out_specs=[row, pl.BlockSpec((1, d), lambda i: (0, 0)), pl.BlockSpec((1, 128), lambda i: (0, 0))],
        out_shape=[jax.ShapeDtypeStruct((t, d), F32), jax.ShapeDtypeStruct((1, d), F32),
                   jax.ShapeDtypeStruct((1, 128), F32)],
        compiler_params=_cparams(("arbitrary",)),
    )(h, g.reshape(1, d), target)


CONV_CT = 128
CONV_TC = 256


def _glu_into(pad_ref, a_ref, g_ref, seq, tc):
    ct = pad_ref.shape[1]
    pad_ref[0:CONV_PAD, :] = jnp.zeros((CONV_PAD, ct), F32)
    for r in range(0, seq, tc):
        a = a_ref[0, r:r + tc, :]
        g = g_ref[0, r:r + tc, :]
        pad_ref[CONV_PAD + r:CONV_PAD + r + tc, :] = a * jax.nn.sigmoid(g)


def _conv_fwd_kernel(a_ref, g_ref, dw_ref, dwb_ref, hc_ref, pad_ref, *, seq, tc):
    ct = pad_ref.shape[1]
    _glu_into(pad_ref, a_ref, g_ref, seq, tc)
    for r in range(0, seq, tc):
        acc = jnp.broadcast_to(dwb_ref[...], (tc, ct))
        for k in range(CONV_K):
            o = CONV_PAD + r - (CONV_K - 1) + k
            acc = acc + dw_ref[k:k + 1, :] * pad_ref[o:o + tc, :]
        hc_ref[0, r:r + tc, :] = acc


def _conv_fwd(p0, dw, dwb, bsz, seq):
    ct = CONV_CT
    tc = min(CONV_TC, seq)
    p3 = p0.reshape(bsz, seq, P0_COLS)
    return pl.pallas_call(
        functools.partial(_conv_fwd_kernel, seq=seq, tc=tc),
        name="conv_fwd",
        grid=(MAIN // ct, bsz),
        in_specs=[pl.BlockSpec((1, seq, ct), lambda j, b: (b, 0, P0_A // ct + j)),
                  pl.BlockSpec((1, seq, ct), lambda j, b: (b, 0, P0_G // ct + j)),
                  pl.BlockSpec((CONV_K, ct), lambda j, b: (0, j)),
                  pl.BlockSpec((1, ct), lambda j, b: (0, j))],
        out_specs=pl.BlockSpec((1, seq, ct), lambda j, b: (b, 0, j)),
        out_shape=jax.ShapeDtypeStruct((bsz, seq, MAIN), F32),
        scratch_shapes=[pltpu.VMEM((seq + CONV_PAD, ct), F32)],
        compiler_params=_cparams(("parallel", "parallel")),
    )(p3, p3, dw, dwb)


def _conv_bwd_kernel(a_ref, g_ref, dhc_ref, dw_ref, da_ref, dg_ref, ddw_ref, pad_ref, dpad_ref, acc_ref,
                     *, seq, tc):
    ct = pad_ref.shape[1]
    b = pl.program_id(1)
    _glu_into(pad_ref, a_ref, g_ref, seq, tc)
    dpad_ref[seq:seq + CONV_PAD, :] = jnp.zeros((CONV_PAD, ct), F32)
    for r in range(0, seq, tc):
        dpad_ref[r:r + tc, :] = dhc_ref[0, r:r + tc, :]
    acc_ref[...] = jnp.zeros(acc_ref.shape, F32)
    for r in range(0, seq, tc):
        dh = dhc_ref[0, r:r + tc, :]
        dglu = jnp.zeros((tc, ct), F32)
        for k in range(CONV_K):
            o = r + (CONV_K - 1) - k
            dglu = dglu + dw_ref[k:k + 1, :] * dpad_ref[o:o + tc, :]
            o = CONV_PAD + r - (CONV_K - 1) + k
            prod = pad_ref[o:o + tc, :] * dh
            acc_ref[k] += jnp.sum(prod.reshape(tc // 8, 8, ct), axis=0)
        acc_ref[CONV_K] += jnp.sum(dh.reshape(tc // 8, 8, ct), axis=0)
        a = a_ref[0, r:r + tc, :]
        sg = jax.nn.sigmoid(g_ref[0, r:r + tc, :])
        da_ref[0, r:r + tc, :] = (dglu * sg).astype(da_ref.dtype)
        dg_ref[0, r:r + tc, :] = (dglu * a * sg * (1.0 - sg)).astype(dg_ref.dtype)
    tot = jnp.sum(acc_ref[...], axis=1)

    @pl.when(b == 0)
    def _():
        ddw_ref[...] = tot

    @pl.when(b > 0)
    def _():
        ddw_ref[...] += tot


def _conv_bwd(p0, dhc, dw, bsz, seq):
    ct = CONV_CT
    tc = min(CONV_TC, seq)
    p3 = p0.reshape(bsz, seq, P0_COLS)
    blk = pl.BlockSpec((1, seq, ct), lambda j, b: (b, 0, j))
    return pl.pallas_call(
        functools.partial(_conv_bwd_kernel, seq=seq, tc=tc),
        name="conv_bwd",
        grid=(MAIN // ct, bsz),
        in_specs=[pl.BlockSpec((1, seq, ct), lambda j, b: (b, 0, P0_A // ct + j)),
                  pl.BlockSpec((1, seq, ct), lambda j, b: (b, 0, P0_G // ct + j)),
                  blk,
                  pl.BlockSpec((CONV_K, ct), lambda j, b: (0, j))],
        out_specs=[blk, blk, pl.BlockSpec((CONV_K + 1, ct), lambda j, b: (0, j))],
        out_shape=[jax.ShapeDtypeStruct((bsz, seq, MAIN), BF16), jax.ShapeDtypeStruct((bsz, seq, MAIN), BF16),
                   jax.ShapeDtypeStruct((CONV_K + 1, MAIN), F32)],
        scratch_shapes=[pltpu.VMEM((seq + CONV_PAD, ct), F32), pltpu.VMEM((seq + CONV_PAD, ct), F32),
                        pltpu.VMEM((CONV_K + 1, 8, ct), F32)],
        compiler_params=_cparams(("parallel", "arbitrary")),
    )(p3, p3, dhc, dw)


def _ln_parts(x, lng, lnb):
    mu = jnp.mean(x, axis=-1, keepdims=True)
    xc = x - mu
    rstd = lax.rsqrt(jnp.mean(xc * xc, axis=-1, keepdims=True) + LN_EPS)
    xh = xc * rstd
    hl = xh * lng + lnb
    return rstd, xh, hl


def _gate_fwd_kernel(*refs, ln):
    if ln:
        main_ref, ymem_ref, z_ref, lng_ref, lnb_ref, y_ref = refs
    else:
        main_ref, ymem_ref, z_ref, y_ref = refs
    x = main_ref[...]
    if ln:
        _, _, hl = _ln_parts(x, lng_ref[...], lnb_ref[...])
        x = hl * jax.nn.sigmoid(hl)
    z = z_ref[...]
    sz = z * jax.nn.sigmoid(z)
    y_ref[:, :MAIN] = (x * sz[:, :MAIN]).astype(y_ref.dtype)
    y_ref[:, MAIN:] = (ymem_ref[...] * sz[:, MAIN:]).astype(y_ref.dtype)


def _gate_fwd(main, ymem, p, ln_g=None, ln_b=None, *, name, tm=256):
    t = main.shape[0]
    tm = _tile(t, tm)
    ln = ln_g is not None
    in_specs = [pl.BlockSpec((tm, MAIN), lambda i: (i, 0)), pl.BlockSpec((tm, MEMW), lambda i: (i, 0)),
                pl.BlockSpec((tm, MIX), lambda i: (i, 0))]
    args = [main, ymem, p]
    if ln:
        in_specs += [pl.BlockSpec((1, MAIN), lambda i: (0, 0))] * 2
        args += [ln_g.reshape(1, MAIN), ln_b.reshape(1, MAIN)]
    return pl.pallas_call(
        functools.partial(_gate_fwd_kernel, ln=ln),
        name=name,
        grid=(t // tm,),
        in_specs=in_specs,
        out_specs=pl.BlockSpec((tm, MIX), lambda i: (i, 0)),
        out_shape=jax.ShapeDtypeStruct((t, MIX), BF16),
        compiler_params=_cparams(("parallel",)),
    )(*args)


def _gate_bwd_kernel(*refs, ln):
    if ln:
        (dy_ref, main_ref, ymem_ref, z_ref, lng_ref, lnb_ref,
         dmain_ref, dymem_ref, dz_ref, dlng_ref, dlnb_ref) = refs
    else:
        dy_ref, main_ref, ymem_ref, z_ref, dmain_ref, dymem_ref, dz_ref = refs
    dy = dy_ref[...]
    z = z_ref[...]
    sg = jax.nn.sigmoid(z)
    sz = z * sg
    dsz = sg * (1.0 + z * (1.0 - sg))
    x = main_ref[...]
    if ln:
        lng = lng_ref[...]
        rstd, xh, hl = _ln_parts(x, lng, lnb_ref[...])
        sh = jax.nn.sigmoid(hl)
        ymain = hl * sh
    else:
        ymain = x
    dym = dy[:, :MAIN] * sz[:, :MAIN]
    dymem_ref[...] = dy[:, MAIN:] * sz[:, MAIN:]
    dz_ref[:, :MAIN] = (dy[:, :MAIN] * ymain * dsz[:, :MAIN]).astype(dz_ref.dtype)
    dz_ref[:, MAIN:] = (dy[:, MAIN:] * ymem_ref[...] * dsz[:, MAIN:]).astype(dz_ref.dtype)
    if not ln:
        dmain_ref[...] = dym
        return
    dhl = dym * (sh * (1.0 + hl * (1.0 - sh)))
    dxh = dhl * lng
    dmain_ref[...] = rstd * (dxh - jnp.mean(dxh, axis=-1, keepdims=True)
                             - xh * jnp.mean(dxh * xh, axis=-1, keepdims=True))
    dlng = jnp.sum(dhl * xh, axis=0, keepdims=True)
    dlnb = jnp.sum(dhl, axis=0, keepdims=True)

    @pl.when(pl.program_id(0) == 0)
    def _():
        dlng_ref[...] = dlng
        dlnb_ref[...] = dlnb

    @pl.when(pl.program_id(0) > 0)
    def _():
        dlng_ref[...] += dlng
        dlnb_ref[...] += dlnb


def _gate_bwd(dy, main, ymem, p, ln_g=None, ln_b=None, *, name, tm=256):
    t = main.shape[0]
    tm = _tile(t, tm)
    ln = ln_g is not None
    r_main = pl.BlockSpec((tm, MAIN), lambda i: (i, 0))
    r_mem = pl.BlockSpec((tm, MEMW), lambda i: (i, 0))
    r_mix = pl.BlockSpec((tm, MIX), lambda i: (i, 0))
    vec = pl.BlockSpec((1, MAIN), lambda i: (0, 0))
    in_specs = [r_mix, r_main, r_mem, r_mix]
    args = [dy, main, ymem, p]
    out_specs = [r_main, r_mem, r_mix]
    out_shape = [jax.ShapeDtypeStruct((t, MAIN), F32), jax.ShapeDtypeStruct((t, MEMW), F32),
                 jax.ShapeDtypeStruct((t, MIX), BF16)]
    if ln:
        in_specs += [vec, vec]
        args += [ln_g.reshape(1, MAIN), ln_b.reshape(1, MAIN)]
        out_specs += [vec, vec]
        out_shape += [jax.ShapeDtypeStruct((1, MAIN), F32)] * 2
    return pl.pallas_call(
        functools.partial(_gate_bwd_kernel, ln=ln),
        name=name,
        grid=(t // tm,),
        in_specs=in_specs,
        out_specs=out_specs,
        out_shape=out_shape,
        compiler_params=_cparams(("arbitrary",)),
    )(*args)


def _dot_nt(a, b):
    return lax.dot_general(a, b, (((1,), (1,)), ((), ())), preferred_element_type=F32)


def _dot_tn(a, b):
    return lax.dot_general(a, b, (((0,), (0,)), ((), ())), preferred_element_type=F32)


def _dot(a, b):
    return jnp.dot(a, b, preferred_element_type=F32)


def _mem_probs(q, k):
    s = _dot_nt(q, k) * MEM_SCALE
    p = jnp.exp(s - jnp.max(s, axis=-1, keepdims=True))
    return p / jnp.sum(p, axis=-1, keepdims=True)


def _mem_fwd_kernel(q_ref, kv_ref, o_ref):
    for h in range(MEM_HEADS):
        c = slice(h * HEAD, (h + 1) * HEAD)
        cv = slice(MEMW + h * HEAD, MEMW + (h + 1) * HEAD)
        p = _mem_probs(q_ref[0, :, c].astype(BF16), kv_ref[0, :, c])
        o_ref[0, :, c] = _dot(p.astype(BF16), kv_ref[0, :, cv])


def _mem_fwd(p, kvm, col_block, bsz, seq, *, name, tq=512):
    tq = _tile(seq, tq)
    p3 = p.reshape(bsz, seq, p.shape[1])
    mlen = kvm.shape[1]
    return pl.pallas_call(
        functools.partial(_mem_fwd_kernel),
        name=name,
        grid=(bsz, seq // tq),
        in_specs=[pl.BlockSpec((1, tq, MEMW), lambda b, i: (b, i, col_block)),
                  pl.BlockSpec((1, mlen, 2 * MEMW), lambda b, i: (b, 0, 0))],
        out_specs=pl.BlockSpec((1, tq, MEMW), lambda b, i: (b, i, 0)),
        out_shape=jax.ShapeDtypeStruct((bsz, seq, MEMW), F32),
        compiler_params=_cparams(("parallel", "parallel")),
    )(p3, kvm)


def _mem_bwd_kernel(q_ref, kv_ref, do_ref, dq_ref, dkv_ref):
    @pl.when(pl.program_id(1) == 0)
    def _():
        dkv_ref[...] = jnp.zeros(dkv_ref.shape, F32)

    for h in range(MEM_HEADS):
        c = slice(h * HEAD, (h + 1) * HEAD)
        cv = slice(MEMW + h * HEAD, MEMW + (h + 1) * HEAD)
        q = q_ref[0, :, c].astype(BF16)
        k = kv_ref[0, :, c]
        v = kv_ref[0, :, cv]
        do = do_ref[0, :, c].astype(BF16)
        p = _mem_probs(q, k)
        dp = _dot_nt(do, v)
        ds = (p * (dp - jnp.sum(p * dp, axis=-1, keepdims=True)) * MEM_SCALE).astype(BF16)
        dq_ref[0, :, c] = _dot(ds, k).astype(dq_ref.dtype)
        dkv_ref[0, :, c] += _dot_tn(ds, q)
        dkv_ref[0, :, cv] += _dot_tn(p.astype(BF16), do)


def _mem_bwd(p, kvm, dymem, col_block, bsz, seq, *, name, tq=512):
    tq = _tile(seq, tq)
    p3 = p.reshape(bsz, seq, p.shape[1])
    mlen = kvm.shape[1]
    return pl.pallas_call(
        functools.partial(_mem_bwd_kernel),
        name=name,
        grid=(bsz, seq // tq),
        in_specs=[pl.BlockSpec((1, tq, MEMW), lambda b, i: (b, i, col_block)),
                  pl.BlockSpec((1, mlen, 2 * MEMW), lambda b, i: (b, 0, 0)),
                  pl.BlockSpec((1, tq, MEMW), lambda b, i: (b, i, 0))],
        out_specs=[pl.BlockSpec((1, tq, MEMW), lambda b, i: (b, i, 0)),
                   pl.BlockSpec((1, mlen, 2 * MEMW), lambda b, i: (b, 0, 0))],
        out_shape=[jax.ShapeDtypeStruct((bsz, seq, MEMW), BF16),
                   jax.ShapeDtypeStruct((bsz, mlen, 2 * MEMW), F32)],
        compiler_params=_cparams(("parallel", "arbitrary")),
    )(p3, kvm, dymem.reshape(bsz, seq, MEMW))


def _swap32(x):
    lane = lax.broadcasted_iota(jnp.int32, x.shape, 1)
    return jnp.where(lane < 32, pltpu.roll(x, 96, 1), pltpu.roll(x, 32, 1))


def _rope(x, cs, sn):
    return x * cs + _swap32(x) * sn


def _rope_t(d, cs, sn):
    return d * cs + _swap32(d * sn)


def _q_rope_kernel(q_ref, cs_ref, sn_ref, o_ref, *, inverse):
    cs = cs_ref[...]
    sn = sn_ref[...]
    for h in range(MLA_HEADS):
        c0 = slice(h * QK_PAD, h * QK_PAD + HEAD)
        c1 = slice(h * QK_PAD + HEAD, (h + 1) * QK_PAD)
        o_ref[:, c0] = q_ref[:, c0].astype(o_ref.dtype)
        x = q_ref[:, c1]
        o_ref[:, c1] = (_rope_t(x, cs, sn) if inverse else _rope(x, cs, sn)).astype(o_ref.dtype)


def _q_rope(q, cs, sn, *, inverse, name, tm=256):
    t, w = q.shape
    tm = _tile(t, tm)
    row = pl.BlockSpec((tm, w), lambda i: (i, 0))
    tab = pl.BlockSpec((tm, 128), lambda i: (i, 0))
    return pl.pallas_call(
        functools.partial(_q_rope_kernel, inverse=inverse),
        name=name,
        grid=(t // tm,),
        in_specs=[row, tab, tab],
        out_specs=row,
        out_shape=jax.ShapeDtypeStruct((t, w), BF16),
        compiler_params=_cparams(("parallel",)),
    )(q, cs, sn)


def _kv_pack_kernel(kv_ref, kr_ref, cs_ref, sn_ref, k_ref, v_ref):
    krr = _rope(kr_ref[...], cs_ref[...], sn_ref[...]).astype(k_ref.dtype)
    for h in range(MLA_HEADS):
        k_ref[:, h * QK_PAD:h * QK_PAD + HEAD] = kv_ref[:, h * 2 * HEAD:h * 2 * HEAD + HEAD].astype(k_ref.dtype)
        k_ref[:, h * QK_PAD + HEAD:(h + 1) * QK_PAD] = krr
        v_ref[:, h * HEAD:(h + 1) * HEAD] = kv_ref[:, h * 2 * HEAD + HEAD:(h + 1) * 2 * HEAD].astype(v_ref.dtype)


def _kv_pack(kv, p1, cs, sn, *, tm=256):
    t = kv.shape[0]
    tm = _tile(t, tm)
    tab = pl.BlockSpec((tm, 128), lambda i: (i, 0))
    return pl.pallas_call(
        functools.partial(_kv_pack_kernel),
        name="kv_pack",
        grid=(t // tm,),
        in_specs=[pl.BlockSpec((tm, MLA_HEADS * 2 * HEAD), lambda i: (i, 0)),
                  pl.BlockSpec((tm, 128), lambda i: (i, P1_KR // 128)), tab, tab],
        out_specs=[pl.BlockSpec((tm, MLA_HEADS * QK_PAD), lambda i: (i, 0)),
                   pl.BlockSpec((tm, MAIN), lambda i: (i, 0))],
        out_shape=[jax.ShapeDtypeStruct((t, MLA_HEADS * QK_PAD), BF16), jax.ShapeDtypeStruct((t, MAIN), BF16)],
        compiler_params=_cparams(("parallel",)),
    )(kv, p1, cs, sn)


def _kv_unpack_kernel(dk_ref, dv_ref, cs_ref, sn_ref, dkv_ref, dkr_ref):
    dkrr = jnp.zeros(dkr_ref.shape, F32)
    for h in range(MLA_HEADS):
        dkv_ref[:, h * 2 * HEAD:h * 2 * HEAD + HEAD] = dk_ref[:, h * QK_PAD:h * QK_PAD + HEAD].astype(dkv_ref.dtype)
        dkv_ref[:, h * 2 * HEAD + HEAD:(h + 1) * 2 * HEAD] = dv_ref[:, h * HEAD:(h + 1) * HEAD].astype(dkv_ref.dtype)
        dkrr = dkrr + dk_ref[:, h * QK_PAD + HEAD:(h + 1) * QK_PAD]
    dkr_ref[...] = _rope_t(dkrr, cs_ref[...], sn_ref[...]).astype(dkr_ref.dtype)


def _kv_unpack(dk, dv, cs, sn, *, tm=256):
    t = dk.shape[0]
    tm = _tile(t, tm)
    tab = pl.BlockSpec((tm, 128), lambda i: (i, 0))
    return pl.pallas_call(
        functools.partial(_kv_unpack_kernel),
        name="kv_unpack",
        grid=(t // tm,),
        in_specs=[pl.BlockSpec((tm, MLA_HEADS * QK_PAD), lambda i: (i, 0)),
                  pl.BlockSpec((tm, MAIN), lambda i: (i, 0)), tab, tab],
        out_specs=[pl.BlockSpec((tm, MLA_HEADS * 2 * HEAD), lambda i: (i, 0)), tab],
        out_shape=[jax.ShapeDtypeStruct((t, MLA_HEADS * 2 * HEAD), BF16), jax.ShapeDtypeStruct((t, 128), BF16)],
        compiler_params=_cparams(("parallel",)),
    )(dk, dv, cs, sn)


ATT_T = 256


def _causal(s, ri, rj, t):
    row = ri + lax.broadcasted_iota(jnp.int32, (t, t), 0)
    col = rj + lax.broadcasted_iota(jnp.int32, (t, t), 1)
    return jnp.where(col <= row, s, NEG)


def _attn_fwd_kernel(q_ref, k_ref, v_ref, o_ref, lse_ref, *, t):
    i = pl.program_id(2)
    q = q_ref[0]

    def body(j, carry):
        m, l, acc = carry
        rj = pl.multiple_of(j * t, t)
        s = _causal(_dot_nt(q, k_ref[0, pl.ds(rj, t), :]) * MLA_SCALE, i * t, rj, t)
        m_new = jnp.maximum(m, jnp.max(s, axis=-1, keepdims=True))
        p = jnp.exp(s - m_new)
        alpha = jnp.exp(m - m_new)
        l = alpha * l + jnp.sum(p, axis=-1, keepdims=True)
        acc = alpha * acc + _dot(p.astype(BF16), v_ref[0, pl.ds(rj, t), :])
        return m_new, l, acc

    init = (jnp.full((t, 1), NEG, F32), jnp.zeros((t, 1), F32), jnp.zeros((t, HEAD), F32))
    m, l, acc = lax.fori_loop(0, i + 1, body, init)
    o_ref[0] = acc / l
    lse_ref[0, 0] = m + jnp.log(l)


def _attn_fwd(qb, kb, vb, bsz, seq):
    t = min(ATT_T, seq)
    q3 = qb.reshape(bsz, seq, MLA_HEADS * QK_PAD)
    k3 = kb.reshape(bsz, seq, MLA_HEADS * QK_PAD)
    v3 = vb.reshape(bsz, seq, MAIN)
    return pl.pallas_call(
        functools.partial(_attn_fwd_kernel, t=t),
        name="attn_fwd",
        grid=(bsz, MLA_HEADS, seq // t),
        in_specs=[pl.BlockSpec((1, t, QK_PAD), lambda b, h, i: (b, i, h)),
                  pl.BlockSpec((1, seq, QK_PAD), lambda b, h, i: (b, 0, h)),
                  pl.BlockSpec((1, seq, HEAD), lambda b, h, i: (b, 0, h))],
        out_specs=[pl.BlockSpec((1, t, HEAD), lambda b, h, i: (b, i, h)),
                   pl.BlockSpec((1, 1, t, 1), lambda b, h, i: (b, h, i, 0))],
        out_shape=[jax.ShapeDtypeStruct((bsz, seq, MAIN), F32),
                   jax.ShapeDtypeStruct((bsz, MLA_HEADS, seq, 1), F32)],
        compiler_params=_cparams(("parallel", "parallel", "parallel")),
    )(q3, k3, v3)


def _attn_bwd_kernel(q_ref, k_ref, v_ref, o_ref, do_ref, lse_ref, dq_ref, dk_ref, dv_ref, delta_ref, *, seq, t):
    nb = seq // t

    def delta_body(i, c):
        ri = pl.multiple_of(i * t, t)
        delta_ref[pl.ds(ri, t), :] = jnp.sum(do_ref[0, pl.ds(ri, t), :] * o_ref[0, pl.ds(ri, t), :],
                                             axis=-1, keepdims=True)
        dq_ref[0, pl.ds(ri, t), :] = jnp.zeros((t, QK_PAD), F32)
        return c

    lax.fori_loop(0, nb, delta_body, 0)

    def kv_body(j, c):
        rj = pl.multiple_of(j * t, t)
        k = k_ref[0, pl.ds(rj, t), :]
        v = v_ref[0, pl.ds(rj, t), :]

        def q_body(i, carry):
            dk, dv = carry
            ri = pl.multiple_of(i * t, t)
            q = q_ref[0, pl.ds(ri, t), :]
            do = do_ref[0, pl.ds(ri, t), :].astype(BF16)
            s = _causal(_dot_nt(q, k) * MLA_SCALE, ri, rj, t)
            p = jnp.exp(s - lse_ref[0, 0, pl.ds(ri, t), :])
            dp = _dot_nt(do, v)
            ds = (p * (dp - delta_ref[pl.ds(ri, t), :]) * MLA_SCALE).astype(BF16)
            dv = dv + _dot_tn(p.astype(BF16), do)
            dk = dk + _dot_tn(ds, q)
            dq_ref[0, pl.ds(ri, t), :] += _dot(ds, k)
            return dk, dv

        dk, dv = lax.fori_loop(j, nb, q_body, (jnp.zeros((t, QK_PAD), F32), jnp.zeros((t, HEAD), F32)))
        dk_ref[0, pl.ds(rj, t), :] = dk
        dv_ref[0, pl.ds(rj, t), :] = dv
        return c

    lax.fori_loop(0, nb, kv_body, 0)


def _attn_bwd(qb, kb, vb, o, do, lse, bsz, seq):
    t = min(ATT_T, seq)
    q3 = qb.reshape(bsz, seq, MLA_HEADS * QK_PAD)
    k3 = kb.reshape(bsz, seq, MLA_HEADS * QK_PAD)
    v3 = vb.reshape(bsz, seq, MAIN)
    qk = pl.BlockSpec((1, seq, QK_PAD), lambda b, h: (b, 0, h))
    vv = pl.BlockSpec((1, seq, HEAD), lambda b, h: (b, 0, h))
    dq, dk, dv = pl.pallas_call(
        functools.partial(_attn_bwd_kernel, seq=seq, t=t),
        name="attn_bwd",
        grid=(bsz, MLA_HEADS),
        in_specs=[qk, qk, vv, vv, vv, pl.BlockSpec((1, 1, seq, 1), lambda b, h: (b, h, 0, 0))],
        out_specs=[qk, qk, vv],
        out_shape=[jax.ShapeDtypeStruct((bsz, seq, MLA_HEADS * QK_PAD), F32),
                   jax.ShapeDtypeStruct((bsz, seq, MLA_HEADS * QK_PAD), F32),
                   jax.ShapeDtypeStruct((bsz, seq, MAIN), F32)],
        scratch_shapes=[pltpu.VMEM((seq, 1), F32)],
        compiler_params=_cparams(("parallel", "parallel")),
    )(q3, k3, v3, o, do.reshape(bsz, seq, MAIN), lse)
    n = bsz * seq
    return dq.reshape(n, -1), dk.reshape(n, -1), dv.reshape(n, -1)


def _adamw_kernel(w_ref, g_ref, m_ref, v_ref, d_ref, nm_ref, nv_ref):
    g = g_ref[...]
    m = ADAM_B1 * m_ref[...] + (1.0 - ADAM_B1) * g
    v = ADAM_B2 * v_ref[...] + (1.0 - ADAM_B2) * (g * g)
    m_hat = m / (1.0 - ADAM_B1 ** ADAM_STEP)
    v_hat = v / (1.0 - ADAM_B2 ** ADAM_STEP)
    d_ref[...] = -ADAM_LR * (m_hat / (jnp.sqrt(v_hat) + ADAM_EPS) + ADAM_WD * w_ref[...])
    nm_ref[...] = m
    nv_ref[...] = v


def _adamw(w, g, m, v, *, name):
    shape = w.shape
    c = shape[-1]
    r = w.size // c
    tr = r
    for cand in (512, 256, 128, 64, 32, 16, 8):
        if r % cand == 0 and cand * c * 4 <= 2 * 1024 * 1024:
            tr = cand
            break
    blk = pl.BlockSpec((tr, c), lambda i: (i, 0))
    outs = pl.pallas_call(
        functools.partial(_adamw_kernel),
        name=name,
        grid=(r // tr,),
        in_specs=[blk] * 4,
        out_specs=[blk] * 3,
        out_shape=[jax.ShapeDtypeStruct((r, c), F32)] * 3,
        compiler_params=_cparams(("parallel",)),
    )(w.reshape(r, c), g.reshape(r, c), m.reshape(r, c), v.reshape(r, c))
    return tuple(o.reshape(shape) for o in outs)


HBM = pl.BlockSpec(memory_space=pl.ANY)


def _place():
    return lax.axis_index("x"), lax.axis_index("y"), lax.axis_index("c")


def _other_chips(x, y):
    return [(1 - x, y), (x, 1 - y), (1 - x, 1 - y)]


def _allgather_kernel(src, out, lsem, ssem, rsem, *, hr):
    x, y, c = _place()
    s = 2 * x + y
    chips = _other_chips(x, y)

    def half(chip_idx, core):
        return out.at[chip_idx, pl.ds(core * hr, hr), :]

    def copy(k, src_ref, dst_ref, to):
        return pltpu.make_async_remote_copy(src_ref=src_ref, dst_ref=dst_ref, send_sem=ssem.at[k], recv_sem=rsem.at[k],
                                            device_id=to, device_id_type=MESH)

    mine = pltpu.make_async_copy(src, out.at[s], lsem)
    mine.start()
    first = [copy(j, src.at[pl.ds(c * hr, hr), :], half(s, c), (*chip, c)) for j, chip in enumerate(chips)]
    for cp in first:
        cp.start()
    passed = []
    for j, chip in enumerate(chips):
        blk = half(2 * chip[0] + chip[1], c)
        copy(j, blk, blk, (*chip, c)).wait_recv()
        cp = copy(3 + j, blk, blk, (x, y, 1 - c))
        cp.start()
        passed.append(cp)
    for j, chip in enumerate(chips):
        blk = half(2 * chip[0] + chip[1], 1 - c)
        copy(3 + j, blk, blk, (x, y, 1 - c)).wait_recv()
    for cp in first + passed:
        cp.wait_send()
    mine.wait()


def _allgather_packed(pk):
    r, c = pk.shape
    return pl.pallas_call(
        functools.partial(_allgather_kernel, hr=r // 2),
        name="allgather_weights",
        in_specs=[HBM],
        out_specs=HBM,
        out_shape=jax.ShapeDtypeStruct((4, r, c), pk.dtype),
        scratch_shapes=[pltpu.SemaphoreType.DMA, pltpu.SemaphoreType.DMA((6,)), pltpu.SemaphoreType.DMA((6,))],
    )(pk)


def _swap_kernel(src, dst, ssem, rsem):
    x, y, c = _place()
    cp = pltpu.make_async_remote_copy(src_ref=src, dst_ref=dst, send_sem=ssem, recv_sem=rsem,
                                      device_id=(x, y, 1 - c), device_id_type=MESH)
    cp.start()
    cp.wait()


def _swap_with_sibling(a):
    return pl.pallas_call(
        functools.partial(_swap_kernel),
        name="rs_sibling_swap",
        in_specs=[HBM],
        out_specs=HBM,
        out_shape=jax.ShapeDtypeStruct(a.shape, a.dtype),
        scratch_shapes=[pltpu.SemaphoreType.DMA, pltpu.SemaphoreType.DMA],
    )(a)


def _chip_exchange_kernel(src, dst, ssem, rsem):
    x, y, c = _place()
    chips = _other_chips(x, y)
    cps = [pltpu.make_async_remote_copy(src_ref=src.at[2 * chip[0] + chip[1]], dst_ref=dst.at[j],
                                        send_sem=ssem.at[j], recv_sem=rsem.at[j],
                                        device_id=(*chip, c), device_id_type=MESH)
           for j, chip in enumerate(chips)]
    for cp in cps:
        cp.start()
    for cp in cps:
        cp.wait()


def _chip_exchange(a):
    return pl.pallas_call(
        functools.partial(_chip_exchange_kernel),
        name="rs_chip_exchange",
        in_specs=[HBM],
        out_specs=HBM,
        out_shape=jax.ShapeDtypeStruct((3,) + a.shape[1:], a.dtype),
        scratch_shapes=[pltpu.SemaphoreType.DMA((3,)), pltpu.SemaphoreType.DMA((3,))],
    )(a)


def _share_halves_kernel(src, out, lsem, ssem, rsem, *, hr):
    x, y, c = _place()
    rows = out.at[pl.ds(c * hr, hr), :]
    mine = pltpu.make_async_copy(src, rows, lsem)
    mine.start()
    cp = pltpu.make_async_remote_copy(src_ref=src, dst_ref=rows, send_sem=ssem, recv_sem=rsem,
                                      device_id=(x, y, 1 - c), device_id_type=MESH)
    cp.start()
    cp.wait()
    mine.wait()


def _share_halves(a):
    hr, c = a.shape
    return pl.pallas_call(
        functools.partial(_share_halves_kernel, hr=hr),
        name="rs_share_halves",
        in_specs=[HBM],
        out_specs=HBM,
        out_shape=jax.ShapeDtypeStruct((2 * hr, c), a.dtype),
        scratch_shapes=[pltpu.SemaphoreType.DMA, pltpu.SemaphoreType.DMA, pltpu.SemaphoreType.DMA],
    )(a)


def _gather_sum_kernel(src, gat, tot, ssem, rsem):
    x, y, c = _place()
    me = 4 * x + 2 * y + c
    gat[me] = src[...]
    flips = [(dx, dy, dc) for dx in (0, 1) for dy in (0, 1) for dc in (0, 1)][1:]
    cps = []
    for k, (dx, dy, dc) in enumerate(flips):
        peer = (1 - x if dx else x, 1 - y if dy else y, 1 - c if dc else c)
        cp = pltpu.make_async_remote_copy(src_ref=src, dst_ref=gat.at[me], send_sem=ssem.at[k], recv_sem=rsem.at[k],
                                          device_id=peer, device_id_type=MESH)
        cp.start()
        cps.append((cp, 4 * peer[0] + 2 * peer[1] + peer[2], peer))
    for k, (cp, idx, peer) in enumerate(cps):
        pltpu.make_async_remote_copy(src_ref=src, dst_ref=gat.at[idx], send_sem=ssem.at[k], recv_sem=rsem.at[k],
                                     device_id=peer, device_id_type=MESH).wait_recv()
    for cp, _, _ in cps:
        cp.wait_send()
    acc = gat[0]
    for d in range(1, 8):
        acc = acc + gat[d]
    tot[...] = acc


def _gather_sum_small(a, *, name):
    vm = pl.BlockSpec(memory_space=pltpu.VMEM)
    return pl.pallas_call(
        functools.partial(_gather_sum_kernel),
        name=name,
        in_specs=[vm],
        out_specs=[vm, vm],
        out_shape=[jax.ShapeDtypeStruct((8,) + a.shape, a.dtype), jax.ShapeDtypeStruct(a.shape, a.dtype)],
        scratch_shapes=[pltpu.SemaphoreType.DMA((7,)), pltpu.SemaphoreType.DMA((7,))],
    )(a)


def _add_pairs_kernel(c_ref, g_ref, r_ref, o_ref):
    o_ref[...] = (g_ref[...] + r_ref[...].astype(F32)).astype(o_ref.dtype)


def _add_pairs(core, g, recv):
    _, hr, c = recv.shape
    nt = hr // PACK_TILE
    return pl.pallas_call(
        functools.partial(_add_pairs_kernel),
        name="rs_add_pairs",
        grid_spec=pltpu.PrefetchScalarGridSpec(
            num_scalar_prefetch=1,
            grid=(4, nt),
            in_specs=[pl.BlockSpec((1, PACK_TILE, c), lambda s, i, cr: (s, cr[0] * nt + i, 0)),
                      pl.BlockSpec((1, PACK_TILE, c), lambda s, i, cr: (s, i, 0))],
            out_specs=pl.BlockSpec((1, PACK_TILE, c), lambda s, i, cr: (s, i, 0)),
        ),
        out_shape=jax.ShapeDtypeStruct(recv.shape, BF16),
        compiler_params=_cparams(("parallel", "parallel")),
    )(core, g, recv)


def _add_final_kernel(sc_ref, g_ref, r_ref, e_ref, o_ref):
    acc = g_ref[0] + r_ref[0].astype(F32)
    for j in range(3):
        acc = acc + e_ref[j].astype(F32)
    o_ref[...] = acc


def _add_final(chip_core, g, recv, exch):
    _, hr, c = recv.shape
    nt = hr // PACK_TILE
    return pl.pallas_call(
        functools.partial(_add_final_kernel),
        name="rs_add_final",
        grid_spec=pltpu.PrefetchScalarGridSpec(
            num_scalar_prefetch=1,
            grid=(nt,),
            in_specs=[pl.BlockSpec((1, PACK_TILE, c), lambda i, sc: (sc[0], sc[1] * nt + i, 0)),
                      pl.BlockSpec((1, PACK_TILE, c), lambda i, sc: (sc[0], i, 0)),
                      pl.BlockSpec((3, PACK_TILE, c), lambda i, sc: (0, i, 0))],
            out_specs=pl.BlockSpec((PACK_TILE, c), lambda i, sc: (i, 0)),
        ),
        out_shape=jax.ShapeDtypeStruct((hr, c), F32),
        compiler_params=_cparams(("parallel",)),
    )(chip_core, g, recv, exch)


def _pack_shards(sh, dtype):
    parts = [sh[n].astype(dtype).reshape(-1, 1024) for n in PACK_ORDER]
    used = sum(p.shape[0] for p in parts)
    parts.append(jnp.zeros((PACK_ROWS - used, 1024), dtype))
    return jnp.concatenate(parts, axis=0)


def _unpack_shards(buf):
    lead = buf.shape[:-2]
    out, o = {}, 0
    for n in PACK_ORDER:
        shp = SHARD_SHAPE[n]
        rows = (shp[0] * shp[1] * (shp[2] if len(shp) == 3 else 1)) // 1024
        out[n] = buf[..., o:o + rows, :].reshape(lead + shp)
        o += rows
    return out


def _full_from_shards(sh):
    full = {}
    for n in ("w_mem_kv", "w_out"):
        a = sh[n]
        full[n] = a.transpose(1, 0, 2, 3).reshape(2, 4 * a.shape[2], 1024)
    for n in ("conv_w_in", "mla_w_in", "mla_w_uq", "mla_w_ukv"):
        a = sh[n]
        full[n] = a.transpose(1, 0, 2).reshape(a.shape[1], 4 * a.shape[2])
    return full


def _shards_from_full(full):
    sh = {}
    for n in ("w_mem_kv", "w_out"):
        a = full[n]
        sh[n] = a.reshape(2, 4, a.shape[1] // 4, 1024).transpose(1, 0, 2, 3)
    for n in ("conv_w_in", "mla_w_in", "mla_w_uq", "mla_w_ukv"):
        a = full[n]
        sh[n] = a.reshape(a.shape[0], 4, a.shape[1] // 4).transpose(1, 0, 2)
    return sh


def _conv_in_to_internal(w):
    return jnp.concatenate([w[:, 3584:], w[:, :3584]], axis=1)


def _conv_in_from_internal(w):
    return jnp.concatenate([w[:, MIX:], w[:, :MIX]], axis=1)


def _mla_in_to_internal(w):
    return jnp.concatenate([w[:, 1344:], w[:, :512], w[:, 832:1344], w[:, 512:768], w[:, 768:832],
                            jnp.zeros((w.shape[0], 64), w.dtype)], axis=1)


def _mla_in_from_internal(w):
    return jnp.concatenate([w[:, P1_CQ:P1_QM], w[:, P1_CKV:P1_KR], w[:, P1_KR:P1_KR + 64], w[:, P1_QM:P1_CKV],
                            w[:, :MIX]], axis=1)


def _uq_to_internal(w):
    w = w.reshape(w.shape[0], MLA_HEADS, HEAD + ROPE)
    return jnp.pad(w, ((0, 0), (0, 0), (0, QK_PAD - HEAD - ROPE))).reshape(w.shape[0], MLA_HEADS * QK_PAD)


def _uq_from_internal(w):
    return w.reshape(w.shape[0], MLA_HEADS, QK_PAD)[:, :, :HEAD + ROPE].reshape(w.shape[0], MLA_HEADS * (HEAD + ROPE))


SMALL_GRAD_ROWS = 456


def _rows128(a, rows):
    flat = a.reshape(-1)
    return jnp.pad(flat, (0, rows * 128 - flat.shape[0])).reshape(rows, 128)


def kernel(x, mem, positions, norm_g, mem_norm_g, w_mem_kv, w_out, conv_w_in, conv_dw, conv_dw_b, conv_ln_g, conv_ln_b, mla_w_in, mla_q_norm_g, mla_w_uq, mla_kv_norm_g, mla_w_ukv, final_norm_g, loss_target, m_norm_g, m_mem_norm_g, m_w_mem_kv, m_w_out, m_conv_w_in, m_conv_dw, m_conv_dw_b, m_conv_ln_g, m_conv_ln_b, m_mla_w_in, m_mla_q_norm_g, m_mla_w_uq, m_mla_kv_norm_g, m_mla_w_ukv, m_final_norm_g, v_norm_g, v_mem_norm_g, v_w_mem_kv, v_w_out, v_conv_w_in, v_conv_dw, v_conv_dw_b, v_conv_ln_g, v_conv_ln_b, v_mla_w_in, v_mla_q_norm_g, v_mla_w_uq, v_mla_kv_norm_g, v_mla_w_ukv, v_final_norm_g):
    bsz, seq, d = x.shape
    n = bsz * seq
    mlen = mem.shape[1]
    ax, ay, ac = _place()
    chip = 2 * ax + ay

    weights = dict(norm_g=norm_g, mem_norm_g=mem_norm_g, w_mem_kv=w_mem_kv, w_out=w_out, conv_w_in=conv_w_in[0],
                   conv_dw=conv_dw[0], conv_dw_b=conv_dw_b, conv_ln_g=conv_ln_g, conv_ln_b=conv_ln_b,
                   mla_w_in=mla_w_in[0], mla_q_norm_g=mla_q_norm_g, mla_w_uq=mla_w_uq[0],
                   mla_kv_norm_g=mla_kv_norm_g, mla_w_ukv=mla_w_ukv[0], final_norm_g=final_norm_g)

    gathered = _allgather_packed(_pack_shards(weights, BF16))
    wf = _full_from_shards(_unpack_shards(gathered))
    w_conv_in = _conv_in_to_internal(wf["conv_w_in"])
    w_mla_in = _mla_in_to_internal(wf["mla_w_in"])
    w_uq = _uq_to_internal(wf["mla_w_uq"])
    w_ukv = wf["mla_w_ukv"]
    w_memkv = wf["w_mem_kv"]
    w_o = wf["w_out"]

    small_in = jnp.concatenate([_rows128(conv_dw[0], 93), _rows128(mla_q_norm_g, 1), _rows128(mla_kv_norm_g, 1),
                                jnp.zeros((1, 128), F32)], axis=0)
    small_all, _ = _gather_sum_small(small_in, name="gather_small_params")
    small_all = small_all[0::2]
    dw_full = small_all[:, :93].reshape(4, -1)[:, :CONV_K * 384].reshape(4, CONV_K, 384)
    dw_full = dw_full.transpose(1, 0, 2).reshape(CONV_K, MAIN)
    qg_full = small_all[:, 93].reshape(Q_RANK)
    kvg_full = small_all[:, 94, :64].reshape(KV_RANK)

    inv_freq = 1.0 / (ROPE_THETA ** (jnp.arange(0, ROPE, 2, dtype=F32) / ROPE))
    ang = positions.astype(F32).reshape(n, 1) * inv_freq
    cos, sin, zer = jnp.cos(ang), jnp.sin(ang), jnp.zeros((n, 64), F32)
    rope_c = jnp.concatenate([cos, cos, zer], axis=1)
    rope_s = jnp.concatenate([-sin, sin, zer], axis=1)

    x2 = x.reshape(n, d)
    mem2 = mem.reshape(bsz * mlen, d)
    tgt2 = loss_target.reshape(n, d)

    memn = [_rms_fwd(mem2, mem_norm_g[i], name=f"mem_norm{i}") for i in range(2)]
    kvm = [_matmul(memn[i], w_memkv[i], out_dtype=BF16, name=f"mem_kv{i}").reshape(bsz, mlen, 2 * MEMW)
           for i in range(2)]

    u0 = _rms_fwd(x2, norm_g[0], name="norm0")
    p0 = _matmul(u0, w_conv_in, name="conv_in_proj")
    hc = _conv_fwd(p0, dw_full, conv_dw_b, bsz, seq).reshape(n, MAIN)
    ymem0 = _mem_fwd(p0, kvm[0], P0_QM // MEMW, bsz, seq, name="mem_attn0").reshape(n, MEMW)
    y0 = _gate_fwd(hc, ymem0, p0, conv_ln_g, conv_ln_b, name="gate0")
    h1 = x2 + _matmul(y0, w_o[0], name="out_proj0")

    u1 = _rms_fwd(h1, norm_g[1], name="norm1")
    p1 = _matmul(u1, w_mla_in, name="mla_in_proj")
    cqn = _rms_fwd(p1, qg_full, width=Q_RANK, col_block=P1_CQ // Q_RANK, name="q_norm")
    ckvn = _rms_fwd(p1, kvg_full, width=KV_RANK, col_block=P1_CKV // KV_RANK, name="kv_norm")
    q = _matmul(cqn, w_uq, name="q_up")
    kv = _matmul(ckvn, w_ukv, name="kv_up")
    qb = _q_rope(q, rope_c, rope_s, inverse=False, name="q_rope")
    kb, vb = _kv_pack(kv, p1, rope_c, rope_s)
    o1, lse = _attn_fwd(qb, kb, vb, bsz, seq)
    o1 = o1.reshape(n, MAIN)
    ymem1 = _mem_fwd(p1, kvm[1], P1_QM // MEMW, bsz, seq, name="mem_attn1").reshape(n, MEMW)
    y1 = _gate_fwd(o1, ymem1, p1, name="gate1")
    h2 = h1 + _matmul(y1, w_o[1], name="out_proj1")

    dh2, d_final_g, loss_part = _final_loss(h2, final_norm_g, tgt2)
    loss = lax.psum(loss_part[0, 0], ("x", "y", "c"))

    g_w_out1 = _matmul(y1, dh2, ta=True, name="d_w_out1")
    dy1 = _matmul(dh2, w_o[1], tb=True, name="d_y1")
    do1, dymem1, dz1 = _gate_bwd(dy1, o1, ymem1, p1, name="gate1_bwd")
    dqm1, dkvm1 = _mem_bwd(p1, kvm[1], dymem1, P1_QM // MEMW, bsz, seq, name="mem_attn1_bwd")
    dq, dk, dv = _attn_bwd(qb, kb, vb, o1.reshape(bsz, seq, MAIN), do1, lse, bsz, seq)
    dqb = _q_rope(dq, rope_c, rope_s, inverse=True, name="q_rope_bwd")
    dkv, dkr = _kv_unpack(dk, dv, rope_c, rope_s)
    g_w_uq = _matmul(cqn, dqb, ta=True, name="d_w_uq")
    dcqn = _matmul(dqb, w_uq, tb=True, name="d_cqn")
    g_w_ukv = _matmul(ckvn, dkv, ta=True, name="d_w_ukv")
    dckvn = _matmul(dkv, w_ukv, tb=True, name="d_ckvn")
    dcq, g_qg = _rms_bwd(p1, qg_full, dcqn, width=Q_RANK, col_block=P1_CQ // Q_RANK, out_dtype=BF16,
                         name="q_norm_bwd")
    dckv, g_kvg = _rms_bwd(p1, kvg_full, dckvn, width=KV_RANK, col_block=P1_CKV // KV_RANK, out_dtype=BF16,
                           name="kv_norm_bwd")
    dp1 = jnp.concatenate([dz1, dcq, dqm1.reshape(n, MEMW), dckv, dkr], axis=1)
    g_w_mla_in = _matmul(u1, dp1, ta=True, name="d_w_mla_in")
    du1 = _matmul(dp1, w_mla_in, tb=True, name="d_u1")
    dh1, g_norm1 = _rms_bwd(h1, norm_g[1], du1, dh2, name="norm1_bwd")

    g_w_out0 = _matmul(y0, dh1, ta=True, name="d_w_out0")
    dy0 = _matmul(dh1, w_o[0], tb=True, name="d_y0")
    dhc, dymem0, dz0, g_ln_g, g_ln_b = _gate_bwd(dy0, hc, ymem0, p0, conv_ln_g, conv_ln_b, name="gate0_bwd")
    dqm0, dkvm0 = _mem_bwd(p0, kvm[0], dymem0, P0_QM // MEMW, bsz, seq, name="mem_attn0_bwd")
    da, dg, g_dw32 = _conv_bwd(p0, dhc.reshape(bsz, seq, MAIN), dw_full, bsz, seq)
    dp0 = jnp.concatenate([dz0, da.reshape(n, MAIN), dg.reshape(n, MAIN), dqm0.reshape(n, MEMW)], axis=1)
    g_w_conv_in = _matmul(u0, dp0, ta=True, name="d_w_conv_in")
    du0 = _matmul(dp0, w_conv_in, tb=True, name="d_u0")
    grad_x, g_norm0 = _rms_bwd(x2, norm_g[0], du0, dh1, name="norm0_bwd")

    g_w_memkv, g_mem_g = [], []
    for i, dkvm in enumerate((dkvm0, dkvm1)):
        dk2 = dkvm.reshape(bsz * mlen, 2 * MEMW)
        g_w_memkv.append(_matmul(memn[i], dk2, ta=True, name=f"d_w_mem_kv{i}"))
        dmemn = _matmul(dk2, w_memkv[i], tb=True, name=f"d_memn{i}")
        g_mem_g.append(_rms_bwd(mem2, mem_norm_g[i], dmemn, name=f"mem_norm{i}_bwd")[1])

    full_g = dict(w_mem_kv=jnp.stack(g_w_memkv), w_out=jnp.stack([g_w_out0, g_w_out1]),
                  conv_w_in=_conv_in_from_internal(g_w_conv_in), mla_w_in=_mla_in_from_internal(g_w_mla_in),
                  mla_w_uq=_uq_from_internal(g_w_uq), mla_w_ukv=g_w_ukv)
    gsh = _shards_from_full(full_g)
    gpk = jnp.concatenate([gsh[nm].reshape(4, -1, 1024) for nm in PACK_ORDER]
                          + [jnp.zeros((4, PACK_ROWS - 4272, 1024), F32)], axis=1)
    hr = PACK_ROWS // 2
    to_sibling = lax.dynamic_slice_in_dim(gpk, (1 - ac) * hr, hr, axis=1).astype(BF16)
    from_sibling = _swap_with_sibling(to_sibling)
    core = jnp.reshape(ac, (1,)).astype(jnp.int32)
    pair = _add_pairs(core, gpk, from_sibling)
    exch = _chip_exchange(pair)
    chip_core = jnp.stack([chip, ac]).astype(jnp.int32)
    red_half = _add_final(chip_core, gpk, from_sibling, exch)
    red = _unpack_shards(_share_halves(red_half))

    small_g = jnp.concatenate([
        _rows128(jnp.concatenate([g_norm0, g_norm1], axis=0), 16), _rows128(jnp.concatenate(g_mem_g, axis=0), 16),
        _rows128(g_dw32[CONV_K], 12), _rows128(g_ln_g, 12), _rows128(g_ln_b, 12), _rows128(d_final_g, 8),
        _rows128(g_dw32[:CONV_K], 372), _rows128(g_qg, 4), _rows128(g_kvg, 2), jnp.zeros((2, 128), F32)], axis=0)
    _, small_sum = _gather_sum_small(small_g, name="allreduce_small_grads")
    flat = small_sum.reshape(-1)

    def take(off, shape):
        size = 1
        for s_ in shape:
            size *= s_
        return flat[off * 128:off * 128 + size].reshape(shape)

    grads = dict(red)
    grads["conv_w_in"] = red["conv_w_in"][None]
    grads["mla_w_in"] = red["mla_w_in"][None]
    grads["mla_w_uq"] = red["mla_w_uq"][None]
    grads["mla_w_ukv"] = red["mla_w_ukv"][None]
    grads["norm_g"] = take(0, (2, D_MODEL))
    grads["mem_norm_g"] = take(16, (2, D_MODEL))
    grads["conv_dw_b"] = take(32, (1, MAIN))
    grads["conv_ln_g"] = take(44, (1, MAIN))
    grads["conv_ln_b"] = take(56, (1, MAIN))
    grads["final_norm_g"] = take(68, (D_MODEL,))
    grads["conv_dw"] = lax.dynamic_slice_in_dim(take(76, (CONV_K, MAIN)), chip * 384, 384, axis=1)[None]
    grads["mla_q_norm_g"] = lax.dynamic_slice_in_dim(take(448, (Q_RANK,)), chip * 128, 128, axis=0)[None]
    grads["mla_kv_norm_g"] = lax.dynamic_slice_in_dim(take(452, (KV_RANK,)), chip * 64, 64, axis=0)[None]

    params = dict(norm_g=norm_g, mem_norm_g=mem_norm_g, w_mem_kv=w_mem_kv, w_out=w_out, conv_w_in=conv_w_in,
                  conv_dw=conv_dw, conv_dw_b=conv_dw_b, conv_ln_g=conv_ln_g, conv_ln_b=conv_ln_b, mla_w_in=mla_w_in,
                  mla_q_norm_g=mla_q_norm_g, mla_w_uq=mla_w_uq, mla_kv_norm_g=mla_kv_norm_g, mla_w_ukv=mla_w_ukv,
                  final_norm_g=final_norm_g)
    mom1 = dict(norm_g=m_norm_g, mem_norm_g=m_mem_norm_g, w_mem_kv=m_w_mem_kv, w_out=m_w_out, conv_w_in=m_conv_w_in,
                conv_dw=m_conv_dw, conv_dw_b=m_conv_dw_b, conv_ln_g=m_conv_ln_g, conv_ln_b=m_conv_ln_b,
                mla_w_in=m_mla_w_in, mla_q_norm_g=m_mla_q_norm_g, mla_w_uq=m_mla_w_uq,
                mla_kv_norm_g=m_mla_kv_norm_g, mla_w_ukv=m_mla_w_ukv, final_norm_g=m_final_norm_g)
    mom2 = dict(norm_g=v_norm_g, mem_norm_g=v_mem_norm_g, w_mem_kv=v_w_mem_kv, w_out=v_w_out, conv_w_in=v_conv_w_in,
                conv_dw=v_conv_dw, conv_dw_b=v_conv_dw_b, conv_ln_g=v_conv_ln_g, conv_ln_b=v_conv_ln_b,
                mla_w_in=v_mla_w_in, mla_q_norm_g=v_mla_q_norm_g, mla_w_uq=v_mla_w_uq,
                mla_kv_norm_g=v_mla_kv_norm_g, mla_w_ukv=v_mla_w_ukv, final_norm_g=v_final_norm_g)
    names = list(params)
    g_out, deltas, new_m, new_v = [], [], [], []
    for nm in names:
        w = params[nm]
        g = grads[nm].reshape(w.shape)
        w2 = w.reshape(1, -1) if w.ndim == 1 else w
        dlt, m_new, v_new = _adamw(w2, g.reshape(w2.shape), mom1[nm].reshape(w2.shape), mom2[nm].reshape(w2.shape),
                                   name=f"adamw_{nm}")
        g_out.append(g)
        deltas.append(dlt.reshape(w.shape))
        new_m.append(m_new.reshape(w.shape))
        new_v.append(v_new.reshape(w.shape))

    return (loss, grad_x.reshape(bsz, seq, d), *g_out, *deltas, *new_m, *new_v)
```

```python
import functools

import jax
import jax.numpy as jnp
from jax import lax
from jax.experimental import pallas as pl
from jax.experimental.pallas import tpu as pltpu

F32 = jnp.float32
BF16 = jnp.bfloat16
MESH = pl.DeviceIdType.MESH

D_MODEL = 1024
MIX = 2048
MAIN = 1536
MEMW = 512
MEM_HEADS = 4
HEAD = 128
CONV_K = 31
CONV_PAD = 32
MLA_HEADS = 12
ROPE = 64
QK_PAD = 256
Q_RANK = 512
KV_RANK = 256
ROPE_THETA = 10000.0
RMS_EPS = 1e-6
LN_EPS = 1e-5
MEM_SCALE = HEAD ** -0.5
MLA_SCALE = (HEAD + ROPE) ** -0.5
NEG = -1e30

P0_COLS = 5632
P0_A, P0_G, P0_QM = 2048, 3584, 5120
P1_COLS = 3456
P1_CQ, P1_QM, P1_CKV, P1_KR = 2048, 2560, 3072, 3328

ADAM_LR = 0.001
ADAM_B1 = 0.9
ADAM_B2 = 0.999
ADAM_EPS = 1e-08
ADAM_WD = 0.01
ADAM_STEP = 10

VMEM_LIMIT = 56 * 1024 * 1024

PACK_ORDER = ("w_mem_kv", "w_out", "conv_w_in", "mla_w_in", "mla_w_uq", "mla_w_ukv")
SHARD_SHAPE = {"w_mem_kv": (2, 256, 1024), "w_out": (2, 512, 1024), "conv_w_in": (1024, 1408),
               "mla_w_in": (1024, 848), "mla_w_uq": (512, 576), "mla_w_ukv": (256, 768)}
PACK_ROWS = 4352
PACK_TILE = 128


def _cparams(sem=None):
    return pltpu.CompilerParams(dimension_semantics=sem, vmem_limit_bytes=VMEM_LIMIT)


def _tile(n, pref):
    if n <= pref:
        return n
    t = (pref // 128) * 128
    while t > 128 and n % t:
        t -= 128
    assert n % t == 0, (n, pref)
    return t


def _mm_kernel(*refs, nk, ta, has_res):
    if has_res:
        a_ref, b_ref, r_ref, o_ref, acc_ref = refs
    else:
        a_ref, b_ref, o_ref, acc_ref = refs
    k = pl.program_id(2)
    a = a_ref[...].astype(BF16)
    b = b_ref[...].astype(BF16)
    dn = (((0 if ta else 1,), (0,)), ((), ()))
    p = lax.dot_general(a, b, dn, preferred_element_type=F32)

    @pl.when(k == 0)
    def _():
        acc_ref[...] = p

    @pl.when(k > 0)
    def _():
        acc_ref[...] += p

    @pl.when(k == nk - 1)
    def _():
        acc = acc_ref[...]
        if has_res:
            acc = r_ref[...] + acc
        o_ref[...] = acc.astype(o_ref.dtype)


def _matmul(a, b, res=None, *, name, ta=False, out_dtype=F32, tm=1024, tn=512, tk=1024):
    m, kd = (a.shape[1], a.shape[0]) if ta else a.shape
    n = b.shape[1]
    assert kd == b.shape[0]
    tm, tn, tk = _tile(m, tm), _tile(n, tn), _tile(kd, tk)
    nk = kd // tk
    a_spec = (pl.BlockSpec((tk, tm), lambda i, j, k: (k, i)) if ta
              else pl.BlockSpec((tm, tk), lambda i, j, k: (i, k)))
    out_spec = pl.BlockSpec((tm, tn), lambda i, j, k: (i, j))
    in_specs = [a_spec, pl.BlockSpec((tk, tn), lambda i, j, k: (k, j))]
    args = [a, b]
    if res is not None:
        in_specs.append(out_spec)
        args.append(res)
    return pl.pallas_call(
        functools.partial(_mm_kernel, nk=nk, ta=ta, has_res=res is not None),
        name=name,
        grid=(m // tm, n // tn, nk),
        in_specs=in_specs,
        out_specs=out_spec,
        out_shape=jax.ShapeDtypeStruct((m, n), out_dtype),
        scratch_shapes=[pltpu.VMEM((tm, tn), F32)],
        compiler_params=_cparams(("parallel", "parallel", "arbitrary")),
    )(*args)


def _rms_fwd_kernel(h_ref, g_ref, o_ref):
    h = h_ref[...]
    rstd = lax.rsqrt(jnp.mean(h * h, axis=-1, keepdims=True) + RMS_EPS)
    o_ref[...] = (h * rstd * g_ref[...]).astype(o_ref.dtype)


def _rms_fwd(h, g, *, name, width=None, col_block=0, tm=512):
    t = h.shape[0]
    width = width or h.shape[1]
    tm = _tile(t, tm)
    return pl.pallas_call(
        functools.partial(_rms_fwd_kernel),
        name=name,
        grid=(t // tm,),
        in_specs=[pl.BlockSpec((tm, width), lambda i: (i, col_block)),
                  pl.BlockSpec((1, width), lambda i: (0, 0))],
        out_specs=pl.BlockSpec((tm, width), lambda i: (i, 0)),
        out_shape=jax.ShapeDtypeStruct((t, width), BF16),
        compiler_params=_cparams(("parallel",)),
    )(h, g.reshape(1, width))


def _rms_bwd_math(h, g, du):
    rstd = lax.rsqrt(jnp.mean(h * h, axis=-1, keepdims=True) + RMS_EPS)
    dug = du * g
    dh = rstd * dug - h * (rstd * rstd * rstd) * jnp.mean(dug * h, axis=-1, keepdims=True)
    dg = jnp.sum(du * h * rstd, axis=0, keepdims=True)
    return dh, dg


def _rms_bwd_kernel(*refs, has_res):
    if has_res:
        h_ref, g_ref, du_ref, res_ref, dh_ref, dg_ref = refs
    else:
        h_ref, g_ref, du_ref, dh_ref, dg_ref = refs
    dh, dg = _rms_bwd_math(h_ref[...], g_ref[...], du_ref[...].astype(F32))
    if has_res:
        dh = dh + res_ref[...]
    dh_ref[...] = dh.astype(dh_ref.dtype)

    @pl.when(pl.program_id(0) == 0)
    def _():
        dg_ref[...] = dg

    @pl.when(pl.program_id(0) > 0)
    def _():
        dg_ref[...] += dg


def _rms_bwd(h, g, du, res=None, *, name, width=None, col_block=0, out_dtype=F32, tm=512):
    t = h.shape[0]
    width = width or h.shape[1]
    tm = _tile(t, tm)
    row = pl.BlockSpec((tm, width), lambda i: (i, 0))
    in_specs = [pl.BlockSpec((tm, width), lambda i: (i, col_block)),
                pl.BlockSpec((1, width), lambda i: (0, 0)), row]
    args = [h, g.reshape(1, width), du]
    if res is not None:
        in_specs.append(row)
        args.append(res)
    return pl.pallas_call(
        functools.partial(_rms_bwd_kernel, has_res=res is not None),
        name=name,
        grid=(t // tm,),
        in_specs=in_specs,
        out_specs=[row, pl.BlockSpec((1, width), lambda i: (0, 0))],
        out_shape=[jax.ShapeDtypeStruct((t, width), out_dtype), jax.ShapeDtypeStruct((1, width), F32)],
        compiler_params=_cparams(("arbitrary",)),
    )(*args)


def _final_kernel(h_ref, g_ref, t_ref, dh_ref, dg_ref, loss_ref):
    h = h_ref[...]
    g = g_ref[...]
    rstd = lax.rsqrt(jnp.mean(h * h, axis=-1, keepdims=True) + RMS_EPS)
    e = h * rstd * g - t_ref[...]
    part = 0.5 * jnp.sum(jnp.mean(e * e, axis=-1, keepdims=True), axis=0, keepdims=True)
    dh, dg = _rms_bwd_math(h, g, e * (1.0 / D_MODEL))
    dh_ref[...] = dh
    part = jnp.broadcast_to(part, loss_ref.shape)

    @pl.when(pl.program_id(0) == 0)
    def _():
        dg_ref[...] = dg
        loss_ref[...] = part

    @pl.when(pl.program_id(0) > 0)
    def _():
        dg_ref[...] += dg
        loss_ref[...] += part


def _final_loss(h, g, target, *, tm=512):
    t, d = h.shape
    tm = _tile(t, tm)
    row = pl.BlockSpec((tm, d), lambda i: (i, 0))
    return pl.pallas_call(
        functools.partial(_final_kernel),
        name="final_loss",
        grid=(t // tm,),
        in_specs=[row, pl.BlockSpec((1, d), lambda i: (0, 0)), row],
        out_specs=[row, pl.BlockSpec((1, d), lambda i: (0, 0)), pl.BlockSpec((1, 128), lambda i: (0, 0))],
        out_shape=[jax.ShapeDtypeStruct((t, d), F32), jax.ShapeDtypeStruct((1, d), F32),
                   jax.ShapeDtypeStruct((1, 128), F32)],
        compiler_params=_cparams(("arbitrary",)),
    )(h, g.reshape(1, d), target)


CONV_CT = 128
CONV_TC = 256


def _glu_into(pad_ref, a_ref, g_ref, seq, tc):
    ct = pad_ref.shape[1]
    pad_ref[0:CONV_PAD, :] = jnp.zeros((CONV_PAD, ct), F32)
    for r in range(0, seq, tc):
        a = a_ref[0, r:r + tc, :]
        g = g_ref[0, r:r + tc, :]
        pad_ref[CONV_PAD + r:CONV_PAD + r + tc, :] = a * jax.nn.sigmoid(g)


def _conv_fwd_kernel(a_ref, g_ref, dw_ref, dwb_ref, hc_ref, pad_ref, *, seq, tc):
    ct = pad_ref.shape[1]
    _glu_into(pad_ref, a_ref, g_ref, seq, tc)
    for r in range(0, seq, tc):
        acc = jnp.broadcast_to(dwb_ref[...], (tc, ct))
        for k in range(CONV_K):
            o = CONV_PAD + r - (CONV_K - 1) + k
            acc = acc + dw_ref[k:k + 1, :] * pad_ref[o:o + tc, :]
        hc_ref[0, r:r + tc, :] = acc


def _conv_fwd(p0, dw, dwb, bsz, seq):
    ct = CONV_CT
    tc = min(CONV_TC, seq)
    p3 = p0.reshape(bsz, seq, P0_COLS)
    return pl.pallas_call(
        functools.partial(_conv_fwd_kernel, seq=seq, tc=tc),
        name="conv_fwd",
        grid=(MAIN // ct, bsz),
        in_specs=[pl.BlockSpec((1, seq, ct), lambda j, b: (b, 0, P0_A // ct + j)),
                  pl.BlockSpec((1, seq, ct), lambda j, b: (b, 0, P0_G // ct + j)),
                  pl.BlockSpec((CONV_K, ct), lambda j, b: (0, j)),
                  pl.BlockSpec((1, ct), lambda j, b: (0, j))],
        out_specs=pl.BlockSpec((1, seq, ct), lambda j, b: (b, 0, j)),
        out_shape=jax.ShapeDtypeStruct((bsz, seq, MAIN), F32),
        scratch_shapes=[pltpu.VMEM((seq + CONV_PAD, ct), F32)],
        compiler_params=_cparams(("parallel", "parallel")),
    )(p3, p3, dw, dwb)


def _conv_bwd_kernel(a_ref, g_ref, dhc_ref, dw_ref, da_ref, dg_ref, ddw_ref, pad_ref, dpad_ref, acc_ref,
                     *, seq, tc):
    ct = pad_ref.shape[1]
    b = pl.program_id(1)
    _glu_into(pad_ref, a_ref, g_ref, seq, tc)
    dpad_ref[seq:seq + CONV_PAD, :] = jnp.zeros((CONV_PAD, ct), F32)
    for r in range(0, seq, tc):
        dpad_ref[r:r + tc, :] = dhc_ref[0, r:r + tc, :]
    acc_ref[...] = jnp.zeros(acc_ref.shape, F32)
    for r in range(0, seq, tc):
        dh = dhc_ref[0, r:r + tc, :]
        dglu = jnp.zeros((tc, ct), F32)
        for k in range(CONV_K):
            o = r + (CONV_K - 1) - k
            dglu = dglu + dw_ref[k:k + 1, :] * dpad_ref[o:o + tc, :]
            o = CONV_PAD + r - (CONV_K - 1) + k
            prod = pad_ref[o:o + tc, :] * dh
            acc_ref[k] += jnp.sum(prod.reshape(tc // 8, 8, ct), axis=0)
        acc_ref[CONV_K] += jnp.sum(dh.reshape(tc // 8, 8, ct), axis=0)
        a = a_ref[0, r:r + tc, :]
        sg = jax.nn.sigmoid(g_ref[0, r:r + tc, :])
        da_ref[0, r:r + tc, :] = (dglu * sg).astype(da_ref.dtype)
        dg_ref[0, r:r + tc, :] = (dglu * a * sg * (1.0 - sg)).astype(dg_ref.dtype)
    tot = jnp.sum(acc_ref[...], axis=1)

    @pl.when(b == 0)
    def _():
        ddw_ref[...] = tot

    @pl.when(b > 0)
    def _():
        ddw_ref[...] += tot


def _conv_bwd(p0, dhc, dw, bsz, seq):
    ct = CONV_CT
    tc = min(CONV_TC, seq)
    p3 = p0.reshape(bsz, seq, P0_COLS)
    blk = pl.BlockSpec((1, seq, ct), lambda j, b: (b, 0, j))
    return pl.pallas_call(
        functools.partial(_conv_bwd_kernel, seq=seq, tc=tc),
        name="conv_bwd",
        grid=(MAIN // ct, bsz),
        in_specs=[pl.BlockSpec((1, seq, ct), lambda j, b: (b, 0, P0_A // ct + j)),
                  pl.BlockSpec((1, seq, ct), lambda j, b: (b, 0, P0_G // ct + j)),
                  blk,
                  pl.BlockSpec((CONV_K, ct), lambda j, b: (0, j))],
        out_specs=[blk, blk, pl.BlockSpec((CONV_K + 1, ct), lambda j, b: (0, j))],
        out_shape=[jax.ShapeDtypeStruct((bsz, seq, MAIN), BF16), jax.ShapeDtypeStruct((bsz, seq, MAIN), BF16),
                   jax.ShapeDtypeStruct((CONV_K + 1, MAIN), F32)],
        scratch_shapes=[pltpu.VMEM((seq + CONV_PAD, ct), F32), pltpu.VMEM((seq + CONV_PAD, ct), F32),
                        pltpu.VMEM((CONV_K + 1, 8, ct), F32)],
        compiler_params=_cparams(("parallel", "arbitrary")),
    )(p3, p3, dhc, dw)


def _ln_parts(x, lng, lnb):
    mu = jnp.mean(x, axis=-1, keepdims=True)
    xc = x - mu
    rstd = lax.rsqrt(jnp.mean(xc * xc, axis=-1, keepdims=True) + LN_EPS)
    xh = xc * rstd
    hl = xh * lng + lnb
    return rstd, xh, hl


def _gate_fwd_kernel(*refs, ln):
    if ln:
        main_ref, ymem_ref, z_ref, lng_ref, lnb_ref, y_ref = refs
    else:
        main_ref, ymem_ref, z_ref, y_ref = refs
    x = main_ref[...]
    if ln:
        _, _, hl = _ln_parts(x, lng_ref[...], lnb_ref[...])
        x = hl * jax.nn.sigmoid(hl)
    z = z_ref[...]
    sz = z * jax.nn.sigmoid(z)
    y_ref[:, :MAIN] = (x * sz[:, :MAIN]).astype(y_ref.dtype)
    y_ref[:, MAIN:] = (ymem_ref[...] * sz[:, MAIN:]).astype(y_ref.dtype)


def _gate_fwd(main, ymem, p, ln_g=None, ln_b=None, *, name, tm=256):
    t = main.shape[0]
    tm = _tile(t, tm)
    ln = ln_g is not None
    in_specs = [pl.BlockSpec((tm, MAIN), lambda i: (i, 0)), pl.BlockSpec((tm, MEMW), lambda i: (i, 0)),
                pl.BlockSpec((tm, MIX), lambda i: (i, 0))]
    args = [main, ymem, p]
    if ln:
        in_specs += [pl.BlockSpec((1, MAIN), lambda i: (0, 0))] * 2
        args += [ln_g.reshape(1, MAIN), ln_b.reshape(1, MAIN)]
    return pl.pallas_call(
        functools.partial(_gate_fwd_kernel, ln=ln),
        name=name,
        grid=(t // tm,),
        in_specs=in_specs,
        out_specs=pl.BlockSpec((tm, MIX), lambda i: (i, 0)),
        out_shape=jax.ShapeDtypeStruct((t, MIX), BF16),
        compiler_params=_cparams(("parallel",)),
    )(*args)


def _gate_bwd_kernel(*refs, ln):
    if ln:
        (dy_ref, main_ref, ymem_ref, z_ref, lng_ref, lnb_ref,
         dmain_ref, dymem_ref, dz_ref, dlng_ref, dlnb_ref) = refs
    else:
        dy_ref, main_ref, ymem_ref, z_ref, dmain_ref, dymem_ref, dz_ref = refs
    dy = dy_ref[...]
    z = z_ref[...]
    sg = jax.nn.sigmoid(z)
    sz = z * sg
    dsz = sg * (1.0 + z * (1.0 - sg))
    x = main_ref[...]
    if ln:
        lng = lng_ref[...]
        rstd, xh, hl = _ln_parts(x, lng, lnb_ref[...])
        sh = jax.nn.sigmoid(hl)
        ymain = hl * sh
    else:
        ymain = x
    dym = dy[:, :MAIN] * sz[:, :MAIN]
    dymem_ref[...] = dy[:, MAIN:] * sz[:, MAIN:]
    dz_ref[:, :MAIN] = (dy[:, :MAIN] * ymain * dsz[:, :MAIN]).astype(dz_ref.dtype)
    dz_ref[:, MAIN:] = (dy[:, MAIN:] * ymem_ref[...] * dsz[:, MAIN:]).astype(dz_ref.dtype)
    if not ln:
        dmain_ref[...] = dym
        return
    dhl = dym * (sh * (1.0 + hl * (1.0 - sh)))
    dxh = dhl * lng
    dmain_ref[...] = rstd * (dxh - jnp.mean(dxh, axis=-1, keepdims=True)
                             - xh * jnp.mean(dxh * xh, axis=-1, keepdims=True))
    dlng = jnp.sum(dhl * xh, axis=0, keepdims=True)
    dlnb = jnp.sum(dhl, axis=0, keepdims=True)

    @pl.when(pl.program_id(0) == 0)
    def _():
        dlng_ref[...] = dlng
        dlnb_ref[...] = dlnb

    @pl.when(pl.program_id(0) > 0)
    def _():
        dlng_ref[...] += dlng
        dlnb_ref[...] += dlnb


def _gate_bwd(dy, main, ymem, p, ln_g=None, ln_b=None, *, name, tm=256):
    t = main.shape[0]
    tm = _tile(t, tm)
    ln = ln_g is not None
    r_main = pl.BlockSpec((tm, MAIN), lambda i: (i, 0))
    r_mem = pl.BlockSpec((tm, MEMW), lambda i: (i, 0))
    r_mix = pl.BlockSpec((tm, MIX), lambda i: (i, 0))
    vec = pl.BlockSpec((1, MAIN), lambda i: (0, 0))
    in_specs = [r_mix, r_main, r_mem, r_mix]
    args = [dy, main, ymem, p]
    out_specs = [r_main, r_mem, r_mix]
    out_shape = [jax.ShapeDtypeStruct((t, MAIN), F32), jax.ShapeDtypeStruct((t, MEMW), F32),
                 jax.ShapeDtypeStruct((t, MIX), BF16)]
    if ln:
        in_specs += [vec, vec]
        args += [ln_g.reshape(1, MAIN), ln_b.reshape(1, MAIN)]
        out_specs += [vec, vec]
        out_shape += [jax.ShapeDtypeStruct((1, MAIN), F32)] * 2
    return pl.pallas_call(
        functools.partial(_gate_bwd_kernel, ln=ln),
        name=name,
        grid=(t // tm,),
        in_specs=in_specs,
        out_specs=out_specs,
        out_shape=out_shape,
        compiler_params=_cparams(("arbitrary",)),
    )(*args)


def _dot_nt(a, b):
    return lax.dot_general(a, b, (((1,), (1,)), ((), ())), preferred_element_type=F32)


def _dot_tn(a, b):
    return lax.dot_general(a, b, (((0,), (0,)), ((), ())), preferred_element_type=F32)


def _dot(a, b):
    return jnp.dot(a, b, preferred_element_type=F32)


def _mem_probs(q, k):
    s = _dot_nt(q, k) * MEM_SCALE
    p = jnp.exp(s - jnp.max(s, axis=-1, keepdims=True))
    return p / jnp.sum(p, axis=-1, keepdims=True)


def _mem_fwd_kernel(q_ref, kv_ref, o_ref):
    for h in range(MEM_HEADS):
        c = slice(h * HEAD, (h + 1) * HEAD)
        cv = slice(MEMW + h * HEAD, MEMW + (h + 1) * HEAD)
        p = _mem_probs(q_ref[0, :, c].astype(BF16), kv_ref[0, :, c])
        o_ref[0, :, c] = _dot(p.astype(BF16), kv_ref[0, :, cv])


def _mem_fwd(p, kvm, col_block, bsz, seq, *, name, tq=512):
    tq = _tile(seq, tq)
    p3 = p.reshape(bsz, seq, p.shape[1])
    mlen = kvm.shape[1]
    return pl.pallas_call(
        functools.partial(_mem_fwd_kernel),
        name=name,
        grid=(bsz, seq // tq),
        in_specs=[pl.BlockSpec((1, tq, MEMW), lambda b, i: (b, i, col_block)),
                  pl.BlockSpec((1, mlen, 2 * MEMW), lambda b, i: (b, 0, 0))],
        out_specs=pl.BlockSpec((1, tq, MEMW), lambda b, i: (b, i, 0)),
        out_shape=jax.ShapeDtypeStruct((bsz, seq, MEMW), F32),
        compiler_params=_cparams(("parallel", "parallel")),
    )(p3, kvm)


def _mem_bwd_kernel(q_ref, kv_ref, do_ref, dq_ref, dkv_ref):
    @pl.when(pl.program_id(1) == 0)
    def _():
        dkv_ref[...] = jnp.zeros(dkv_ref.shape, F32)

    for h in range(MEM_HEADS):
        c = slice(h * HEAD, (h + 1) * HEAD)
        cv = slice(MEMW + h * HEAD, MEMW + (h + 1) * HEAD)
        q = q_ref[0, :, c].astype(BF16)
        k = kv_ref[0, :, c]
        v = kv_ref[0, :, cv]
        do = do_ref[0, :, c].astype(BF16)
        p = _mem_probs(q, k)
        dp = _dot_nt(do, v)
        ds = (p * (dp - jnp.sum(p * dp, axis=-1, keepdims=True)) * MEM_SCALE).astype(BF16)
        dq_ref[0, :, c] = _dot(ds, k).astype(dq_ref.dtype)
        dkv_ref[0, :, c] += _dot_tn(ds, q)
        dkv_ref[0, :, cv] += _dot_tn(p.astype(BF16), do)


def _mem_bwd(p, kvm, dymem, col_block, bsz, seq, *, name, tq=512):
    tq = _tile(seq, tq)
    p3 = p.reshape(bsz, seq, p.shape[1])
    mlen = kvm.shape[1]
    return pl.pallas_call(
        functools.partial(_mem_bwd_kernel),
        name=name,
        grid=(bsz, seq // tq),
        in_specs=[pl.BlockSpec((1, tq, MEMW), lambda b, i: (b, i, col_block)),
                  pl.BlockSpec((1, mlen, 2 * MEMW), lambda b, i: (b, 0, 0)),
                  pl.BlockSpec((1, tq, MEMW), lambda b, i: (b, i, 0))],
        out_specs=[pl.BlockSpec((1, tq, MEMW), lambda b, i: (b, i, 0)),
                   pl.BlockSpec((1, mlen, 2 * MEMW), lambda b, i: (b, 0, 0))],
        out_shape=[jax.ShapeDtypeStruct((bsz, seq, MEMW), BF16),
                   jax.ShapeDtypeStruct((bsz, mlen, 2 * MEMW), F32)],
        compiler_params=_cparams(("parallel", "arbitrary")),
    )(p3, kvm, dymem.reshape(bsz, seq, MEMW))


def _swap32(x):
    lane = lax.broadcasted_iota(jnp.int32, x.shape, 1)
    return jnp.where(lane < 32, pltpu.roll(x, 96, 1), pltpu.roll(x, 32, 1))


def _rope(x, cs, sn):
    return x * cs + _swap32(x) * sn


def _rope_t(d, cs, sn):
    return d * cs + _swap32(d * sn)


def _q_rope_kernel(q_ref, cs_ref, sn_ref, o_ref, *, inverse):
    cs = cs_ref[...]
    sn = sn_ref[...]
    for h in range(MLA_HEADS):
        c0 = slice(h * QK_PAD, h * QK_PAD + HEAD)
        c1 = slice(h * QK_PAD + HEAD, (h + 1) * QK_PAD)
        o_ref[:, c0] = q_ref[:, c0].astype(o_ref.dtype)
        x = q_ref[:, c1]
        o_ref[:, c1] = (_rope_t(x, cs, sn) if inverse else _rope(x, cs, sn)).astype(o_ref.dtype)


def _q_rope(q, cs, sn, *, inverse, name, tm=256):
    t, w = q.shape
    tm = _tile(t, tm)
    row = pl.BlockSpec((tm, w), lambda i: (i, 0))
    tab = pl.BlockSpec((tm, 128), lambda i: (i, 0))
    return pl.pallas_call(
        functools.partial(_q_rope_kernel, inverse=inverse),
        name=name,
        grid=(t // tm,),
        in_specs=[row, tab, tab],
        out_specs=row,
        out_shape=jax.ShapeDtypeStruct((t, w), BF16),
        compiler_params=_cparams(("parallel",)),
    )(q, cs, sn)


def _kv_pack_kernel(kv_ref, kr_ref, cs_ref, sn_ref, k_ref, v_ref):
    krr = _rope(kr_ref[...], cs_ref[...], sn_ref[...]).astype(k_ref.dtype)
    for h in range(MLA_HEADS):
        k_ref[:, h * QK_PAD:h * QK_PAD + HEAD] = kv_ref[:, h * 2 * HEAD:h * 2 * HEAD + HEAD].astype(k_ref.dtype)
        k_ref[:, h * QK_PAD + HEAD:(h + 1) * QK_PAD] = krr
        v_ref[:, h * HEAD:(h + 1) * HEAD] = kv_ref[:, h * 2 * HEAD + HEAD:(h + 1) * 2 * HEAD].astype(v_ref.dtype)


def _kv_pack(kv, p1, cs, sn, *, tm=256):
    t = kv.shape[0]
    tm = _tile(t, tm)
    tab = pl.BlockSpec((tm, 128), lambda i: (i, 0))
    return pl.pallas_call(
        functools.partial(_kv_pack_kernel),
        name="kv_pack",
        grid=(t // tm,),
        in_specs=[pl.BlockSpec((tm, MLA_HEADS * 2 * HEAD), lambda i: (i, 0)),
                  pl.BlockSpec((tm, 128), lambda i: (i, P1_KR // 128)), tab, tab],
        out_specs=[pl.BlockSpec((tm, MLA_HEADS * QK_PAD), lambda i: (i, 0)),
                   pl.BlockSpec((tm, MAIN), lambda i: (i, 0))],
        out_shape=[jax.ShapeDtypeStruct((t, MLA_HEADS * QK_PAD), BF16), jax.ShapeDtypeStruct((t, MAIN), BF16)],
        compiler_params=_cparams(("parallel",)),
    )(kv, p1, cs, sn)


def _kv_unpack_kernel(dk_ref, dv_ref, cs_ref, sn_ref, dkv_ref, dkr_ref):
    dkrr = jnp.zeros(dkr_ref.shape, F32)
    for h in range(MLA_HEADS):
        dkv_ref[:, h * 2 * HEAD:h * 2 * HEAD + HEAD] = dk_ref[:, h * QK_PAD:h * QK_PAD + HEAD].astype(dkv_ref.dtype)
        dkv_ref[:, h * 2 * HEAD + HEAD:(h + 1) * 2 * HEAD] = dv_ref[:, h * HEAD:(h + 1) * HEAD].astype(dkv_ref.dtype)
        dkrr = dkrr + dk_ref[:, h * QK_PAD + HEAD:(h + 1) * QK_PAD]
    dkr_ref[...] = _rope_t(dkrr, cs_ref[...], sn_ref[...]).astype(dkr_ref.dtype)


def _kv_unpack(dk, dv, cs, sn, *, tm=256):
    t = dk.shape[0]
    tm = _tile(t, tm)
    tab = pl.BlockSpec((tm, 128), lambda i: (i, 0))
    return pl.pallas_call(
        functools.partial(_kv_unpack_kernel),
        name="kv_unpack",
        grid=(t // tm,),
        in_specs=[pl.BlockSpec((tm, MLA_HEADS * QK_PAD), lambda i: (i, 0)),
                  pl.BlockSpec((tm, MAIN), lambda i: (i, 0)), tab, tab],
        out_specs=[pl.BlockSpec((tm, MLA_HEADS * 2 * HEAD), lambda i: (i, 0)), tab],
        out_shape=[jax.ShapeDtypeStruct((t, MLA_HEADS * 2 * HEAD), BF16), jax.ShapeDtypeStruct((t, 128), BF16)],
        compiler_params=_cparams(("parallel",)),
    )(dk, dv, cs, sn)


ATT_T = 256


def _causal(s, t):
    row = lax.broadcasted_iota(jnp.int32, (t, t), 0)
    col = lax.broadcasted_iota(jnp.int32, (t, t), 1)
    return jnp.where(col <= row, s, NEG)


def _attn_fwd_kernel(q_ref, k_ref, v_ref, o_ref, lse_ref, *, seq, t):
    for i in range(seq // t):
        own = slice(i * t, (i + 1) * t)
        q = q_ref[0, own, :]
        sd = _causal(_dot_nt(q, k_ref[0, own, :]) * MLA_SCALE, t)
        m = jnp.max(sd, axis=-1, keepdims=True)
        if i:
            so = _dot_nt(q, k_ref[0, :i * t, :]) * MLA_SCALE
            m = jnp.maximum(m, jnp.max(so, axis=-1, keepdims=True))
        pd = jnp.exp(sd - m)
        l = jnp.sum(pd, axis=-1, keepdims=True)
        acc = _dot(pd.astype(BF16), v_ref[0, own, :])
        if i:
            po = jnp.exp(so - m)
            l = l + jnp.sum(po, axis=-1, keepdims=True)
            acc = acc + _dot(po.astype(BF16), v_ref[0, :i * t, :])
        o_ref[0, own, :] = acc / l
        lse_ref[0, 0, own, :] = m + jnp.log(l)


def _attn_fwd(qb, kb, vb, bsz, seq):
    t = min(ATT_T, seq)
    q3 = qb.reshape(bsz, seq, MLA_HEADS * QK_PAD)
    k3 = kb.reshape(bsz, seq, MLA_HEADS * QK_PAD)
    v3 = vb.reshape(bsz, seq, MAIN)
    qk = pl.BlockSpec((1, seq, QK_PAD), lambda b, h: (b, 0, h))
    vv = pl.BlockSpec((1, seq, HEAD), lambda b, h: (b, 0, h))
    return pl.pallas_call(
        functools.partial(_attn_fwd_kernel, seq=seq, t=t),
        name="attn_fwd",
        grid=(bsz, MLA_HEADS),
        in_specs=[qk, qk, vv],
        out_specs=[vv, pl.BlockSpec((1, 1, seq, 1), lambda b, h: (b, h, 0, 0))],
        out_shape=[jax.ShapeDtypeStruct((bsz, seq, MAIN), F32),
                   jax.ShapeDtypeStruct((bsz, MLA_HEADS, seq, 1), F32)],
        compiler_params=_cparams(("parallel", "parallel")),
    )(q3, k3, v3)


def _attn_bwd_kernel(q_ref, k_ref, v_ref, o_ref, do_ref, lse_ref, dq_ref, dk_ref, dv_ref, delta_ref, *, seq, t):
    for r in range(0, seq, t):
        rows = slice(r, r + t)
        delta_ref[rows, :] = jnp.sum(do_ref[0, rows, :] * o_ref[0, rows, :], axis=-1, keepdims=True)
    dq_ref[...] = jnp.zeros(dq_ref.shape, F32)

    def piece(rows, k, v, masked):
        q = q_ref[0, rows, :]
        do = do_ref[0, rows, :].astype(BF16)
        s = _dot_nt(q, k) * MLA_SCALE
        if masked:
            s = _causal(s, t)
        p = jnp.exp(s - lse_ref[0, 0, rows, :])
        dp = _dot_nt(do, v)
        ds = (p * (dp - delta_ref[rows, :]) * MLA_SCALE).astype(BF16)
        dq_ref[0, rows, :] += _dot(ds, k)
        return _dot_tn(ds, q), _dot_tn(p.astype(BF16), do)

    for j in range(seq // t):
        own = slice(j * t, (j + 1) * t)
        k = k_ref[0, own, :]
        v = v_ref[0, own, :]
        dk, dv = piece(own, k, v, True)
        if (j + 1) * t < seq:
            dk2, dv2 = piece(slice((j + 1) * t, seq), k, v, False)
            dk, dv = dk + dk2, dv + dv2
        dk_ref[0, own, :] = dk
        dv_ref[0, own, :] = dv


def _attn_bwd(qb, kb, vb, o, do, lse, bsz, seq):
    t = min(ATT_T, seq)
    q3 = qb.reshape(bsz, seq, MLA_HEADS * QK_PAD)
    k3 = kb.reshape(bsz, seq, MLA_HEADS * QK_PAD)
    v3 = vb.reshape(bsz, seq, MAIN)
    qk = pl.BlockSpec((1, seq, QK_PAD), lambda b, h: (b, 0, h))
    vv = pl.BlockSpec((1, seq, HEAD), lambda b, h: (b, 0, h))
    dq, dk, dv = pl.pallas_call(
        functools.partial(_attn_bwd_kernel, seq=seq, t=t),
        name="attn_bwd",
        grid=(bsz, MLA_HEADS),
        in_specs=[qk, qk, vv, vv, vv, pl.BlockSpec((1, 1, seq, 1), lambda b, h: (b, h, 0, 0))],
        out_specs=[qk, qk, vv],
        out_shape=[jax.ShapeDtypeStruct((bsz, seq, MLA_HEADS * QK_PAD), F32),
                   jax.ShapeDtypeStruct((bsz, seq, MLA_HEADS * QK_PAD), F32),
                   jax.ShapeDtypeStruct((bsz, seq, MAIN), F32)],
        scratch_shapes=[pltpu.VMEM((seq, 1), F32)],
        compiler_params=_cparams(("parallel", "parallel")),
    )(q3, k3, v3, o, do.reshape(bsz, seq, MAIN), lse)
    n = bsz * seq
    return dq.reshape(n, -1), dk.reshape(n, -1), dv.reshape(n, -1)


def _adamw_kernel(w_ref, g_ref, m_ref, v_ref, d_ref, nm_ref, nv_ref):
    g = g_ref[...]
    m = ADAM_B1 * m_ref[...] + (1.0 - ADAM_B1) * g
    v = ADAM_B2 * v_ref[...] + (1.0 - ADAM_B2) * (g * g)
    m_hat = m / (1.0 - ADAM_B1 ** ADAM_STEP)
    v_hat = v / (1.0 - ADAM_B2 ** ADAM_STEP)
    d_ref[...] = -ADAM_LR * (m_hat / (jnp.sqrt(v_hat) + ADAM_EPS) + ADAM_WD * w_ref[...])
    nm_ref[...] = m
    nv_ref[...] = v


def _adamw(w, g, m, v, *, name):
    shape = w.shape
    c = shape[-1]
    r = w.size // c
    tr = r
    for cand in (512, 256, 128, 64, 32, 16, 8):
        if r % cand == 0 and cand * c * 4 <= 2 * 1024 * 1024:
            tr = cand
            break
    blk = pl.BlockSpec((tr, c), lambda i: (i, 0))
    outs = pl.pallas_call(
        functools.partial(_adamw_kernel),
        name=name,
        grid=(r // tr,),
        in_specs=[blk] * 4,
        out_specs=[blk] * 3,
        out_shape=[jax.ShapeDtypeStruct((r, c), F32)] * 3,
        compiler_params=_cparams(("parallel",)),
    )(w.reshape(r, c), g.reshape(r, c), m.reshape(r, c), v.reshape(r, c))
    return tuple(o.reshape(shape) for o in outs)


HBM = pl.BlockSpec(memory_space=pl.ANY)


def _place():
    return lax.axis_index("x"), lax.axis_index("y"), lax.axis_index("c")


def _other_chips(x, y):
    return [(1 - x, y), (x, 1 - y), (1 - x, 1 - y)]


def _allgather_kernel(src, out, ssem, rsem, *, hr):
    x, y, c = _place()
    s = 2 * x + y
    chips = _other_chips(x, y)

    def half(chip_idx, core):
        return out.at[chip_idx, pl.ds(core * hr, hr), :]

    def copy(k, src_ref, dst_ref, to):
        return pltpu.make_async_remote_copy(src_ref=src_ref, dst_ref=dst_ref, send_sem=ssem.at[k], recv_sem=rsem.at[k],
                                            device_id=to, device_id_type=MESH)

    first = [copy(j, src.at[pl.ds(c * hr, hr), :], half(s, c), (*chip, c)) for j, chip in enumerate(chips)]
    for cp in first:
        cp.start()
    passed = []
    for j, chip in enumerate(chips):
        blk = half(2 * chip[0] + chip[1], c)
        copy(j, blk, blk, (*chip, c)).wait_recv()
        cp = copy(3 + j, blk, blk, (x, y, 1 - c))
        cp.start()
        passed.append(cp)
    for j, chip in enumerate(chips):
        blk = half(2 * chip[0] + chip[1], 1 - c)
        copy(3 + j, blk, blk, (x, y, 1 - c)).wait_recv()
    for cp in first + passed:
        cp.wait_send()


def _allgather_packed(pk, chip):
    r, c = pk.shape
    others = pl.pallas_call(
        functools.partial(_allgather_kernel, hr=r // 2),
        name="allgather_weights",
        in_specs=[HBM],
        out_specs=HBM,
        out_shape=jax.ShapeDtypeStruct((4, r, c), pk.dtype),
        scratch_shapes=[pltpu.SemaphoreType.DMA((6,)), pltpu.SemaphoreType.DMA((6,))],
    )(pk)
    return lax.dynamic_update_slice(others, pk[None], (chip, 0, 0))


def _swap_kernel(src, dst, ssem, rsem):
    x, y, c = _place()
    cp = pltpu.make_async_remote_copy(src_ref=src, dst_ref=dst, send_sem=ssem, recv_sem=rsem,
                                      device_id=(x, y, 1 - c), device_id_type=MESH)
    cp.start()
    cp.wait()


def _swap_with_sibling(a, *, name):
    return pl.pallas_call(
        functools.partial(_swap_kernel),
        name=name,
        in_specs=[HBM],
        out_specs=HBM,
        out_shape=jax.ShapeDtypeStruct(a.shape, a.dtype),
        scratch_shapes=[pltpu.SemaphoreType.DMA, pltpu.SemaphoreType.DMA],
    )(a)


def _chip_exchange_kernel(src, dst, ssem, rsem):
    x, y, c = _place()
    chips = _other_chips(x, y)
    cps = [pltpu.make_async_remote_copy(src_ref=src.at[2 * chip[0] + chip[1]], dst_ref=dst.at[j],
                                        send_sem=ssem.at[j], recv_sem=rsem.at[j],
                                        device_id=(*chip, c), device_id_type=MESH)
           for j, chip in enumerate(chips)]
    for cp in cps:
        cp.start()
    for cp in cps:
        cp.wait()


def _chip_exchange(a):
    return pl.pallas_call(
        functools.partial(_chip_exchange_kernel),
        name="rs_chip_exchange",
        in_specs=[HBM],
        out_specs=HBM,
        out_shape=jax.ShapeDtypeStruct((3,) + a.shape[1:], a.dtype),
        scratch_shapes=[pltpu.SemaphoreType.DMA((3,)), pltpu.SemaphoreType.DMA((3,))],
    )(a)


def _gather_sum_kernel(src, gat, tot, ssem, rsem):
    x, y, c = _place()
    me = 4 * x + 2 * y + c
    gat[me] = src[...]
    flips = [(dx, dy, dc) for dx in (0, 1) for dy in (0, 1) for dc in (0, 1)][1:]
    cps = []
    for k, (dx, dy, dc) in enumerate(flips):
        peer = (1 - x if dx else x, 1 - y if dy else y, 1 - c if dc else c)
        cp = pltpu.make_async_remote_copy(src_ref=src, dst_ref=gat.at[me], send_sem=ssem.at[k], recv_sem=rsem.at[k],
                                          device_id=peer, device_id_type=MESH)
        cp.start()
        cps.append((cp, 4 * peer[0] + 2 * peer[1] + peer[2], peer))
    for k, (cp, idx, peer) in enumerate(cps):
        pltpu.make_async_remote_copy(src_ref=src, dst_ref=gat.at[idx], send_sem=ssem.at[k], recv_sem=rsem.at[k],
                                     device_id=peer, device_id_type=MESH).wait_recv()
    for cp, _, _ in cps:
        cp.wait_send()
    acc = gat[0]
    for d in range(1, 8):
        acc = acc + gat[d]
    tot[...] = acc


def _gather_sum_small(a, *, name):
    vm = pl.BlockSpec(memory_space=pltpu.VMEM)
    return pl.pallas_call(
        functools.partial(_gather_sum_kernel),
        name=name,
        in_specs=[vm],
        out_specs=[vm, vm],
        out_shape=[jax.ShapeDtypeStruct((8,) + a.shape, a.dtype), jax.ShapeDtypeStruct(a.shape, a.dtype)],
        scratch_shapes=[pltpu.SemaphoreType.DMA((7,)), pltpu.SemaphoreType.DMA((7,))],
    )(a)


def _add_pairs_kernel(c_ref, g_ref, r_ref, o_ref):
    o_ref[...] = (g_ref[...] + r_ref[...].astype(F32)).astype(o_ref.dtype)


def _add_pairs(core, g, recv):
    _, hr, c = recv.shape
    nt = hr // PACK_TILE
    return pl.pallas_call(
        functools.partial(_add_pairs_kernel),
        name="rs_add_pairs",
        grid_spec=pltpu.PrefetchScalarGridSpec(
            num_scalar_prefetch=1,
            grid=(4, nt),
            in_specs=[pl.BlockSpec((1, PACK_TILE, c), lambda s, i, cr: (s, cr[0] * nt + i, 0)),
                      pl.BlockSpec((1, PACK_TILE, c), lambda s, i, cr: (s, i, 0))],
            out_specs=pl.BlockSpec((1, PACK_TILE, c), lambda s, i, cr: (s, i, 0)),
        ),
        out_shape=jax.ShapeDtypeStruct(recv.shape, BF16),
        compiler_params=_cparams(("parallel", "parallel")),
    )(core, g, recv)


def _add_final_kernel(sc_ref, g_ref, r_ref, e_ref, o_ref):
    acc = g_ref[0] + r_ref[0].astype(F32)
    for j in range(3):
        acc = acc + e_ref[j].astype(F32)
    o_ref[...] = acc


def _add_final(chip_core, g, recv, exch):
    _, hr, c = recv.shape
    nt = hr // PACK_TILE
    return pl.pallas_call(
        functools.partial(_add_final_kernel),
        name="rs_add_final",
        grid_spec=pltpu.PrefetchScalarGridSpec(
            num_scalar_prefetch=1,
            grid=(nt,),
            in_specs=[pl.BlockSpec((1, PACK_TILE, c), lambda i, sc: (sc[0], sc[1] * nt + i, 0)),
                      pl.BlockSpec((1, PACK_TILE, c), lambda i, sc: (sc[0], i, 0)),
                      pl.BlockSpec((3, PACK_TILE, c), lambda i, sc: (0, i, 0))],
            out_specs=pl.BlockSpec((PACK_TILE, c), lambda i, sc: (i, 0)),
        ),
        out_shape=jax.ShapeDtypeStruct((hr, c), F32),
        compiler_params=_cparams(("parallel",)),
    )(chip_core, g, recv, exch)


def _pack_shards(sh, dtype):
    parts = [sh[n].astype(dtype).reshape(-1, 1024) for n in PACK_ORDER]
    used = sum(p.shape[0] for p in parts)
    parts.append(jnp.zeros((PACK_ROWS - used, 1024), dtype))
    return jnp.concatenate(parts, axis=0)


def _unpack_shards(buf):
    lead = buf.shape[:-2]
    out, o = {}, 0
    for n in PACK_ORDER:
        shp = SHARD_SHAPE[n]
        rows = (shp[0] * shp[1] * (shp[2] if len(shp) == 3 else 1)) // 1024
        out[n] = buf[..., o:o + rows, :].reshape(lead + shp)
        o += rows
    return out


def _full_from_shards(sh):
    full = {}
    for n in ("w_mem_kv", "w_out"):
        a = sh[n]
        full[n] = a.transpose(1, 0, 2, 3).reshape(2, 4 * a.shape[2], 1024)
    for n in ("conv_w_in", "mla_w_in", "mla_w_uq", "mla_w_ukv"):
        a = sh[n]
        full[n] = a.transpose(1, 0, 2).reshape(a.shape[1], 4 * a.shape[2])
    return full


def _shards_from_full(full):
    sh = {}
    for n in ("w_mem_kv", "w_out"):
        a = full[n]
        sh[n] = a.reshape(2, 4, a.shape[1] // 4, 1024).transpose(1, 0, 2, 3)
    for n in ("conv_w_in", "mla_w_in", "mla_w_uq", "mla_w_ukv"):
        a = full[n]
        sh[n] = a.reshape(a.shape[0], 4, a.shape[1] // 4).transpose(1, 0, 2)
    return sh


def _conv_in_to_internal(w):
    return jnp.concatenate([w[:, 3584:], w[:, :3584]], axis=1)


def _conv_in_from_internal(w):
    return jnp.concatenate([w[:, MIX:], w[:, :MIX]], axis=1)


def _mla_in_to_internal(w):
    return jnp.concatenate([w[:, 1344:], w[:, :512], w[:, 832:1344], w[:, 512:768], w[:, 768:832],
                            jnp.zeros((w.shape[0], 64), w.dtype)], axis=1)


def _mla_in_from_internal(w):
    return jnp.concatenate([w[:, P1_CQ:P1_QM], w[:, P1_CKV:P1_KR], w[:, P1_KR:P1_KR + 64], w[:, P1_QM:P1_CKV],
                            w[:, :MIX]], axis=1)


def _uq_to_internal(w):
    w = w.reshape(w.shape[0], MLA_HEADS, HEAD + ROPE)
    return jnp.pad(w, ((0, 0), (0, 0), (0, QK_PAD - HEAD - ROPE))).reshape(w.shape[0], MLA_HEADS * QK_PAD)


def _uq_from_internal(w):
    return w.reshape(w.shape[0], MLA_HEADS, QK_PAD)[:, :, :HEAD + ROPE].reshape(w.shape[0], MLA_HEADS * (HEAD + ROPE))


SMALL_GRAD_ROWS = 456


def _rows128(a, rows):
    flat = a.reshape(-1)
    return jnp.pad(flat, (0, rows * 128 - flat.shape[0])).reshape(rows, 128)


def kernel(x, mem, positions, norm_g, mem_norm_g, w_mem_kv, w_out, conv_w_in, conv_dw, conv_dw_b, conv_ln_g, conv_ln_b, mla_w_in, mla_q_norm_g, mla_w_uq, mla_kv_norm_g, mla_w_ukv, final_norm_g, loss_target, m_norm_g, m_mem_norm_g, m_w_mem_kv, m_w_out, m_conv_w_in, m_conv_dw, m_conv_dw_b, m_conv_ln_g, m_conv_ln_b, m_mla_w_in, m_mla_q_norm_g, m_mla_w_uq, m_mla_kv_norm_g, m_mla_w_ukv, m_final_norm_g, v_norm_g, v_mem_norm_g, v_w_mem_kv, v_w_out, v_conv_w_in, v_conv_dw, v_conv_dw_b, v_conv_ln_g, v_conv_ln_b, v_mla_w_in, v_mla_q_norm_g, v_mla_w_uq, v_mla_kv_norm_g, v_mla_w_ukv, v_final_norm_g):
    bsz, seq, d = x.shape
    n = bsz * seq
    mlen = mem.shape[1]
    ax, ay, ac = _place()
    chip = 2 * ax + ay

    weights = dict(norm_g=norm_g, mem_norm_g=mem_norm_g, w_mem_kv=w_mem_kv, w_out=w_out, conv_w_in=conv_w_in[0],
                   conv_dw=conv_dw[0], conv_dw_b=conv_dw_b, conv_ln_g=conv_ln_g, conv_ln_b=conv_ln_b,
                   mla_w_in=mla_w_in[0], mla_q_norm_g=mla_q_norm_g, mla_w_uq=mla_w_uq[0],
                   mla_kv_norm_g=mla_kv_norm_g, mla_w_ukv=mla_w_ukv[0], final_norm_g=final_norm_g)

    gathered = _allgather_packed(_pack_shards(weights, BF16), chip)
    wf = _full_from_shards(_unpack_shards(gathered))
    w_conv_in = _conv_in_to_internal(wf["conv_w_in"])
    w_mla_in = _mla_in_to_internal(wf["mla_w_in"])
    w_uq = _uq_to_internal(wf["mla_w_uq"])
    w_ukv = wf["mla_w_ukv"]
    w_memkv = wf["w_mem_kv"]
    w_o = wf["w_out"]
    w_conv_in_t, w_mla_in_t, w_uq_t, w_ukv_t = w_conv_in.T, w_mla_in.T, w_uq.T, w_ukv.T
    w_memkv_t = w_memkv.transpose(0, 2, 1)
    w_o_t = w_o.transpose(0, 2, 1)

    small_in = jnp.concatenate([_rows128(conv_dw[0], 93), _rows128(mla_q_norm_g, 1), _rows128(mla_kv_norm_g, 1),
                                jnp.zeros((1, 128), F32)], axis=0)
    small_all, _ = _gather_sum_small(small_in, name="gather_small_params")
    small_all = small_all[0::2]
    dw_full = small_all[:, :93].reshape(4, -1)[:, :CONV_K * 384].reshape(4, CONV_K, 384)
    dw_full = dw_full.transpose(1, 0, 2).reshape(CONV_K, MAIN)
    qg_full = small_all[:, 93].reshape(Q_RANK)
    kvg_full = small_all[:, 94, :64].reshape(KV_RANK)

    inv_freq = 1.0 / (ROPE_THETA ** (jnp.arange(0, ROPE, 2, dtype=F32) / ROPE))
    ang = positions.astype(F32).reshape(n, 1) * inv_freq
    cos, sin, zer = jnp.cos(ang), jnp.sin(ang), jnp.zeros((n, 64), F32)
    rope_c = jnp.concatenate([cos, cos, zer], axis=1)
    rope_s = jnp.concatenate([-sin, sin, zer], axis=1)

    x2 = x.reshape(n, d)
    mem2 = mem.reshape(bsz * mlen, d)
    tgt2 = loss_target.reshape(n, d)

    memn = [_rms_fwd(mem2, mem_norm_g[i], name=f"mem_norm{i}") for i in range(2)]
    kvm = [_matmul(memn[i], w_memkv[i], out_dtype=BF16, name=f"mem_kv{i}").reshape(bsz, mlen, 2 * MEMW)
           for i in range(2)]

    u0 = _rms_fwd(x2, norm_g[0], name="norm0")
    p0 = _matmul(u0, w_conv_in, name="conv_in_proj")
    hc = _conv_fwd(p0, dw_full, conv_dw_b, bsz, seq).reshape(n, MAIN)
    ymem0 = _mem_fwd(p0, kvm[0], P0_QM // MEMW, bsz, seq, name="mem_attn0").reshape(n, MEMW)
    y0 = _gate_fwd(hc, ymem0, p0, conv_ln_g, conv_ln_b, name="gate0")
    h1 = _matmul(y0, w_o[0], x2, name="out_proj0")

    u1 = _rms_fwd(h1, norm_g[1], name="norm1")
    p1 = _matmul(u1, w_mla_in, name="mla_in_proj")
    cqn = _rms_fwd(p1, qg_full, width=Q_RANK, col_block=P1_CQ // Q_RANK, name="q_norm")
    ckvn = _rms_fwd(p1, kvg_full, width=KV_RANK, col_block=P1_CKV // KV_RANK, name="kv_norm")
    q = _matmul(cqn, w_uq, name="q_up")
    kv = _matmul(ckvn, w_ukv, name="kv_up")
    qb = _q_rope(q, rope_c, rope_s, inverse=False, name="q_rope")
    kb, vb = _kv_pack(kv, p1, rope_c, rope_s)
    o1, lse = _attn_fwd(qb, kb, vb, bsz, seq)
    o1 = o1.reshape(n, MAIN)
    ymem1 = _mem_fwd(p1, kvm[1], P1_QM // MEMW, bsz, seq, name="mem_attn1").reshape(n, MEMW)
    y1 = _gate_fwd(o1, ymem1, p1, name="gate1")
    h2 = _matmul(y1, w_o[1], h1, name="out_proj1")

    dh2, d_final_g, loss_part = _final_loss(h2, final_norm_g, tgt2)
    loss = lax.psum(loss_part[0, 0], ("x", "y", "c"))

    g_w_out1 = _matmul(y1, dh2, ta=True, name="d_w_out1")
    dy1 = _matmul(dh2, w_o_t[1], name="d_y1")
    do1, dymem1, dz1 = _gate_bwd(dy1, o1, ymem1, p1, name="gate1_bwd")
    dqm1, dkvm1 = _mem_bwd(p1, kvm[1], dymem1, P1_QM // MEMW, bsz, seq, name="mem_attn1_bwd")
    dq, dk, dv = _attn_bwd(qb, kb, vb, o1.reshape(bsz, seq, MAIN), do1, lse, bsz, seq)
    dqb = _q_rope(dq, rope_c, rope_s, inverse=True, name="q_rope_bwd")
    dkv, dkr = _kv_unpack(dk, dv, rope_c, rope_s)
    g_w_uq = _matmul(cqn, dqb, ta=True, name="d_w_uq")
    dcqn = _matmul(dqb, w_uq_t, name="d_cqn")
    g_w_ukv = _matmul(ckvn, dkv, ta=True, name="d_w_ukv")
    dckvn = _matmul(dkv, w_ukv_t, name="d_ckvn")
    dcq, g_qg = _rms_bwd(p1, qg_full, dcqn, width=Q_RANK, col_block=P1_CQ // Q_RANK, out_dtype=BF16,
                         name="q_norm_bwd")
    dckv, g_kvg = _rms_bwd(p1, kvg_full, dckvn, width=KV_RANK, col_block=P1_CKV // KV_RANK, out_dtype=BF16,
                           name="kv_norm_bwd")
    dp1 = jnp.concatenate([dz1, dcq, dqm1.reshape(n, MEMW), dckv, dkr], axis=1)
    g_w_mla_in = _matmul(u1, dp1, ta=True, name="d_w_mla_in")
    du1 = _matmul(dp1, w_mla_in_t, name="d_u1")
    dh1, g_norm1 = _rms_bwd(h1, norm_g[1], du1, dh2, name="norm1_bwd")

    g_w_out0 = _matmul(y0, dh1, ta=True, name="d_w_out0")
    dy0 = _matmul(dh1, w_o_t[0], name="d_y0")
    dhc, dymem0, dz0, g_ln_g, g_ln_b = _gate_bwd(dy0, hc, ymem0, p0, conv_ln_g, conv_ln_b, name="gate0_bwd")
    dqm0, dkvm0 = _mem_bwd(p0, kvm[0], dymem0, P0_QM // MEMW, bsz, seq, name="mem_attn0_bwd")
    da, dg, g_dw32 = _conv_bwd(p0, dhc.reshape(bsz, seq, MAIN), dw_full, bsz, seq)
    dp0 = jnp.concatenate([dz0, da.reshape(n, MAIN), dg.reshape(n, MAIN), dqm0.reshape(n, MEMW)], axis=1)
    g_w_conv_in = _matmul(u0, dp0, ta=True, name="d_w_conv_in")
    du0 = _matmul(dp0, w_conv_in_t, name="d_u0")
    grad_x, g_norm0 = _rms_bwd(x2, norm_g[0], du0, dh1, name="norm0_bwd")

    g_w_memkv, g_mem_g = [], []
    for i, dkvm in enumerate((dkvm0, dkvm1)):
        dk2 = dkvm.reshape(bsz * mlen, 2 * MEMW)
        g_w_memkv.append(_matmul(memn[i], dk2, ta=True, name=f"d_w_mem_kv{i}"))
        dmemn = _matmul(dk2, w_memkv_t[i], name=f"d_memn{i}")
        g_mem_g.append(_rms_bwd(mem2, mem_norm_g[i], dmemn, name=f"mem_norm{i}_bwd")[1])

    full_g = dict(w_mem_kv=jnp.stack(g_w_memkv), w_out=jnp.stack([g_w_out0, g_w_out1]),
                  conv_w_in=_conv_in_from_internal(g_w_conv_in), mla_w_in=_mla_in_from_internal(g_w_mla_in),
                  mla_w_uq=_uq_from_internal(g_w_uq), mla_w_ukv=g_w_ukv)
    gsh = _shards_from_full(full_g)
    gpk = jnp.concatenate([gsh[nm].reshape(4, -1, 1024) for nm in PACK_ORDER]
                          + [jnp.zeros((4, PACK_ROWS - 4272, 1024), F32)], axis=1)
    hr = PACK_ROWS // 2
    to_sibling = lax.dynamic_slice_in_dim(gpk, (1 - ac) * hr, hr, axis=1).astype(BF16)
    from_sibling = _swap_with_sibling(to_sibling, name="rs_sibling_swap")
    core = jnp.reshape(ac, (1,)).astype(jnp.int32)
    pair = _add_pairs(core, gpk, from_sibling)
    exch = _chip_exchange(pair)
    chip_core = jnp.stack([chip, ac]).astype(jnp.int32)
    red_half = _add_final(chip_core, gpk, from_sibling, exch)
    other_half = _swap_with_sibling(red_half, name="rs_share_halves")
    red_full = jnp.concatenate([jnp.where(ac == 0, red_half, other_half),
                                jnp.where(ac == 0, other_half, red_half)], axis=0)
    red = _unpack_shards(red_full)

    small_g = jnp.concatenate([
        _rows128(jnp.concatenate([g_norm0, g_norm1], axis=0), 16), _rows128(jnp.concatenate(g_mem_g, axis=0), 16),
        _rows128(g_dw32[CONV_K], 12), _rows128(g_ln_g, 12), _rows128(g_ln_b, 12), _rows128(d_final_g, 8),
        _rows128(g_dw32[:CONV_K], 372), _rows128(g_qg, 4), _rows128(g_kvg, 2), jnp.zeros((2, 128), F32)], axis=0)
    _, small_sum = _gather_sum_small(small_g, name="allreduce_small_grads")
    flat = small_sum.reshape(-1)

    def take(off, shape):
        size = 1
        for s_ in shape:
            size *= s_
        return flat[off * 128:off * 128 + size].reshape(shape)

    grads = dict(red)
    grads["conv_w_in"] = red["conv_w_in"][None]
    grads["mla_w_in"] = red["mla_w_in"][None]
    grads["mla_w_uq"] = red["mla_w_uq"][None]
    grads["mla_w_ukv"] = red["mla_w_ukv"][None]
    grads["norm_g"] = take(0, (2, D_MODEL))
    grads["mem_norm_g"] = take(16, (2, D_MODEL))
    grads["conv_dw_b"] = take(32, (1, MAIN))
    grads["conv_ln_g"] = take(44, (1, MAIN))
    grads["conv_ln_b"] = take(56, (1, MAIN))
    grads["final_norm_g"] = take(68, (D_MODEL,))
    grads["conv_dw"] = lax.dynamic_slice_in_dim(take(76, (CONV_K, MAIN)), chip * 384, 384, axis=1)[None]
    grads["mla_q_norm_g"] = lax.dynamic_slice_in_dim(take(448, (Q_RANK,)), chip * 128, 128, axis=0)[None]
    grads["mla_kv_norm_g"] = lax.dynamic_slice_in_dim(take(452, (KV_RANK,)), chip * 64, 64, axis=0)[None]

    params = dict(norm_g=norm_g, mem_norm_g=mem_norm_g, w_mem_kv=w_mem_kv, w_out=w_out, conv_w_in=conv_w_in,
                  conv_dw=conv_dw, conv_dw_b=conv_dw_b, conv_ln_g=conv_ln_g, conv_ln_b=conv_ln_b, mla_w_in=mla_w_in,
                  mla_q_norm_g=mla_q_norm_g, mla_w_uq=mla_w_uq, mla_kv_norm_g=mla_kv_norm_g, mla_w_ukv=mla_w_ukv,
                  final_norm_g=final_norm_g)
    mom1 = dict(norm_g=m_norm_g, mem_norm_g=m_mem_norm_g, w_mem_kv=m_w_mem_kv, w_out=m_w_out, conv_w_in=m_conv_w_in,
                conv_dw=m_conv_dw, conv_dw_b=m_conv_dw_b, conv_ln_g=m_conv_ln_g, conv_ln_b=m_conv_ln_b,
                mla_w_in=m_mla_w_in, mla_q_norm_g=m_mla_q_norm_g, mla_w_uq=m_mla_w_uq,
                mla_kv_norm_g=m_mla_kv_norm_g, mla_w_ukv=m_mla_w_ukv, final_norm_g=m_final_norm_g)
    mom2 = dict(norm_g=v_norm_g, mem_norm_g=v_mem_norm_g, w_mem_kv=v_w_mem_kv, w_out=v_w_out, conv_w_in=v_conv_w_in,
                conv_dw=v_conv_dw, conv_dw_b=v_conv_dw_b, conv_ln_g=v_conv_ln_g, conv_ln_b=v_conv_ln_b,
                mla_w_in=v_mla_w_in, mla_q_norm_g=v_mla_q_norm_g, mla_w_uq=v_mla_w_uq,
                mla_kv_norm_g=v_mla_kv_norm_g, mla_w_ukv=v_mla_w_ukv, final_norm_g=v_final_norm_g)
    names = list(params)
    g_out, deltas, new_m, new_v = [], [], [], []
    for nm in names:
        w = params[nm]
        g = grads[nm].reshape(w.shape)
        w2 = w.reshape(1, -1) if w.ndim == 1 else w
        dlt, m_new, v_new = _adamw(w2, g.reshape(w2.shape), mom1[nm].reshape(w2.shape), mom2[nm].reshape(w2.shape),
                                   name=f"adamw_{nm}")
        g_out.append(g)
        deltas.append(dlt.reshape(w.shape))
        new_m.append(m_new.reshape(w.shape))
        new_v.append(v_new.reshape(w.shape))

    return (loss, grad_x.reshape(bsz, seq, d), *g_out, *deltas, *new_m, *new_v)
```

```python
import functools

import jax
import jax.numpy as jnp
from jax import lax
from jax.experimental import pallas as pl
from jax.experimental.pallas import tpu as pltpu

F32 = jnp.float32
BF16 = jnp.bfloat16
MESH = pl.DeviceIdType.MESH

D_MODEL = 1024
MIX = 2048
MAIN = 1536
MEMW = 512
MEM_HEADS = 4
HEAD = 128
CONV_K = 31
CONV_PAD = 32
MLA_HEADS = 12
ROPE = 64
QK_PAD = 256
Q_RANK = 512
KV_RANK = 256
ROPE_THETA = 10000.0
RMS_EPS = 1e-6
LN_EPS = 1e-5
MEM_SCALE = HEAD ** -0.5
MLA_SCALE = (HEAD + ROPE) ** -0.5
NEG = -1e30

P0_COLS = 5632
P0_A, P0_G, P0_QM, P0_Z = 0, 1536, 3072, 3584
P1_COLS = 3456
P1_Z, P1_CQ, P1_QM, P1_CKV, P1_KR = 0, 2048, 2560, 3072, 3328

ADAM_LR = 0.001
ADAM_B1 = 0.9
ADAM_B2 = 0.999
ADAM_EPS = 1e-08
ADAM_WD = 0.01
ADAM_STEP = 10

VMEM_LIMIT = 56 * 1024 * 1024

ADD_TILE = 128


def _cparams(sem=None):
    return pltpu.CompilerParams(dimension_semantics=sem, vmem_limit_bytes=VMEM_LIMIT)


def _tile(n, pref):
    if n <= pref:
        return n
    t = (pref // 128) * 128
    while t > 128 and n % t:
        t -= 128
    assert n % t == 0, (n, pref)
    return t


def _mm_kernel(*refs, nk, ta, has_res):
    if has_res:
        a_ref, b_ref, r_ref, *o_refs, acc_ref = refs
    else:
        a_ref, b_ref, *o_refs, acc_ref = refs
    k = pl.program_id(2)
    a = a_ref[...].astype(BF16)
    b = b_ref[...].astype(BF16)
    dn = (((0 if ta else 1,), (0,)), ((), ()))
    p = lax.dot_general(a, b, dn, preferred_element_type=F32)

    @pl.when(k == 0)
    def _():
        acc_ref[...] = p

    @pl.when(k > 0)
    def _():
        acc_ref[...] += p

    @pl.when(k == nk - 1)
    def _():
        acc = acc_ref[...]
        if has_res:
            acc = r_ref[...] + acc
        for o in o_refs:
            o[...] = acc.astype(o.dtype)


def _matmul(a, b, res=None, *, name, ta=False, out_dtype=F32, also_bf16=False, tm=1024, tn=512, tk=1024):
    m, kd = (a.shape[1], a.shape[0]) if ta else a.shape
    n = b.shape[1]
    assert kd == b.shape[0]
    tm, tn, tk = _tile(m, tm), _tile(n, tn), _tile(kd, tk)
    nk = kd // tk
    a_spec = (pl.BlockSpec((tk, tm), lambda i, j, k: (k, i)) if ta
              else pl.BlockSpec((tm, tk), lambda i, j, k: (i, k)))
    out_spec = pl.BlockSpec((tm, tn), lambda i, j, k: (i, j))
    in_specs = [a_spec, pl.BlockSpec((tk, tn), lambda i, j, k: (k, j))]
    args = [a, b]
    if res is not None:
        in_specs.append(out_spec)
        args.append(res)
    out_shape = [jax.ShapeDtypeStruct((m, n), out_dtype)]
    if also_bf16:
        out_shape.append(jax.ShapeDtypeStruct((m, n), BF16))
    outs = pl.pallas_call(
        functools.partial(_mm_kernel, nk=nk, ta=ta, has_res=res is not None),
        name=name,
        grid=(m // tm, n // tn, nk),
        in_specs=in_specs,
        out_specs=[out_spec] * len(out_shape),
        out_shape=out_shape,
        scratch_shapes=[pltpu.VMEM((tm, tn), F32)],
        compiler_params=_cparams(("parallel", "parallel", "arbitrary")),
    )(*args)
    return tuple(outs) if also_bf16 else outs[0]


def _rms_fwd_kernel(h_ref, g_ref, o_ref):
    h = h_ref[...]
    rstd = lax.rsqrt(jnp.mean(h * h, axis=-1, keepdims=True) + RMS_EPS)
    o_ref[...] = (h * rstd * g_ref[...]).astype(o_ref.dtype)


def _rms_fwd(h, g, *, name, width=None, col_block=0, tm=512):
    t = h.shape[0]
    width = width or h.shape[1]
    tm = _tile(t, tm)
    return pl.pallas_call(
        functools.partial(_rms_fwd_kernel),
        name=name,
        grid=(t // tm,),
        in_specs=[pl.BlockSpec((tm, width), lambda i: (i, col_block)),
                  pl.BlockSpec((1, width), lambda i: (0, 0))],
        out_specs=pl.BlockSpec((tm, width), lambda i: (i, 0)),
        out_shape=jax.ShapeDtypeStruct((t, width), BF16),
        compiler_params=_cparams(("parallel",)),
    )(h, g.reshape(1, width))


def _rms_bwd_math(h, g, du):
    rstd = lax.rsqrt(jnp.mean(h * h, axis=-1, keepdims=True) + RMS_EPS)
    dug = du * g
    dh = rstd * dug - h * (rstd * rstd * rstd) * jnp.mean(dug * h, axis=-1, keepdims=True)
    dg = jnp.sum(du * h * rstd, axis=0, keepdims=True)
    return dh, dg


def _rms_bwd_kernel(*refs, has_res):
    if has_res:
        h_ref, g_ref, du_ref, res_ref, dh_ref, dg_ref = refs
    else:
        h_ref, g_ref, du_ref, dh_ref, dg_ref = refs
    dh, dg = _rms_bwd_math(h_ref[...], g_ref[...], du_ref[...].astype(F32))
    if has_res:
        dh = dh + res_ref[...]
    dh_ref[...] = dh.astype(dh_ref.dtype)

    @pl.when(pl.program_id(0) == 0)
    def _():
        dg_ref[...] = dg

    @pl.when(pl.program_id(0) > 0)
    def _():
        dg_ref[...] += dg


def _rms_bwd(h, g, du, res=None, *, name, width=None, col_block=0, out_dtype=F32, tm=512):
    t = h.shape[0]
    width = width or h.shape[1]
    tm = _tile(t, tm)
    row = pl.BlockSpec((tm, width), lambda i: (i, 0))
    in_specs = [pl.BlockSpec((tm, width), lambda i: (i, col_block)),
                pl.BlockSpec((1, width), lambda i: (0, 0)), row]
    args = [h, g.reshape(1, width), du]
    if res is not None:
        in_specs.append(row)
        args.append(res)
    return pl.pallas_call(
        functools.partial(_rms_bwd_kernel, has_res=res is not None),
        name=name,
        grid=(t // tm,),
        in_specs=in_specs,
        out_specs=[row, pl.BlockSpec((1, width), lambda i: (0, 0))],
        out_shape=[jax.ShapeDtypeStruct((t, width), out_dtype), jax.ShapeDtypeStruct((1, width), F32)],
        compiler_params=_cparams(("arbitrary",)),
    )(*args)


def _final_kernel(h_ref, g_ref, t_ref, dh_ref, dg_ref, loss_ref):
    h = h_ref[...]
    g = g_ref[...]
    rstd = lax.rsqrt(jnp.mean(h * h, axis=-1, keepdims=True) + RMS_EPS)
    e = h * rstd * g - t_ref[...]
    part = 0.5 * jnp.sum(jnp.mean(e * e, axis=-1, keepdims=True), axis=0, keepdims=True)
    dh, dg = _rms_bwd_math(h, g, e * (1.0 / D_MODEL))
    dh_ref[...] = dh
    part = jnp.broadcast_to(part, loss_ref.shape)

    @pl.when(pl.program_id(0) == 0)
    def _():
        dg_ref[...] = dg
        loss_ref[...] = part

    @pl.when(pl.program_id(0) > 0)
    def _():
        dg_ref[...] += dg
        loss_ref[...] += part


def _final_loss(h, g, target, *, tm=512):
    t, d = h.shape
    tm = _tile(t, tm)
    row = pl.BlockSpec((tm, d), lambda i: (i, 0))
    return pl.pallas_call(
        functools.partial(_final_kernel),
        name="final_loss",
        grid=(t // tm,),
        in_specs=[row, pl.BlockSpec((1, d), lambda i: (0, 0)), row],
        out_specs=[row, pl.BlockSpec((1, d), lambda i: (0, 0)), pl.BlockSpec((1, 128), lambda i: (0, 0))],
        out_shape=[jax.ShapeDtypeStruct((t, d), F32), jax.ShapeDtypeStruct((1, d), F32),
                   jax.ShapeDtypeStruct((1, 128), F32)],
        compiler_params=_cparams(("arbitrary",)),
    )(h, g.reshape(1, d), target)


CONV_CT = 128
CONV_TC = 256


def _glu_into(pad_ref, a_ref, g_ref, seq, tc):
    ct = pad_ref.shape[1]
    pad_ref[0:CONV_PAD, :] = jnp.zeros((CONV_PAD, ct), F32)
    for r in range(0, seq, tc):
        a = a_ref[0, r:r + tc, :]
        g = g_ref[0, r:r + tc, :]
        pad_ref[CONV_PAD + r:CONV_PAD + r + tc, :] = a * jax.nn.sigmoid(g)


def _conv_fwd_kernel(a_ref, g_ref, dw_ref, dwb_ref, hc_ref, pad_ref, *, seq, tc):
    ct = pad_ref.shape[1]
    _glu_into(pad_ref, a_ref, g_ref, seq, tc)
    for r in range(0, seq, tc):
        acc = jnp.broadcast_to(dwb_ref[...], (tc, ct))
        for k in range(CONV_K):
            o = CONV_PAD + r - (CONV_K - 1) + k
            acc = acc + dw_ref[k:k + 1, :] * pad_ref[o:o + tc, :]
        hc_ref[0, r:r + tc, :] = acc


def _conv_fwd(p0, dw, dwb, bsz, seq):
    ct = CONV_CT
    tc = min(CONV_TC, seq)
    p3 = p0.reshape(bsz, seq, P0_COLS)
    return pl.pallas_call(
        functools.partial(_conv_fwd_kernel, seq=seq, tc=tc),
        name="conv_fwd",
        grid=(MAIN // ct, bsz),
        in_specs=[pl.BlockSpec((1, seq, ct), lambda j, b: (b, 0, P0_A // ct + j)),
                  pl.BlockSpec((1, seq, ct), lambda j, b: (b, 0, P0_G // ct + j)),
                  pl.BlockSpec((CONV_K, ct), lambda j, b: (0, j)),
                  pl.BlockSpec((1, ct), lambda j, b: (0, j))],
        out_specs=pl.BlockSpec((1, seq, ct), lambda j, b: (b, 0, j)),
        out_shape=jax.ShapeDtypeStruct((bsz, seq, MAIN), F32),
        scratch_shapes=[pltpu.VMEM((seq + CONV_PAD, ct), F32)],
        compiler_params=_cparams(("parallel", "parallel")),
    )(p3, p3, dw, dwb)


def _conv_bwd_kernel(a_ref, g_ref, dhc_ref, dw_ref, da_ref, dg_ref, ddw_ref, pad_ref, dpad_ref, acc_ref,
                     *, seq, tc):
    ct = pad_ref.shape[1]
    b = pl.program_id(1)
    _glu_into(pad_ref, a_ref, g_ref, seq, tc)
    dpad_ref[seq:seq + CONV_PAD, :] = jnp.zeros((CONV_PAD, ct), F32)
    for r in range(0, seq, tc):
        dpad_ref[r:r + tc, :] = dhc_ref[0, r:r + tc, :]
    acc_ref[...] = jnp.zeros(acc_ref.shape, F32)
    for r in range(0, seq, tc):
        dh = dhc_ref[0, r:r + tc, :]
        dglu = jnp.zeros((tc, ct), F32)
        for k in range(CONV_K):
            o = r + (CONV_K - 1) - k
            dglu = dglu + dw_ref[k:k + 1, :] * dpad_ref[o:o + tc, :]
            o = CONV_PAD + r - (CONV_K - 1) + k
            prod = pad_ref[o:o + tc, :] * dh
            acc_ref[k] += jnp.sum(prod.reshape(tc // 8, 8, ct), axis=0)
        acc_ref[CONV_K] += jnp.sum(dh.reshape(tc // 8, 8, ct), axis=0)
        a = a_ref[0, r:r + tc, :]
        sg = jax.nn.sigmoid(g_ref[0, r:r + tc, :])
        da_ref[0, r:r + tc, :] = (dglu * sg).astype(da_ref.dtype)
        dg_ref[0, r:r + tc, :] = (dglu * a * sg * (1.0 - sg)).astype(dg_ref.dtype)
    tot = jnp.sum(acc_ref[...], axis=1)

    @pl.when(b == 0)
    def _():
        ddw_ref[...] = tot

    @pl.when(b > 0)
    def _():
        ddw_ref[...] += tot


def _conv_bwd(p0, dhc, dw, bsz, seq):
    ct = CONV_CT
    tc = min(CONV_TC, seq)
    p3 = p0.reshape(bsz, seq, P0_COLS)
    blk = pl.BlockSpec((1, seq, ct), lambda j, b: (b, 0, j))
    return pl.pallas_call(
        functools.partial(_conv_bwd_kernel, seq=seq, tc=tc),
        name="conv_bwd",
        grid=(MAIN // ct, bsz),
        in_specs=[pl.BlockSpec((1, seq, ct), lambda j, b: (b, 0, P0_A // ct + j)),
                  pl.BlockSpec((1, seq, ct), lambda j, b: (b, 0, P0_G // ct + j)),
                  blk,
                  pl.BlockSpec((CONV_K, ct), lambda j, b: (0, j))],
        out_specs=[blk, blk, pl.BlockSpec((CONV_K + 1, ct), lambda j, b: (0, j))],
        out_shape=[jax.ShapeDtypeStruct((bsz, seq, MAIN), BF16), jax.ShapeDtypeStruct((bsz, seq, MAIN), BF16),
                   jax.ShapeDtypeStruct((CONV_K + 1, MAIN), F32)],
        scratch_shapes=[pltpu.VMEM((seq + CONV_PAD, ct), F32), pltpu.VMEM((seq + CONV_PAD, ct), F32),
                        pltpu.VMEM((CONV_K + 1, 8, ct), F32)],
        compiler_params=_cparams(("parallel", "arbitrary")),
    )(p3, p3, dhc, dw)


def _ln_parts(x, lng, lnb):
    mu = jnp.mean(x, axis=-1, keepdims=True)
    xc = x - mu
    rstd = lax.rsqrt(jnp.mean(xc * xc, axis=-1, keepdims=True) + LN_EPS)
    xh = xc * rstd
    hl = xh * lng + lnb
    return rstd, xh, hl


GATE_CHUNK = 512
GATE_NZ = MIX // GATE_CHUNK


def _z_specs(tm, zb):
    return [pl.BlockSpec((tm, GATE_CHUNK), lambda i, j=j: (i, zb + j)) for j in range(GATE_NZ)]


def _gate_fwd_kernel(*refs, ln):
    main_ref, ymem_ref, *z_refs = refs[:2 + GATE_NZ]
    y_ref = refs[-1]
    x = main_ref[...]
    if ln:
        _, _, hl = _ln_parts(x, refs[-3][...], refs[-2][...])
        x = hl * jax.nn.sigmoid(hl)
    for j, z_ref in enumerate(z_refs):
        cols = slice(j * GATE_CHUNK, (j + 1) * GATE_CHUNK)
        z = z_ref[...]
        src = x[:, cols] if j < MAIN // GATE_CHUNK else ymem_ref[...]
        y_ref[:, cols] = (src * (z * jax.nn.sigmoid(z))).astype(y_ref.dtype)


def _gate_fwd(main, ymem, p, zb, ln_g=None, ln_b=None, *, name, tm=256):
    t = main.shape[0]
    tm = _tile(t, tm)
    ln = ln_g is not None
    in_specs = [pl.BlockSpec((tm, MAIN), lambda i: (i, 0)), pl.BlockSpec((tm, MEMW), lambda i: (i, 0))]
    in_specs += _z_specs(tm, zb)
    args = [main, ymem] + [p] * GATE_NZ
    if ln:
        in_specs += [pl.BlockSpec((1, MAIN), lambda i: (0, 0))] * 2
        args += [ln_g.reshape(1, MAIN), ln_b.reshape(1, MAIN)]
    return pl.pallas_call(
        functools.partial(_gate_fwd_kernel, ln=ln),
        name=name,
        grid=(t // tm,),
        in_specs=in_specs,
        out_specs=pl.BlockSpec((tm, MIX), lambda i: (i, 0)),
        out_shape=jax.ShapeDtypeStruct((t, MIX), BF16),
        compiler_params=_cparams(("parallel",)),
    )(*args)


def _gate_bwd_kernel(*refs, ln):
    dy_ref, main_ref, ymem_ref, *z_refs = refs[:3 + GATE_NZ]
    if ln:
        lng_ref, lnb_ref, dmain_ref, dymem_ref, dz_ref, dlng_ref, dlnb_ref = refs[3 + GATE_NZ:]
    else:
        dmain_ref, dymem_ref, dz_ref = refs[3 + GATE_NZ:]
    x = main_ref[...]
    if ln:
        lng = lng_ref[...]
        rstd, xh, hl = _ln_parts(x, lng, lnb_ref[...])
        sh = jax.nn.sigmoid(hl)
        ymain = hl * sh
    else:
        ymain = x
    for j, z_ref in enumerate(z_refs):
        cols = slice(j * GATE_CHUNK, (j + 1) * GATE_CHUNK)
        dy = dy_ref[:, cols]
        z = z_ref[...]
        sg = jax.nn.sigmoid(z)
        dsz = sg * (1.0 + z * (1.0 - sg))
        if j < MAIN // GATE_CHUNK:
            src = ymain[:, cols]
            dmain_ref[:, cols] = dy * (z * sg)
        else:
            src = ymem_ref[...]
            dymem_ref[...] = dy * (z * sg)
        dz_ref[:, cols] = (dy * src * dsz).astype(dz_ref.dtype)
    if not ln:
        return
    dym = dmain_ref[...]
    dhl = dym * (sh * (1.0 + hl * (1.0 - sh)))
    dxh = dhl * lng
    dmain_ref[...] = rstd * (dxh - jnp.mean(dxh, axis=-1, keepdims=True)
                             - xh * jnp.mean(dxh * xh, axis=-1, keepdims=True))
    dlng = jnp.sum(dhl * xh, axis=0, keepdims=True)
    dlnb = jnp.sum(dhl, axis=0, keepdims=True)

    @pl.when(pl.program_id(0) == 0)
    def _():
        dlng_ref[...] = dlng
        dlnb_ref[...] = dlnb

    @pl.when(pl.program_id(0) > 0)
    def _():
        dlng_ref[...] += dlng
        dlnb_ref[...] += dlnb


def _gate_bwd(dy, main, ymem, p, zb, ln_g=None, ln_b=None, *, name, tm=256):
    t = main.shape[0]
    tm = _tile(t, tm)
    ln = ln_g is not None
    r_main = pl.BlockSpec((tm, MAIN), lambda i: (i, 0))
    r_mem = pl.BlockSpec((tm, MEMW), lambda i: (i, 0))
    r_mix = pl.BlockSpec((tm, MIX), lambda i: (i, 0))
    vec = pl.BlockSpec((1, MAIN), lambda i: (0, 0))
    in_specs = [r_mix, r_main, r_mem] + _z_specs(tm, zb)
    args = [dy, main, ymem] + [p] * GATE_NZ
    out_specs = [r_main, r_mem, r_mix]
    out_shape = [jax.ShapeDtypeStruct((t, MAIN), F32), jax.ShapeDtypeStruct((t, MEMW), F32),
                 jax.ShapeDtypeStruct((t, MIX), BF16)]
    if ln:
        in_specs += [vec, vec]
        args += [ln_g.reshape(1, MAIN), ln_b.reshape(1, MAIN)]
        out_specs += [vec, vec]
        out_shape += [jax.ShapeDtypeStruct((1, MAIN), F32)] * 2
    return pl.pallas_call(
        functools.partial(_gate_bwd_kernel, ln=ln),
        name=name,
        grid=(t // tm,),
        in_specs=in_specs,
        out_specs=out_specs,
        out_shape=out_shape,
        compiler_params=_cparams(("arbitrary",)),
    )(*args)


def _dot_nt(a, b):
    return lax.dot_general(a, b, (((1,), (1,)), ((), ())), preferred_element_type=F32)


def _dot_tn(a, b):
    return lax.dot_general(a, b, (((0,), (0,)), ((), ())), preferred_element_type=F32)


def _dot(a, b):
    return jnp.dot(a, b, preferred_element_type=F32)


def _mem_probs(q, k):
    s = _dot_nt(q, k) * MEM_SCALE
    p = jnp.exp(s - jnp.max(s, axis=-1, keepdims=True))
    return p / jnp.sum(p, axis=-1, keepdims=True)


def _mem_fwd_kernel(q_ref, kv_ref, o_ref):
    for h in range(MEM_HEADS):
        c = slice(h * HEAD, (h + 1) * HEAD)
        cv = slice(MEMW + h * HEAD, MEMW + (h + 1) * HEAD)
        p = _mem_probs(q_ref[0, :, c].astype(BF16), kv_ref[0, :, c])
        o_ref[0, :, c] = _dot(p.astype(BF16), kv_ref[0, :, cv])


def _mem_fwd(p, kvm, col_block, bsz, seq, *, name, tq=512):
    tq = _tile(seq, tq)
    p3 = p.reshape(bsz, seq, p.shape[1])
    mlen = kvm.shape[1]
    return pl.pallas_call(
        functools.partial(_mem_fwd_kernel),
        name=name,
        grid=(bsz, seq // tq),
        in_specs=[pl.BlockSpec((1, tq, MEMW), lambda b, i: (b, i, col_block)),
                  pl.BlockSpec((1, mlen, 2 * MEMW), lambda b, i: (b, 0, 0))],
        out_specs=pl.BlockSpec((1, tq, MEMW), lambda b, i: (b, i, 0)),
        out_shape=jax.ShapeDtypeStruct((bsz, seq, MEMW), F32),
        compiler_params=_cparams(("parallel", "parallel")),
    )(p3, kvm)


def _mem_bwd_kernel(q_ref, kv_ref, do_ref, dq_ref, dkv_ref):
    @pl.when(pl.program_id(1) == 0)
    def _():
        dkv_ref[...] = jnp.zeros(dkv_ref.shape, F32)

    for h in range(MEM_HEADS):
        c = slice(h * HEAD, (h + 1) * HEAD)
        cv = slice(MEMW + h * HEAD, MEMW + (h + 1) * HEAD)
        q = q_ref[0, :, c].astype(BF16)
        k = kv_ref[0, :, c]
        v = kv_ref[0, :, cv]
        do = do_ref[0, :, c].astype(BF16)
        p = _mem_probs(q, k)
        dp = _dot_nt(do, v)
        ds = (p * (dp - jnp.sum(p * dp, axis=-1, keepdims=True)) * MEM_SCALE).astype(BF16)
        dq_ref[0, :, c] = _dot(ds, k).astype(dq_ref.dtype)
        dkv_ref[0, :, c] += _dot_tn(ds, q)
        dkv_ref[0, :, cv] += _dot_tn(p.astype(BF16), do)


def _mem_bwd(p, kvm, dymem, col_block, bsz, seq, *, name, tq=512):
    tq = _tile(seq, tq)
    p3 = p.reshape(bsz, seq, p.shape[1])
    mlen = kvm.shape[1]
    return pl.pallas_call(
        functools.partial(_mem_bwd_kernel),
        name=name,
        grid=(bsz, seq // tq),
        in_specs=[pl.BlockSpec((1, tq, MEMW), lambda b, i: (b, i, col_block)),
                  pl.BlockSpec((1, mlen, 2 * MEMW), lambda b, i: (b, 0, 0)),
                  pl.BlockSpec((1, tq, MEMW), lambda b, i: (b, i, 0))],
        out_specs=[pl.BlockSpec((1, tq, MEMW), lambda b, i: (b, i, 0)),
                   pl.BlockSpec((1, mlen, 2 * MEMW), lambda b, i: (b, 0, 0))],
        out_shape=[jax.ShapeDtypeStruct((bsz, seq, MEMW), BF16),
                   jax.ShapeDtypeStruct((bsz, mlen, 2 * MEMW), F32)],
        compiler_params=_cparams(("parallel", "arbitrary")),
    )(p3, kvm, dymem.reshape(bsz, seq, MEMW))


def _swap32(x):
    lane = lax.broadcasted_iota(jnp.int32, x.shape, 1)
    return jnp.where(lane < 32, pltpu.roll(x, 96, 1), pltpu.roll(x, 32, 1))


def _rope(x, cs, sn):
    return x * cs + _swap32(x) * sn


def _rope_t(d, cs, sn):
    return d * cs + _swap32(d * sn)


def _q_rope_kernel(q_ref, cs_ref, sn_ref, o_ref, *, inverse):
    cs = cs_ref[...]
    sn = sn_ref[...]
    for h in range(MLA_HEADS):
        c0 = slice(h * QK_PAD, h * QK_PAD + HEAD)
        c1 = slice(h * QK_PAD + HEAD, (h + 1) * QK_PAD)
        o_ref[:, c0] = q_ref[:, c0].astype(o_ref.dtype)
        x = q_ref[:, c1]
        o_ref[:, c1] = (_rope_t(x, cs, sn) if inverse else _rope(x, cs, sn)).astype(o_ref.dtype)


def _q_rope(q, cs, sn, *, inverse, name, tm=256):
    t, w = q.shape
    tm = _tile(t, tm)
    row = pl.BlockSpec((tm, w), lambda i: (i, 0))
    tab = pl.BlockSpec((tm, 128), lambda i: (i, 0))
    return pl.pallas_call(
        functools.partial(_q_rope_kernel, inverse=inverse),
        name=name,
        grid=(t // tm,),
        in_specs=[row, tab, tab],
        out_specs=row,
        out_shape=jax.ShapeDtypeStruct((t, w), BF16),
        compiler_params=_cparams(("parallel",)),
    )(q, cs, sn)


def _kv_pack_kernel(kv_ref, kr_ref, cs_ref, sn_ref, k_ref, v_ref):
    krr = _rope(kr_ref[...], cs_ref[...], sn_ref[...]).astype(k_ref.dtype)
    for h in range(MLA_HEADS):
        k_ref[:, h * QK_PAD:h * QK_PAD + HEAD] = kv_ref[:, h * 2 * HEAD:h * 2 * HEAD + HEAD].astype(k_ref.dtype)
        k_ref[:, h * QK_PAD + HEAD:(h + 1) * QK_PAD] = krr
        v_ref[:, h * HEAD:(h + 1) * HEAD] = kv_ref[:, h * 2 * HEAD + HEAD:(h + 1) * 2 * HEAD].astype(v_ref.dtype)


def _kv_pack(kv, p1, cs, sn, *, tm=256):
    t = kv.shape[0]
    tm = _tile(t, tm)
    tab = pl.BlockSpec((tm, 128), lambda i: (i, 0))
    return pl.pallas_call(
        functools.partial(_kv_pack_kernel),
        name="kv_pack",
        grid=(t // tm,),
        in_specs=[pl.BlockSpec((tm, MLA_HEADS * 2 * HEAD), lambda i: (i, 0)),
                  pl.BlockSpec((tm, 128), lambda i: (i, P1_KR // 128)), tab, tab],
        out_specs=[pl.BlockSpec((tm, MLA_HEADS * QK_PAD), lambda i: (i, 0)),
                   pl.BlockSpec((tm, MAIN), lambda i: (i, 0))],
        out_shape=[jax.ShapeDtypeStruct((t, MLA_HEADS * QK_PAD), BF16), jax.ShapeDtypeStruct((t, MAIN), BF16)],
        compiler_params=_cparams(("parallel",)),
    )(kv, p1, cs, sn)


def _kv_unpack_kernel(dk_ref, dv_ref, cs_ref, sn_ref, dkv_ref, dkr_ref):
    dkrr = jnp.zeros(dkr_ref.shape, F32)
    for h in range(MLA_HEADS):
        dkv_ref[:, h * 2 * HEAD:h * 2 * HEAD + HEAD] = dk_ref[:, h * QK_PAD:h * QK_PAD + HEAD].astype(dkv_ref.dtype)
        dkv_ref[:, h * 2 * HEAD + HEAD:(h + 1) * 2 * HEAD] = dv_ref[:, h * HEAD:(h + 1) * HEAD].astype(dkv_ref.dtype)
        dkrr = dkrr + dk_ref[:, h * QK_PAD + HEAD:(h + 1) * QK_PAD]
    dkr_ref[...] = _rope_t(dkrr, cs_ref[...], sn_ref[...]).astype(dkr_ref.dtype)


def _kv_unpack(dk, dv, cs, sn, *, tm=256):
    t = dk.shape[0]
    tm = _tile(t, tm)
    tab = pl.BlockSpec((tm, 128), lambda i: (i, 0))
    return pl.pallas_call(
        functools.partial(_kv_unpack_kernel),
        name="kv_unpack",
        grid=(t // tm,),
        in_specs=[pl.BlockSpec((tm, MLA_HEADS * QK_PAD), lambda i: (i, 0)),
                  pl.BlockSpec((tm, MAIN), lambda i: (i, 0)), tab, tab],
        out_specs=[pl.BlockSpec((tm, MLA_HEADS * 2 * HEAD), lambda i: (i, 0)), tab],
        out_shape=[jax.ShapeDtypeStruct((t, MLA_HEADS * 2 * HEAD), BF16), jax.ShapeDtypeStruct((t, 128), BF16)],
        compiler_params=_cparams(("parallel",)),
    )(dk, dv, cs, sn)


ATT_T = 256


def _causal(s, t):
    row = lax.broadcasted_iota(jnp.int32, (t, t), 0)
    col = lax.broadcasted_iota(jnp.int32, (t, t), 1)
    return jnp.where(col <= row, s, NEG)


def _attn_fwd_kernel(q_ref, k_ref, v_ref, o_ref, lse_ref, *, seq, t):
    for i in range(seq // t):
        own = slice(i * t, (i + 1) * t)
        q = q_ref[0, own, :]
        sd = _causal(_dot_nt(q, k_ref[0, own, :]) * MLA_SCALE, t)
        m = jnp.max(sd, axis=-1, keepdims=True)
        if i:
            so = _dot_nt(q, k_ref[0, :i * t, :]) * MLA_SCALE
            m = jnp.maximum(m, jnp.max(so, axis=-1, keepdims=True))
        pd = jnp.exp(sd - m)
        l = jnp.sum(pd, axis=-1, keepdims=True)
        acc = _dot(pd.astype(BF16), v_ref[0, own, :])
        if i:
            po = jnp.exp(so - m)
            l = l + jnp.sum(po, axis=-1, keepdims=True)
            acc = acc + _dot(po.astype(BF16), v_ref[0, :i * t, :])
        o_ref[0, own, :] = acc / l
        lse_ref[0, 0, own, :] = m + jnp.log(l)


def _attn_fwd(qb, kb, vb, bsz, seq):
    t = min(ATT_T, seq)
    q3 = qb.reshape(bsz, seq, MLA_HEADS * QK_PAD)
    k3 = kb.reshape(bsz, seq, MLA_HEADS * QK_PAD)
    v3 = vb.reshape(bsz, seq, MAIN)
    qk = pl.BlockSpec((1, seq, QK_PAD), lambda b, h: (b, 0, h))
    vv = pl.BlockSpec((1, seq, HEAD), lambda b, h: (b, 0, h))
    return pl.pallas_call(
        functools.partial(_attn_fwd_kernel, seq=seq, t=t),
        name="attn_fwd",
        grid=(bsz, MLA_HEADS),
        in_specs=[qk, qk, vv],
        out_specs=[vv, pl.BlockSpec((1, 1, seq, 1), lambda b, h: (b, h, 0, 0))],
        out_shape=[jax.ShapeDtypeStruct((bsz, seq, MAIN), F32),
                   jax.ShapeDtypeStruct((bsz, MLA_HEADS, seq, 1), F32)],
        compiler_params=_cparams(("parallel", "parallel")),
    )(q3, k3, v3)


def _attn_bwd_kernel(q_ref, k_ref, v_ref, o_ref, do_ref, lse_ref, dq_ref, dk_ref, dv_ref, delta_ref, *, seq, t):
    for r in range(0, seq, t):
        rows = slice(r, r + t)
        delta_ref[rows, :] = jnp.sum(do_ref[0, rows, :] * o_ref[0, rows, :], axis=-1, keepdims=True)
    dq_ref[...] = jnp.zeros(dq_ref.shape, F32)

    def piece(rows, k, v, masked):
        q = q_ref[0, rows, :]
        do = do_ref[0, rows, :].astype(BF16)
        s = _dot_nt(q, k) * MLA_SCALE
        if masked:
            s = _causal(s, t)
        p = jnp.exp(s - lse_ref[0, 0, rows, :])
        dp = _dot_nt(do, v)
        ds = (p * (dp - delta_ref[rows, :]) * MLA_SCALE).astype(BF16)
        dq_ref[0, rows, :] += _dot(ds, k)
        return _dot_tn(ds, q), _dot_tn(p.astype(BF16), do)

    for j in range(seq // t):
        own = slice(j * t, (j + 1) * t)
        k = k_ref[0, own, :]
        v = v_ref[0, own, :]
        dk, dv = piece(own, k, v, True)
        if (j + 1) * t < seq:
            dk2, dv2 = piece(slice((j + 1) * t, seq), k, v, False)
            dk, dv = dk + dk2, dv + dv2
        dk_ref[0, own, :] = dk
        dv_ref[0, own, :] = dv


def _attn_bwd(qb, kb, vb, o, do, lse, bsz, seq):
    t = min(ATT_T, seq)
    q3 = qb.reshape(bsz, seq, MLA_HEADS * QK_PAD)
    k3 = kb.reshape(bsz, seq, MLA_HEADS * QK_PAD)
    v3 = vb.reshape(bsz, seq, MAIN)
    qk = pl.BlockSpec((1, seq, QK_PAD), lambda b, h: (b, 0, h))
    vv = pl.BlockSpec((1, seq, HEAD), lambda b, h: (b, 0, h))
    dq, dk, dv = pl.pallas_call(
        functools.partial(_attn_bwd_kernel, seq=seq, t=t),
        name="attn_bwd",
        grid=(bsz, MLA_HEADS),
        in_specs=[qk, qk, vv, vv, vv, pl.BlockSpec((1, 1, seq, 1), lambda b, h: (b, h, 0, 0))],
        out_specs=[qk, qk, vv],
        out_shape=[jax.ShapeDtypeStruct((bsz, seq, MLA_HEADS * QK_PAD), F32),
                   jax.ShapeDtypeStruct((bsz, seq, MLA_HEADS * QK_PAD), F32),
                   jax.ShapeDtypeStruct((bsz, seq, MAIN), F32)],
        scratch_shapes=[pltpu.VMEM((seq, 1), F32)],
        compiler_params=_cparams(("parallel", "parallel")),
    )(q3, k3, v3, o, do.reshape(bsz, seq, MAIN), lse)
    n = bsz * seq
    return dq.reshape(n, -1), dk.reshape(n, -1), dv.reshape(n, -1)


def _adamw_kernel(w_ref, g_ref, m_ref, v_ref, d_ref, nm_ref, nv_ref):
    g = g_ref[...]
    m = ADAM_B1 * m_ref[...] + (1.0 - ADAM_B1) * g
    v = ADAM_B2 * v_ref[...] + (1.0 - ADAM_B2) * (g * g)
    m_hat = m / (1.0 - ADAM_B1 ** ADAM_STEP)
    v_hat = v / (1.0 - ADAM_B2 ** ADAM_STEP)
    d_ref[...] = -ADAM_LR * (m_hat / (jnp.sqrt(v_hat) + ADAM_EPS) + ADAM_WD * w_ref[...])
    nm_ref[...] = m
    nv_ref[...] = v


def _adamw(w, g, m, v, *, name):
    shape = w.shape
    c = shape[-1]
    r = w.size // c
    tr = r
    for cand in (512, 256, 128, 64, 32, 16, 8):
        if r % cand == 0 and cand * c * 4 <= 2 * 1024 * 1024:
            tr = cand
            break
    blk = pl.BlockSpec((tr, c), lambda i: (i, 0))
    outs = pl.pallas_call(
        functools.partial(_adamw_kernel),
        name=name,
        grid=(r // tr,),
        in_specs=[blk] * 4,
        out_specs=[blk] * 3,
        out_shape=[jax.ShapeDtypeStruct((r, c), F32)] * 3,
        compiler_params=_cparams(("parallel",)),
    )(w.reshape(r, c), g.reshape(r, c), m.reshape(r, c), v.reshape(r, c))
    return tuple(o.reshape(shape) for o in outs)


HBM = pl.BlockSpec(memory_space=pl.ANY)


def _place():
    return lax.axis_index("x"), lax.axis_index("y"), lax.axis_index("c")


def _other_chips(x, y):
    return [(1 - x, y), (x, 1 - y), (1 - x, 1 - y)]


class _Geom:
    def __init__(self, kind, rows, cols):
        self.kind, self.rows, self.cols, self.hr = kind, rows, cols, rows // 2
        self.full_shape = {"rows": (4 * rows, cols), "cols": (rows, 4 * cols), "chips": (4, rows, cols)}[kind]
        self.nt = self.hr // ADD_TILE

    def view(self, ref, s, h):
        if self.kind == "rows":
            return ref.at[pl.ds(s * self.rows + h * self.hr, self.hr), :]
        if self.kind == "cols":
            return ref.at[pl.ds(h * self.hr, self.hr), pl.ds(s * self.cols, self.cols)]
        return ref.at[s, pl.ds(h * self.hr, self.hr), :]

    def shard_half(self, ref, h):
        return ref.at[pl.ds(h * self.hr, self.hr), :]

    def put_shard(self, full, shard, chip):
        if self.kind == "rows":
            return lax.dynamic_update_slice(full, shard, (chip * self.rows, 0))
        if self.kind == "cols":
            return lax.dynamic_update_slice(full, shard, (0, chip * self.cols))
        return lax.dynamic_update_slice(full, shard[None], (chip, 0, 0))

    def tile_spec(self, chip_half_of):
        if self.kind == "rows":
            def index(*a):
                s, h, i = chip_half_of(*a)
                return (s * (self.rows // ADD_TILE) + h * self.nt + i, 0)
            return pl.BlockSpec((ADD_TILE, self.cols), index)
        if self.kind == "cols":
            def index(*a):
                s, h, i = chip_half_of(*a)
                return (h * self.nt + i, s)
            return pl.BlockSpec((ADD_TILE, self.cols), index)

        def index(*a):
            s, h, i = chip_half_of(*a)
            return (s, h * self.nt + i, 0)
        return pl.BlockSpec((None, ADD_TILE, self.cols), index)


GEOMS = (("w_mem_kv0", _Geom("rows", 256, 1024)), ("w_mem_kv1", _Geom("rows", 256, 1024)),
         ("w_out0", _Geom("rows", 512, 1024)), ("w_out1", _Geom("rows", 512, 1024)),
         ("conv_w_in", _Geom("cols", 1024, 1408)), ("mla_w_ukv", _Geom("cols", 256, 768)),
         ("mla_w_in", _Geom("chips", 1024, 848)), ("mla_w_uq", _Geom("chips", 512, 576)))
N_BIG = len(GEOMS)


def _remote(k, src_ref, dst_ref, to, ssem, rsem):
    return pltpu.make_async_remote_copy(src_ref=src_ref, dst_ref=dst_ref, send_sem=ssem.at[k], recv_sem=rsem.at[k],
                                        device_id=to, device_id_type=MESH)


def _allgather_kernel(*refs):
    srcs, outs, (ssem, rsem) = refs[:N_BIG], refs[N_BIG:2 * N_BIG], refs[2 * N_BIG:]
    x, y, c = _place()
    s = 2 * x + y
    chips = _other_chips(x, y)
    sibling = (x, y, 1 - c)
    first = []
    for w, (_, g) in enumerate(GEOMS):
        for j, chip in enumerate(chips):
            cp = _remote(6 * w + j, g.shard_half(srcs[w], c), g.view(outs[w], s, c), (*chip, c), ssem, rsem)
            cp.start()
            first.append(cp)
    passed = []
    for w, (_, g) in enumerate(GEOMS):
        for j, chip in enumerate(chips):
            blk = g.view(outs[w], 2 * chip[0] + chip[1], c)
            _remote(6 * w + j, blk, blk, (*chip, c), ssem, rsem).wait_recv()
            cp = _remote(6 * w + 3 + j, blk, blk, sibling, ssem, rsem)
            cp.start()
            passed.append(cp)
    for w, (_, g) in enumerate(GEOMS):
        for j, chip in enumerate(chips):
            blk = g.view(outs[w], 2 * chip[0] + chip[1], 1 - c)
            _remote(6 * w + 3 + j, blk, blk, sibling, ssem, rsem).wait_recv()
    for cp in first + passed:
        cp.wait_send()


def _allgather_weights(shards, chip):
    others = pl.pallas_call(
        functools.partial(_allgather_kernel),
        name="allgather_weights",
        in_specs=[HBM] * N_BIG,
        out_specs=[HBM] * N_BIG,
        out_shape=[jax.ShapeDtypeStruct(g.full_shape, BF16) for _, g in GEOMS],
        scratch_shapes=[pltpu.SemaphoreType.DMA((6 * N_BIG,)), pltpu.SemaphoreType.DMA((6 * N_BIG,))],
    )(*shards)
    return [g.put_shard(o, sh, chip) for (_, g), o, sh in zip(GEOMS, others, shards)]


def _swap_halves_kernel(*refs):
    srcs, dsts, (ssem, rsem) = refs[:N_BIG], refs[N_BIG:2 * N_BIG], refs[2 * N_BIG:]
    x, y, c = _place()
    cps = []
    for w, (_, g) in enumerate(GEOMS):
        for s in range(4):
            cps.append(_remote(4 * w + s, g.view(srcs[w], s, 1 - c), dsts[w].at[s], (x, y, 1 - c), ssem, rsem))
    for cp in cps:
        cp.start()
    for cp in cps:
        cp.wait()


def _swap_halves(gb):
    return pl.pallas_call(
        functools.partial(_swap_halves_kernel),
        name="rs_sibling_swap",
        in_specs=[HBM] * N_BIG,
        out_specs=[HBM] * N_BIG,
        out_shape=[jax.ShapeDtypeStruct((4, g.hr, g.cols), BF16) for _, g in GEOMS],
        scratch_shapes=[pltpu.SemaphoreType.DMA((4 * N_BIG,)), pltpu.SemaphoreType.DMA((4 * N_BIG,))],
    )(*gb)


def _chip_exchange_kernel(*refs):
    srcs, dsts, (ssem, rsem) = refs[:N_BIG], refs[N_BIG:2 * N_BIG], refs[2 * N_BIG:]
    x, y, c = _place()
    cps = []
    for w in range(N_BIG):
        for j, chip in enumerate(_other_chips(x, y)):
            cps.append(_remote(3 * w + j, srcs[w].at[2 * chip[0] + chip[1]], dsts[w].at[j], (*chip, c), ssem, rsem))
    for cp in cps:
        cp.start()
    for cp in cps:
        cp.wait()


def _chip_exchange(pairs):
    return pl.pallas_call(
        functools.partial(_chip_exchange_kernel),
        name="rs_chip_exchange",
        in_specs=[HBM] * N_BIG,
        out_specs=[HBM] * N_BIG,
        out_shape=[jax.ShapeDtypeStruct((3,) + p.shape[1:], p.dtype) for p in pairs],
        scratch_shapes=[pltpu.SemaphoreType.DMA((3 * N_BIG,)), pltpu.SemaphoreType.DMA((3 * N_BIG,))],
    )(*pairs)


def _share_kernel(*refs):
    srcs, dsts, (ssem, rsem) = refs[:N_BIG], refs[N_BIG:2 * N_BIG], refs[2 * N_BIG:]
    x, y, c = _place()
    cps = [_remote(w, srcs[w], dsts[w], (x, y, 1 - c), ssem, rsem) for w in range(N_BIG)]
    for cp in cps:
        cp.start()
    for cp in cps:
        cp.wait()


def _share_with_sibling(halves):
    return pl.pallas_call(
        functools.partial(_share_kernel),
        name="rs_share_halves",
        in_specs=[HBM] * N_BIG,
        out_specs=[HBM] * N_BIG,
        out_shape=[jax.ShapeDtypeStruct(h.shape, h.dtype) for h in halves],
        scratch_shapes=[pltpu.SemaphoreType.DMA((N_BIG,)), pltpu.SemaphoreType.DMA((N_BIG,))],
    )(*halves)


def _gather_sum_kernel(src, gat, tot, ssem, rsem):
    x, y, c = _place()
    me = 4 * x + 2 * y + c
    gat[me] = src[...]
    flips = [(dx, dy, dc) for dx in (0, 1) for dy in (0, 1) for dc in (0, 1)][1:]
    cps = []
    for k, (dx, dy, dc) in enumerate(flips):
        peer = (1 - x if dx else x, 1 - y if dy else y, 1 - c if dc else c)
        cp = pltpu.make_async_remote_copy(src_ref=src, dst_ref=gat.at[me], send_sem=ssem.at[k], recv_sem=rsem.at[k],
                                          device_id=peer, device_id_type=MESH)
        cp.start()
        cps.append((cp, 4 * peer[0] + 2 * peer[1] + peer[2], peer))
    for k, (cp, idx, peer) in enumerate(cps):
        pltpu.make_async_remote_copy(src_ref=src, dst_ref=gat.at[idx], send_sem=ssem.at[k], recv_sem=rsem.at[k],
                                     device_id=peer, device_id_type=MESH).wait_recv()
    for cp, _, _ in cps:
        cp.wait_send()
    acc = gat[0]
    for d in range(1, 8):
        acc = acc + gat[d]
    tot[...] = acc


def _gather_sum_small(a, *, name):
    vm = pl.BlockSpec(memory_space=pltpu.VMEM)
    return pl.pallas_call(
        functools.partial(_gather_sum_kernel),
        name=name,
        in_specs=[vm],
        out_specs=[vm, vm],
        out_shape=[jax.ShapeDtypeStruct((8,) + a.shape, a.dtype), jax.ShapeDtypeStruct(a.shape, a.dtype)],
        scratch_shapes=[pltpu.SemaphoreType.DMA((7,)), pltpu.SemaphoreType.DMA((7,))],
    )(a)


def _add_pairs_kernel(c_ref, g_ref, r_ref, o_ref):
    o_ref[...] = (g_ref[...] + r_ref[...].astype(F32)).astype(o_ref.dtype)


def _add_pairs(geom, core, g, recv, *, name):
    half = pl.BlockSpec((None, ADD_TILE, geom.cols), lambda s, i, cr: (s, i, 0))
    return pl.pallas_call(
        functools.partial(_add_pairs_kernel),
        name=name,
        grid_spec=pltpu.PrefetchScalarGridSpec(
            num_scalar_prefetch=1,
            grid=(4, geom.nt),
            in_specs=[geom.tile_spec(lambda s, i, cr: (s, cr[0], i)), half],
            out_specs=half,
        ),
        out_shape=jax.ShapeDtypeStruct(recv.shape, BF16),
        compiler_params=_cparams(("parallel", "parallel")),
    )(core, g, recv)


def _add_final_kernel(sc_ref, g_ref, r_ref, e_ref, o_ref):
    acc = g_ref[...] + r_ref[...].astype(F32)
    for j in range(3):
        acc = acc + e_ref[j].astype(F32)
    o_ref[...] = acc


def _add_final(geom, chip_core, g, recv, exch, *, name):
    return pl.pallas_call(
        functools.partial(_add_final_kernel),
        name=name,
        grid_spec=pltpu.PrefetchScalarGridSpec(
            num_scalar_prefetch=1,
            grid=(geom.nt,),
            in_specs=[geom.tile_spec(lambda i, sc: (sc[0], sc[1], i)),
                      pl.BlockSpec((None, ADD_TILE, geom.cols), lambda i, sc: (sc[0], i, 0)),
                      pl.BlockSpec((3, ADD_TILE, geom.cols), lambda i, sc: (0, i, 0))],
            out_specs=pl.BlockSpec((ADD_TILE, geom.cols), lambda i, sc: (i, 0)),
        ),
        out_shape=jax.ShapeDtypeStruct((geom.hr, geom.cols), F32),
        compiler_params=_cparams(("parallel",)),
    )(chip_core, g, recv, exch)


def _chip_major(w):
    return w.reshape(w.shape[0], 4, w.shape[1] // 4).transpose(1, 0, 2)


def _from_chip_major(w):
    return w.transpose(1, 0, 2).reshape(w.shape[1], 4 * w.shape[2])


def _mla_in_to_internal(w):
    return jnp.concatenate([w[:, 1344:], w[:, :512], w[:, 832:1344], w[:, 512:768], w[:, 768:832],
                            jnp.zeros((w.shape[0], 64), w.dtype)], axis=1)


def _mla_in_from_internal(w):
    return jnp.concatenate([w[:, P1_CQ:P1_QM], w[:, P1_CKV:P1_KR], w[:, P1_KR:P1_KR + 64], w[:, P1_QM:P1_CKV],
                            w[:, :MIX]], axis=1)


def _uq_to_internal(w):
    w = w.reshape(w.shape[0], MLA_HEADS, HEAD + ROPE)
    return jnp.pad(w, ((0, 0), (0, 0), (0, QK_PAD - HEAD - ROPE))).reshape(w.shape[0], MLA_HEADS * QK_PAD)


def _uq_from_internal(w):
    return w.reshape(w.shape[0], MLA_HEADS, QK_PAD)[:, :, :HEAD + ROPE].reshape(w.shape[0], MLA_HEADS * (HEAD + ROPE))


SMALL_GRAD_ROWS = 456


def _rows128(a, rows):
    flat = a.reshape(-1)
    return jnp.pad(flat, (0, rows * 128 - flat.shape[0])).reshape(rows, 128)


def kernel(x, mem, positions, norm_g, mem_norm_g, w_mem_kv, w_out, conv_w_in, conv_dw, conv_dw_b, conv_ln_g, conv_ln_b, mla_w_in, mla_q_norm_g, mla_w_uq, mla_kv_norm_g, mla_w_ukv, final_norm_g, loss_target, m_norm_g, m_mem_norm_g, m_w_mem_kv, m_w_out, m_conv_w_in, m_conv_dw, m_conv_dw_b, m_conv_ln_g, m_conv_ln_b, m_mla_w_in, m_mla_q_norm_g, m_mla_w_uq, m_mla_kv_norm_g, m_mla_w_ukv, m_final_norm_g, v_norm_g, v_mem_norm_g, v_w_mem_kv, v_w_out, v_conv_w_in, v_conv_dw, v_conv_dw_b, v_conv_ln_g, v_conv_ln_b, v_mla_w_in, v_mla_q_norm_g, v_mla_w_uq, v_mla_kv_norm_g, v_mla_w_ukv, v_final_norm_g):
    bsz, seq, d = x.shape
    n = bsz * seq
    mlen = mem.shape[1]
    ax, ay, ac = _place()
    chip = 2 * ax + ay

    shards = dict(w_mem_kv0=w_mem_kv[0], w_mem_kv1=w_mem_kv[1], w_out0=w_out[0], w_out1=w_out[1],
                  conv_w_in=conv_w_in[0], mla_w_ukv=mla_w_ukv[0], mla_w_in=mla_w_in[0], mla_w_uq=mla_w_uq[0])
    wf = dict(zip([nm for nm, _ in GEOMS],
                  _allgather_weights([shards[nm].astype(BF16) for nm, _ in GEOMS], chip)))
    w_conv_in = wf["conv_w_in"]
    w_mla_in = _mla_in_to_internal(_from_chip_major(wf["mla_w_in"]))
    w_uq = _uq_to_internal(_from_chip_major(wf["mla_w_uq"]))
    w_ukv = wf["mla_w_ukv"]
    w_memkv = [wf["w_mem_kv0"], wf["w_mem_kv1"]]
    w_o = [wf["w_out0"], wf["w_out1"]]
    w_conv_in_t, w_mla_in_t, w_uq_t, w_ukv_t = w_conv_in.T, w_mla_in.T, w_uq.T, w_ukv.T
    w_memkv_t = [w.T for w in w_memkv]
    w_o_t = [w.T for w in w_o]

    small_in = jnp.concatenate([_rows128(conv_dw[0], 93), _rows128(mla_q_norm_g, 1), _rows128(mla_kv_norm_g, 1),
                                jnp.zeros((1, 128), F32)], axis=0)
    small_all, _ = _gather_sum_small(small_in, name="gather_small_params")
    small_all = small_all[0::2]
    dw_full = small_all[:, :93].reshape(4, -1)[:, :CONV_K * 384].reshape(4, CONV_K, 384)
    dw_full = dw_full.transpose(1, 0, 2).reshape(CONV_K, MAIN)
    qg_full = small_all[:, 93].reshape(Q_RANK)
    kvg_full = small_all[:, 94, :64].reshape(KV_RANK)

    inv_freq = 1.0 / (ROPE_THETA ** (jnp.arange(0, ROPE, 2, dtype=F32) / ROPE))
    ang = positions.astype(F32).reshape(n, 1) * inv_freq
    cos, sin, zer = jnp.cos(ang), jnp.sin(ang), jnp.zeros((n, 64), F32)
    rope_c = jnp.concatenate([cos, cos, zer], axis=1)
    rope_s = jnp.concatenate([-sin, sin, zer], axis=1)

    x2 = x.reshape(n, d)
    mem2 = mem.reshape(bsz * mlen, d)
    tgt2 = loss_target.reshape(n, d)

    memn = [_rms_fwd(mem2, mem_norm_g[i], name=f"mem_norm{i}") for i in range(2)]
    kvm = [_matmul(memn[i], w_memkv[i], out_dtype=BF16, name=f"mem_kv{i}").reshape(bsz, mlen, 2 * MEMW)
           for i in range(2)]

    u0 = _rms_fwd(x2, norm_g[0], name="norm0")
    p0 = _matmul(u0, w_conv_in, name="conv_in_proj")
    hc = _conv_fwd(p0, dw_full, conv_dw_b, bsz, seq).reshape(n, MAIN)
    ymem0 = _mem_fwd(p0, kvm[0], P0_QM // MEMW, bsz, seq, name="mem_attn0").reshape(n, MEMW)
    y0 = _gate_fwd(hc, ymem0, p0, P0_Z // GATE_CHUNK, conv_ln_g, conv_ln_b, name="gate0")
    h1 = _matmul(y0, w_o[0], x2, name="out_proj0")

    u1 = _rms_fwd(h1, norm_g[1], name="norm1")
    p1 = _matmul(u1, w_mla_in, name="mla_in_proj")
    cqn = _rms_fwd(p1, qg_full, width=Q_RANK, col_block=P1_CQ // Q_RANK, name="q_norm")
    ckvn = _rms_fwd(p1, kvg_full, width=KV_RANK, col_block=P1_CKV // KV_RANK, name="kv_norm")
    q = _matmul(cqn, w_uq, name="q_up")
    kv = _matmul(ckvn, w_ukv, name="kv_up")
    qb = _q_rope(q, rope_c, rope_s, inverse=False, name="q_rope")
    kb, vb = _kv_pack(kv, p1, rope_c, rope_s)
    o1, lse = _attn_fwd(qb, kb, vb, bsz, seq)
    o1 = o1.reshape(n, MAIN)
    ymem1 = _mem_fwd(p1, kvm[1], P1_QM // MEMW, bsz, seq, name="mem_attn1").reshape(n, MEMW)
    y1 = _gate_fwd(o1, ymem1, p1, P1_Z // GATE_CHUNK, name="gate1")
    h2 = _matmul(y1, w_o[1], h1, name="out_proj1")

    dh2, d_final_g, loss_part = _final_loss(h2, final_norm_g, tgt2)
    loss = lax.psum(loss_part[0, 0], ("x", "y", "c"))

    gbig = {}
    gbig["w_out1"] = _matmul(y1, dh2, ta=True, also_bf16=True, name="d_w_out1")
    dy1 = _matmul(dh2, w_o_t[1], name="d_y1")
    do1, dymem1, dz1 = _gate_bwd(dy1, o1, ymem1, p1, P1_Z // GATE_CHUNK, name="gate1_bwd")
    dqm1, dkvm1 = _mem_bwd(p1, kvm[1], dymem1, P1_QM // MEMW, bsz, seq, name="mem_attn1_bwd")
    dq, dk, dv = _attn_bwd(qb, kb, vb, o1.reshape(bsz, seq, MAIN), do1, lse, bsz, seq)
    dqb = _q_rope(dq, rope_c, rope_s, inverse=True, name="q_rope_bwd")
    dkv, dkr = _kv_unpack(dk, dv, rope_c, rope_s)
    g_w_uq = _matmul(cqn, dqb, ta=True, name="d_w_uq")
    dcqn = _matmul(dqb, w_uq_t, name="d_cqn")
    gbig["mla_w_ukv"] = _matmul(ckvn, dkv, ta=True, also_bf16=True, name="d_w_ukv")
    dckvn = _matmul(dkv, w_ukv_t, name="d_ckvn")
    dcq, g_qg = _rms_bwd(p1, qg_full, dcqn, width=Q_RANK, col_block=P1_CQ // Q_RANK, out_dtype=BF16,
                         name="q_norm_bwd")
    dckv, g_kvg = _rms_bwd(p1, kvg_full, dckvn, width=KV_RANK, col_block=P1_CKV // KV_RANK, out_dtype=BF16,
                           name="kv_norm_bwd")
    dp1 = jnp.concatenate([dz1, dcq, dqm1.reshape(n, MEMW), dckv, dkr], axis=1)
    g_w_mla_in = _matmul(u1, dp1, ta=True, name="d_w_mla_in")
    du1 = _matmul(dp1, w_mla_in_t, name="d_u1")
    dh1, g_norm1 = _rms_bwd(h1, norm_g[1], du1, dh2, name="norm1_bwd")

    gbig["w_out0"] = _matmul(y0, dh1, ta=True, also_bf16=True, name="d_w_out0")
    dy0 = _matmul(dh1, w_o_t[0], name="d_y0")
    dhc, dymem0, dz0, g_ln_g, g_ln_b = _gate_bwd(dy0, hc, ymem0, p0, P0_Z // GATE_CHUNK, conv_ln_g, conv_ln_b,
                                                 name="gate0_bwd")
    dqm0, dkvm0 = _mem_bwd(p0, kvm[0], dymem0, P0_QM // MEMW, bsz, seq, name="mem_attn0_bwd")
    da, dg, g_dw32 = _conv_bwd(p0, dhc.reshape(bsz, seq, MAIN), dw_full, bsz, seq)
    dp0 = jnp.concatenate([da.reshape(n, MAIN), dg.reshape(n, MAIN), dqm0.reshape(n, MEMW), dz0], axis=1)
    gbig["conv_w_in"] = _matmul(u0, dp0, ta=True, also_bf16=True, name="d_w_conv_in")
    du0 = _matmul(dp0, w_conv_in_t, name="d_u0")
    grad_x, g_norm0 = _rms_bwd(x2, norm_g[0], du0, dh1, name="norm0_bwd")

    g_mem_g = []
    for i, dkvm in enumerate((dkvm0, dkvm1)):
        dk2 = dkvm.reshape(bsz * mlen, 2 * MEMW)
        gbig[f"w_mem_kv{i}"] = _matmul(memn[i], dk2, ta=True, also_bf16=True, name=f"d_w_mem_kv{i}")
        dmemn = _matmul(dk2, w_memkv_t[i], name=f"d_memn{i}")
        g_mem_g.append(_rms_bwd(mem2, mem_norm_g[i], dmemn, name=f"mem_norm{i}_bwd")[1])

    for nm, g_int, back in (("mla_w_in", g_w_mla_in, _mla_in_from_internal), ("mla_w_uq", g_w_uq, _uq_from_internal)):
        g_cm = _chip_major(back(g_int))
        gbig[nm] = (g_cm, g_cm.astype(BF16))
    g32 = [gbig[nm][0] for nm, _ in GEOMS]
    from_sibling = _swap_halves([gbig[nm][1] for nm, _ in GEOMS])
    core = jnp.reshape(ac, (1,)).astype(jnp.int32)
    pairs = [_add_pairs(g, core, g32[w], from_sibling[w], name=f"rs_add_pairs_{nm}")
             for w, (nm, g) in enumerate(GEOMS)]
    exch = _chip_exchange(pairs)
    chip_core = jnp.stack([chip, ac]).astype(jnp.int32)
    mine = [_add_final(g, chip_core, g32[w], from_sibling[w], exch[w], name=f"rs_add_final_{nm}")
            for w, (nm, g) in enumerate(GEOMS)]
    theirs = _share_with_sibling(mine)
    red = {}
    for (nm, _), a, b in zip(GEOMS, mine, theirs):
        red[nm] = jnp.concatenate([jnp.where(ac == 0, a, b), jnp.where(ac == 0, b, a)], axis=0)
    red["w_mem_kv"] = jnp.stack([red["w_mem_kv0"], red["w_mem_kv1"]])
    red["w_out"] = jnp.stack([red["w_out0"], red["w_out1"]])

    small_g = jnp.concatenate([
        _rows128(jnp.concatenate([g_norm0, g_norm1], axis=0), 16), _rows128(jnp.concatenate(g_mem_g, axis=0), 16),
        _rows128(g_dw32[CONV_K], 12), _rows128(g_ln_g, 12), _rows128(g_ln_b, 12), _rows128(d_final_g, 8),
        _rows128(g_dw32[:CONV_K], 372), _rows128(g_qg, 4), _rows128(g_kvg, 2), jnp.zeros((2, 128), F32)], axis=0)
    _, small_sum = _gather_sum_small(small_g, name="allreduce_small_grads")
    flat = small_sum.reshape(-1)

    def take(off, shape):
        size = 1
        for s_ in shape:
            size *= s_
        return flat[off * 128:off * 128 + size].reshape(shape)

    grads = dict(red)
    grads["conv_w_in"] = red["conv_w_in"][None]
    grads["mla_w_in"] = red["mla_w_in"][None]
    grads["mla_w_uq"] = red["mla_w_uq"][None]
    grads["mla_w_ukv"] = red["mla_w_ukv"][None]
    grads["norm_g"] = take(0, (2, D_MODEL))
    grads["mem_norm_g"] = take(16, (2, D_MODEL))
    grads["conv_dw_b"] = take(32, (1, MAIN))
    grads["conv_ln_g"] = take(44, (1, MAIN))
    grads["conv_ln_b"] = take(56, (1, MAIN))
    grads["final_norm_g"] = take(68, (D_MODEL,))
    grads["conv_dw"] = lax.dynamic_slice_in_dim(take(76, (CONV_K, MAIN)), chip * 384, 384, axis=1)[None]
    grads["mla_q_norm_g"] = lax.dynamic_slice_in_dim(take(448, (Q_RANK,)), chip * 128, 128, axis=0)[None]
    grads["mla_kv_norm_g"] = lax.dynamic_slice_in_dim(take(452, (KV_RANK,)), chip * 64, 64, axis=0)[None]

    params = dict(norm_g=norm_g, mem_norm_g=mem_norm_g, w_mem_kv=w_mem_kv, w_out=w_out, conv_w_in=conv_w_in,
                  conv_dw=conv_dw, conv_dw_b=conv_dw_b, conv_ln_g=conv_ln_g, conv_ln_b=conv_ln_b, mla_w_in=mla_w_in,
                  mla_q_norm_g=mla_q_norm_g, mla_w_uq=mla_w_uq, mla_kv_norm_g=mla_kv_norm_g, mla_w_ukv=mla_w_ukv,
                  final_norm_g=final_norm_g)
    mom1 = dict(norm_g=m_norm_g, mem_norm_g=m_mem_norm_g, w_mem_kv=m_w_mem_kv, w_out=m_w_out, conv_w_in=m_conv_w_in,
                conv_dw=m_conv_dw, conv_dw_b=m_conv_dw_b, conv_ln_g=m_conv_ln_g, conv_ln_b=m_conv_ln_b,
                mla_w_in=m_mla_w_in, mla_q_norm_g=m_mla_q_norm_g, mla_w_uq=m_mla_w_uq,
                mla_kv_norm_g=m_mla_kv_norm_g, mla_w_ukv=m_mla_w_ukv, final_norm_g=m_final_norm_g)
    mom2 = dict(norm_g=v_norm_g, mem_norm_g=v_mem_norm_g, w_mem_kv=v_w_mem_kv, w_out=v_w_out, conv_w_in=v_conv_w_in,
                conv_dw=v_conv_dw, conv_dw_b=v_conv_dw_b, conv_ln_g=v_conv_ln_g, conv_ln_b=v_conv_ln_b,
                mla_w_in=v_mla_w_in, mla_q_norm_g=v_mla_q_norm_g, mla_w_uq=v_mla_w_uq,
                mla_kv_norm_g=v_mla_kv_norm_g, mla_w_ukv=v_mla_w_ukv, final_norm_g=v_final_norm_g)
    names = list(params)
    g_out, deltas, new_m, new_v = [], [], [], []
    for nm in names:
        w = params[nm]
        g = grads[nm].reshape(w.shape)
        w2 = w.reshape(1, -1) if w.ndim == 1 else w
        dlt, m_new, v_new = _adamw(w2, g.reshape(w2.shape), mom1[nm].reshape(w2.shape), mom2[nm].reshape(w2.shape),
                                   name=f"adamw_{nm}")
        g_out.append(g)
        deltas.append(dlt.reshape(w.shape))
        new_m.append(m_new.reshape(w.shape))
        new_v.append(v_new.reshape(w.shape))

    return (loss, grad_x.reshape(bsz, seq, d), *g_out, *deltas, *new_m, *new_v)
```

```python
import functools

import jax
import jax.numpy as jnp
from jax import lax
from jax.experimental import pallas as pl
from jax.experimental.pallas import tpu as pltpu

F32 = jnp.float32
BF16 = jnp.bfloat16
MESH = pl.DeviceIdType.MESH

D_MODEL = 1024
MIX = 2048
MAIN = 1536
MEMW = 512
MEM_HEADS = 4
HEAD = 128
CONV_K = 31
CONV_PAD = 32
MLA_HEADS = 12
ROPE = 64
QK_PAD = 256
Q_RANK = 512
KV_RANK = 256
ROPE_THETA = 10000.0
RMS_EPS = 1e-6
LN_EPS = 1e-5
MEM_SCALE = HEAD ** -0.5
MLA_SCALE = (HEAD + ROPE) ** -0.5
NEG = -1e30

P0_COLS = 5632
P0_A, P0_G, P0_QM, P0_Z = 0, 1536, 3072, 3584
P1_COLS = 3456
P1_Z, P1_CQ, P1_QM, P1_CKV, P1_KR = 0, 2048, 2560, 3072, 3328

ADAM_LR = 0.001
ADAM_B1 = 0.9
ADAM_B2 = 0.999
ADAM_EPS = 1e-08
ADAM_WD = 0.01
ADAM_STEP = 10

VMEM_LIMIT = 56 * 1024 * 1024

ADD_TILE = 128


def _cparams(sem=None):
    return pltpu.CompilerParams(dimension_semantics=sem, vmem_limit_bytes=VMEM_LIMIT)


def _tile(n, pref):
    if n <= pref:
        return n
    t = (pref // 128) * 128
    while t > 128 and n % t:
        t -= 128
    assert n % t == 0, (n, pref)
    return t


def _mm_kernel(*refs, nk, ta, has_res):
    if has_res:
        a_ref, b_ref, r_ref, *o_refs, acc_ref = refs
    else:
        a_ref, b_ref, *o_refs, acc_ref = refs
    k = pl.program_id(2)
    a = a_ref[...].astype(BF16)
    b = b_ref[...].astype(BF16)
    dn = (((0 if ta else 1,), (0,)), ((), ()))
    p = lax.dot_general(a, b, dn, preferred_element_type=F32)

    def finish(acc):
        if has_res:
            acc = r_ref[...] + acc
        for o in o_refs:
            o[...] = acc.astype(o.dtype)

    if nk == 1:
        finish(p)
        return

    @pl.when(k == 0)
    def _():
        acc_ref[...] = p

    @pl.when(jnp.logical_and(k > 0, k < nk - 1))
    def _():
        acc_ref[...] += p

    @pl.when(k == nk - 1)
    def _():
        finish(acc_ref[...] + p)


def _matmul(a, b, res=None, *, name, ta=False, out_dtype=F32, also_bf16=False, tm=1024, tn=512, tk=2048):
    m, kd = (a.shape[1], a.shape[0]) if ta else a.shape
    n = b.shape[1]
    assert kd == b.shape[0]
    tm, tn, tk = _tile(m, tm), _tile(n, tn), _tile(kd, tk)
    nk = kd // tk
    a_spec = (pl.BlockSpec((tk, tm), lambda i, j, k: (k, i)) if ta
              else pl.BlockSpec((tm, tk), lambda i, j, k: (i, k)))
    out_spec = pl.BlockSpec((tm, tn), lambda i, j, k: (i, j))
    in_specs = [a_spec, pl.BlockSpec((tk, tn), lambda i, j, k: (k, j))]
    args = [a, b]
    if res is not None:
        in_specs.append(out_spec)
        args.append(res)
    out_shape = [jax.ShapeDtypeStruct((m, n), out_dtype)]
    if also_bf16:
        out_shape.append(jax.ShapeDtypeStruct((m, n), BF16))
    outs = pl.pallas_call(
        functools.partial(_mm_kernel, nk=nk, ta=ta, has_res=res is not None),
        name=name,
        grid=(m // tm, n // tn, nk),
        in_specs=in_specs,
        out_specs=[out_spec] * len(out_shape),
        out_shape=out_shape,
        scratch_shapes=[pltpu.VMEM((tm, tn), F32)],
        compiler_params=_cparams(("parallel", "parallel", "arbitrary")),
    )(*args)
    return tuple(outs) if also_bf16 else outs[0]


def _rms_fwd_kernel(h_ref, g_ref, o_ref):
    h = h_ref[...]
    rstd = lax.rsqrt(jnp.mean(h * h, axis=-1, keepdims=True) + RMS_EPS)
    o_ref[...] = (h * rstd * g_ref[...]).astype(o_ref.dtype)


def _rms_fwd(h, g, *, name, width=None, col_block=0, tm=512):
    t = h.shape[0]
    width = width or h.shape[1]
    tm = _tile(t, tm)
    return pl.pallas_call(
        functools.partial(_rms_fwd_kernel),
        name=name,
        grid=(t // tm,),
        in_specs=[pl.BlockSpec((tm, width), lambda i: (i, col_block)),
                  pl.BlockSpec((1, width), lambda i: (0, 0))],
        out_specs=pl.BlockSpec((tm, width), lambda i: (i, 0)),
        out_shape=jax.ShapeDtypeStruct((t, width), BF16),
        compiler_params=_cparams(("parallel",)),
    )(h, g.reshape(1, width))


def _rms_bwd_math(h, g, du):
    rstd = lax.rsqrt(jnp.mean(h * h, axis=-1, keepdims=True) + RMS_EPS)
    dug = du * g
    dh = rstd * dug - h * (rstd * rstd * rstd) * jnp.mean(dug * h, axis=-1, keepdims=True)
    dg = jnp.sum(du * h * rstd, axis=0, keepdims=True)
    return dh, dg


def _rms_bwd_kernel(*refs, has_res):
    if has_res:
        h_ref, g_ref, du_ref, res_ref, dh_ref, dg_ref = refs
    else:
        h_ref, g_ref, du_ref, dh_ref, dg_ref = refs
    dh, dg = _rms_bwd_math(h_ref[...], g_ref[...], du_ref[...].astype(F32))
    if has_res:
        dh = dh + res_ref[...]
    dh_ref[...] = dh.astype(dh_ref.dtype)

    @pl.when(pl.program_id(0) == 0)
    def _():
        dg_ref[...] = dg

    @pl.when(pl.program_id(0) > 0)
    def _():
        dg_ref[...] += dg


def _rms_bwd(h, g, du, res=None, *, name, width=None, col_block=0, out_dtype=F32, tm=512):
    t = h.shape[0]
    width = width or h.shape[1]
    tm = _tile(t, tm)
    row = pl.BlockSpec((tm, width), lambda i: (i, 0))
    in_specs = [pl.BlockSpec((tm, width), lambda i: (i, col_block)),
                pl.BlockSpec((1, width), lambda i: (0, 0)), row]
    args = [h, g.reshape(1, width), du]
    if res is not None:
        in_specs.append(row)
        args.append(res)
    return pl.pallas_call(
        functools.partial(_rms_bwd_kernel, has_res=res is not None),
        name=name,
        grid=(t // tm,),
        in_specs=in_specs,
        out_specs=[row, pl.BlockSpec((1, width), lambda i: (0, 0))],
        out_shape=[jax.ShapeDtypeStruct((t, width), out_dtype), jax.ShapeDtypeStruct((1, width), F32)],
        compiler_params=_cparams(("arbitrary",)),
    )(*args)


def _final_kernel(h_ref, g_ref, t_ref, dh_ref, dg_ref, loss_ref):
    h = h_ref[...]
    g = g_ref[...]
    rstd = lax.rsqrt(jnp.mean(h * h, axis=-1, keepdims=True) + RMS_EPS)
    e = h * rstd * g - t_ref[...]
    part = 0.5 * jnp.sum(jnp.mean(e * e, axis=-1, keepdims=True), axis=0, keepdims=True)
    dh, dg = _rms_bwd_math(h, g, e * (1.0 / D_MODEL))
    dh_ref[...] = dh
    part = jnp.broadcast_to(part, loss_ref.shape)

    @pl.when(pl.program_id(0) == 0)
    def _():
        dg_ref[...] = dg
        loss_ref[...] = part

    @pl.when(pl.program_id(0) > 0)
    def _():
        dg_ref[...] += dg
        loss_ref[...] += part


def _final_loss(h, g, target, *, tm=512):
    t, d = h.shape
    tm = _tile(t, tm)
    row = pl.BlockSpec((tm, d), lambda i: (i, 0))
    return pl.pallas_call(
        functools.partial(_final_kernel),
        name="final_loss",
        grid=(t // tm,),
        in_specs=[row, pl.BlockSpec((1, d), lambda i: (0, 0)), row],
        out_specs=[row, pl.BlockSpec((1, d), lambda i: (0, 0)), pl.BlockSpec((1, 128), lambda i: (0, 0))],
        out_shape=[jax.ShapeDtypeStruct((t, d), F32), jax.ShapeDtypeStruct((1, d), F32),
                   jax.ShapeDtypeStruct((1, 128), F32)],
        compiler_params=_cparams(("arbitrary",)),
    )(h, g.reshape(1, d), target)


CONV_CT = 128
CONV_TC = 256


def _glu_into(pad_ref, a_ref, g_ref, seq, tc):
    ct = pad_ref.shape[1]
    pad_ref[0:CONV_PAD, :] = jnp.zeros((CONV_PAD, ct), F32)
    for r in range(0, seq, tc):
        a = a_ref[0, r:r + tc, :]
        g = g_ref[0, r:r + tc, :]
        pad_ref[CONV_PAD + r:CONV_PAD + r + tc, :] = a * jax.nn.sigmoid(g)


def _conv_fwd_kernel(a_ref, g_ref, dw_ref, dwb_ref, hc_ref, pad_ref, *, seq, tc):
    ct = pad_ref.shape[1]
    _glu_into(pad_ref, a_ref, g_ref, seq, tc)
    for r in range(0, seq, tc):
        acc = jnp.broadcast_to(dwb_ref[...], (tc, ct))
        for k in range(CONV_K):
            o = CONV_PAD + r - (CONV_K - 1) + k
            acc = acc + dw_ref[k:k + 1, :] * pad_ref[o:o + tc, :]
        hc_ref[0, r:r + tc, :] = acc


def _conv_fwd(p0, dw, dwb, bsz, seq):
    ct = CONV_CT
    tc = min(CONV_TC, seq)
    p3 = p0.reshape(bsz, seq, P0_COLS)
    return pl.pallas_call(
        functools.partial(_conv_fwd_kernel, seq=seq, tc=tc),
        name="conv_fwd",
        grid=(MAIN // ct, bsz),
        in_specs=[pl.BlockSpec((1, seq, ct), lambda j, b: (b, 0, P0_A // ct + j)),
                  pl.BlockSpec((1, seq, ct), lambda j, b: (b, 0, P0_G // ct + j)),
                  pl.BlockSpec((CONV_K, ct), lambda j, b: (0, j)),
                  pl.BlockSpec((1, ct), lambda j, b: (0, j))],
        out_specs=pl.BlockSpec((1, seq, ct), lambda j, b: (b, 0, j)),
        out_shape=jax.ShapeDtypeStruct((bsz, seq, MAIN), F32),
        scratch_shapes=[pltpu.VMEM((seq + CONV_PAD, ct), F32)],
        compiler_params=_cparams(("parallel", "parallel")),
    )(p3, p3, dw, dwb)


def _conv_bwd_kernel(a_ref, g_ref, dhc_ref, dw_ref, da_ref, dg_ref, ddw_ref, pad_ref, dpad_ref, acc_ref,
                     *, seq, tc):
    ct = pad_ref.shape[1]
    b = pl.program_id(1)
    _glu_into(pad_ref, a_ref, g_ref, seq, tc)
    dpad_ref[seq:seq + CONV_PAD, :] = jnp.zeros((CONV_PAD, ct), F32)
    for r in range(0, seq, tc):
        dpad_ref[r:r + tc, :] = dhc_ref[0, r:r + tc, :]
    acc_ref[...] = jnp.zeros(acc_ref.shape, F32)
    for r in range(0, seq, tc):
        dh = dhc_ref[0, r:r + tc, :]
        dglu = jnp.zeros((tc, ct), F32)
        for k in range(CONV_K):
            o = r + (CONV_K - 1) - k
            dglu = dglu + dw_ref[k:k + 1, :] * dpad_ref[o:o + tc, :]
            o = CONV_PAD + r - (CONV_K - 1) + k
            prod = pad_ref[o:o + tc, :] * dh
            acc_ref[k] += jnp.sum(prod.reshape(tc // 8, 8, ct), axis=0)
        acc_ref[CONV_K] += jnp.sum(dh.reshape(tc // 8, 8, ct), axis=0)
        a = a_ref[0, r:r + tc, :]
        sg = jax.nn.sigmoid(g_ref[0, r:r + tc, :])
        da_ref[0, r:r + tc, :] = (dglu * sg).astype(da_ref.dtype)
        dg_ref[0, r:r + tc, :] = (dglu * a * sg * (1.0 - sg)).astype(dg_ref.dtype)
    tot = jnp.sum(acc_ref[...], axis=1)

    @pl.when(b == 0)
    def _():
        ddw_ref[...] = tot

    @pl.when(b > 0)
    def _():
        ddw_ref[...] += tot


def _conv_bwd(p0, dhc, dw, bsz, seq):
    ct = CONV_CT
    tc = min(CONV_TC, seq)
    p3 = p0.reshape(bsz, seq, P0_COLS)
    blk = pl.BlockSpec((1, seq, ct), lambda j, b: (b, 0, j))
    return pl.pallas_call(
        functools.partial(_conv_bwd_kernel, seq=seq, tc=tc),
        name="conv_bwd",
        grid=(MAIN // ct, bsz),
        in_specs=[pl.BlockSpec((1, seq, ct), lambda j, b: (b, 0, P0_A // ct + j)),
                  pl.BlockSpec((1, seq, ct), lambda j, b: (b, 0, P0_G // ct + j)),
                  blk,
                  pl.BlockSpec((CONV_K, ct), lambda j, b: (0, j))],
        out_specs=[blk, blk, pl.BlockSpec((CONV_K + 1, ct), lambda j, b: (0, j))],
        out_shape=[jax.ShapeDtypeStruct((bsz, seq, MAIN), BF16), jax.ShapeDtypeStruct((bsz, seq, MAIN), BF16),
                   jax.ShapeDtypeStruct((CONV_K + 1, MAIN), F32)],
        scratch_shapes=[pltpu.VMEM((seq + CONV_PAD, ct), F32), pltpu.VMEM((seq + CONV_PAD, ct), F32),
                        pltpu.VMEM((CONV_K + 1, 8, ct), F32)],
        compiler_params=_cparams(("parallel", "arbitrary")),
    )(p3, p3, dhc, dw)


def _ln_parts(x, lng, lnb):
    mu = jnp.mean(x, axis=-1, keepdims=True)
    xc = x - mu
    rstd = lax.rsqrt(jnp.mean(xc * xc, axis=-1, keepdims=True) + LN_EPS)
    xh = xc * rstd
    hl = xh * lng + lnb
    return rstd, xh, hl


GATE_CHUNK = 512
GATE_NZ = MIX // GATE_CHUNK


def _z_specs(tm, zb):
    return [pl.BlockSpec((tm, GATE_CHUNK), lambda i, j=j: (i, zb + j)) for j in range(GATE_NZ)]


def _gate_fwd_kernel(*refs, ln):
    main_ref, ymem_ref, *z_refs = refs[:2 + GATE_NZ]
    y_ref = refs[-1]
    x = main_ref[...]
    if ln:
        _, _, hl = _ln_parts(x, refs[-3][...], refs[-2][...])
        x = hl * jax.nn.sigmoid(hl)
    for j, z_ref in enumerate(z_refs):
        cols = slice(j * GATE_CHUNK, (j + 1) * GATE_CHUNK)
        z = z_ref[...]
        src = x[:, cols] if j < MAIN // GATE_CHUNK else ymem_ref[...]
        y_ref[:, cols] = (src * (z * jax.nn.sigmoid(z))).astype(y_ref.dtype)


def _gate_fwd(main, ymem, p, zb, ln_g=None, ln_b=None, *, name, tm=256):
    t = main.shape[0]
    tm = _tile(t, tm)
    ln = ln_g is not None
    in_specs = [pl.BlockSpec((tm, MAIN), lambda i: (i, 0)), pl.BlockSpec((tm, MEMW), lambda i: (i, 0))]
    in_specs += _z_specs(tm, zb)
    args = [main, ymem] + [p] * GATE_NZ
    if ln:
        in_specs += [pl.BlockSpec((1, MAIN), lambda i: (0, 0))] * 2
        args += [ln_g.reshape(1, MAIN), ln_b.reshape(1, MAIN)]
    return pl.pallas_call(
        functools.partial(_gate_fwd_kernel, ln=ln),
        name=name,
        grid=(t // tm,),
        in_specs=in_specs,
        out_specs=pl.BlockSpec((tm, MIX), lambda i: (i, 0)),
        out_shape=jax.ShapeDtypeStruct((t, MIX), BF16),
        compiler_params=_cparams(("parallel",)),
    )(*args)


def _gate_bwd_kernel(*refs, ln):
    dy_ref, main_ref, ymem_ref, *z_refs = refs[:3 + GATE_NZ]
    if ln:
        lng_ref, lnb_ref, dmain_ref, dymem_ref, dz_ref, dlng_ref, dlnb_ref = refs[3 + GATE_NZ:]
    else:
        dmain_ref, dymem_ref, dz_ref = refs[3 + GATE_NZ:]
    x = main_ref[...]
    if ln:
        lng = lng_ref[...]
        rstd, xh, hl = _ln_parts(x, lng, lnb_ref[...])
        sh = jax.nn.sigmoid(hl)
        ymain = hl * sh
    else:
        ymain = x
    for j, z_ref in enumerate(z_refs):
        cols = slice(j * GATE_CHUNK, (j + 1) * GATE_CHUNK)
        dy = dy_ref[:, cols]
        z = z_ref[...]
        sg = jax.nn.sigmoid(z)
        dsz = sg * (1.0 + z * (1.0 - sg))
        if j < MAIN // GATE_CHUNK:
            src = ymain[:, cols]
            dmain_ref[:, cols] = dy * (z * sg)
        else:
            src = ymem_ref[...]
            dymem_ref[...] = dy * (z * sg)
        dz_ref[:, cols] = (dy * src * dsz).astype(dz_ref.dtype)
    if not ln:
        return
    dym = dmain_ref[...]
    dhl = dym * (sh * (1.0 + hl * (1.0 - sh)))
    dxh = dhl * lng
    dmain_ref[...] = rstd * (dxh - jnp.mean(dxh, axis=-1, keepdims=True)
                             - xh * jnp.mean(dxh * xh, axis=-1, keepdims=True))
    dlng = jnp.sum(dhl * xh, axis=0, keepdims=True)
    dlnb = jnp.sum(dhl, axis=0, keepdims=True)

    @pl.when(pl.program_id(0) == 0)
    def _():
        dlng_ref[...] = dlng
        dlnb_ref[...] = dlnb

    @pl.when(pl.program_id(0) > 0)
    def _():
        dlng_ref[...] += dlng
        dlnb_ref[...] += dlnb


def _gate_bwd(dy, main, ymem, p, zb, ln_g=None, ln_b=None, *, name, tm=256):
    t = main.shape[0]
    tm = _tile(t, tm)
    ln = ln_g is not None
    r_main = pl.BlockSpec((tm, MAIN), lambda i: (i, 0))
    r_mem = pl.BlockSpec((tm, MEMW), lambda i: (i, 0))
    r_mix = pl.BlockSpec((tm, MIX), lambda i: (i, 0))
    vec = pl.BlockSpec((1, MAIN), lambda i: (0, 0))
    in_specs = [r_mix, r_main, r_mem] + _z_specs(tm, zb)
    args = [dy, main, ymem] + [p] * GATE_NZ
    out_specs = [r_main, r_mem, r_mix]
    out_shape = [jax.ShapeDtypeStruct((t, MAIN), F32), jax.ShapeDtypeStruct((t, MEMW), F32),
                 jax.ShapeDtypeStruct((t, MIX), BF16)]
    if ln:
        in_specs += [vec, vec]
        args += [ln_g.reshape(1, MAIN), ln_b.reshape(1, MAIN)]
        out_specs += [vec, vec]
        out_shape += [jax.ShapeDtypeStruct((1, MAIN), F32)] * 2
    return pl.pallas_call(
        functools.partial(_gate_bwd_kernel, ln=ln),
        name=name,
        grid=(t // tm,),
        in_specs=in_specs,
        out_specs=out_specs,
        out_shape=out_shape,
        compiler_params=_cparams(("arbitrary",)),
    )(*args)


def _dot_nt(a, b):
    return lax.dot_general(a, b, (((1,), (1,)), ((), ())), preferred_element_type=F32)


def _dot_tn(a, b):
    return lax.dot_general(a, b, (((0,), (0,)), ((), ())), preferred_element_type=F32)


def _dot(a, b):
    return jnp.dot(a, b, preferred_element_type=F32)


def _mem_probs(q, k):
    s = _dot_nt(q, k) * MEM_SCALE
    p = jnp.exp(s - jnp.max(s, axis=-1, keepdims=True))
    return p / jnp.sum(p, axis=-1, keepdims=True)


def _mem_fwd_kernel(q_ref, kv_ref, o_ref):
    for h in range(MEM_HEADS):
        c = slice(h * HEAD, (h + 1) * HEAD)
        cv = slice(MEMW + h * HEAD, MEMW + (h + 1) * HEAD)
        p = _mem_probs(q_ref[0, :, c].astype(BF16), kv_ref[0, :, c])
        o_ref[0, :, c] = _dot(p.astype(BF16), kv_ref[0, :, cv])


def _mem_fwd(p, kvm, col_block, bsz, seq, *, name, tq=512):
    tq = _tile(seq, tq)
    p3 = p.reshape(bsz, seq, p.shape[1])
    mlen = kvm.shape[1]
    return pl.pallas_call(
        functools.partial(_mem_fwd_kernel),
        name=name,
        grid=(bsz, seq // tq),
        in_specs=[pl.BlockSpec((1, tq, MEMW), lambda b, i: (b, i, col_block)),
                  pl.BlockSpec((1, mlen, 2 * MEMW), lambda b, i: (b, 0, 0))],
        out_specs=pl.BlockSpec((1, tq, MEMW), lambda b, i: (b, i, 0)),
        out_shape=jax.ShapeDtypeStruct((bsz, seq, MEMW), F32),
        compiler_params=_cparams(("parallel", "parallel")),
    )(p3, kvm)


def _mem_bwd_kernel(q_ref, kv_ref, do_ref, dq_ref, dkv_ref):
    @pl.when(pl.program_id(1) == 0)
    def _():
        dkv_ref[...] = jnp.zeros(dkv_ref.shape, F32)

    for h in range(MEM_HEADS):
        c = slice(h * HEAD, (h + 1) * HEAD)
        cv = slice(MEMW + h * HEAD, MEMW + (h + 1) * HEAD)
        q = q_ref[0, :, c].astype(BF16)
        k = kv_ref[0, :, c]
        v = kv_ref[0, :, cv]
        do = do_ref[0, :, c].astype(BF16)
        p = _mem_probs(q, k)
        dp = _dot_nt(do, v)
        ds = (p * (dp - jnp.sum(p * dp, axis=-1, keepdims=True)) * MEM_SCALE).astype(BF16)
        dq_ref[0, :, c] = _dot(ds, k).astype(dq_ref.dtype)
        dkv_ref[0, :, c] += _dot_tn(ds, q)
        dkv_ref[0, :, cv] += _dot_tn(p.astype(BF16), do)


def _mem_bwd(p, kvm, dymem, col_block, bsz, seq, *, name, tq=512):
    tq = _tile(seq, tq)
    p3 = p.reshape(bsz, seq, p.shape[1])
    mlen = kvm.shape[1]
    return pl.pallas_call(
        functools.partial(_mem_bwd_kernel),
        name=name,
        grid=(bsz, seq // tq),
        in_specs=[pl.BlockSpec((1, tq, MEMW), lambda b, i: (b, i, col_block)),
                  pl.BlockSpec((1, mlen, 2 * MEMW), lambda b, i: (b, 0, 0)),
                  pl.BlockSpec((1, tq, MEMW), lambda b, i: (b, i, 0))],
        out_specs=[pl.BlockSpec((1, tq, MEMW), lambda b, i: (b, i, 0)),
                   pl.BlockSpec((1, mlen, 2 * MEMW), lambda b, i: (b, 0, 0))],
        out_shape=[jax.ShapeDtypeStruct((bsz, seq, MEMW), BF16),
                   jax.ShapeDtypeStruct((bsz, mlen, 2 * MEMW), F32)],
        compiler_params=_cparams(("parallel", "arbitrary")),
    )(p3, kvm, dymem.reshape(bsz, seq, MEMW))


def _swap32(x):
    lane = lax.broadcasted_iota(jnp.int32, x.shape, 1)
    return jnp.where(lane < 32, pltpu.roll(x, 96, 1), pltpu.roll(x, 32, 1))


def _rope(x, cs, sn):
    return x * cs + _swap32(x) * sn


def _rope_t(d, cs, sn):
    return d * cs + _swap32(d * sn)


UP_HEADS = 2


def _q_up_kernel(a_ref, b_ref, cs_ref, sn_ref, o_ref):
    acc = _dot(a_ref[...], b_ref[...])
    cs = cs_ref[...]
    sn = sn_ref[...]
    for h in range(UP_HEADS):
        c0 = slice(h * QK_PAD, h * QK_PAD + HEAD)
        c1 = slice(h * QK_PAD + HEAD, (h + 1) * QK_PAD)
        o_ref[:, c0] = acc[:, c0].astype(o_ref.dtype)
        o_ref[:, c1] = _rope(acc[:, c1], cs, sn).astype(o_ref.dtype)


def _q_up(cqn, w_uq, cs, sn, *, tm=512):
    t, kd = cqn.shape
    tm = _tile(t, tm)
    tn = UP_HEADS * QK_PAD
    tab = pl.BlockSpec((tm, 128), lambda i, j: (i, 0))
    return pl.pallas_call(
        functools.partial(_q_up_kernel),
        name="q_up_rope",
        grid=(t // tm, MLA_HEADS // UP_HEADS),
        in_specs=[pl.BlockSpec((tm, kd), lambda i, j: (i, 0)), pl.BlockSpec((kd, tn), lambda i, j: (0, j)), tab, tab],
        out_specs=pl.BlockSpec((tm, tn), lambda i, j: (i, j)),
        out_shape=jax.ShapeDtypeStruct((t, MLA_HEADS * QK_PAD), BF16),
        compiler_params=_cparams(("parallel", "parallel")),
    )(cqn, w_uq, cs, sn)


def _kv_up_kernel(a_ref, b_ref, kr_ref, cs_ref, sn_ref, k_ref, v_ref):
    acc = _dot(a_ref[...], b_ref[...])
    krr = _rope(kr_ref[...], cs_ref[...], sn_ref[...]).astype(k_ref.dtype)
    for h in range(UP_HEADS):
        k_ref[:, h * QK_PAD:h * QK_PAD + HEAD] = acc[:, h * 2 * HEAD:h * 2 * HEAD + HEAD].astype(k_ref.dtype)
        k_ref[:, h * QK_PAD + HEAD:(h + 1) * QK_PAD] = krr
        v_ref[:, h * HEAD:(h + 1) * HEAD] = acc[:, h * 2 * HEAD + HEAD:(h + 1) * 2 * HEAD].astype(v_ref.dtype)


def _kv_up(ckvn, w_ukv, p1, cs, sn, *, tm=512):
    t, kd = ckvn.shape
    tm = _tile(t, tm)
    tab = pl.BlockSpec((tm, 128), lambda i, j: (i, 0))
    return pl.pallas_call(
        functools.partial(_kv_up_kernel),
        name="kv_up_pack",
        grid=(t // tm, MLA_HEADS // UP_HEADS),
        in_specs=[pl.BlockSpec((tm, kd), lambda i, j: (i, 0)),
                  pl.BlockSpec((kd, UP_HEADS * 2 * HEAD), lambda i, j: (0, j)),
                  pl.BlockSpec((tm, 128), lambda i, j: (i, P1_KR // 128)), tab, tab],
        out_specs=[pl.BlockSpec((tm, UP_HEADS * QK_PAD), lambda i, j: (i, j)),
                   pl.BlockSpec((tm, UP_HEADS * HEAD), lambda i, j: (i, j))],
        out_shape=[jax.ShapeDtypeStruct((t, MLA_HEADS * QK_PAD), BF16), jax.ShapeDtypeStruct((t, MAIN), BF16)],
        compiler_params=_cparams(("parallel", "parallel")),
    )(ckvn, w_ukv, p1, cs, sn)


def _kr_bwd_kernel(d_ref, cs_ref, sn_ref, o_ref):
    acc = d_ref[:, :HEAD]
    for h in range(1, MLA_HEADS):
        acc = acc + d_ref[:, h * HEAD:(h + 1) * HEAD]
    o_ref[...] = _rope_t(acc, cs_ref[...], sn_ref[...]).astype(o_ref.dtype)


def _kr_bwd(dkrr, cs, sn, *, tm=512):
    t = dkrr.shape[0]
    tm = _tile(t, tm)
    tab = pl.BlockSpec((tm, 128), lambda i: (i, 0))
    return pl.pallas_call(
        functools.partial(_kr_bwd_kernel),
        name="kr_bwd",
        grid=(t // tm,),
        in_specs=[pl.BlockSpec((tm, MAIN), lambda i: (i, 0)), tab, tab],
        out_specs=tab,
        out_shape=jax.ShapeDtypeStruct((t, 128), BF16),
        compiler_params=_cparams(("parallel",)),
    )(dkrr, cs, sn)


ATT_T = 256


def _causal(s, t):
    row = lax.broadcasted_iota(jnp.int32, (t, t), 0)
    col = lax.broadcasted_iota(jnp.int32, (t, t), 1)
    return jnp.where(col <= row, s, NEG)


def _attn_fwd_kernel(q_ref, k_ref, v_ref, o_ref, lse_ref, *, seq, t):
    for i in range(seq // t):
        own = slice(i * t, (i + 1) * t)
        q = q_ref[0, own, :]
        sd = _causal(_dot_nt(q, k_ref[0, own, :]) * MLA_SCALE, t)
        m = jnp.max(sd, axis=-1, keepdims=True)
        if i:
            so = _dot_nt(q, k_ref[0, :i * t, :]) * MLA_SCALE
            m = jnp.maximum(m, jnp.max(so, axis=-1, keepdims=True))
        pd = jnp.exp(sd - m)
        l = jnp.sum(pd, axis=-1, keepdims=True)
        acc = _dot(pd.astype(BF16), v_ref[0, own, :])
        if i:
            po = jnp.exp(so - m)
            l = l + jnp.sum(po, axis=-1, keepdims=True)
            acc = acc + _dot(po.astype(BF16), v_ref[0, :i * t, :])
        o_ref[0, own, :] = acc / l
        lse_ref[0, 0, own, :] = m + jnp.log(l)


def _attn_fwd(qb, kb, vb, bsz, seq):
    t = min(ATT_T, seq)
    q3 = qb.reshape(bsz, seq, MLA_HEADS * QK_PAD)
    k3 = kb.reshape(bsz, seq, MLA_HEADS * QK_PAD)
    v3 = vb.reshape(bsz, seq, MAIN)
    qk = pl.BlockSpec((1, seq, QK_PAD), lambda b, h: (b, 0, h))
    vv = pl.BlockSpec((1, seq, HEAD), lambda b, h: (b, 0, h))
    return pl.pallas_call(
        functools.partial(_attn_fwd_kernel, seq=seq, t=t),
        name="attn_fwd",
        grid=(bsz, MLA_HEADS),
        in_specs=[qk, qk, vv],
        out_specs=[vv, pl.BlockSpec((1, 1, seq, 1), lambda b, h: (b, h, 0, 0))],
        out_shape=[jax.ShapeDtypeStruct((bsz, seq, MAIN), F32),
                   jax.ShapeDtypeStruct((bsz, MLA_HEADS, seq, 1), F32)],
        compiler_params=_cparams(("parallel", "parallel")),
    )(q3, k3, v3)


def _attn_bwd_kernel(q_ref, k_ref, v_ref, o_ref, do_ref, lse_ref, cs_ref, sn_ref, dq_ref, dkv_ref, dkr_ref,
                     delta_ref, dqacc_ref, *, seq, t):
    for r in range(0, seq, t):
        rows = slice(r, r + t)
        delta_ref[rows, :] = jnp.sum(do_ref[0, rows, :] * o_ref[0, rows, :], axis=-1, keepdims=True)
    dqacc_ref[...] = jnp.zeros(dqacc_ref.shape, F32)

    def piece(rows, k, v, masked):
        q = q_ref[0, rows, :]
        do = do_ref[0, rows, :].astype(BF16)
        s = _dot_nt(q, k) * MLA_SCALE
        if masked:
            s = _causal(s, t)
        p = jnp.exp(s - lse_ref[0, 0, rows, :])
        dp = _dot_nt(do, v)
        ds = (p * (dp - delta_ref[rows, :]) * MLA_SCALE).astype(BF16)
        dqacc_ref[rows, :] += _dot(ds, k)
        return _dot_tn(ds, q), _dot_tn(p.astype(BF16), do)

    for j in range(seq // t):
        own = slice(j * t, (j + 1) * t)
        k = k_ref[0, own, :]
        v = v_ref[0, own, :]
        dk, dv = piece(own, k, v, True)
        if (j + 1) * t < seq:
            dk2, dv2 = piece(slice((j + 1) * t, seq), k, v, False)
            dk, dv = dk + dk2, dv + dv2
        dkv_ref[0, own, :HEAD] = dk[:, :HEAD].astype(dkv_ref.dtype)
        dkv_ref[0, own, HEAD:] = dv.astype(dkv_ref.dtype)
        dkr_ref[0, own, :] = dk[:, HEAD:]
    for r in range(0, seq, t):
        rows = slice(r, r + t)
        dq = dqacc_ref[rows, :]
        dq_ref[0, rows, :HEAD] = dq[:, :HEAD].astype(dq_ref.dtype)
        dq_ref[0, rows, HEAD:] = _rope_t(dq[:, HEAD:], cs_ref[0, rows, :], sn_ref[0, rows, :]).astype(dq_ref.dtype)


def _attn_bwd(qb, kb, vb, o, do, lse, cs, sn, bsz, seq):
    t = min(ATT_T, seq)
    q3 = qb.reshape(bsz, seq, MLA_HEADS * QK_PAD)
    k3 = kb.reshape(bsz, seq, MLA_HEADS * QK_PAD)
    v3 = vb.reshape(bsz, seq, MAIN)
    qk = pl.BlockSpec((1, seq, QK_PAD), lambda b, h: (b, 0, h))
    vv = pl.BlockSpec((1, seq, HEAD), lambda b, h: (b, 0, h))
    tab = pl.BlockSpec((1, seq, 128), lambda b, h: (b, 0, 0))
    dq, dkv, dkr = pl.pallas_call(
        functools.partial(_attn_bwd_kernel, seq=seq, t=t),
        name="attn_bwd",
        grid=(bsz, MLA_HEADS),
        in_specs=[qk, qk, vv, vv, vv, pl.BlockSpec((1, 1, seq, 1), lambda b, h: (b, h, 0, 0)), tab, tab],
        out_specs=[qk, qk, vv],
        out_shape=[jax.ShapeDtypeStruct((bsz, seq, MLA_HEADS * QK_PAD), BF16),
                   jax.ShapeDtypeStruct((bsz, seq, MLA_HEADS * 2 * HEAD), BF16),
                   jax.ShapeDtypeStruct((bsz, seq, MAIN), F32)],
        scratch_shapes=[pltpu.VMEM((seq, 1), F32), pltpu.VMEM((seq, QK_PAD), F32)],
        compiler_params=_cparams(("parallel", "parallel")),
    )(q3, k3, v3, o, do.reshape(bsz, seq, MAIN), lse, cs.reshape(bsz, seq, 128), sn.reshape(bsz, seq, 128))
    n = bsz * seq
    return dq.reshape(n, -1), dkv.reshape(n, -1), dkr.reshape(n, -1)


def _adamw_kernel(w_ref, g_ref, m_ref, v_ref, d_ref, nm_ref, nv_ref):
    g = g_ref[...]
    m = ADAM_B1 * m_ref[...] + (1.0 - ADAM_B1) * g
    v = ADAM_B2 * v_ref[...] + (1.0 - ADAM_B2) * (g * g)
    m_hat = m / (1.0 - ADAM_B1 ** ADAM_STEP)
    v_hat = v / (1.0 - ADAM_B2 ** ADAM_STEP)
    d_ref[...] = -ADAM_LR * (m_hat / (jnp.sqrt(v_hat) + ADAM_EPS) + ADAM_WD * w_ref[...])
    nm_ref[...] = m
    nv_ref[...] = v


def _adamw(w, g, m, v, *, name):
    shape = w.shape
    c = shape[-1]
    r = w.size // c
    tr = r
    for cand in (512, 256, 128, 64, 32, 16, 8):
        if r % cand == 0 and cand * c * 4 <= 2 * 1024 * 1024:
            tr = cand
            break
    blk = pl.BlockSpec((tr, c), lambda i: (i, 0))
    outs = pl.pallas_call(
        functools.partial(_adamw_kernel),
        name=name,
        grid=(r // tr,),
        in_specs=[blk] * 4,
        out_specs=[blk] * 3,
        out_shape=[jax.ShapeDtypeStruct((r, c), F32)] * 3,
        compiler_params=_cparams(("parallel",)),
    )(w.reshape(r, c), g.reshape(r, c), m.reshape(r, c), v.reshape(r, c))
    return tuple(o.reshape(shape) for o in outs)


HBM = pl.BlockSpec(memory_space=pl.ANY)


def _place():
    return lax.axis_index("x"), lax.axis_index("y"), lax.axis_index("c")


def _other_chips(x, y):
    return [(1 - x, y), (x, 1 - y), (1 - x, 1 - y)]


class _Geom:
    def __init__(self, kind, rows, cols):
        self.kind, self.rows, self.cols, self.hr = kind, rows, cols, rows // 2
        self.full_shape = {"rows": (4 * rows, cols), "cols": (rows, 4 * cols), "chips": (4, rows, cols)}[kind]
        self.nt = self.hr // ADD_TILE

    def view(self, ref, s, h):
        if self.kind == "rows":
            return ref.at[pl.ds(s * self.rows + h * self.hr, self.hr), :]
        if self.kind == "cols":
            return ref.at[pl.ds(h * self.hr, self.hr), pl.ds(s * self.cols, self.cols)]
        return ref.at[s, pl.ds(h * self.hr, self.hr), :]

    def shard_half(self, ref, h):
        return ref.at[pl.ds(h * self.hr, self.hr), :]

    def put_shard(self, full, shard, chip):
        if self.kind == "rows":
            return lax.dynamic_update_slice(full, shard, (chip * self.rows, 0))
        if self.kind == "cols":
            return lax.dynamic_update_slice(full, shard, (0, chip * self.cols))
        return lax.dynamic_update_slice(full, shard[None], (chip, 0, 0))

    def tile_spec(self, chip_half_of):
        if self.kind == "rows":
            def index(*a):
                s, h, i = chip_half_of(*a)
                return (s * (self.rows // ADD_TILE) + h * self.nt + i, 0)
            return pl.BlockSpec((ADD_TILE, self.cols), index)
        if self.kind == "cols":
            def index(*a):
                s, h, i = chip_half_of(*a)
                return (h * self.nt + i, s)
            return pl.BlockSpec((ADD_TILE, self.cols), index)

        def index(*a):
            s, h, i = chip_half_of(*a)
            return (s, h * self.nt + i, 0)
        return pl.BlockSpec((None, ADD_TILE, self.cols), index)


GEOMS = (("w_mem_kv0", _Geom("rows", 256, 1024)), ("w_mem_kv1", _Geom("rows", 256, 1024)),
         ("w_out0", _Geom("rows", 512, 1024)), ("w_out1", _Geom("rows", 512, 1024)),
         ("conv_w_in", _Geom("cols", 1024, 1408)), ("mla_w_ukv", _Geom("cols", 256, 768)),
         ("mla_w_in", _Geom("chips", 1024, 848)), ("mla_w_uq", _Geom("chips", 512, 576)))
N_BIG = len(GEOMS)


def _remote(k, src_ref, dst_ref, to, ssem, rsem):
    return pltpu.make_async_remote_copy(src_ref=src_ref, dst_ref=dst_ref, send_sem=ssem.at[k], recv_sem=rsem.at[k],
                                        device_id=to, device_id_type=MESH)


def _allgather_kernel(*refs):
    srcs, outs, (ssem, rsem) = refs[:N_BIG], refs[N_BIG:2 * N_BIG], refs[2 * N_BIG:]
    x, y, c = _place()
    s = 2 * x + y
    chips = _other_chips(x, y)
    sibling = (x, y, 1 - c)
    first = []
    for w, (_, g) in enumerate(GEOMS):
        for j, chip in enumerate(chips):
            cp = _remote(6 * w + j, g.shard_half(srcs[w], c), g.view(outs[w], s, c), (*chip, c), ssem, rsem)
            cp.start()
            first.append(cp)
    passed = []
    for w, (_, g) in enumerate(GEOMS):
        for j, chip in enumerate(chips):
            blk = g.view(outs[w], 2 * chip[0] + chip[1], c)
            _remote(6 * w + j, blk, blk, (*chip, c), ssem, rsem).wait_recv()
            cp = _remote(6 * w + 3 + j, blk, blk, sibling, ssem, rsem)
            cp.start()
            passed.append(cp)
    for w, (_, g) in enumerate(GEOMS):
        for j, chip in enumerate(chips):
            blk = g.view(outs[w], 2 * chip[0] + chip[1], 1 - c)
            _remote(6 * w + 3 + j, blk, blk, sibling, ssem, rsem).wait_recv()
    for cp in first + passed:
        cp.wait_send()


def _allgather_weights(shards, chip):
    others = pl.pallas_call(
        functools.partial(_allgather_kernel),
        name="allgather_weights",
        in_specs=[HBM] * N_BIG,
        out_specs=[HBM] * N_BIG,
        out_shape=[jax.ShapeDtypeStruct(g.full_shape, BF16) for _, g in GEOMS],
        scratch_shapes=[pltpu.SemaphoreType.DMA((6 * N_BIG,)), pltpu.SemaphoreType.DMA((6 * N_BIG,))],
    )(*shards)
    return [g.put_shard(o, sh, chip) for (_, g), o, sh in zip(GEOMS, others, shards)]


def _swap_halves_kernel(*refs):
    srcs, dsts, (ssem, rsem) = refs[:N_BIG], refs[N_BIG:2 * N_BIG], refs[2 * N_BIG:]
    x, y, c = _place()
    cps = []
    for w, (_, g) in enumerate(GEOMS):
        for s in range(4):
            cps.append(_remote(4 * w + s, g.view(srcs[w], s, 1 - c), dsts[w].at[s], (x, y, 1 - c), ssem, rsem))
    for cp in cps:
        cp.start()
    for cp in cps:
        cp.wait()


def _swap_halves(gb):
    return pl.pallas_call(
        functools.partial(_swap_halves_kernel),
        name="rs_sibling_swap",
        in_specs=[HBM] * N_BIG,
        out_specs=[HBM] * N_BIG,
        out_shape=[jax.ShapeDtypeStruct((4, g.hr, g.cols), BF16) for _, g in GEOMS],
        scratch_shapes=[pltpu.SemaphoreType.DMA((4 * N_BIG,)), pltpu.SemaphoreType.DMA((4 * N_BIG,))],
    )(*gb)


def _chip_exchange_kernel(*refs):
    srcs, dsts, (ssem, rsem) = refs[:N_BIG], refs[N_BIG:2 * N_BIG], refs[2 * N_BIG:]
    x, y, c = _place()
    cps = []
    for w in range(N_BIG):
        for j, chip in enumerate(_other_chips(x, y)):
            cps.append(_remote(3 * w + j, srcs[w].at[2 * chip[0] + chip[1]], dsts[w].at[j], (*chip, c), ssem, rsem))
    for cp in cps:
        cp.start()
    for cp in cps:
        cp.wait()


def _chip_exchange(pairs):
    return pl.pallas_call(
        functools.partial(_chip_exchange_kernel),
        name="rs_chip_exchange",
        in_specs=[HBM] * N_BIG,
        out_specs=[HBM] * N_BIG,
        out_shape=[jax.ShapeDtypeStruct((3,) + p.shape[1:], p.dtype) for p in pairs],
        scratch_shapes=[pltpu.SemaphoreType.DMA((3 * N_BIG,)), pltpu.SemaphoreType.DMA((3 * N_BIG,))],
    )(*pairs)


def _share_kernel(*refs):
    srcs, dsts, (ssem, rsem) = refs[:N_BIG], refs[N_BIG:2 * N_BIG], refs[2 * N_BIG:]
    x, y, c = _place()
    cps = [_remote(w, srcs[w], dsts[w], (x, y, 1 - c), ssem, rsem) for w in range(N_BIG)]
    for cp in cps:
        cp.start()
    for cp in cps:
        cp.wait()


def _share_with_sibling(halves):
    return pl.pallas_call(
        functools.partial(_share_kernel),
        name="rs_share_halves",
        in_specs=[HBM] * N_BIG,
        out_specs=[HBM] * N_BIG,
        out_shape=[jax.ShapeDtypeStruct(h.shape, h.dtype) for h in halves],
        scratch_shapes=[pltpu.SemaphoreType.DMA((N_BIG,)), pltpu.SemaphoreType.DMA((N_BIG,))],
    )(*halves)


def _gather_sum_kernel(src, gat, tot, ssem, rsem):
    x, y, c = _place()
    me = 4 * x + 2 * y + c
    gat[me] = src[...]
    flips = [(dx, dy, dc) for dx in (0, 1) for dy in (0, 1) for dc in (0, 1)][1:]
    cps = []
    for k, (dx, dy, dc) in enumerate(flips):
        peer = (1 - x if dx else x, 1 - y if dy else y, 1 - c if dc else c)
        cp = pltpu.make_async_remote_copy(src_ref=src, dst_ref=gat.at[me], send_sem=ssem.at[k], recv_sem=rsem.at[k],
                                          device_id=peer, device_id_type=MESH)
        cp.start()
        cps.append((cp, 4 * peer[0] + 2 * peer[1] + peer[2], peer))
    for k, (cp, idx, peer) in enumerate(cps):
        pltpu.make_async_remote_copy(src_ref=src, dst_ref=gat.at[idx], send_sem=ssem.at[k], recv_sem=rsem.at[k],
                                     device_id=peer, device_id_type=MESH).wait_recv()
    for cp, _, _ in cps:
        cp.wait_send()
    acc = gat[0]
    for d in range(1, 8):
        acc = acc + gat[d]
    tot[...] = acc


def _gather_sum_small(a, *, name):
    vm = pl.BlockSpec(memory_space=pltpu.VMEM)
    return pl.pallas_call(
        functools.partial(_gather_sum_kernel),
        name=name,
        in_specs=[vm],
        out_specs=[vm, vm],
        out_shape=[jax.ShapeDtypeStruct((8,) + a.shape, a.dtype), jax.ShapeDtypeStruct(a.shape, a.dtype)],
        scratch_shapes=[pltpu.SemaphoreType.DMA((7,)), pltpu.SemaphoreType.DMA((7,))],
    )(a)


def _add_pairs_kernel(c_ref, g_ref, r_ref, o_ref):
    o_ref[...] = (g_ref[...] + r_ref[...].astype(F32)).astype(o_ref.dtype)


def _add_pairs(geom, core, g, recv, *, name):
    half = pl.BlockSpec((None, ADD_TILE, geom.cols), lambda s, i, cr: (s, i, 0))
    return pl.pallas_call(
        functools.partial(_add_pairs_kernel),
        name=name,
        grid_spec=pltpu.PrefetchScalarGridSpec(
            num_scalar_prefetch=1,
            grid=(4, geom.nt),
            in_specs=[geom.tile_spec(lambda s, i, cr: (s, cr[0], i)), half],
            out_specs=half,
        ),
        out_shape=jax.ShapeDtypeStruct(recv.shape, BF16),
        compiler_params=_cparams(("parallel", "parallel")),
    )(core, g, recv)


def _add_final_kernel(sc_ref, g_ref, r_ref, e_ref, o_ref):
    acc = g_ref[...] + r_ref[...].astype(F32)
    for j in range(3):
        acc = acc + e_ref[j].astype(F32)
    o_ref[...] = acc


def _add_final(geom, chip_core, g, recv, exch, *, name):
    return pl.pallas_call(
        functools.partial(_add_final_kernel),
        name=name,
        grid_spec=pltpu.PrefetchScalarGridSpec(
            num_scalar_prefetch=1,
            grid=(geom.nt,),
            in_specs=[geom.tile_spec(lambda i, sc: (sc[0], sc[1], i)),
                      pl.BlockSpec((None, ADD_TILE, geom.cols), lambda i, sc: (sc[0], i, 0)),
                      pl.BlockSpec((3, ADD_TILE, geom.cols), lambda i, sc: (0, i, 0))],
            out_specs=pl.BlockSpec((ADD_TILE, geom.cols), lambda i, sc: (i, 0)),
        ),
        out_shape=jax.ShapeDtypeStruct((geom.hr, geom.cols), F32),
        compiler_params=_cparams(("parallel",)),
    )(chip_core, g, recv, exch)


def _chip_major(w):
    return w.reshape(w.shape[0], 4, w.shape[1] // 4).transpose(1, 0, 2)


def _from_chip_major(w):
    return w.transpose(1, 0, 2).reshape(w.shape[1], 4 * w.shape[2])


def _mla_in_to_internal(w):
    return jnp.concatenate([w[:, 1344:], w[:, :512], w[:, 832:1344], w[:, 512:768], w[:, 768:832],
                            jnp.zeros((w.shape[0], 64), w.dtype)], axis=1)


def _mla_in_from_internal(w):
    return jnp.concatenate([w[:, P1_CQ:P1_QM], w[:, P1_CKV:P1_KR], w[:, P1_KR:P1_KR + 64], w[:, P1_QM:P1_CKV],
                            w[:, :MIX]], axis=1)


def _uq_to_internal(w):
    w = w.reshape(w.shape[0], MLA_HEADS, HEAD + ROPE)
    return jnp.pad(w, ((0, 0), (0, 0), (0, QK_PAD - HEAD - ROPE))).reshape(w.shape[0], MLA_HEADS * QK_PAD)


def _uq_from_internal(w):
    return w.reshape(w.shape[0], MLA_HEADS, QK_PAD)[:, :, :HEAD + ROPE].reshape(w.shape[0], MLA_HEADS * (HEAD + ROPE))


def _rows128(a, rows):
    flat = a.reshape(-1)
    return jnp.pad(flat, (0, rows * 128 - flat.shape[0])).reshape(rows, 128)


def kernel(x, mem, positions, norm_g, mem_norm_g, w_mem_kv, w_out, conv_w_in, conv_dw, conv_dw_b, conv_ln_g, conv_ln_b, mla_w_in, mla_q_norm_g, mla_w_uq, mla_kv_norm_g, mla_w_ukv, final_norm_g, loss_target, m_norm_g, m_mem_norm_g, m_w_mem_kv, m_w_out, m_conv_w_in, m_conv_dw, m_conv_dw_b, m_conv_ln_g, m_conv_ln_b, m_mla_w_in, m_mla_q_norm_g, m_mla_w_uq, m_mla_kv_norm_g, m_mla_w_ukv, m_final_norm_g, v_norm_g, v_mem_norm_g, v_w_mem_kv, v_w_out, v_conv_w_in, v_conv_dw, v_conv_dw_b, v_conv_ln_g, v_conv_ln_b, v_mla_w_in, v_mla_q_norm_g, v_mla_w_uq, v_mla_kv_norm_g, v_mla_w_ukv, v_final_norm_g):
    bsz, seq, d = x.shape
    n = bsz * seq
    mlen = mem.shape[1]
    ax, ay, ac = _place()
    chip = 2 * ax + ay

    shards = dict(w_mem_kv0=w_mem_kv[0], w_mem_kv1=w_mem_kv[1], w_out0=w_out[0], w_out1=w_out[1],
                  conv_w_in=conv_w_in[0], mla_w_ukv=mla_w_ukv[0], mla_w_in=mla_w_in[0], mla_w_uq=mla_w_uq[0])
    wf = dict(zip([nm for nm, _ in GEOMS],
                  _allgather_weights([shards[nm].astype(BF16) for nm, _ in GEOMS], chip)))
    w_conv_in = wf["conv_w_in"]
    w_mla_in = _mla_in_to_internal(_from_chip_major(wf["mla_w_in"]))
    w_uq = _uq_to_internal(_from_chip_major(wf["mla_w_uq"]))
    w_ukv = wf["mla_w_ukv"]
    w_memkv = [wf["w_mem_kv0"], wf["w_mem_kv1"]]
    w_o = [wf["w_out0"], wf["w_out1"]]
    w_conv_in_t, w_mla_in_t, w_uq_t, w_ukv_t = w_conv_in.T, w_mla_in.T, w_uq.T, w_ukv.T
    w_memkv_t = [w.T for w in w_memkv]
    w_o_t = [w.T for w in w_o]

    small_in = jnp.concatenate([_rows128(conv_dw[0], 96), _rows128(mla_q_norm_g, 8), _rows128(mla_kv_norm_g, 8)],
                               axis=0)
    small_all, _ = _gather_sum_small(small_in, name="gather_small_params")
    small_all = small_all[0::2]
    dw_full = small_all[:, :93].reshape(4, -1)[:, :CONV_K * 384].reshape(4, CONV_K, 384)
    dw_full = dw_full.transpose(1, 0, 2).reshape(CONV_K, MAIN)
    qg_full = small_all[:, 96].reshape(Q_RANK)
    kvg_full = small_all[:, 104, :64].reshape(KV_RANK)

    inv_freq = 1.0 / (ROPE_THETA ** (jnp.arange(0, ROPE, 2, dtype=F32) / ROPE))
    ang = positions.astype(F32).reshape(n, 1) * inv_freq
    cos, sin, zer = jnp.cos(ang), jnp.sin(ang), jnp.zeros((n, 64), F32)
    rope_c = jnp.concatenate([cos, cos, zer], axis=1)
    rope_s = jnp.concatenate([-sin, sin, zer], axis=1)

    x2 = x.reshape(n, d)
    mem2 = mem.reshape(bsz * mlen, d)
    tgt2 = loss_target.reshape(n, d)

    memn = [_rms_fwd(mem2, mem_norm_g[i], name=f"mem_norm{i}") for i in range(2)]
    kvm = [_matmul(memn[i], w_memkv[i], out_dtype=BF16, name=f"mem_kv{i}").reshape(bsz, mlen, 2 * MEMW)
           for i in range(2)]

    u0 = _rms_fwd(x2, norm_g[0], name="norm0")
    p0 = _matmul(u0, w_conv_in, name="conv_in_proj")
    hc = _conv_fwd(p0, dw_full, conv_dw_b, bsz, seq).reshape(n, MAIN)
    ymem0 = _mem_fwd(p0, kvm[0], P0_QM // MEMW, bsz, seq, name="mem_attn0").reshape(n, MEMW)
    y0 = _gate_fwd(hc, ymem0, p0, P0_Z // GATE_CHUNK, conv_ln_g, conv_ln_b, name="gate0")
    h1 = _matmul(y0, w_o[0], x2, name="out_proj0")

    u1 = _rms_fwd(h1, norm_g[1], name="norm1")
    p1 = _matmul(u1, w_mla_in, name="mla_in_proj")
    cqn = _rms_fwd(p1, qg_full, width=Q_RANK, col_block=P1_CQ // Q_RANK, name="q_norm")
    ckvn = _rms_fwd(p1, kvg_full, width=KV_RANK, col_block=P1_CKV // KV_RANK, name="kv_norm")
    qb = _q_up(cqn, w_uq, rope_c, rope_s)
    kb, vb = _kv_up(ckvn, w_ukv, p1, rope_c, rope_s)
    o1, lse = _attn_fwd(qb, kb, vb, bsz, seq)
    o1 = o1.reshape(n, MAIN)
    ymem1 = _mem_fwd(p1, kvm[1], P1_QM // MEMW, bsz, seq, name="mem_attn1").reshape(n, MEMW)
    y1 = _gate_fwd(o1, ymem1, p1, P1_Z // GATE_CHUNK, name="gate1")
    h2 = _matmul(y1, w_o[1], h1, name="out_proj1")

    dh2, d_final_g, loss_part = _final_loss(h2, final_norm_g, tgt2)
    loss = lax.psum(loss_part[0, 0], ("x", "y", "c"))

    gbig = {}
    gbig["w_out1"] = _matmul(y1, dh2, ta=True, also_bf16=True, name="d_w_out1")
    dy1 = _matmul(dh2, w_o_t[1], name="d_y1")
    do1, dymem1, dz1 = _gate_bwd(dy1, o1, ymem1, p1, P1_Z // GATE_CHUNK, name="gate1_bwd")
    dqm1, dkvm1 = _mem_bwd(p1, kvm[1], dymem1, P1_QM // MEMW, bsz, seq, name="mem_attn1_bwd")
    dqb, dkv, dkrr = _attn_bwd(qb, kb, vb, o1.reshape(bsz, seq, MAIN), do1, lse, rope_c, rope_s, bsz, seq)
    dkr = _kr_bwd(dkrr, rope_c, rope_s)
    g_w_uq = _matmul(cqn, dqb, ta=True, name="d_w_uq")
    dcqn = _matmul(dqb, w_uq_t, name="d_cqn")
    gbig["mla_w_ukv"] = _matmul(ckvn, dkv, ta=True, also_bf16=True, name="d_w_ukv")
    dckvn = _matmul(dkv, w_ukv_t, name="d_ckvn")
    dcq, g_qg = _rms_bwd(p1, qg_full, dcqn, width=Q_RANK, col_block=P1_CQ // Q_RANK, out_dtype=BF16,
                         name="q_norm_bwd")
    dckv, g_kvg = _rms_bwd(p1, kvg_full, dckvn, width=KV_RANK, col_block=P1_CKV // KV_RANK, out_dtype=BF16,
                           name="kv_norm_bwd")
    dp1 = jnp.concatenate([dz1, dcq, dqm1.reshape(n, MEMW), dckv, dkr], axis=1)
    g_w_mla_in = _matmul(u1, dp1, ta=True, name="d_w_mla_in")
    du1 = _matmul(dp1, w_mla_in_t, name="d_u1")
    dh1, g_norm1 = _rms_bwd(h1, norm_g[1], du1, dh2, name="norm1_bwd")

    gbig["w_out0"] = _matmul(y0, dh1, ta=True, also_bf16=True, name="d_w_out0")
    dy0 = _matmul(dh1, w_o_t[0], name="d_y0")
    dhc, dymem0, dz0, g_ln_g, g_ln_b = _gate_bwd(dy0, hc, ymem0, p0, P0_Z // GATE_CHUNK, conv_ln_g, conv_ln_b,
                                                 name="gate0_bwd")
    dqm0, dkvm0 = _mem_bwd(p0, kvm[0], dymem0, P0_QM // MEMW, bsz, seq, name="mem_attn0_bwd")
    da, dg, g_dw32 = _conv_bwd(p0, dhc.reshape(bsz, seq, MAIN), dw_full, bsz, seq)
    dp0 = jnp.concatenate([da.reshape(n, MAIN), dg.reshape(n, MAIN), dqm0.reshape(n, MEMW), dz0], axis=1)
    gbig["conv_w_in"] = _matmul(u0, dp0, ta=True, also_bf16=True, name="d_w_conv_in")
    du0 = _matmul(dp0, w_conv_in_t, name="d_u0")
    grad_x, g_norm0 = _rms_bwd(x2, norm_g[0], du0, dh1, name="norm0_bwd")

    g_mem_g = []
    for i, dkvm in enumerate((dkvm0, dkvm1)):
        dk2 = dkvm.reshape(bsz * mlen, 2 * MEMW)
        gbig[f"w_mem_kv{i}"] = _matmul(memn[i], dk2, ta=True, also_bf16=True, name=f"d_w_mem_kv{i}")
        dmemn = _matmul(dk2, w_memkv_t[i], name=f"d_memn{i}")
        g_mem_g.append(_rms_bwd(mem2, mem_norm_g[i], dmemn, name=f"mem_norm{i}_bwd")[1])

    for nm, g_int, back in (("mla_w_in", g_w_mla_in, _mla_in_from_internal), ("mla_w_uq", g_w_uq, _uq_from_internal)):
        g_cm = _chip_major(back(g_int))
        gbig[nm] = (g_cm, g_cm.astype(BF16))
    g32 = [gbig[nm][0] for nm, _ in GEOMS]
    from_sibling = _swap_halves([gbig[nm][1] for nm, _ in GEOMS])
    core = jnp.reshape(ac, (1,)).astype(jnp.int32)
    pairs = [_add_pairs(g, core, g32[w], from_sibling[w], name=f"rs_add_pairs_{nm}")
             for w, (nm, g) in enumerate(GEOMS)]
    exch = _chip_exchange(pairs)
    chip_core = jnp.stack([chip, ac]).astype(jnp.int32)
    mine = [_add_final(g, chip_core, g32[w], from_sibling[w], exch[w], name=f"rs_add_final_{nm}")
            for w, (nm, g) in enumerate(GEOMS)]
    theirs = _share_with_sibling(mine)
    red = {}
    for (nm, _), a, b in zip(GEOMS, mine, theirs):
        red[nm] = jnp.concatenate([jnp.where(ac == 0, a, b), jnp.where(ac == 0, b, a)], axis=0)
    red["w_mem_kv"] = jnp.stack([red["w_mem_kv0"], red["w_mem_kv1"]])
    red["w_out"] = jnp.stack([red["w_out0"], red["w_out1"]])

    small_g = jnp.concatenate([
        _rows128(jnp.concatenate([g_norm0, g_norm1], axis=0), 16), _rows128(jnp.concatenate(g_mem_g, axis=0), 16),
        _rows128(g_dw32[CONV_K], 16), _rows128(g_ln_g, 16), _rows128(g_ln_b, 16), _rows128(d_final_g, 8),
        _rows128(g_dw32[:CONV_K], 376), _rows128(g_qg, 8), _rows128(g_kvg, 8)], axis=0)
    _, small_sum = _gather_sum_small(small_g, name="allreduce_small_grads")
    flat = small_sum.reshape(-1)

    def take(off, shape):
        size = 1
        for s_ in shape:
            size *= s_
        return flat[off * 128:off * 128 + size].reshape(shape)

    grads = dict(red)
    grads["conv_w_in"] = red["conv_w_in"][None]
    grads["mla_w_in"] = red["mla_w_in"][None]
    grads["mla_w_uq"] = red["mla_w_uq"][None]
    grads["mla_w_ukv"] = red["mla_w_ukv"][None]
    grads["norm_g"] = take(0, (2, D_MODEL))
    grads["mem_norm_g"] = take(16, (2, D_MODEL))
    grads["conv_dw_b"] = take(32, (1, MAIN))
    grads["conv_ln_g"] = take(48, (1, MAIN))
    grads["conv_ln_b"] = take(64, (1, MAIN))
    grads["final_norm_g"] = take(80, (D_MODEL,))
    grads["conv_dw"] = lax.dynamic_slice_in_dim(take(88, (CONV_K, MAIN)), chip * 384, 384, axis=1)[None]
    grads["mla_q_norm_g"] = lax.dynamic_slice_in_dim(take(464, (Q_RANK,)), chip * 128, 128, axis=0)[None]
    grads["mla_kv_norm_g"] = lax.dynamic_slice_in_dim(take(472, (KV_RANK,)), chip * 64, 64, axis=0)[None]

    params = dict(norm_g=norm_g, mem_norm_g=mem_norm_g, w_mem_kv=w_mem_kv, w_out=w_out, conv_w_in=conv_w_in,
                  conv_dw=conv_dw, conv_dw_b=conv_dw_b, conv_ln_g=conv_ln_g, conv_ln_b=conv_ln_b, mla_w_in=mla_w_in,
                  mla_q_norm_g=mla_q_norm_g, mla_w_uq=mla_w_uq, mla_kv_norm_g=mla_kv_norm_g, mla_w_ukv=mla_w_ukv,
                  final_norm_g=final_norm_g)
    mom1 = dict(norm_g=m_norm_g, mem_norm_g=m_mem_norm_g, w_mem_kv=m_w_mem_kv, w_out=m_w_out, conv_w_in=m_conv_w_in,
                conv_dw=m_conv_dw, conv_dw_b=m_conv_dw_b, conv_ln_g=m_conv_ln_g, conv_ln_b=m_conv_ln_b,
                mla_w_in=m_mla_w_in, mla_q_norm_g=m_mla_q_norm_g, mla_w_uq=m_mla_w_uq,
                mla_kv_norm_g=m_mla_kv_norm_g, mla_w_ukv=m_mla_w_ukv, final_norm_g=m_final_norm_g)
    mom2 = dict(norm_g=v_norm_g, mem_norm_g=v_mem_norm_g, w_mem_kv=v_w_mem_kv, w_out=v_w_out, conv_w_in=v_conv_w_in,
                conv_dw=v_conv_dw, conv_dw_b=v_conv_dw_b, conv_ln_g=v_conv_ln_g, conv_ln_b=v_conv_ln_b,
                mla_w_in=v_mla_w_in, mla_q_norm_g=v_mla_q_norm_g, mla_w_uq=v_mla_w_uq,
                mla_kv_norm_g=v_mla_kv_norm_g, mla_w_ukv=v_mla_w_ukv, final_norm_g=v_final_norm_g)
    names = list(params)
    g_out, deltas, new_m, new_v = [], [], [], []
    for nm in names:
        w = params[nm]
        g = grads[nm].reshape(w.shape)
        w2 = w.reshape(1, -1) if w.ndim == 1 else w
        dlt, m_new, v_new = _adamw(w2, g.reshape(w2.shape), mom1[nm].reshape(w2.shape), mom2[nm].reshape(w2.shape),
                                   name=f"adamw_{nm}")
        g_out.append(g)
        deltas.append(dlt.reshape(w.shape))
        new_m.append(m_new.reshape(w.shape))
        new_v.append(v_new.reshape(w.shape))

    return (loss, grad_x.reshape(bsz, seq, d), *g_out, *deltas, *new_m, *new_v)
```

```python
import functools

import jax
import jax.numpy as jnp
from jax import lax
from jax.experimental import pallas as pl
from jax.experimental.pallas import tpu as pltpu

F32 = jnp.float32
BF16 = jnp.bfloat16
MESH = pl.DeviceIdType.MESH

D_MODEL = 1024
MIX = 2048
MAIN = 1536
MEMW = 512
MEM_HEADS = 4
HEAD = 128
CONV_K = 31
CONV_PAD = 32
MLA_HEADS = 12
ROPE = 64
QK_PAD = 256
Q_RANK = 512
KV_RANK = 256
ROPE_THETA = 10000.0
RMS_EPS = 1e-6
LN_EPS = 1e-5
MEM_SCALE = HEAD ** -0.5
MLA_SCALE = (HEAD + ROPE) ** -0.5
NEG = -1e30

P0_COLS = 5632
P0_A, P0_G, P0_QM, P0_Z = 0, 1536, 3072, 3584
P1_COLS = 3456
P1_Z, P1_CQ, P1_QM, P1_CKV, P1_KR = 0, 2048, 2560, 3072, 3328

ADAM_LR = 0.001
ADAM_B1 = 0.9
ADAM_B2 = 0.999
ADAM_EPS = 1e-08
ADAM_WD = 0.01
ADAM_STEP = 10

VMEM_LIMIT = 56 * 1024 * 1024

ADD_TILE = 128


def _cparams(sem=None):
    return pltpu.CompilerParams(dimension_semantics=sem, vmem_limit_bytes=VMEM_LIMIT)


HBM = pl.BlockSpec(memory_space=pl.ANY)


class _Comm:
    def __init__(self, ins, out_shapes, n_sem, start, finish, aliases=None):
        self.ins, self.out_shapes, self.n_sem = list(ins), list(out_shapes), n_sem
        self.start, self.finish, self.aliases = start, finish, dict(aliases or {})


def _call(kernel_fn, comm, *, name, grid, in_specs, out_specs, out_shape, scratch_shapes, semantics, args):
    if comm is None:
        outs = pl.pallas_call(kernel_fn, name=name, grid=grid, in_specs=in_specs, out_specs=out_specs,
                              out_shape=out_shape, scratch_shapes=scratch_shapes,
                              compiler_params=_cparams(semantics))(*args)
        return list(outs), []
    n_in, n_out, nci, nco = len(in_specs), len(out_shape), len(comm.ins), len(comm.out_shapes)

    def body(*refs):
        ins, cins = refs[:n_in], refs[n_in:n_in + nci]
        outs = refs[n_in + nci:n_in + nci + n_out]
        couts = refs[n_in + nci + n_out:n_in + nci + n_out + nco]
        scratch, (ssem, rsem) = refs[n_in + nci + n_out + nco:-2], refs[-2:]
        ids = [pl.program_id(ax) for ax in range(len(grid))]
        first = functools.reduce(jnp.logical_and, [i == 0 for i in ids])
        last = functools.reduce(jnp.logical_and, [i == g - 1 for i, g in zip(ids, grid)])

        @pl.when(first)
        def _():
            comm.start(cins, couts, ssem, rsem)

        kernel_fn(*ins, *outs, *scratch)

        @pl.when(last)
        def _():
            comm.finish(cins, couts, ssem, rsem)

    outs = pl.pallas_call(
        body, name=name, grid=grid,
        in_specs=list(in_specs) + [HBM] * nci,
        out_specs=list(out_specs) + [HBM] * nco,
        out_shape=list(out_shape) + comm.out_shapes,
        scratch_shapes=list(scratch_shapes) + [pltpu.SemaphoreType.DMA((comm.n_sem,))] * 2,
        input_output_aliases={n_in + i: n_out + o for i, o in comm.aliases.items()},
        compiler_params=_cparams(("arbitrary",) * len(grid)),
    )(*args, *comm.ins)
    return list(outs[:n_out]), list(outs[n_out:])


def _tile(n, pref):
    if n <= pref:
        return n
    t = (pref // 128) * 128
    while t > 128 and n % t:
        t -= 128
    assert n % t == 0, (n, pref)
    return t


def _mm_kernel(*refs, nk, ta, has_res):
    if has_res:
        a_ref, b_ref, r_ref, *o_refs, acc_ref = refs
    else:
        a_ref, b_ref, *o_refs, acc_ref = refs
    k = pl.program_id(2)
    a = a_ref[...].astype(BF16)
    b = b_ref[...].astype(BF16)
    dn = (((0 if ta else 1,), (0,)), ((), ()))
    p = lax.dot_general(a, b, dn, preferred_element_type=F32)

    def finish(acc):
        if has_res:
            acc = r_ref[...] + acc
        for o in o_refs:
            o[...] = acc.astype(o.dtype)

    if nk == 1:
        finish(p)
        return

    @pl.when(k == 0)
    def _():
        acc_ref[...] = p

    @pl.when(jnp.logical_and(k > 0, k < nk - 1))
    def _():
        acc_ref[...] += p

    @pl.when(k == nk - 1)
    def _():
        finish(acc_ref[...] + p)


def _matmul(a, b, res=None, *, name, ta=False, out_dtype=F32, also_bf16=False, comm=None, tm=1024, tn=512, tk=2048):
    m, kd = (a.shape[1], a.shape[0]) if ta else a.shape
    n = b.shape[1]
    assert kd == b.shape[0]
    tm, tn, tk = _tile(m, tm), _tile(n, tn), _tile(kd, tk)
    nk = kd // tk
    a_spec = (pl.BlockSpec((tk, tm), lambda i, j, k: (k, i)) if ta
              else pl.BlockSpec((tm, tk), lambda i, j, k: (i, k)))
    out_spec = pl.BlockSpec((tm, tn), lambda i, j, k: (i, j))
    in_specs = [a_spec, pl.BlockSpec((tk, tn), lambda i, j, k: (k, j))]
    args = [a, b]
    if res is not None:
        in_specs.append(out_spec)
        args.append(res)
    out_shape = [jax.ShapeDtypeStruct((m, n), out_dtype)]
    if also_bf16:
        out_shape.append(jax.ShapeDtypeStruct((m, n), BF16))
    outs, carried = _call(
        functools.partial(_mm_kernel, nk=nk, ta=ta, has_res=res is not None), comm,
        name=name,
        grid=(m // tm, n // tn, nk),
        in_specs=in_specs,
        out_specs=[out_spec] * len(out_shape),
        out_shape=out_shape,
        scratch_shapes=[pltpu.VMEM((tm, tn), F32)],
        semantics=("parallel", "parallel", "arbitrary"),
        args=args)
    result = tuple(outs) if also_bf16 else outs[0]
    return result if comm is None else (result, carried)


def _rms_fwd_kernel(h_ref, g_ref, o_ref):
    h = h_ref[...]
    rstd = lax.rsqrt(jnp.mean(h * h, axis=-1, keepdims=True) + RMS_EPS)
    o_ref[...] = (h * rstd * g_ref[...]).astype(o_ref.dtype)


def _rms_fwd(h, g, *, name, width=None, col_block=0, tm=512):
    t = h.shape[0]
    width = width or h.shape[1]
    tm = _tile(t, tm)
    return pl.pallas_call(
        functools.partial(_rms_fwd_kernel),
        name=name,
        grid=(t // tm,),
        in_specs=[pl.BlockSpec((tm, width), lambda i: (i, col_block)),
                  pl.BlockSpec((1, width), lambda i: (0, 0))],
        out_specs=pl.BlockSpec((tm, width), lambda i: (i, 0)),
        out_shape=jax.ShapeDtypeStruct((t, width), BF16),
        compiler_params=_cparams(("parallel",)),
    )(h, g.reshape(1, width))


def _rms_bwd_math(h, g, du):
    rstd = lax.rsqrt(jnp.mean(h * h, axis=-1, keepdims=True) + RMS_EPS)
    dug = du * g
    dh = rstd * dug - h * (rstd * rstd * rstd) * jnp.mean(dug * h, axis=-1, keepdims=True)
    dg = jnp.sum(du * h * rstd, axis=0, keepdims=True)
    return dh, dg


def _rms_bwd_kernel(*refs, has_res):
    if has_res:
        h_ref, g_ref, du_ref, res_ref, dh_ref, dg_ref = refs
    else:
        h_ref, g_ref, du_ref, dh_ref, dg_ref = refs
    dh, dg = _rms_bwd_math(h_ref[...], g_ref[...], du_ref[...].astype(F32))
    if has_res:
        dh = dh + res_ref[...]
    dh_ref[...] = dh.astype(dh_ref.dtype)

    @pl.when(pl.program_id(0) == 0)
    def _():
        dg_ref[...] = dg

    @pl.when(pl.program_id(0) > 0)
    def _():
        dg_ref[...] += dg


def _rms_bwd(h, g, du, res=None, *, name, width=None, col_block=0, out_dtype=F32, tm=512):
    t = h.shape[0]
    width = width or h.shape[1]
    tm = _tile(t, tm)
    row = pl.BlockSpec((tm, width), lambda i: (i, 0))
    in_specs = [pl.BlockSpec((tm, width), lambda i: (i, col_block)),
                pl.BlockSpec((1, width), lambda i: (0, 0)), row]
    args = [h, g.reshape(1, width), du]
    if res is not None:
        in_specs.append(row)
        args.append(res)
    return pl.pallas_call(
        functools.partial(_rms_bwd_kernel, has_res=res is not None),
        name=name,
        grid=(t // tm,),
        in_specs=in_specs,
        out_specs=[row, pl.BlockSpec((1, width), lambda i: (0, 0))],
        out_shape=[jax.ShapeDtypeStruct((t, width), out_dtype), jax.ShapeDtypeStruct((1, width), F32)],
        compiler_params=_cparams(("arbitrary",)),
    )(*args)


def _final_kernel(h_ref, g_ref, t_ref, dh_ref, dg_ref, loss_ref):
    h = h_ref[...]
    g = g_ref[...]
    rstd = lax.rsqrt(jnp.mean(h * h, axis=-1, keepdims=True) + RMS_EPS)
    e = h * rstd * g - t_ref[...]
    part = 0.5 * jnp.sum(jnp.mean(e * e, axis=-1, keepdims=True), axis=0, keepdims=True)
    dh, dg = _rms_bwd_math(h, g, e * (1.0 / D_MODEL))
    dh_ref[...] = dh
    part = jnp.broadcast_to(part, loss_ref.shape)

    @pl.when(pl.program_id(0) == 0)
    def _():
        dg_ref[...] = dg
        loss_ref[...] = part

    @pl.when(pl.program_id(0) > 0)
    def _():
        dg_ref[...] += dg
        loss_ref[...] += part


def _final_loss(h, g, target, *, tm=512):
    t, d = h.shape
    tm = _tile(t, tm)
    row = pl.BlockSpec((tm, d), lambda i: (i, 0))
    return pl.pallas_call(
        functools.partial(_final_kernel),
        name="final_loss",
        grid=(t // tm,),
        in_specs=[row, pl.BlockSpec((1, d), lambda i: (0, 0)), row],
        out_specs=[row, pl.BlockSpec((1, d), lambda i: (0, 0)), pl.BlockSpec((1, 128), lambda i: (0, 0))],
        out_shape=[jax.ShapeDtypeStruct((t, d), F32), jax.ShapeDtypeStruct((1, d), F32),
                   jax.ShapeDtypeStruct((1, 128), F32)],
        compiler_params=_cparams(("arbitrary",)),
    )(h, g.reshape(1, d), target)


CONV_CT = 128
CONV_TC = 256


def _glu_into(pad_ref, a_ref, g_ref, seq, tc):
    ct = pad_ref.shape[1]
    pad_ref[0:CONV_PAD, :] = jnp.zeros((CONV_PAD, ct), F32)
    for r in range(0, seq, tc):
        a = a_ref[0, r:r + tc, :]
        g = g_ref[0, r:r + tc, :]
        pad_ref[CONV_PAD + r:CONV_PAD + r + tc, :] = a * jax.nn.sigmoid(g)


def _conv_fwd_kernel(a_ref, g_ref, dw_ref, dwb_ref, hc_ref, pad_ref, *, seq, tc):
    ct = pad_ref.shape[1]
    _glu_into(pad_ref, a_ref, g_ref, seq, tc)
    for r in range(0, seq, tc):
        acc = jnp.broadcast_to(dwb_ref[...], (tc, ct))
        for k in range(CONV_K):
            o = CONV_PAD + r - (CONV_K - 1) + k
            acc = acc + dw_ref[k:k + 1, :] * pad_ref[o:o + tc, :]
        hc_ref[0, r:r + tc, :] = acc


def _conv_fwd(p0, dw, dwb, bsz, seq):
    ct = CONV_CT
    tc = min(CONV_TC, seq)
    p3 = p0.reshape(bsz, seq, P0_COLS)
    return pl.pallas_call(
        functools.partial(_conv_fwd_kernel, seq=seq, tc=tc),
        name="conv_fwd",
        grid=(MAIN // ct, bsz),
        in_specs=[pl.BlockSpec((1, seq, ct), lambda j, b: (b, 0, P0_A // ct + j)),
                  pl.BlockSpec((1, seq, ct), lambda j, b: (b, 0, P0_G // ct + j)),
                  pl.BlockSpec((CONV_K, ct), lambda j, b: (0, j)),
                  pl.BlockSpec((1, ct), lambda j, b: (0, j))],
        out_specs=pl.BlockSpec((1, seq, ct), lambda j, b: (b, 0, j)),
        out_shape=jax.ShapeDtypeStruct((bsz, seq, MAIN), F32),
        scratch_shapes=[pltpu.VMEM((seq + CONV_PAD, ct), F32)],
        compiler_params=_cparams(("parallel", "parallel")),
    )(p3, p3, dw, dwb)


def _conv_bwd_kernel(a_ref, g_ref, dhc_ref, dw_ref, da_ref, dg_ref, ddw_ref, pad_ref, dpad_ref, acc_ref,
                     *, seq, tc):
    ct = pad_ref.shape[1]
    b = pl.program_id(1)
    _glu_into(pad_ref, a_ref, g_ref, seq, tc)
    dpad_ref[seq:seq + CONV_PAD, :] = jnp.zeros((CONV_PAD, ct), F32)
    for r in range(0, seq, tc):
        dpad_ref[r:r + tc, :] = dhc_ref[0, r:r + tc, :]
    acc_ref[...] = jnp.zeros(acc_ref.shape, F32)
    for r in range(0, seq, tc):
        dh = dhc_ref[0, r:r + tc, :]
        dglu = jnp.zeros((tc, ct), F32)
        for k in range(CONV_K):
            o = r + (CONV_K - 1) - k
            dglu = dglu + dw_ref[k:k + 1, :] * dpad_ref[o:o + tc, :]
            o = CONV_PAD + r - (CONV_K - 1) + k
            prod = pad_ref[o:o + tc, :] * dh
            acc_ref[k] += jnp.sum(prod.reshape(tc // 8, 8, ct), axis=0)
        acc_ref[CONV_K] += jnp.sum(dh.reshape(tc // 8, 8, ct), axis=0)
        a = a_ref[0, r:r + tc, :]
        sg = jax.nn.sigmoid(g_ref[0, r:r + tc, :])
        da_ref[0, r:r + tc, :] = (dglu * sg).astype(da_ref.dtype)
        dg_ref[0, r:r + tc, :] = (dglu * a * sg * (1.0 - sg)).astype(dg_ref.dtype)
    tot = jnp.sum(acc_ref[...], axis=1)

    @pl.when(b == 0)
    def _():
        ddw_ref[...] = tot

    @pl.when(b > 0)
    def _():
        ddw_ref[...] += tot


def _conv_bwd(p0, dhc, dw, bsz, seq, comm=None):
    ct = CONV_CT
    tc = min(CONV_TC, seq)
    p3 = p0.reshape(bsz, seq, P0_COLS)
    blk = pl.BlockSpec((1, seq, ct), lambda j, b: (b, 0, j))
    return _call(
        functools.partial(_conv_bwd_kernel, seq=seq, tc=tc), comm,
        name="conv_bwd",
        grid=(MAIN // ct, bsz),
        in_specs=[pl.BlockSpec((1, seq, ct), lambda j, b: (b, 0, P0_A // ct + j)),
                  pl.BlockSpec((1, seq, ct), lambda j, b: (b, 0, P0_G // ct + j)),
                  blk,
                  pl.BlockSpec((CONV_K, ct), lambda j, b: (0, j))],
        out_specs=[blk, blk, pl.BlockSpec((CONV_K + 1, ct), lambda j, b: (0, j))],
        out_shape=[jax.ShapeDtypeStruct((bsz, seq, MAIN), BF16), jax.ShapeDtypeStruct((bsz, seq, MAIN), BF16),
                   jax.ShapeDtypeStruct((CONV_K + 1, MAIN), F32)],
        scratch_shapes=[pltpu.VMEM((seq + CONV_PAD, ct), F32), pltpu.VMEM((seq + CONV_PAD, ct), F32),
                        pltpu.VMEM((CONV_K + 1, 8, ct), F32)],
        semantics=("parallel", "arbitrary"),
        args=[p3, p3, dhc, dw])


def _ln_parts(x, lng, lnb):
    mu = jnp.mean(x, axis=-1, keepdims=True)
    xc = x - mu
    rstd = lax.rsqrt(jnp.mean(xc * xc, axis=-1, keepdims=True) + LN_EPS)
    xh = xc * rstd
    hl = xh * lng + lnb
    return rstd, xh, hl


GATE_CHUNK = 512
GATE_NZ = MIX // GATE_CHUNK


def _z_specs(tm, zb):
    return [pl.BlockSpec((tm, GATE_CHUNK), lambda i, j=j: (i, zb + j)) for j in range(GATE_NZ)]


def _gate_fwd_kernel(*refs, ln):
    main_ref, ymem_ref, *z_refs = refs[:2 + GATE_NZ]
    y_ref = refs[-1]
    x = main_ref[...]
    if ln:
        _, _, hl = _ln_parts(x, refs[-3][...], refs[-2][...])
        x = hl * jax.nn.sigmoid(hl)
    for j, z_ref in enumerate(z_refs):
        cols = slice(j * GATE_CHUNK, (j + 1) * GATE_CHUNK)
        z = z_ref[...]
        src = x[:, cols] if j < MAIN // GATE_CHUNK else ymem_ref[...]
        y_ref[:, cols] = (src * (z * jax.nn.sigmoid(z))).astype(y_ref.dtype)


def _gate_fwd(main, ymem, p, zb, ln_g=None, ln_b=None, *, name, tm=256):
    t = main.shape[0]
    tm = _tile(t, tm)
    ln = ln_g is not None
    in_specs = [pl.BlockSpec((tm, MAIN), lambda i: (i, 0)), pl.BlockSpec((tm, MEMW), lambda i: (i, 0))]
    in_specs += _z_specs(tm, zb)
    args = [main, ymem] + [p] * GATE_NZ
    if ln:
        in_specs += [pl.BlockSpec((1, MAIN), lambda i: (0, 0))] * 2
        args += [ln_g.reshape(1, MAIN), ln_b.reshape(1, MAIN)]
    return pl.pallas_call(
        functools.partial(_gate_fwd_kernel, ln=ln),
        name=name,
        grid=(t // tm,),
        in_specs=in_specs,
        out_specs=pl.BlockSpec((tm, MIX), lambda i: (i, 0)),
        out_shape=jax.ShapeDtypeStruct((t, MIX), BF16),
        compiler_params=_cparams(("parallel",)),
    )(*args)


def _gate_bwd_kernel(*refs, ln):
    dy_ref, main_ref, ymem_ref, *z_refs = refs[:3 + GATE_NZ]
    if ln:
        lng_ref, lnb_ref, dmain_ref, dymem_ref, dz_ref, dlng_ref, dlnb_ref = refs[3 + GATE_NZ:]
    else:
        dmain_ref, dymem_ref, dz_ref = refs[3 + GATE_NZ:]
    x = main_ref[...]
    if ln:
        lng = lng_ref[...]
        rstd, xh, hl = _ln_parts(x, lng, lnb_ref[...])
        sh = jax.nn.sigmoid(hl)
        ymain = hl * sh
    else:
        ymain = x
    for j, z_ref in enumerate(z_refs):
        cols = slice(j * GATE_CHUNK, (j + 1) * GATE_CHUNK)
        dy = dy_ref[:, cols]
        z = z_ref[...]
        sg = jax.nn.sigmoid(z)
        dsz = sg * (1.0 + z * (1.0 - sg))
        if j < MAIN // GATE_CHUNK:
            src = ymain[:, cols]
            dmain_ref[:, cols] = dy * (z * sg)
        else:
            src = ymem_ref[...]
            dymem_ref[...] = dy * (z * sg)
        dz_ref[:, cols] = (dy * src * dsz).astype(dz_ref.dtype)
    if not ln:
        return
    dym = dmain_ref[...]
    dhl = dym * (sh * (1.0 + hl * (1.0 - sh)))
    dxh = dhl * lng
    dmain_ref[...] = rstd * (dxh - jnp.mean(dxh, axis=-1, keepdims=True)
                             - xh * jnp.mean(dxh * xh, axis=-1, keepdims=True))
    dlng = jnp.sum(dhl * xh, axis=0, keepdims=True)
    dlnb = jnp.sum(dhl, axis=0, keepdims=True)

    @pl.when(pl.program_id(0) == 0)
    def _():
        dlng_ref[...] = dlng
        dlnb_ref[...] = dlnb

    @pl.when(pl.program_id(0) > 0)
    def _():
        dlng_ref[...] += dlng
        dlnb_ref[...] += dlnb


def _gate_bwd(dy, main, ymem, p, zb, ln_g=None, ln_b=None, *, name, tm=256):
    t = main.shape[0]
    tm = _tile(t, tm)
    ln = ln_g is not None
    r_main = pl.BlockSpec((tm, MAIN), lambda i: (i, 0))
    r_mem = pl.BlockSpec((tm, MEMW), lambda i: (i, 0))
    r_mix = pl.BlockSpec((tm, MIX), lambda i: (i, 0))
    vec = pl.BlockSpec((1, MAIN), lambda i: (0, 0))
    in_specs = [r_mix, r_main, r_mem] + _z_specs(tm, zb)
    args = [dy, main, ymem] + [p] * GATE_NZ
    out_specs = [r_main, r_mem, r_mix]
    out_shape = [jax.ShapeDtypeStruct((t, MAIN), F32), jax.ShapeDtypeStruct((t, MEMW), F32),
                 jax.ShapeDtypeStruct((t, MIX), BF16)]
    if ln:
        in_specs += [vec, vec]
        args += [ln_g.reshape(1, MAIN), ln_b.reshape(1, MAIN)]
        out_specs += [vec, vec]
        out_shape += [jax.ShapeDtypeStruct((1, MAIN), F32)] * 2
    return pl.pallas_call(
        functools.partial(_gate_bwd_kernel, ln=ln),
        name=name,
        grid=(t // tm,),
        in_specs=in_specs,
        out_specs=out_specs,
        out_shape=out_shape,
        compiler_params=_cparams(("arbitrary",)),
    )(*args)


def _dot_nt(a, b):
    return lax.dot_general(a, b, (((1,), (1,)), ((), ())), preferred_element_type=F32)


def _dot_tn(a, b):
    return lax.dot_general(a, b, (((0,), (0,)), ((), ())), preferred_element_type=F32)


def _dot(a, b):
    return jnp.dot(a, b, preferred_element_type=F32)


def _mem_probs(q, k):
    s = _dot_nt(q, k) * MEM_SCALE
    p = jnp.exp(s - jnp.max(s, axis=-1, keepdims=True))
    return p / jnp.sum(p, axis=-1, keepdims=True)


def _mem_fwd_kernel(q_ref, kv_ref, o_ref):
    for h in range(MEM_HEADS):
        c = slice(h * HEAD, (h + 1) * HEAD)
        cv = slice(MEMW + h * HEAD, MEMW + (h + 1) * HEAD)
        p = _mem_probs(q_ref[0, :, c].astype(BF16), kv_ref[0, :, c])
        o_ref[0, :, c] = _dot(p.astype(BF16), kv_ref[0, :, cv])


def _mem_fwd(p, kvm, col_block, bsz, seq, *, name, tq=512):
    tq = _tile(seq, tq)
    p3 = p.reshape(bsz, seq, p.shape[1])
    mlen = kvm.shape[1]
    return pl.pallas_call(
        functools.partial(_mem_fwd_kernel),
        name=name,
        grid=(bsz, seq // tq),
        in_specs=[pl.BlockSpec((1, tq, MEMW), lambda b, i: (b, i, col_block)),
                  pl.BlockSpec((1, mlen, 2 * MEMW), lambda b, i: (b, 0, 0))],
        out_specs=pl.BlockSpec((1, tq, MEMW), lambda b, i: (b, i, 0)),
        out_shape=jax.ShapeDtypeStruct((bsz, seq, MEMW), F32),
        compiler_params=_cparams(("parallel", "parallel")),
    )(p3, kvm)


def _mem_bwd_kernel(q_ref, kv_ref, do_ref, dq_ref, dkv_ref):
    @pl.when(pl.program_id(1) == 0)
    def _():
        dkv_ref[...] = jnp.zeros(dkv_ref.shape, F32)

    for h in range(MEM_HEADS):
        c = slice(h * HEAD, (h + 1) * HEAD)
        cv = slice(MEMW + h * HEAD, MEMW + (h + 1) * HEAD)
        q = q_ref[0, :, c].astype(BF16)
        k = kv_ref[0, :, c]
        v = kv_ref[0, :, cv]
        do = do_ref[0, :, c].astype(BF16)
        p = _mem_probs(q, k)
        dp = _dot_nt(do, v)
        ds = (p * (dp - jnp.sum(p * dp, axis=-1, keepdims=True)) * MEM_SCALE).astype(BF16)
        dq_ref[0, :, c] = _dot(ds, k).astype(dq_ref.dtype)
        dkv_ref[0, :, c] += _dot_tn(ds, q)
        dkv_ref[0, :, cv] += _dot_tn(p.astype(BF16), do)


def _mem_bwd(p, kvm, dymem, col_block, bsz, seq, *, name, tq=512):
    tq = _tile(seq, tq)
    p3 = p.reshape(bsz, seq, p.shape[1])
    mlen = kvm.shape[1]
    return pl.pallas_call(
        functools.partial(_mem_bwd_kernel),
        name=name,
        grid=(bsz, seq // tq),
        in_specs=[pl.BlockSpec((1, tq, MEMW), lambda b, i: (b, i, col_block)),
                  pl.BlockSpec((1, mlen, 2 * MEMW), lambda b, i: (b, 0, 0)),
                  pl.BlockSpec((1, tq, MEMW), lambda b, i: (b, i, 0))],
        out_specs=[pl.BlockSpec((1, tq, MEMW), lambda b, i: (b, i, 0)),
                   pl.BlockSpec((1, mlen, 2 * MEMW), lambda b, i: (b, 0, 0))],
        out_shape=[jax.ShapeDtypeStruct((bsz, seq, MEMW), BF16),
                   jax.ShapeDtypeStruct((bsz, mlen, 2 * MEMW), F32)],
        compiler_params=_cparams(("parallel", "arbitrary")),
    )(p3, kvm, dymem.reshape(bsz, seq, MEMW))


def _swap32(x):
    lane = lax.broadcasted_iota(jnp.int32, x.shape, 1)
    return jnp.where(lane < 32, pltpu.roll(x, 96, 1), pltpu.roll(x, 32, 1))


def _rope(x, cs, sn):
    return x * cs + _swap32(x) * sn


def _rope_t(d, cs, sn):
    return d * cs + _swap32(d * sn)


UP_HEADS = 2


def _q_up_kernel(a_ref, b_ref, cs_ref, sn_ref, o_ref):
    acc = _dot(a_ref[...], b_ref[...])
    cs = cs_ref[...]
    sn = sn_ref[...]
    for h in range(UP_HEADS):
        c0 = slice(h * QK_PAD, h * QK_PAD + HEAD)
        c1 = slice(h * QK_PAD + HEAD, (h + 1) * QK_PAD)
        o_ref[:, c0] = acc[:, c0].astype(o_ref.dtype)
        o_ref[:, c1] = _rope(acc[:, c1], cs, sn).astype(o_ref.dtype)


def _q_up(cqn, w_uq, cs, sn, *, tm=512):
    t, kd = cqn.shape
    tm = _tile(t, tm)
    tn = UP_HEADS * QK_PAD
    tab = pl.BlockSpec((tm, 128), lambda i, j: (i, 0))
    return pl.pallas_call(
        functools.partial(_q_up_kernel),
        name="q_up_rope",
        grid=(t // tm, MLA_HEADS // UP_HEADS),
        in_specs=[pl.BlockSpec((tm, kd), lambda i, j: (i, 0)), pl.BlockSpec((kd, tn), lambda i, j: (0, j)), tab, tab],
        out_specs=pl.BlockSpec((tm, tn), lambda i, j: (i, j)),
        out_shape=jax.ShapeDtypeStruct((t, MLA_HEADS * QK_PAD), BF16),
        compiler_params=_cparams(("parallel", "parallel")),
    )(cqn, w_uq, cs, sn)


def _kv_up_kernel(a_ref, b_ref, kr_ref, cs_ref, sn_ref, k_ref, v_ref):
    acc = _dot(a_ref[...], b_ref[...])
    krr = _rope(kr_ref[...], cs_ref[...], sn_ref[...]).astype(k_ref.dtype)
    for h in range(UP_HEADS):
        k_ref[:, h * QK_PAD:h * QK_PAD + HEAD] = acc[:, h * 2 * HEAD:h * 2 * HEAD + HEAD].astype(k_ref.dtype)
        k_ref[:, h * QK_PAD + HEAD:(h + 1) * QK_PAD] = krr
        v_ref[:, h * HEAD:(h + 1) * HEAD] = acc[:, h * 2 * HEAD + HEAD:(h + 1) * 2 * HEAD].astype(v_ref.dtype)


def _kv_up(ckvn, w_ukv, p1, cs, sn, *, tm=512):
    t, kd = ckvn.shape
    tm = _tile(t, tm)
    tab = pl.BlockSpec((tm, 128), lambda i, j: (i, 0))
    return pl.pallas_call(
        functools.partial(_kv_up_kernel),
        name="kv_up_pack",
        grid=(t // tm, MLA_HEADS // UP_HEADS),
        in_specs=[pl.BlockSpec((tm, kd), lambda i, j: (i, 0)),
                  pl.BlockSpec((kd, UP_HEADS * 2 * HEAD), lambda i, j: (0, j)),
                  pl.BlockSpec((tm, 128), lambda i, j: (i, P1_KR // 128)), tab, tab],
        out_specs=[pl.BlockSpec((tm, UP_HEADS * QK_PAD), lambda i, j: (i, j)),
                   pl.BlockSpec((tm, UP_HEADS * HEAD), lambda i, j: (i, j))],
        out_shape=[jax.ShapeDtypeStruct((t, MLA_HEADS * QK_PAD), BF16), jax.ShapeDtypeStruct((t, MAIN), BF16)],
        compiler_params=_cparams(("parallel", "parallel")),
    )(ckvn, w_ukv, p1, cs, sn)


def _kr_bwd_kernel(d_ref, cs_ref, sn_ref, o_ref):
    acc = d_ref[:, :HEAD]
    for h in range(1, MLA_HEADS):
        acc = acc + d_ref[:, h * HEAD:(h + 1) * HEAD]
    o_ref[...] = _rope_t(acc, cs_ref[...], sn_ref[...]).astype(o_ref.dtype)


def _kr_bwd(dkrr, cs, sn, *, tm=512):
    t = dkrr.shape[0]
    tm = _tile(t, tm)
    tab = pl.BlockSpec((tm, 128), lambda i: (i, 0))
    return pl.pallas_call(
        functools.partial(_kr_bwd_kernel),
        name="kr_bwd",
        grid=(t // tm,),
        in_specs=[pl.BlockSpec((tm, MAIN), lambda i: (i, 0)), tab, tab],
        out_specs=tab,
        out_shape=jax.ShapeDtypeStruct((t, 128), BF16),
        compiler_params=_cparams(("parallel",)),
    )(dkrr, cs, sn)


ATT_T = 256


def _causal(s, t):
    row = lax.broadcasted_iota(jnp.int32, (t, t), 0)
    col = lax.broadcasted_iota(jnp.int32, (t, t), 1)
    return jnp.where(col <= row, s, NEG)


def _attn_fwd_kernel(q_ref, k_ref, v_ref, o_ref, lse_ref, *, seq, t):
    for i in range(seq // t):
        own = slice(i * t, (i + 1) * t)
        q = q_ref[0, own, :]
        sd = _causal(_dot_nt(q, k_ref[0, own, :]) * MLA_SCALE, t)
        m = jnp.max(sd, axis=-1, keepdims=True)
        if i:
            so = _dot_nt(q, k_ref[0, :i * t, :]) * MLA_SCALE
            m = jnp.maximum(m, jnp.max(so, axis=-1, keepdims=True))
        pd = jnp.exp(sd - m)
        l = jnp.sum(pd, axis=-1, keepdims=True)
        acc = _dot(pd.astype(BF16), v_ref[0, own, :])
        if i:
            po = jnp.exp(so - m)
            l = l + jnp.sum(po, axis=-1, keepdims=True)
            acc = acc + _dot(po.astype(BF16), v_ref[0, :i * t, :])
        o_ref[0, own, :] = acc / l
        lse_ref[0, 0, own, :] = m + jnp.log(l)


def _attn_fwd(qb, kb, vb, bsz, seq):
    t = min(ATT_T, seq)
    q3 = qb.reshape(bsz, seq, MLA_HEADS * QK_PAD)
    k3 = kb.reshape(bsz, seq, MLA_HEADS * QK_PAD)
    v3 = vb.reshape(bsz, seq, MAIN)
    qk = pl.BlockSpec((1, seq, QK_PAD), lambda b, h: (b, 0, h))
    vv = pl.BlockSpec((1, seq, HEAD), lambda b, h: (b, 0, h))
    return pl.pallas_call(
        functools.partial(_attn_fwd_kernel, seq=seq, t=t),
        name="attn_fwd",
        grid=(bsz, MLA_HEADS),
        in_specs=[qk, qk, vv],
        out_specs=[vv, pl.BlockSpec((1, 1, seq, 1), lambda b, h: (b, h, 0, 0))],
        out_shape=[jax.ShapeDtypeStruct((bsz, seq, MAIN), F32),
                   jax.ShapeDtypeStruct((bsz, MLA_HEADS, seq, 1), F32)],
        compiler_params=_cparams(("parallel", "parallel")),
    )(q3, k3, v3)


def _attn_bwd_kernel(q_ref, k_ref, v_ref, o_ref, do_ref, lse_ref, cs_ref, sn_ref, dq_ref, dkv_ref, dkr_ref,
                     delta_ref, dqacc_ref, *, seq, t):
    for r in range(0, seq, t):
        rows = slice(r, r + t)
        delta_ref[rows, :] = jnp.sum(do_ref[0, rows, :] * o_ref[0, rows, :], axis=-1, keepdims=True)
    dqacc_ref[...] = jnp.zeros(dqacc_ref.shape, F32)

    def piece(rows, k, v, masked):
        q = q_ref[0, rows, :]
        do = do_ref[0, rows, :].astype(BF16)
        s = _dot_nt(q, k) * MLA_SCALE
        if masked:
            s = _causal(s, t)
        p = jnp.exp(s - lse_ref[0, 0, rows, :])
        dp = _dot_nt(do, v)
        ds = (p * (dp - delta_ref[rows, :]) * MLA_SCALE).astype(BF16)
        dqacc_ref[rows, :] += _dot(ds, k)
        return _dot_tn(ds, q), _dot_tn(p.astype(BF16), do)

    for j in range(seq // t):
        own = slice(j * t, (j + 1) * t)
        k = k_ref[0, own, :]
        v = v_ref[0, own, :]
        dk, dv = piece(own, k, v, True)
        if (j + 1) * t < seq:
            dk2, dv2 = piece(slice((j + 1) * t, seq), k, v, False)
            dk, dv = dk + dk2, dv + dv2
        dkv_ref[0, own, :HEAD] = dk[:, :HEAD].astype(dkv_ref.dtype)
        dkv_ref[0, own, HEAD:] = dv.astype(dkv_ref.dtype)
        dkr_ref[0, own, :] = dk[:, HEAD:]
    for r in range(0, seq, t):
        rows = slice(r, r + t)
        dq = dqacc_ref[rows, :]
        dq_ref[0, rows, :HEAD] = dq[:, :HEAD].astype(dq_ref.dtype)
        dq_ref[0, rows, HEAD:] = _rope_t(dq[:, HEAD:], cs_ref[0, rows, :], sn_ref[0, rows, :]).astype(dq_ref.dtype)


def _attn_bwd(qb, kb, vb, o, do, lse, cs, sn, bsz, seq):
    t = min(ATT_T, seq)
    q3 = qb.reshape(bsz, seq, MLA_HEADS * QK_PAD)
    k3 = kb.reshape(bsz, seq, MLA_HEADS * QK_PAD)
    v3 = vb.reshape(bsz, seq, MAIN)
    qk = pl.BlockSpec((1, seq, QK_PAD), lambda b, h: (b, 0, h))
    vv = pl.BlockSpec((1, seq, HEAD), lambda b, h: (b, 0, h))
    tab = pl.BlockSpec((1, seq, 128), lambda b, h: (b, 0, 0))
    dq, dkv, dkr = pl.pallas_call(
        functools.partial(_attn_bwd_kernel, seq=seq, t=t),
        name="attn_bwd",
        grid=(bsz, MLA_HEADS),
        in_specs=[qk, qk, vv, vv, vv, pl.BlockSpec((1, 1, seq, 1), lambda b, h: (b, h, 0, 0)), tab, tab],
        out_specs=[qk, qk, vv],
        out_shape=[jax.ShapeDtypeStruct((bsz, seq, MLA_HEADS * QK_PAD), BF16),
                   jax.ShapeDtypeStruct((bsz, seq, MLA_HEADS * 2 * HEAD), BF16),
                   jax.ShapeDtypeStruct((bsz, seq, MAIN), F32)],
        scratch_shapes=[pltpu.VMEM((seq, 1), F32), pltpu.VMEM((seq, QK_PAD), F32)],
        compiler_params=_cparams(("parallel", "parallel")),
    )(q3, k3, v3, o, do.reshape(bsz, seq, MAIN), lse, cs.reshape(bsz, seq, 128), sn.reshape(bsz, seq, 128))
    n = bsz * seq
    return dq.reshape(n, -1), dkv.reshape(n, -1), dkr.reshape(n, -1)


def _adamw_kernel(w_ref, g_ref, m_ref, v_ref, d_ref, nm_ref, nv_ref):
    g = g_ref[...]
    m = ADAM_B1 * m_ref[...] + (1.0 - ADAM_B1) * g
    v = ADAM_B2 * v_ref[...] + (1.0 - ADAM_B2) * (g * g)
    m_hat = m / (1.0 - ADAM_B1 ** ADAM_STEP)
    v_hat = v / (1.0 - ADAM_B2 ** ADAM_STEP)
    d_ref[...] = -ADAM_LR * (m_hat / (jnp.sqrt(v_hat) + ADAM_EPS) + ADAM_WD * w_ref[...])
    nm_ref[...] = m
    nv_ref[...] = v


def _adamw(w, g, m, v, *, name):
    shape = w.shape
    c = shape[-1]
    r = w.size // c
    tr = r
    for cand in (512, 256, 128, 64, 32, 16, 8):
        if r % cand == 0 and cand * c * 4 <= 2 * 1024 * 1024:
            tr = cand
            break
    blk = pl.BlockSpec((tr, c), lambda i: (i, 0))
    outs = pl.pallas_call(
        functools.partial(_adamw_kernel),
        name=name,
        grid=(r // tr,),
        in_specs=[blk] * 4,
        out_specs=[blk] * 3,
        out_shape=[jax.ShapeDtypeStruct((r, c), F32)] * 3,
        compiler_params=_cparams(("parallel",)),
    )(w.reshape(r, c), g.reshape(r, c), m.reshape(r, c), v.reshape(r, c))
    return tuple(o.reshape(shape) for o in outs)


def _place():
    return lax.axis_index("x"), lax.axis_index("y"), lax.axis_index("c")


def _other_chips(x, y):
    return [(1 - x, y), (x, 1 - y), (1 - x, 1 - y)]


class _Geom:
    def __init__(self, kind, rows, cols):
        self.kind, self.rows, self.cols, self.hr = kind, rows, cols, rows // 2
        self.full_shape = {"rows": (4 * rows, cols), "cols": (rows, 4 * cols), "chips": (4, rows, cols)}[kind]
        self.nt = self.hr // ADD_TILE

    def view(self, ref, s, h):
        if self.kind == "rows":
            return ref.at[pl.ds(s * self.rows + h * self.hr, self.hr), :]
        if self.kind == "cols":
            return ref.at[pl.ds(h * self.hr, self.hr), pl.ds(s * self.cols, self.cols)]
        return ref.at[s, pl.ds(h * self.hr, self.hr), :]

    def shard_half(self, ref, h):
        return ref.at[pl.ds(h * self.hr, self.hr), :]

    def shard_view(self, ref, s):
        if self.kind == "rows":
            return ref.at[pl.ds(s * self.rows, self.rows), :]
        if self.kind == "cols":
            return ref.at[:, pl.ds(s * self.cols, self.cols)]
        return ref.at[s]

    def tile_spec(self, chip_half_of):
        if self.kind == "rows":
            def index(*a):
                s, h, i = chip_half_of(*a)
                return (s * (self.rows // ADD_TILE) + h * self.nt + i, 0)
            return pl.BlockSpec((ADD_TILE, self.cols), index)
        if self.kind == "cols":
            def index(*a):
                s, h, i = chip_half_of(*a)
                return (h * self.nt + i, s)
            return pl.BlockSpec((ADD_TILE, self.cols), index)

        def index(*a):
            s, h, i = chip_half_of(*a)
            return (s, h * self.nt + i, 0)
        return pl.BlockSpec((None, ADD_TILE, self.cols), index)


GEOMS = (("w_mem_kv0", _Geom("rows", 256, 1024)), ("w_mem_kv1", _Geom("rows", 256, 1024)),
         ("w_out0", _Geom("rows", 512, 1024)), ("w_out1", _Geom("rows", 512, 1024)),
         ("conv_w_in", _Geom("cols", 1024, 1408)), ("mla_w_ukv", _Geom("cols", 256, 768)),
         ("mla_w_in", _Geom("chips", 1024, 848)), ("mla_w_uq", _Geom("chips", 512, 576)))
N_BIG = len(GEOMS)


def _remote(k, src_ref, dst_ref, to, ssem, rsem):
    return pltpu.make_async_remote_copy(src_ref=src_ref, dst_ref=dst_ref, send_sem=ssem.at[k], recv_sem=rsem.at[k],
                                        device_id=to, device_id_type=MESH)


GROUP_A = ("w_mem_kv0", "w_out0", "conv_w_in")
GROUP_B = ("w_mem_kv1", "w_out1", "mla_w_ukv", "mla_w_in", "mla_w_uq")
GEOM = dict(GEOMS)


def _gather_over_ici(geoms, srcs, outs, ssem, rsem, act):
    x, y, c = _place()
    s = 2 * x + y
    n = len(geoms)
    for w, g in enumerate(geoms):
        mine = pltpu.make_async_copy(srcs[w], g.shard_view(outs[w], s), ssem.at[3 * n + w])
        sends = [_remote(3 * w + j, g.shard_half(srcs[w], c), g.view(outs[w], s, c), (*chip, c), ssem, rsem)
                 for j, chip in enumerate(_other_chips(x, y))]
        if act == "start":
            mine.start()
            for cp in sends:
                cp.start()
        else:
            for j, chip in enumerate(_other_chips(x, y)):
                blk = g.view(outs[w], 2 * chip[0] + chip[1], c)
                _remote(3 * w + j, blk, blk, (*chip, c), ssem, rsem).wait_recv()
            for cp in sends:
                cp.wait_send()
            mine.wait()


def _forward_to_sibling(geoms, fulls, ssem, rsem, act):
    x, y, c = _place()
    for w, g in enumerate(geoms):
        for j, chip in enumerate(_other_chips(x, y)):
            s = 2 * chip[0] + chip[1]
            mine, theirs = g.view(fulls[w], s, c), g.view(fulls[w], s, 1 - c)
            if act == "start":
                _remote(3 * w + j, mine, mine, (x, y, 1 - c), ssem, rsem).start()
            else:
                _remote(3 * w + j, theirs, theirs, (x, y, 1 - c), ssem, rsem).wait_recv()
                _remote(3 * w + j, mine, mine, (x, y, 1 - c), ssem, rsem).wait_send()


def _gather_ici_comm(names, shards):
    geoms = [GEOM[nm] for nm in names]
    return _Comm(shards, [jax.ShapeDtypeStruct(g.full_shape, BF16) for g in geoms], 4 * len(geoms),
                 lambda i, o, ss, rs: _gather_over_ici(geoms, i, o, ss, rs, "start"),
                 lambda i, o, ss, rs: _gather_over_ici(geoms, i, o, ss, rs, "wait"))


def _gather_forward_comm(names, fulls):
    geoms = [GEOM[nm] for nm in names]
    return _Comm(fulls, [jax.ShapeDtypeStruct(f.shape, f.dtype) for f in fulls], 3 * len(geoms),
                 lambda i, o, ss, rs: _forward_to_sibling(geoms, o, ss, rs, "start"),
                 lambda i, o, ss, rs: _forward_to_sibling(geoms, o, ss, rs, "wait"),
                 aliases={w: w for w in range(len(geoms))})


def _allgather_kernel(*refs, geoms):
    n = len(geoms)
    srcs, outs, (ssem, rsem, fsem, gsem) = refs[:n], refs[n:2 * n], refs[2 * n:]
    _gather_over_ici(geoms, srcs, outs, ssem, rsem, "start")
    _gather_over_ici(geoms, srcs, outs, ssem, rsem, "wait")
    _forward_to_sibling(geoms, outs, fsem, gsem, "start")
    _forward_to_sibling(geoms, outs, fsem, gsem, "wait")


def _allgather_weights(names, shards):
    geoms = [GEOM[nm] for nm in names]
    n = len(geoms)
    return pl.pallas_call(
        functools.partial(_allgather_kernel, geoms=geoms),
        name="allgather_weights",
        in_specs=[HBM] * n,
        out_specs=[HBM] * n,
        out_shape=[jax.ShapeDtypeStruct(g.full_shape, BF16) for g in geoms],
        scratch_shapes=[pltpu.SemaphoreType.DMA((4 * n,)), pltpu.SemaphoreType.DMA((4 * n,)),
                        pltpu.SemaphoreType.DMA((3 * n,)), pltpu.SemaphoreType.DMA((3 * n,))],
    )(*shards)


def _swap_halves_kernel(*refs, geoms):
    n = len(geoms)
    srcs, dsts, (ssem, rsem) = refs[:n], refs[n:2 * n], refs[2 * n:]
    x, y, c = _place()
    cps = []
    for w, g in enumerate(geoms):
        for s in range(4):
            cps.append(_remote(4 * w + s, g.view(srcs[w], s, 1 - c), dsts[w].at[s], (x, y, 1 - c), ssem, rsem))
    for cp in cps:
        cp.start()
    for cp in cps:
        cp.wait()


def _swap_halves(names, gb, *, name):
    geoms = [GEOM[nm] for nm in names]
    n = len(geoms)
    return pl.pallas_call(
        functools.partial(_swap_halves_kernel, geoms=geoms),
        name=name,
        in_specs=[HBM] * n,
        out_specs=[HBM] * n,
        out_shape=[jax.ShapeDtypeStruct((4, g.hr, g.cols), BF16) for g in geoms],
        scratch_shapes=[pltpu.SemaphoreType.DMA((4 * n,)), pltpu.SemaphoreType.DMA((4 * n,))],
    )(*gb)


def _exchange_with_chips(srcs, dsts, ssem, rsem, act):
    x, y, c = _place()
    for w in range(len(srcs)):
        for j, chip in enumerate(_other_chips(x, y)):
            cp = _remote(3 * w + j, srcs[w].at[2 * chip[0] + chip[1]], dsts[w].at[j], (*chip, c), ssem, rsem)
            if act == "start":
                cp.start()
            else:
                cp.wait()


def _exchange_comm(pairs):
    return _Comm(pairs, [jax.ShapeDtypeStruct((3,) + p.shape[1:], p.dtype) for p in pairs], 3 * len(pairs),
                 lambda i, o, ss, rs: _exchange_with_chips(i, o, ss, rs, "start"),
                 lambda i, o, ss, rs: _exchange_with_chips(i, o, ss, rs, "wait"))


def _chip_exchange_kernel(*refs):
    n = (len(refs) - 2) // 2
    srcs, dsts, (ssem, rsem) = refs[:n], refs[n:2 * n], refs[2 * n:]
    _exchange_with_chips(srcs, dsts, ssem, rsem, "start")
    _exchange_with_chips(srcs, dsts, ssem, rsem, "wait")


def _chip_exchange(pairs, *, name):
    n = len(pairs)
    return pl.pallas_call(
        functools.partial(_chip_exchange_kernel),
        name=name,
        in_specs=[HBM] * n,
        out_specs=[HBM] * n,
        out_shape=[jax.ShapeDtypeStruct((3,) + p.shape[1:], p.dtype) for p in pairs],
        scratch_shapes=[pltpu.SemaphoreType.DMA((3 * n,)), pltpu.SemaphoreType.DMA((3 * n,))],
    )(*pairs)


def _share_kernel(*refs):
    srcs, dsts, (ssem, rsem) = refs[:N_BIG], refs[N_BIG:2 * N_BIG], refs[2 * N_BIG:]
    x, y, c = _place()
    cps = [_remote(w, srcs[w], dsts[w], (x, y, 1 - c), ssem, rsem) for w in range(N_BIG)]
    for cp in cps:
        cp.start()
    for cp in cps:
        cp.wait()


def _share_with_sibling(halves):
    return pl.pallas_call(
        functools.partial(_share_kernel),
        name="rs_share_halves",
        in_specs=[HBM] * N_BIG,
        out_specs=[HBM] * N_BIG,
        out_shape=[jax.ShapeDtypeStruct(h.shape, h.dtype) for h in halves],
        scratch_shapes=[pltpu.SemaphoreType.DMA((N_BIG,)), pltpu.SemaphoreType.DMA((N_BIG,))],
    )(*halves)


def _gather_sum_kernel(src, gat, tot, ssem, rsem):
    x, y, c = _place()
    me = 4 * x + 2 * y + c
    gat[me] = src[...]
    flips = [(dx, dy, dc) for dx in (0, 1) for dy in (0, 1) for dc in (0, 1)][1:]
    cps = []
    for k, (dx, dy, dc) in enumerate(flips):
        peer = (1 - x if dx else x, 1 - y if dy else y, 1 - c if dc else c)
        cp = pltpu.make_async_remote_copy(src_ref=src, dst_ref=gat.at[me], send_sem=ssem.at[k], recv_sem=rsem.at[k],
                                          device_id=peer, device_id_type=MESH)
        cp.start()
        cps.append((cp, 4 * peer[0] + 2 * peer[1] + peer[2], peer))
    for k, (cp, idx, peer) in enumerate(cps):
        pltpu.make_async_remote_copy(src_ref=src, dst_ref=gat.at[idx], send_sem=ssem.at[k], recv_sem=rsem.at[k],
                                     device_id=peer, device_id_type=MESH).wait_recv()
    for cp, _, _ in cps:
        cp.wait_send()
    acc = gat[0]
    for d in range(1, 8):
        acc = acc + gat[d]
    tot[...] = acc


def _gather_sum_small(a, *, name):
    vm = pl.BlockSpec(memory_space=pltpu.VMEM)
    return pl.pallas_call(
        functools.partial(_gather_sum_kernel),
        name=name,
        in_specs=[vm],
        out_specs=[vm, vm],
        out_shape=[jax.ShapeDtypeStruct((8,) + a.shape, a.dtype), jax.ShapeDtypeStruct(a.shape, a.dtype)],
        scratch_shapes=[pltpu.SemaphoreType.DMA((7,)), pltpu.SemaphoreType.DMA((7,))],
    )(a)


def _add_pairs_kernel(c_ref, g_ref, r_ref, o_ref):
    o_ref[...] = (g_ref[...] + r_ref[...].astype(F32)).astype(o_ref.dtype)


def _add_pairs(geom, core, g, recv, *, name):
    half = pl.BlockSpec((None, ADD_TILE, geom.cols), lambda s, i, cr: (s, i, 0))
    return pl.pallas_call(
        functools.partial(_add_pairs_kernel),
        name=name,
        grid_spec=pltpu.PrefetchScalarGridSpec(
            num_scalar_prefetch=1,
            grid=(4, geom.nt),
            in_specs=[geom.tile_spec(lambda s, i, cr: (s, cr[0], i)), half],
            out_specs=half,
        ),
        out_shape=jax.ShapeDtypeStruct(recv.shape, BF16),
        compiler_params=_cparams(("parallel", "parallel")),
    )(core, g, recv)


def _add_final_kernel(sc_ref, g_ref, r_ref, e_ref, o_ref):
    acc = g_ref[...] + r_ref[...].astype(F32)
    for j in range(3):
        acc = acc + e_ref[j].astype(F32)
    o_ref[...] = acc


def _add_final(geom, chip_core, g, recv, exch, *, name):
    return pl.pallas_call(
        functools.partial(_add_final_kernel),
        name=name,
        grid_spec=pltpu.PrefetchScalarGridSpec(
            num_scalar_prefetch=1,
            grid=(geom.nt,),
            in_specs=[geom.tile_spec(lambda i, sc: (sc[0], sc[1], i)),
                      pl.BlockSpec((None, ADD_TILE, geom.cols), lambda i, sc: (sc[0], i, 0)),
                      pl.BlockSpec((3, ADD_TILE, geom.cols), lambda i, sc: (0, i, 0))],
            out_specs=pl.BlockSpec((ADD_TILE, geom.cols), lambda i, sc: (i, 0)),
        ),
        out_shape=jax.ShapeDtypeStruct((geom.hr, geom.cols), F32),
        compiler_params=_cparams(("parallel",)),
    )(chip_core, g, recv, exch)


def _chip_major(w):
    return w.reshape(w.shape[0], 4, w.shape[1] // 4).transpose(1, 0, 2)


def _from_chip_major(w):
    return w.transpose(1, 0, 2).reshape(w.shape[1], 4 * w.shape[2])


def _mla_in_to_internal(w):
    return jnp.concatenate([w[:, 1344:], w[:, :512], w[:, 832:1344], w[:, 512:768], w[:, 768:832],
                            jnp.zeros((w.shape[0], 64), w.dtype)], axis=1)


def _mla_in_from_internal(w):
    return jnp.concatenate([w[:, P1_CQ:P1_QM], w[:, P1_CKV:P1_KR], w[:, P1_KR:P1_KR + 64], w[:, P1_QM:P1_CKV],
                            w[:, :MIX]], axis=1)


def _uq_to_internal(w):
    w = w.reshape(w.shape[0], MLA_HEADS, HEAD + ROPE)
    return jnp.pad(w, ((0, 0), (0, 0), (0, QK_PAD - HEAD - ROPE))).reshape(w.shape[0], MLA_HEADS * QK_PAD)


def _uq_from_internal(w):
    return w.reshape(w.shape[0], MLA_HEADS, QK_PAD)[:, :, :HEAD + ROPE].reshape(w.shape[0], MLA_HEADS * (HEAD + ROPE))


def _rows128(a, rows):
    flat = a.reshape(-1)
    return jnp.pad(flat, (0, rows * 128 - flat.shape[0])).reshape(rows, 128)


def kernel(x, mem, positions, norm_g, mem_norm_g, w_mem_kv, w_out, conv_w_in, conv_dw, conv_dw_b, conv_ln_g, conv_ln_b, mla_w_in, mla_q_norm_g, mla_w_uq, mla_kv_norm_g, mla_w_ukv, final_norm_g, loss_target, m_norm_g, m_mem_norm_g, m_w_mem_kv, m_w_out, m_conv_w_in, m_conv_dw, m_conv_dw_b, m_conv_ln_g, m_conv_ln_b, m_mla_w_in, m_mla_q_norm_g, m_mla_w_uq, m_mla_kv_norm_g, m_mla_w_ukv, m_final_norm_g, v_norm_g, v_mem_norm_g, v_w_mem_kv, v_w_out, v_conv_w_in, v_conv_dw, v_conv_dw_b, v_conv_ln_g, v_conv_ln_b, v_mla_w_in, v_mla_q_norm_g, v_mla_w_uq, v_mla_kv_norm_g, v_mla_w_ukv, v_final_norm_g):
    bsz, seq, d = x.shape
    n = bsz * seq
    mlen = mem.shape[1]
    ax, ay, ac = _place()
    chip = 2 * ax + ay

    shards = dict(w_mem_kv0=w_mem_kv[0], w_mem_kv1=w_mem_kv[1], w_out0=w_out[0], w_out1=w_out[1],
                  conv_w_in=conv_w_in[0], mla_w_ukv=mla_w_ukv[0], mla_w_in=mla_w_in[0], mla_w_uq=mla_w_uq[0])
    wf = dict(zip(GROUP_A, _allgather_weights(GROUP_A, [shards[nm].astype(BF16) for nm in GROUP_A])))
    w_conv_in = wf["conv_w_in"]
    w_conv_in_t = w_conv_in.T

    small_in = jnp.concatenate([_rows128(conv_dw[0], 96), _rows128(mla_q_norm_g, 8), _rows128(mla_kv_norm_g, 8)],
                               axis=0)
    small_all, _ = _gather_sum_small(small_in, name="gather_small_params")
    small_all = small_all[0::2]
    dw_full = small_all[:, :93].reshape(4, -1)[:, :CONV_K * 384].reshape(4, CONV_K, 384)
    dw_full = dw_full.transpose(1, 0, 2).reshape(CONV_K, MAIN)
    qg_full = small_all[:, 96].reshape(Q_RANK)
    kvg_full = small_all[:, 104, :64].reshape(KV_RANK)

    inv_freq = 1.0 / (ROPE_THETA ** (jnp.arange(0, ROPE, 2, dtype=F32) / ROPE))
    ang = positions.astype(F32).reshape(n, 1) * inv_freq
    cos, sin, zer = jnp.cos(ang), jnp.sin(ang), jnp.zeros((n, 64), F32)
    rope_c = jnp.concatenate([cos, cos, zer], axis=1)
    rope_s = jnp.concatenate([-sin, sin, zer], axis=1)

    x2 = x.reshape(n, d)
    mem2 = mem.reshape(bsz * mlen, d)
    tgt2 = loss_target.reshape(n, d)

    memn = [_rms_fwd(mem2, mem_norm_g[i], name=f"mem_norm{i}") for i in range(2)]
    kvm = [_matmul(memn[0], wf["w_mem_kv0"], out_dtype=BF16, name="mem_kv0").reshape(bsz, mlen, 2 * MEMW)]

    u0 = _rms_fwd(x2, norm_g[0], name="norm0")
    p0, landed = _matmul(u0, w_conv_in, name="conv_in_proj",
                         comm=_gather_ici_comm(GROUP_B, [shards[nm].astype(BF16) for nm in GROUP_B]))
    hc = _conv_fwd(p0, dw_full, conv_dw_b, bsz, seq).reshape(n, MAIN)
    ymem0 = _mem_fwd(p0, kvm[0], P0_QM // MEMW, bsz, seq, name="mem_attn0").reshape(n, MEMW)
    y0 = _gate_fwd(hc, ymem0, p0, P0_Z // GATE_CHUNK, conv_ln_g, conv_ln_b, name="gate0")
    h1, gathered_b = _matmul(y0, wf["w_out0"], x2, name="out_proj0", comm=_gather_forward_comm(GROUP_B, landed))
    wf.update(zip(GROUP_B, gathered_b))
    w_mla_in = _mla_in_to_internal(_from_chip_major(wf["mla_w_in"]))
    w_uq = _uq_to_internal(_from_chip_major(wf["mla_w_uq"]))
    w_ukv = wf["mla_w_ukv"]
    w_memkv = [wf["w_mem_kv0"], wf["w_mem_kv1"]]
    w_o = [wf["w_out0"], wf["w_out1"]]
    w_mla_in_t, w_uq_t, w_ukv_t = w_mla_in.T, w_uq.T, w_ukv.T
    w_memkv_t = [w.T for w in w_memkv]
    w_o_t = [w.T for w in w_o]
    kvm.append(_matmul(memn[1], w_memkv[1], out_dtype=BF16, name="mem_kv1").reshape(bsz, mlen, 2 * MEMW))

    u1 = _rms_fwd(h1, norm_g[1], name="norm1")
    p1 = _matmul(u1, w_mla_in, name="mla_in_proj")
    cqn = _rms_fwd(p1, qg_full, width=Q_RANK, col_block=P1_CQ // Q_RANK, name="q_norm")
    ckvn = _rms_fwd(p1, kvg_full, width=KV_RANK, col_block=P1_CKV // KV_RANK, name="kv_norm")
    qb = _q_up(cqn, w_uq, rope_c, rope_s)
    kb, vb = _kv_up(ckvn, w_ukv, p1, rope_c, rope_s)
    o1, lse = _attn_fwd(qb, kb, vb, bsz, seq)
    o1 = o1.reshape(n, MAIN)
    ymem1 = _mem_fwd(p1, kvm[1], P1_QM // MEMW, bsz, seq, name="mem_attn1").reshape(n, MEMW)
    y1 = _gate_fwd(o1, ymem1, p1, P1_Z // GATE_CHUNK, name="gate1")
    h2 = _matmul(y1, w_o[1], h1, name="out_proj1")

    dh2, d_final_g, loss_part = _final_loss(h2, final_norm_g, tgt2)
    loss = lax.psum(loss_part[0, 0], ("x", "y", "c"))

    gbig = {}
    gbig["w_out1"] = _matmul(y1, dh2, ta=True, also_bf16=True, name="d_w_out1")
    dy1 = _matmul(dh2, w_o_t[1], name="d_y1")
    do1, dymem1, dz1 = _gate_bwd(dy1, o1, ymem1, p1, P1_Z // GATE_CHUNK, name="gate1_bwd")
    dqm1, dkvm1 = _mem_bwd(p1, kvm[1], dymem1, P1_QM // MEMW, bsz, seq, name="mem_attn1_bwd")
    dqb, dkv, dkrr = _attn_bwd(qb, kb, vb, o1.reshape(bsz, seq, MAIN), do1, lse, rope_c, rope_s, bsz, seq)
    dkr = _kr_bwd(dkrr, rope_c, rope_s)
    g_w_uq = _matmul(cqn, dqb, ta=True, name="d_w_uq")
    dcqn = _matmul(dqb, w_uq_t, name="d_cqn")
    gbig["mla_w_ukv"] = _matmul(ckvn, dkv, ta=True, also_bf16=True, name="d_w_ukv")
    dckvn = _matmul(dkv, w_ukv_t, name="d_ckvn")
    dcq, g_qg = _rms_bwd(p1, qg_full, dcqn, width=Q_RANK, col_block=P1_CQ // Q_RANK, out_dtype=BF16,
                         name="q_norm_bwd")
    dckv, g_kvg = _rms_bwd(p1, kvg_full, dckvn, width=KV_RANK, col_block=P1_CKV // KV_RANK, out_dtype=BF16,
                           name="kv_norm_bwd")
    dp1 = jnp.concatenate([dz1, dcq, dqm1.reshape(n, MEMW), dckv, dkr], axis=1)
    g_w_mla_in = _matmul(u1, dp1, ta=True, name="d_w_mla_in")
    du1 = _matmul(dp1, w_mla_in_t, name="d_u1")
    dh1, g_norm1 = _rms_bwd(h1, norm_g[1], du1, dh2, name="norm1_bwd")

    def mem_kv_bwd(i, dkvm):
        dk2 = dkvm.reshape(bsz * mlen, 2 * MEMW)
        gbig[f"w_mem_kv{i}"] = _matmul(memn[i], dk2, ta=True, also_bf16=True, name=f"d_w_mem_kv{i}")
        dmemn = _matmul(dk2, w_memkv_t[i], name=f"d_memn{i}")
        return _rms_bwd(mem2, mem_norm_g[i], dmemn, name=f"mem_norm{i}_bwd")[1]

    g_mem_g1 = mem_kv_bwd(1, dkvm1)
    for nm, g_int, back in (("mla_w_in", g_w_mla_in, _mla_in_from_internal), ("mla_w_uq", g_w_uq, _uq_from_internal)):
        g_cm = _chip_major(back(g_int))
        gbig[nm] = (g_cm, g_cm.astype(BF16))

    core = jnp.reshape(ac, (1,)).astype(jnp.int32)
    chip_core = jnp.stack([chip, ac]).astype(jnp.int32)

    def pair_sums(names, tag):
        from_sib = _swap_halves(names, [gbig[nm][1] for nm in names], name=f"rs_sibling_swap_{tag}")
        return from_sib, [_add_pairs(GEOM[nm], core, gbig[nm][0], fs, name=f"rs_add_pairs_{nm}")
                          for nm, fs in zip(names, from_sib)]

    from_sib_b, pairs_b = pair_sums(GROUP_B, "b")

    gbig["w_out0"] = _matmul(y0, dh1, ta=True, also_bf16=True, name="d_w_out0")
    dy0 = _matmul(dh1, w_o_t[0], name="d_y0")
    dhc, dymem0, dz0, g_ln_g, g_ln_b = _gate_bwd(dy0, hc, ymem0, p0, P0_Z // GATE_CHUNK, conv_ln_g, conv_ln_b,
                                                 name="gate0_bwd")
    dqm0, dkvm0 = _mem_bwd(p0, kvm[0], dymem0, P0_QM // MEMW, bsz, seq, name="mem_attn0_bwd")
    (da, dg, g_dw32), exch_b = _conv_bwd(p0, dhc.reshape(bsz, seq, MAIN), dw_full, bsz, seq,
                                         comm=_exchange_comm(pairs_b))
    dp0 = jnp.concatenate([da.reshape(n, MAIN), dg.reshape(n, MAIN), dqm0.reshape(n, MEMW), dz0], axis=1)
    gbig["conv_w_in"] = _matmul(u0, dp0, ta=True, also_bf16=True, name="d_w_conv_in")
    du0 = _matmul(dp0, w_conv_in_t, name="d_u0")
    grad_x, g_norm0 = _rms_bwd(x2, norm_g[0], du0, dh1, name="norm0_bwd")

    g_mem_g = [mem_kv_bwd(0, dkvm0), g_mem_g1]

    from_sib_a, pairs_a = pair_sums(GROUP_A, "a")
    exch_a = _chip_exchange(pairs_a, name="rs_chip_exchange_a")
    from_sibling = dict(zip(GROUP_A + GROUP_B, list(from_sib_a) + list(from_sib_b)))
    exch = dict(zip(GROUP_A + GROUP_B, list(exch_a) + list(exch_b)))
    mine = [_add_final(g, chip_core, gbig[nm][0], from_sibling[nm], exch[nm], name=f"rs_add_final_{nm}")
            for nm, g in GEOMS]
    theirs = _share_with_sibling(mine)
    red = {}
    for (nm, _), a, b in zip(GEOMS, mine, theirs):
        red[nm] = jnp.concatenate([jnp.where(ac == 0, a, b), jnp.where(ac == 0, b, a)], axis=0)
    red["w_mem_kv"] = jnp.stack([red["w_mem_kv0"], red["w_mem_kv1"]])
    red["w_out"] = jnp.stack([red["w_out0"], red["w_out1"]])

    small_g = jnp.concatenate([
        _rows128(jnp.concatenate([g_norm0, g_norm1], axis=0), 16), _rows128(jnp.concatenate(g_mem_g, axis=0), 16),
        _rows128(g_dw32[CONV_K], 16), _rows128(g_ln_g, 16), _rows128(g_ln_b, 16), _rows128(d_final_g, 8),
        _rows128(g_dw32[:CONV_K], 376), _rows128(g_qg, 8), _rows128(g_kvg, 8)], axis=0)
    _, small_sum = _gather_sum_small(small_g, name="allreduce_small_grads")
    flat = small_sum.reshape(-1)

    def take(off, shape):
        size = 1
        for s_ in shape:
            size *= s_
        return flat[off * 128:off * 128 + size].reshape(shape)

    grads = dict(red)
    grads["conv_w_in"] = red["conv_w_in"][None]
    grads["mla_w_in"] = red["mla_w_in"][None]
    grads["mla_w_uq"] = red["mla_w_uq"][None]
    grads["mla_w_ukv"] = red["mla_w_ukv"][None]
    grads["norm_g"] = take(0, (2, D_MODEL))
    grads["mem_norm_g"] = take(16, (2, D_MODEL))
    grads["conv_dw_b"] = take(32, (1, MAIN))
    grads["conv_ln_g"] = take(48, (1, MAIN))
    grads["conv_ln_b"] = take(64, (1, MAIN))
    grads["final_norm_g"] = take(80, (D_MODEL,))
    grads["conv_dw"] = lax.dynamic_slice_in_dim(take(88, (CONV_K, MAIN)), chip * 384, 384, axis=1)[None]
    grads["mla_q_norm_g"] = lax.dynamic_slice_in_dim(take(464, (Q_RANK,)), chip * 128, 128, axis=0)[None]
    grads["mla_kv_norm_g"] = lax.dynamic_slice_in_dim(take(472, (KV_RANK,)), chip * 64, 64, axis=0)[None]

    params = dict(norm_g=norm_g, mem_norm_g=mem_norm_g, w_mem_kv=w_mem_kv, w_out=w_out, conv_w_in=conv_w_in,
                  conv_dw=conv_dw, conv_dw_b=conv_dw_b, conv_ln_g=conv_ln_g, conv_ln_b=conv_ln_b, mla_w_in=mla_w_in,
                  mla_q_norm_g=mla_q_norm_g, mla_w_uq=mla_w_uq, mla_kv_norm_g=mla_kv_norm_g, mla_w_ukv=mla_w_ukv,
                  final_norm_g=final_norm_g)
    mom1 = dict(norm_g=m_norm_g, mem_norm_g=m_mem_norm_g, w_mem_kv=m_w_mem_kv, w_out=m_w_out, conv_w_in=m_conv_w_in,
                conv_dw=m_conv_dw, conv_dw_b=m_conv_dw_b, conv_ln_g=m_conv_ln_g, conv_ln_b=m_conv_ln_b,
                mla_w_in=m_mla_w_in, mla_q_norm_g=m_mla_q_norm_g, mla_w_uq=m_mla_w_uq,
                mla_kv_norm_g=m_mla_kv_norm_g, mla_w_ukv=m_mla_w_ukv, final_norm_g=m_final_norm_g)
    mom2 = dict(norm_g=v_norm_g, mem_norm_g=v_mem_norm_g, w_mem_kv=v_w_mem_kv, w_out=v_w_out, conv_w_in=v_conv_w_in,
                conv_dw=v_conv_dw, conv_dw_b=v_conv_dw_b, conv_ln_g=v_conv_ln_g, conv_ln_b=v_conv_ln_b,
                mla_w_in=v_mla_w_in, mla_q_norm_g=v_mla_q_norm_g, mla_w_uq=v_mla_w_uq,
                mla_kv_norm_g=v_mla_kv_norm_g, mla_w_ukv=v_mla_w_ukv, final_norm_g=v_final_norm_g)
    names = list(params)
    g_out, deltas, new_m, new_v = [], [], [], []
    for nm in names:
        w = params[nm]
        g = grads[nm].reshape(w.shape)
        w2 = w.reshape(1, -1) if w.ndim == 1 else w
        dlt, m_new, v_new = _adamw(w2, g.reshape(w2.shape), mom1[nm].reshape(w2.shape), mom2[nm].reshape(w2.shape),
                                   name=f"adamw_{nm}")
        g_out.append(g)
        deltas.append(dlt.reshape(w.shape))
        new_m.append(m_new.reshape(w.shape))
        new_v.append(v_new.reshape(w.shape))

    return (loss, grad_x.reshape(bsz, seq, d), *g_out, *deltas, *new_m, *new_v)
```

```python
import functools

import jax
import jax.numpy as jnp
from jax import lax
from jax.experimental import pallas as pl
from jax.experimental.pallas import tpu as pltpu

F32 = jnp.float32
BF16 = jnp.bfloat16
MESH = pl.DeviceIdType.MESH

D_MODEL = 1024
MIX = 2048
MAIN = 1536
MEMW = 512
MEM_HEADS = 4
HEAD = 128
CONV_K = 31
CONV_PAD = 32
MLA_HEADS = 12
ROPE = 64
QK_PAD = 256
Q_RANK = 512
KV_RANK = 256
ROPE_THETA = 10000.0
RMS_EPS = 1e-6
LN_EPS = 1e-5
MEM_SCALE = HEAD ** -0.5
MLA_SCALE = (HEAD + ROPE) ** -0.5
NEG = -1e30

P0_COLS = 5632
P0_A, P0_G, P0_QM, P0_Z = 0, 1536, 3072, 3584
P1_COLS = 3456
P1_Z, P1_CQ, P1_QM, P1_CKV, P1_KR = 0, 2048, 2560, 3072, 3328

ADAM_LR = 0.001
ADAM_B1 = 0.9
ADAM_B2 = 0.999
ADAM_EPS = 1e-08
ADAM_WD = 0.01
ADAM_STEP = 10

VMEM_LIMIT = 56 * 1024 * 1024

ADD_TILE = 128


def _cparams(sem=None):
    return pltpu.CompilerParams(dimension_semantics=sem, vmem_limit_bytes=VMEM_LIMIT)


HBM = pl.BlockSpec(memory_space=pl.ANY)


class _Comm:
    def __init__(self, ins, out_shapes, n_sem, start, finish, aliases=None):
        self.ins, self.out_shapes, self.n_sem = list(ins), list(out_shapes), n_sem
        self.start, self.finish, self.aliases = start, finish, dict(aliases or {})


def _call(kernel_fn, comm, *, name, grid, in_specs, out_specs, out_shape, scratch_shapes, semantics, args):
    if comm is None:
        outs = pl.pallas_call(kernel_fn, name=name, grid=grid, in_specs=in_specs, out_specs=out_specs,
                              out_shape=out_shape, scratch_shapes=scratch_shapes,
                              compiler_params=_cparams(semantics))(*args)
        return list(outs), []
    n_in, n_out, nci, nco = len(in_specs), len(out_shape), len(comm.ins), len(comm.out_shapes)

    def body(*refs):
        ins, cins = refs[:n_in], refs[n_in:n_in + nci]
        outs = refs[n_in + nci:n_in + nci + n_out]
        couts = refs[n_in + nci + n_out:n_in + nci + n_out + nco]
        scratch, (ssem, rsem) = refs[n_in + nci + n_out + nco:-2], refs[-2:]
        ids = [pl.program_id(ax) for ax in range(len(grid))]
        first = functools.reduce(jnp.logical_and, [i == 0 for i in ids])
        last = functools.reduce(jnp.logical_and, [i == g - 1 for i, g in zip(ids, grid)])

        @pl.when(first)
        def _():
            comm.start(cins, couts, ssem, rsem)

        kernel_fn(*ins, *outs, *scratch)

        @pl.when(last)
        def _():
            comm.finish(cins, couts, ssem, rsem)

    outs = pl.pallas_call(
        body, name=name, grid=grid,
        in_specs=list(in_specs) + [HBM] * nci,
        out_specs=list(out_specs) + [HBM] * nco,
        out_shape=list(out_shape) + comm.out_shapes,
        scratch_shapes=list(scratch_shapes) + [pltpu.SemaphoreType.DMA((comm.n_sem,))] * 2,
        input_output_aliases={n_in + i: n_out + o for i, o in comm.aliases.items()},
        compiler_params=_cparams(("arbitrary",) * len(grid)),
    )(*args, *comm.ins)
    return list(outs[:n_out]), list(outs[n_out:])


def _tile(n, pref):
    if n <= pref:
        return n
    t = (pref // 128) * 128
    while t > 128 and n % t:
        t -= 128
    assert n % t == 0, (n, pref)
    return t


def _mm_kernel(*refs, nk, ta, has_res):
    if has_res:
        a_ref, b_ref, r_ref, *o_refs, acc_ref = refs
    else:
        a_ref, b_ref, *o_refs, acc_ref = refs
    k = pl.program_id(2)
    a = a_ref[...].astype(BF16)
    b = b_ref[...].astype(BF16)
    dn = (((0 if ta else 1,), (0,)), ((), ()))
    p = lax.dot_general(a, b, dn, preferred_element_type=F32)

    def finish(acc):
        if has_res:
            acc = r_ref[...] + acc
        for o in o_refs:
            o[...] = acc.astype(o.dtype)

    if nk == 1:
        finish(p)
        return

    @pl.when(k == 0)
    def _():
        acc_ref[...] = p

    @pl.when(jnp.logical_and(k > 0, k < nk - 1))
    def _():
        acc_ref[...] += p

    @pl.when(k == nk - 1)
    def _():
        finish(acc_ref[...] + p)


def _matmul(a, b, res=None, *, name, ta=False, out_dtype=F32, also_bf16=False, comm=None, tm=1024, tn=512, tk=2048):
    m, kd = (a.shape[1], a.shape[0]) if ta else a.shape
    n = b.shape[1]
    assert kd == b.shape[0]
    tm, tn, tk = _tile(m, tm), _tile(n, tn), _tile(kd, tk)
    nk = kd // tk
    a_spec = (pl.BlockSpec((tk, tm), lambda i, j, k: (k, i)) if ta
              else pl.BlockSpec((tm, tk), lambda i, j, k: (i, k)))
    out_spec = pl.BlockSpec((tm, tn), lambda i, j, k: (i, j))
    in_specs = [a_spec, pl.BlockSpec((tk, tn), lambda i, j, k: (k, j))]
    args = [a, b]
    if res is not None:
        in_specs.append(out_spec)
        args.append(res)
    out_shape = [jax.ShapeDtypeStruct((m, n), out_dtype)]
    if also_bf16:
        out_shape.append(jax.ShapeDtypeStruct((m, n), BF16))
    outs, carried = _call(
        functools.partial(_mm_kernel, nk=nk, ta=ta, has_res=res is not None), comm,
        name=name,
        grid=(m // tm, n // tn, nk),
        in_specs=in_specs,
        out_specs=[out_spec] * len(out_shape),
        out_shape=out_shape,
        scratch_shapes=[pltpu.VMEM((tm, tn), F32)],
        semantics=("parallel", "parallel", "arbitrary"),
        args=args)
    result = tuple(outs) if also_bf16 else outs[0]
    return result if comm is None else (result, carried)


def _rms_fwd_kernel(h_ref, g_ref, o_ref):
    h = h_ref[...]
    rstd = lax.rsqrt(jnp.mean(h * h, axis=-1, keepdims=True) + RMS_EPS)
    o_ref[...] = (h * rstd * g_ref[...]).astype(o_ref.dtype)


def _rms_fwd(h, g, *, name, width=None, col_block=0, tm=512):
    t = h.shape[0]
    width = width or h.shape[1]
    tm = _tile(t, tm)
    return pl.pallas_call(
        functools.partial(_rms_fwd_kernel),
        name=name,
        grid=(t // tm,),
        in_specs=[pl.BlockSpec((tm, width), lambda i: (i, col_block)),
                  pl.BlockSpec((1, width), lambda i: (0, 0))],
        out_specs=pl.BlockSpec((tm, width), lambda i: (i, 0)),
        out_shape=jax.ShapeDtypeStruct((t, width), BF16),
        compiler_params=_cparams(("parallel",)),
    )(h, g.reshape(1, width))


def _rms_bwd_math(h, g, du):
    rstd = lax.rsqrt(jnp.mean(h * h, axis=-1, keepdims=True) + RMS_EPS)
    dug = du * g
    dh = rstd * dug - h * (rstd * rstd * rstd) * jnp.mean(dug * h, axis=-1, keepdims=True)
    dg = jnp.sum(du * h * rstd, axis=0, keepdims=True)
    return dh, dg


def _rms_bwd_kernel(*refs, has_res):
    if has_res:
        h_ref, g_ref, du_ref, res_ref, dh_ref, dg_ref = refs
    else:
        h_ref, g_ref, du_ref, dh_ref, dg_ref = refs
    dh, dg = _rms_bwd_math(h_ref[...], g_ref[...], du_ref[...].astype(F32))
    if has_res:
        dh = dh + res_ref[...]
    dh_ref[...] = dh.astype(dh_ref.dtype)

    @pl.when(pl.program_id(0) == 0)
    def _():
        dg_ref[...] = dg

    @pl.when(pl.program_id(0) > 0)
    def _():
        dg_ref[...] += dg


def _rms_bwd(h, g, du, res=None, *, name, width=None, col_block=0, out_dtype=F32, tm=512):
    t = h.shape[0]
    width = width or h.shape[1]
    tm = _tile(t, tm)
    row = pl.BlockSpec((tm, width), lambda i: (i, 0))
    in_specs = [pl.BlockSpec((tm, width), lambda i: (i, col_block)),
                pl.BlockSpec((1, width), lambda i: (0, 0)), row]
    args = [h, g.reshape(1, width), du]
    if res is not None:
        in_specs.append(row)
        args.append(res)
    return pl.pallas_call(
        functools.partial(_rms_bwd_kernel, has_res=res is not None),
        name=name,
        grid=(t // tm,),
        in_specs=in_specs,
        out_specs=[row, pl.BlockSpec((1, width), lambda i: (0, 0))],
        out_shape=[jax.ShapeDtypeStruct((t, width), out_dtype), jax.ShapeDtypeStruct((1, width), F32)],
        compiler_params=_cparams(("arbitrary",)),
    )(*args)


def _final_kernel(h_ref, g_ref, t_ref, dh_ref, dg_ref, loss_ref):
    h = h_ref[...]
    g = g_ref[...]
    rstd = lax.rsqrt(jnp.mean(h * h, axis=-1, keepdims=True) + RMS_EPS)
    e = h * rstd * g - t_ref[...]
    part = 0.5 * jnp.sum(jnp.mean(e * e, axis=-1, keepdims=True), axis=0, keepdims=True)
    dh, dg = _rms_bwd_math(h, g, e * (1.0 / D_MODEL))
    dh_ref[...] = dh
    part = jnp.broadcast_to(part, loss_ref.shape)

    @pl.when(pl.program_id(0) == 0)
    def _():
        dg_ref[...] = dg
        loss_ref[...] = part

    @pl.when(pl.program_id(0) > 0)
    def _():
        dg_ref[...] += dg
        loss_ref[...] += part


def _final_loss(h, g, target, *, tm=512):
    t, d = h.shape
    tm = _tile(t, tm)
    row = pl.BlockSpec((tm, d), lambda i: (i, 0))
    return pl.pallas_call(
        functools.partial(_final_kernel),
        name="final_loss",
        grid=(t // tm,),
        in_specs=[row, pl.BlockSpec((1, d), lambda i: (0, 0)), row],
        out_specs=[row, pl.BlockSpec((1, d), lambda i: (0, 0)), pl.BlockSpec((1, 128), lambda i: (0, 0))],
        out_shape=[jax.ShapeDtypeStruct((t, d), F32), jax.ShapeDtypeStruct((1, d), F32),
                   jax.ShapeDtypeStruct((1, 128), F32)],
        compiler_params=_cparams(("arbitrary",)),
    )(h, g.reshape(1, d), target)


CONV_CT = 128
CONV_TC = 256


def _glu_into(pad_ref, a_ref, g_ref, seq, tc):
    ct = pad_ref.shape[1]
    pad_ref[0:CONV_PAD, :] = jnp.zeros((CONV_PAD, ct), F32)
    for r in range(0, seq, tc):
        a = a_ref[0, r:r + tc, :]
        g = g_ref[0, r:r + tc, :]
        pad_ref[CONV_PAD + r:CONV_PAD + r + tc, :] = a * jax.nn.sigmoid(g)


def _conv_fwd_kernel(a_ref, g_ref, dw_ref, dwb_ref, hc_ref, pad_ref, *, seq, tc):
    ct = pad_ref.shape[1]
    _glu_into(pad_ref, a_ref, g_ref, seq, tc)
    for r in range(0, seq, tc):
        acc = jnp.broadcast_to(dwb_ref[...], (tc, ct))
        for k in range(CONV_K):
            o = CONV_PAD + r - (CONV_K - 1) + k
            acc = acc + dw_ref[k:k + 1, :] * pad_ref[o:o + tc, :]
        hc_ref[0, r:r + tc, :] = acc


def _conv_fwd(p0, dw, dwb, bsz, seq, comm=None):
    ct = CONV_CT
    tc = min(CONV_TC, seq)
    p3 = p0.reshape(bsz, seq, P0_COLS)
    outs, carried = _call(
        functools.partial(_conv_fwd_kernel, seq=seq, tc=tc), comm,
        name="conv_fwd",
        grid=(MAIN // ct, bsz),
        in_specs=[pl.BlockSpec((1, seq, ct), lambda j, b: (b, 0, P0_A // ct + j)),
                  pl.BlockSpec((1, seq, ct), lambda j, b: (b, 0, P0_G // ct + j)),
                  pl.BlockSpec((CONV_K, ct), lambda j, b: (0, j)),
                  pl.BlockSpec((1, ct), lambda j, b: (0, j))],
        out_specs=[pl.BlockSpec((1, seq, ct), lambda j, b: (b, 0, j))],
        out_shape=[jax.ShapeDtypeStruct((bsz, seq, MAIN), F32)],
        scratch_shapes=[pltpu.VMEM((seq + CONV_PAD, ct), F32)],
        semantics=("parallel", "parallel"),
        args=[p3, p3, dw, dwb])
    return outs[0], carried


def _conv_bwd_kernel(a_ref, g_ref, dhc_ref, dw_ref, da_ref, dg_ref, ddw_ref, pad_ref, dpad_ref, acc_ref,
                     *, seq, tc):
    ct = pad_ref.shape[1]
    b = pl.program_id(1)
    _glu_into(pad_ref, a_ref, g_ref, seq, tc)
    dpad_ref[seq:seq + CONV_PAD, :] = jnp.zeros((CONV_PAD, ct), F32)
    for r in range(0, seq, tc):
        dpad_ref[r:r + tc, :] = dhc_ref[0, r:r + tc, :]
    acc_ref[...] = jnp.zeros(acc_ref.shape, F32)
    for r in range(0, seq, tc):
        dh = dhc_ref[0, r:r + tc, :]
        dglu = jnp.zeros((tc, ct), F32)
        for k in range(CONV_K):
            o = r + (CONV_K - 1) - k
            dglu = dglu + dw_ref[k:k + 1, :] * dpad_ref[o:o + tc, :]
            o = CONV_PAD + r - (CONV_K - 1) + k
            prod = pad_ref[o:o + tc, :] * dh
            acc_ref[k] += jnp.sum(prod.reshape(tc // 8, 8, ct), axis=0)
        acc_ref[CONV_K] += jnp.sum(dh.reshape(tc // 8, 8, ct), axis=0)
        a = a_ref[0, r:r + tc, :]
        sg = jax.nn.sigmoid(g_ref[0, r:r + tc, :])
        da_ref[0, r:r + tc, :] = (dglu * sg).astype(da_ref.dtype)
        dg_ref[0, r:r + tc, :] = (dglu * a * sg * (1.0 - sg)).astype(dg_ref.dtype)
    tot = jnp.sum(acc_ref[...], axis=1)

    @pl.when(b == 0)
    def _():
        ddw_ref[...] = tot

    @pl.when(b > 0)
    def _():
        ddw_ref[...] += tot


def _conv_bwd(p0, dhc, dw, bsz, seq, comm=None):
    ct = CONV_CT
    tc = min(CONV_TC, seq)
    p3 = p0.reshape(bsz, seq, P0_COLS)
    blk = pl.BlockSpec((1, seq, ct), lambda j, b: (b, 0, j))
    return _call(
        functools.partial(_conv_bwd_kernel, seq=seq, tc=tc), comm,
        name="conv_bwd",
        grid=(MAIN // ct, bsz),
        in_specs=[pl.BlockSpec((1, seq, ct), lambda j, b: (b, 0, P0_A // ct + j)),
                  pl.BlockSpec((1, seq, ct), lambda j, b: (b, 0, P0_G // ct + j)),
                  blk,
                  pl.BlockSpec((CONV_K, ct), lambda j, b: (0, j))],
        out_specs=[blk, blk, pl.BlockSpec((CONV_K + 1, ct), lambda j, b: (0, j))],
        out_shape=[jax.ShapeDtypeStruct((bsz, seq, MAIN), BF16), jax.ShapeDtypeStruct((bsz, seq, MAIN), BF16),
                   jax.ShapeDtypeStruct((CONV_K + 1, MAIN), F32)],
        scratch_shapes=[pltpu.VMEM((seq + CONV_PAD, ct), F32), pltpu.VMEM((seq + CONV_PAD, ct), F32),
                        pltpu.VMEM((CONV_K + 1, 8, ct), F32)],
        semantics=("parallel", "arbitrary"),
        args=[p3, p3, dhc, dw])


def _ln_parts(x, lng, lnb):
    mu = jnp.mean(x, axis=-1, keepdims=True)
    xc = x - mu
    rstd = lax.rsqrt(jnp.mean(xc * xc, axis=-1, keepdims=True) + LN_EPS)
    xh = xc * rstd
    hl = xh * lng + lnb
    return rstd, xh, hl


GATE_CHUNK = 512
GATE_NZ = MIX // GATE_CHUNK


def _z_specs(tm, zb):
    return [pl.BlockSpec((tm, GATE_CHUNK), lambda i, j=j: (i, zb + j)) for j in range(GATE_NZ)]


def _gate_fwd_kernel(*refs, ln):
    main_ref, ymem_ref, *z_refs = refs[:2 + GATE_NZ]
    y_ref = refs[-1]
    x = main_ref[...]
    if ln:
        _, _, hl = _ln_parts(x, refs[-3][...], refs[-2][...])
        x = hl * jax.nn.sigmoid(hl)
    for j, z_ref in enumerate(z_refs):
        cols = slice(j * GATE_CHUNK, (j + 1) * GATE_CHUNK)
        z = z_ref[...]
        src = x[:, cols] if j < MAIN // GATE_CHUNK else ymem_ref[...]
        y_ref[:, cols] = (src * (z * jax.nn.sigmoid(z))).astype(y_ref.dtype)


def _gate_fwd(main, ymem, p, zb, ln_g=None, ln_b=None, *, name, tm=256):
    t = main.shape[0]
    tm = _tile(t, tm)
    ln = ln_g is not None
    in_specs = [pl.BlockSpec((tm, MAIN), lambda i: (i, 0)), pl.BlockSpec((tm, MEMW), lambda i: (i, 0))]
    in_specs += _z_specs(tm, zb)
    args = [main, ymem] + [p] * GATE_NZ
    if ln:
        in_specs += [pl.BlockSpec((1, MAIN), lambda i: (0, 0))] * 2
        args += [ln_g.reshape(1, MAIN), ln_b.reshape(1, MAIN)]
    return pl.pallas_call(
        functools.partial(_gate_fwd_kernel, ln=ln),
        name=name,
        grid=(t // tm,),
        in_specs=in_specs,
        out_specs=pl.BlockSpec((tm, MIX), lambda i: (i, 0)),
        out_shape=jax.ShapeDtypeStruct((t, MIX), BF16),
        compiler_params=_cparams(("parallel",)),
    )(*args)


def _gate_bwd_kernel(*refs, ln):
    dy_ref, main_ref, ymem_ref, *z_refs = refs[:3 + GATE_NZ]
    if ln:
        lng_ref, lnb_ref, dmain_ref, dymem_ref, dz_ref, dlng_ref, dlnb_ref = refs[3 + GATE_NZ:]
    else:
        dmain_ref, dymem_ref, dz_ref = refs[3 + GATE_NZ:]
    x = main_ref[...]
    if ln:
        lng = lng_ref[...]
        rstd, xh, hl = _ln_parts(x, lng, lnb_ref[...])
        sh = jax.nn.sigmoid(hl)
        ymain = hl * sh
    else:
        ymain = x
    for j, z_ref in enumerate(z_refs):
        cols = slice(j * GATE_CHUNK, (j + 1) * GATE_CHUNK)
        dy = dy_ref[:, cols]
        z = z_ref[...]
        sg = jax.nn.sigmoid(z)
        dsz = sg * (1.0 + z * (1.0 - sg))
        if j < MAIN // GATE_CHUNK:
            src = ymain[:, cols]
            dmain_ref[:, cols] = dy * (z * sg)
        else:
            src = ymem_ref[...]
            dymem_ref[...] = dy * (z * sg)
        dz_ref[:, cols] = (dy * src * dsz).astype(dz_ref.dtype)
    if not ln:
        return
    dym = dmain_ref[...]
    dhl = dym * (sh * (1.0 + hl * (1.0 - sh)))
    dxh = dhl * lng
    dmain_ref[...] = rstd * (dxh - jnp.mean(dxh, axis=-1, keepdims=True)
                             - xh * jnp.mean(dxh * xh, axis=-1, keepdims=True))
    dlng = jnp.sum(dhl * xh, axis=0, keepdims=True)
    dlnb = jnp.sum(dhl, axis=0, keepdims=True)

    @pl.when(pl.program_id(0) == 0)
    def _():
        dlng_ref[...] = dlng
        dlnb_ref[...] = dlnb

    @pl.when(pl.program_id(0) > 0)
    def _():
        dlng_ref[...] += dlng
        dlnb_ref[...] += dlnb


def _gate_bwd(dy, main, ymem, p, zb, ln_g=None, ln_b=None, *, name, tm=256):
    t = main.shape[0]
    tm = _tile(t, tm)
    ln = ln_g is not None
    r_main = pl.BlockSpec((tm, MAIN), lambda i: (i, 0))
    r_mem = pl.BlockSpec((tm, MEMW), lambda i: (i, 0))
    r_mix = pl.BlockSpec((tm, MIX), lambda i: (i, 0))
    vec = pl.BlockSpec((1, MAIN), lambda i: (0, 0))
    in_specs = [r_mix, r_main, r_mem] + _z_specs(tm, zb)
    args = [dy, main, ymem] + [p] * GATE_NZ
    out_specs = [r_main, r_mem, r_mix]
    out_shape = [jax.ShapeDtypeStruct((t, MAIN), F32), jax.ShapeDtypeStruct((t, MEMW), F32),
                 jax.ShapeDtypeStruct((t, MIX), BF16)]
    if ln:
        in_specs += [vec, vec]
        args += [ln_g.reshape(1, MAIN), ln_b.reshape(1, MAIN)]
        out_specs += [vec, vec]
        out_shape += [jax.ShapeDtypeStruct((1, MAIN), F32)] * 2
    return pl.pallas_call(
        functools.partial(_gate_bwd_kernel, ln=ln),
        name=name,
        grid=(t // tm,),
        in_specs=in_specs,
        out_specs=out_specs,
        out_shape=out_shape,
        compiler_params=_cparams(("arbitrary",)),
    )(*args)


def _dot_nt(a, b):
    return lax.dot_general(a, b, (((1,), (1,)), ((), ())), preferred_element_type=F32)


def _dot_tn(a, b):
    return lax.dot_general(a, b, (((0,), (0,)), ((), ())), preferred_element_type=F32)


def _dot(a, b):
    return jnp.dot(a, b, preferred_element_type=F32)


def _mem_probs(q, k):
    s = _dot_nt(q, k) * MEM_SCALE
    p = jnp.exp(s - jnp.max(s, axis=-1, keepdims=True))
    return p / jnp.sum(p, axis=-1, keepdims=True)


def _mem_fwd_kernel(q_ref, kv_ref, o_ref):
    for h in range(MEM_HEADS):
        c = slice(h * HEAD, (h + 1) * HEAD)
        cv = slice(MEMW + h * HEAD, MEMW + (h + 1) * HEAD)
        p = _mem_probs(q_ref[0, :, c].astype(BF16), kv_ref[0, :, c])
        o_ref[0, :, c] = _dot(p.astype(BF16), kv_ref[0, :, cv])


def _mem_fwd(p, kvm, col_block, bsz, seq, *, name, tq=512):
    tq = _tile(seq, tq)
    p3 = p.reshape(bsz, seq, p.shape[1])
    mlen = kvm.shape[1]
    return pl.pallas_call(
        functools.partial(_mem_fwd_kernel),
        name=name,
        grid=(bsz, seq // tq),
        in_specs=[pl.BlockSpec((1, tq, MEMW), lambda b, i: (b, i, col_block)),
                  pl.BlockSpec((1, mlen, 2 * MEMW), lambda b, i: (b, 0, 0))],
        out_specs=pl.BlockSpec((1, tq, MEMW), lambda b, i: (b, i, 0)),
        out_shape=jax.ShapeDtypeStruct((bsz, seq, MEMW), F32),
        compiler_params=_cparams(("parallel", "parallel")),
    )(p3, kvm)


def _mem_bwd_kernel(q_ref, kv_ref, do_ref, dq_ref, dkv_ref):
    @pl.when(pl.program_id(1) == 0)
    def _():
        dkv_ref[...] = jnp.zeros(dkv_ref.shape, F32)

    for h in range(MEM_HEADS):
        c = slice(h * HEAD, (h + 1) * HEAD)
        cv = slice(MEMW + h * HEAD, MEMW + (h + 1) * HEAD)
        q = q_ref[0, :, c].astype(BF16)
        k = kv_ref[0, :, c]
        v = kv_ref[0, :, cv]
        do = do_ref[0, :, c].astype(BF16)
        p = _mem_probs(q, k)
        dp = _dot_nt(do, v)
        ds = (p * (dp - jnp.sum(p * dp, axis=-1, keepdims=True)) * MEM_SCALE).astype(BF16)
        dq_ref[0, :, c] = _dot(ds, k).astype(dq_ref.dtype)
        dkv_ref[0, :, c] += _dot_tn(ds, q)
        dkv_ref[0, :, cv] += _dot_tn(p.astype(BF16), do)


def _mem_bwd(p, kvm, dymem, col_block, bsz, seq, *, name, tq=512):
    tq = _tile(seq, tq)
    p3 = p.reshape(bsz, seq, p.shape[1])
    mlen = kvm.shape[1]
    return pl.pallas_call(
        functools.partial(_mem_bwd_kernel),
        name=name,
        grid=(bsz, seq // tq),
        in_specs=[pl.BlockSpec((1, tq, MEMW), lambda b, i: (b, i, col_block)),
                  pl.BlockSpec((1, mlen, 2 * MEMW), lambda b, i: (b, 0, 0)),
                  pl.BlockSpec((1, tq, MEMW), lambda b, i: (b, i, 0))],
        out_specs=[pl.BlockSpec((1, tq, MEMW), lambda b, i: (b, i, 0)),
                   pl.BlockSpec((1, mlen, 2 * MEMW), lambda b, i: (b, 0, 0))],
        out_shape=[jax.ShapeDtypeStruct((bsz, seq, MEMW), BF16),
                   jax.ShapeDtypeStruct((bsz, mlen, 2 * MEMW), F32)],
        compiler_params=_cparams(("parallel", "arbitrary")),
    )(p3, kvm, dymem.reshape(bsz, seq, MEMW))


def _swap32(x):
    lane = lax.broadcasted_iota(jnp.int32, x.shape, 1)
    return jnp.where(lane < 32, pltpu.roll(x, 96, 1), pltpu.roll(x, 32, 1))


def _rope(x, cs, sn):
    return x * cs + _swap32(x) * sn


def _rope_t(d, cs, sn):
    return d * cs + _swap32(d * sn)


UP_HEADS = 2


def _q_up_kernel(a_ref, b_ref, cs_ref, sn_ref, o_ref):
    acc = _dot(a_ref[...], b_ref[...])
    cs = cs_ref[...]
    sn = sn_ref[...]
    for h in range(UP_HEADS):
        c0 = slice(h * QK_PAD, h * QK_PAD + HEAD)
        c1 = slice(h * QK_PAD + HEAD, (h + 1) * QK_PAD)
        o_ref[:, c0] = acc[:, c0].astype(o_ref.dtype)
        o_ref[:, c1] = _rope(acc[:, c1], cs, sn).astype(o_ref.dtype)


def _q_up(cqn, w_uq, cs, sn, *, tm=512):
    t, kd = cqn.shape
    tm = _tile(t, tm)
    tn = UP_HEADS * QK_PAD
    tab = pl.BlockSpec((tm, 128), lambda i, j: (i, 0))
    return pl.pallas_call(
        functools.partial(_q_up_kernel),
        name="q_up_rope",
        grid=(t // tm, MLA_HEADS // UP_HEADS),
        in_specs=[pl.BlockSpec((tm, kd), lambda i, j: (i, 0)), pl.BlockSpec((kd, tn), lambda i, j: (0, j)), tab, tab],
        out_specs=pl.BlockSpec((tm, tn), lambda i, j: (i, j)),
        out_shape=jax.ShapeDtypeStruct((t, MLA_HEADS * QK_PAD), BF16),
        compiler_params=_cparams(("parallel", "parallel")),
    )(cqn, w_uq, cs, sn)


def _kv_up_kernel(a_ref, b_ref, kr_ref, cs_ref, sn_ref, k_ref, v_ref):
    acc = _dot(a_ref[...], b_ref[...])
    krr = _rope(kr_ref[...], cs_ref[...], sn_ref[...]).astype(k_ref.dtype)
    for h in range(UP_HEADS):
        k_ref[:, h * QK_PAD:h * QK_PAD + HEAD] = acc[:, h * 2 * HEAD:h * 2 * HEAD + HEAD].astype(k_ref.dtype)
        k_ref[:, h * QK_PAD + HEAD:(h + 1) * QK_PAD] = krr
        v_ref[:, h * HEAD:(h + 1) * HEAD] = acc[:, h * 2 * HEAD + HEAD:(h + 1) * 2 * HEAD].astype(v_ref.dtype)


def _kv_up(ckvn, w_ukv, p1, cs, sn, *, tm=512):
    t, kd = ckvn.shape
    tm = _tile(t, tm)
    tab = pl.BlockSpec((tm, 128), lambda i, j: (i, 0))
    return pl.pallas_call(
        functools.partial(_kv_up_kernel),
        name="kv_up_pack",
        grid=(t // tm, MLA_HEADS // UP_HEADS),
        in_specs=[pl.BlockSpec((tm, kd), lambda i, j: (i, 0)),
                  pl.BlockSpec((kd, UP_HEADS * 2 * HEAD), lambda i, j: (0, j)),
                  pl.BlockSpec((tm, 128), lambda i, j: (i, P1_KR // 128)), tab, tab],
        out_specs=[pl.BlockSpec((tm, UP_HEADS * QK_PAD), lambda i, j: (i, j)),
                   pl.BlockSpec((tm, UP_HEADS * HEAD), lambda i, j: (i, j))],
        out_shape=[jax.ShapeDtypeStruct((t, MLA_HEADS * QK_PAD), BF16), jax.ShapeDtypeStruct((t, MAIN), BF16)],
        compiler_params=_cparams(("parallel", "parallel")),
    )(ckvn, w_ukv, p1, cs, sn)


def _kr_bwd_kernel(d_ref, cs_ref, sn_ref, o_ref):
    acc = d_ref[:, :HEAD]
    for h in range(1, MLA_HEADS):
        acc = acc + d_ref[:, h * HEAD:(h + 1) * HEAD]
    o_ref[...] = _rope_t(acc, cs_ref[...], sn_ref[...]).astype(o_ref.dtype)


def _kr_bwd(dkrr, cs, sn, *, tm=512):
    t = dkrr.shape[0]
    tm = _tile(t, tm)
    tab = pl.BlockSpec((tm, 128), lambda i: (i, 0))
    return pl.pallas_call(
        functools.partial(_kr_bwd_kernel),
        name="kr_bwd",
        grid=(t // tm,),
        in_specs=[pl.BlockSpec((tm, MAIN), lambda i: (i, 0)), tab, tab],
        out_specs=tab,
        out_shape=jax.ShapeDtypeStruct((t, 128), BF16),
        compiler_params=_cparams(("parallel",)),
    )(dkrr, cs, sn)


ATT_T = 256


def _causal(s, t):
    row = lax.broadcasted_iota(jnp.int32, (t, t), 0)
    col = lax.broadcasted_iota(jnp.int32, (t, t), 1)
    return jnp.where(col <= row, s, NEG)


def _attn_fwd_kernel(q_ref, k_ref, v_ref, o_ref, lse_ref, *, seq, t):
    for i in range(seq // t):
        own = slice(i * t, (i + 1) * t)
        q = q_ref[0, own, :]
        sd = _causal(_dot_nt(q, k_ref[0, own, :]) * MLA_SCALE, t)
        m = jnp.max(sd, axis=-1, keepdims=True)
        if i:
            so = _dot_nt(q, k_ref[0, :i * t, :]) * MLA_SCALE
            m = jnp.maximum(m, jnp.max(so, axis=-1, keepdims=True))
        pd = jnp.exp(sd - m)
        l = jnp.sum(pd, axis=-1, keepdims=True)
        acc = _dot(pd.astype(BF16), v_ref[0, own, :])
        if i:
            po = jnp.exp(so - m)
            l = l + jnp.sum(po, axis=-1, keepdims=True)
            acc = acc + _dot(po.astype(BF16), v_ref[0, :i * t, :])
        o_ref[0, own, :] = acc / l
        lse_ref[0, 0, own, :] = m + jnp.log(l)


def _attn_fwd(qb, kb, vb, bsz, seq):
    t = min(ATT_T, seq)
    q3 = qb.reshape(bsz, seq, MLA_HEADS * QK_PAD)
    k3 = kb.reshape(bsz, seq, MLA_HEADS * QK_PAD)
    v3 = vb.reshape(bsz, seq, MAIN)
    qk = pl.BlockSpec((1, seq, QK_PAD), lambda b, h: (b, 0, h))
    vv = pl.BlockSpec((1, seq, HEAD), lambda b, h: (b, 0, h))
    return pl.pallas_call(
        functools.partial(_attn_fwd_kernel, seq=seq, t=t),
        name="attn_fwd",
        grid=(bsz, MLA_HEADS),
        in_specs=[qk, qk, vv],
        out_specs=[vv, pl.BlockSpec((1, 1, seq, 1), lambda b, h: (b, h, 0, 0))],
        out_shape=[jax.ShapeDtypeStruct((bsz, seq, MAIN), F32),
                   jax.ShapeDtypeStruct((bsz, MLA_HEADS, seq, 1), F32)],
        compiler_params=_cparams(("parallel", "parallel")),
    )(q3, k3, v3)


def _attn_bwd_kernel(q_ref, k_ref, v_ref, o_ref, do_ref, lse_ref, cs_ref, sn_ref, dq_ref, dkv_ref, dkr_ref,
                     delta_ref, dqacc_ref, *, seq, t):
    for r in range(0, seq, t):
        rows = slice(r, r + t)
        delta_ref[rows, :] = jnp.sum(do_ref[0, rows, :] * o_ref[0, rows, :], axis=-1, keepdims=True)
    dqacc_ref[...] = jnp.zeros(dqacc_ref.shape, F32)

    def piece(rows, k, v, masked):
        q = q_ref[0, rows, :]
        do = do_ref[0, rows, :].astype(BF16)
        s = _dot_nt(q, k) * MLA_SCALE
        if masked:
            s = _causal(s, t)
        p = jnp.exp(s - lse_ref[0, 0, rows, :])
        dp = _dot_nt(do, v)
        ds = (p * (dp - delta_ref[rows, :]) * MLA_SCALE).astype(BF16)
        dqacc_ref[rows, :] += _dot(ds, k)
        return _dot_tn(ds, q), _dot_tn(p.astype(BF16), do)

    for j in range(seq // t):
        own = slice(j * t, (j + 1) * t)
        k = k_ref[0, own, :]
        v = v_ref[0, own, :]
        dk, dv = piece(own, k, v, True)
        if (j + 1) * t < seq:
            dk2, dv2 = piece(slice((j + 1) * t, seq), k, v, False)
            dk, dv = dk + dk2, dv + dv2
        dkv_ref[0, own, :HEAD] = dk[:, :HEAD].astype(dkv_ref.dtype)
        dkv_ref[0, own, HEAD:] = dv.astype(dkv_ref.dtype)
        dkr_ref[0, own, :] = dk[:, HEAD:]
    for r in range(0, seq, t):
        rows = slice(r, r + t)
        dq = dqacc_ref[rows, :]
        dq_ref[0, rows, :HEAD] = dq[:, :HEAD].astype(dq_ref.dtype)
        dq_ref[0, rows, HEAD:] = _rope_t(dq[:, HEAD:], cs_ref[0, rows, :], sn_ref[0, rows, :]).astype(dq_ref.dtype)


def _attn_bwd(qb, kb, vb, o, do, lse, cs, sn, bsz, seq):
    t = min(ATT_T, seq)
    q3 = qb.reshape(bsz, seq, MLA_HEADS * QK_PAD)
    k3 = kb.reshape(bsz, seq, MLA_HEADS * QK_PAD)
    v3 = vb.reshape(bsz, seq, MAIN)
    qk = pl.BlockSpec((1, seq, QK_PAD), lambda b, h: (b, 0, h))
    vv = pl.BlockSpec((1, seq, HEAD), lambda b, h: (b, 0, h))
    tab = pl.BlockSpec((1, seq, 128), lambda b, h: (b, 0, 0))
    dq, dkv, dkr = pl.pallas_call(
        functools.partial(_attn_bwd_kernel, seq=seq, t=t),
        name="attn_bwd",
        grid=(bsz, MLA_HEADS),
        in_specs=[qk, qk, vv, vv, vv, pl.BlockSpec((1, 1, seq, 1), lambda b, h: (b, h, 0, 0)), tab, tab],
        out_specs=[qk, qk, vv],
        out_shape=[jax.ShapeDtypeStruct((bsz, seq, MLA_HEADS * QK_PAD), BF16),
                   jax.ShapeDtypeStruct((bsz, seq, MLA_HEADS * 2 * HEAD), BF16),
                   jax.ShapeDtypeStruct((bsz, seq, MAIN), F32)],
        scratch_shapes=[pltpu.VMEM((seq, 1), F32), pltpu.VMEM((seq, QK_PAD), F32)],
        compiler_params=_cparams(("parallel", "parallel")),
    )(q3, k3, v3, o, do.reshape(bsz, seq, MAIN), lse, cs.reshape(bsz, seq, 128), sn.reshape(bsz, seq, 128))
    n = bsz * seq
    return dq.reshape(n, -1), dkv.reshape(n, -1), dkr.reshape(n, -1)


def _adamw_kernel(w_ref, g_ref, m_ref, v_ref, d_ref, nm_ref, nv_ref):
    g = g_ref[...]
    m = ADAM_B1 * m_ref[...] + (1.0 - ADAM_B1) * g
    v = ADAM_B2 * v_ref[...] + (1.0 - ADAM_B2) * (g * g)
    m_hat = m / (1.0 - ADAM_B1 ** ADAM_STEP)
    v_hat = v / (1.0 - ADAM_B2 ** ADAM_STEP)
    d_ref[...] = -ADAM_LR * (m_hat / (jnp.sqrt(v_hat) + ADAM_EPS) + ADAM_WD * w_ref[...])
    nm_ref[...] = m
    nv_ref[...] = v


def _adamw(w, g, m, v, *, name):
    shape = w.shape
    c = shape[-1]
    r = w.size // c
    tr = r
    for cand in (512, 256, 128, 64, 32, 16, 8):
        if r % cand == 0 and cand * c * 4 <= 2 * 1024 * 1024:
            tr = cand
            break
    blk = pl.BlockSpec((tr, c), lambda i: (i, 0))
    outs = pl.pallas_call(
        functools.partial(_adamw_kernel),
        name=name,
        grid=(r // tr,),
        in_specs=[blk] * 4,
        out_specs=[blk] * 3,
        out_shape=[jax.ShapeDtypeStruct((r, c), F32)] * 3,
        compiler_params=_cparams(("parallel",)),
    )(w.reshape(r, c), g.reshape(r, c), m.reshape(r, c), v.reshape(r, c))
    return tuple(o.reshape(shape) for o in outs)


def _place():
    return lax.axis_index("x"), lax.axis_index("y"), lax.axis_index("c")


def _other_chips(x, y):
    return [(1 - x, y), (x, 1 - y), (1 - x, 1 - y)]


class _Geom:
    def __init__(self, kind, rows, cols):
        self.kind, self.rows, self.cols, self.hr = kind, rows, cols, rows // 2
        self.full_shape = {"rows": (4 * rows, cols), "cols": (rows, 4 * cols), "chips": (4, rows, cols)}[kind]
        self.nt = self.hr // ADD_TILE

    def view(self, ref, s, h):
        if self.kind == "rows":
            return ref.at[pl.ds(s * self.rows + h * self.hr, self.hr), :]
        if self.kind == "cols":
            return ref.at[pl.ds(h * self.hr, self.hr), pl.ds(s * self.cols, self.cols)]
        return ref.at[s, pl.ds(h * self.hr, self.hr), :]

    def shard_half(self, ref, h):
        return ref.at[pl.ds(h * self.hr, self.hr), :]

    def shard_view(self, ref, s):
        if self.kind == "rows":
            return ref.at[pl.ds(s * self.rows, self.rows), :]
        if self.kind == "cols":
            return ref.at[:, pl.ds(s * self.cols, self.cols)]
        return ref.at[s]

    def tile_spec(self, chip_half_of):
        if self.kind == "rows":
            def index(*a):
                s, h, i = chip_half_of(*a)
                return (s * (self.rows // ADD_TILE) + h * self.nt + i, 0)
            return pl.BlockSpec((ADD_TILE, self.cols), index)
        if self.kind == "cols":
            def index(*a):
                s, h, i = chip_half_of(*a)
                return (h * self.nt + i, s)
            return pl.BlockSpec((ADD_TILE, self.cols), index)

        def index(*a):
            s, h, i = chip_half_of(*a)
            return (s, h * self.nt + i, 0)
        return pl.BlockSpec((None, ADD_TILE, self.cols), index)


GEOMS = (("w_mem_kv0", _Geom("rows", 256, 1024)), ("w_mem_kv1", _Geom("rows", 256, 1024)),
         ("w_out0", _Geom("rows", 512, 1024)), ("w_out1", _Geom("rows", 512, 1024)),
         ("conv_w_in", _Geom("cols", 1024, 1408)), ("mla_w_ukv", _Geom("cols", 256, 768)),
         ("mla_w_in", _Geom("chips", 1024, 848)), ("mla_w_uq", _Geom("chips", 512, 576)))
N_BIG = len(GEOMS)


def _remote(k, src_ref, dst_ref, to, ssem, rsem):
    return pltpu.make_async_remote_copy(src_ref=src_ref, dst_ref=dst_ref, send_sem=ssem.at[k], recv_sem=rsem.at[k],
                                        device_id=to, device_id_type=MESH)


GROUP_A = ("w_mem_kv0", "w_out0", "conv_w_in")
GROUP_B = ("w_mem_kv1", "w_out1", "mla_w_ukv", "mla_w_in", "mla_w_uq")
GATHER_0 = ("conv_w_in",)
GATHER_1 = ("w_mem_kv0", "w_out0", "mla_w_in")
GATHER_2 = ("w_mem_kv1", "w_out1", "mla_w_ukv", "mla_w_uq")
GEOM = dict(GEOMS)


def _gather_over_ici(geoms, srcs, outs, ssem, rsem, act, base=0):
    x, y, c = _place()
    s = 2 * x + y
    n = len(geoms)
    locals_ = [pltpu.make_async_copy(srcs[w], g.shard_view(outs[w], s), ssem.at[base + 3 * n + w])
               for w, g in enumerate(geoms)]
    sends = [_remote(base + 3 * w + j, g.shard_half(srcs[w], c), g.view(outs[w], s, c), (*chip, c), ssem, rsem)
             for w, g in enumerate(geoms) for j, chip in enumerate(_other_chips(x, y))]
    if act == "start":
        for cp in sends + locals_:
            cp.start()
        return
    for w, g in enumerate(geoms):
        for j, chip in enumerate(_other_chips(x, y)):
            blk = g.view(outs[w], 2 * chip[0] + chip[1], c)
            _remote(base + 3 * w + j, blk, blk, (*chip, c), ssem, rsem).wait_recv()
    for cp in sends:
        cp.wait_send()
    for cp in locals_:
        cp.wait()


def _forward_to_sibling(geoms, fulls, ssem, rsem, act, base=0):
    x, y, c = _place()
    for w, g in enumerate(geoms):
        for j, chip in enumerate(_other_chips(x, y)):
            s = 2 * chip[0] + chip[1]
            mine, theirs = g.view(fulls[w], s, c), g.view(fulls[w], s, 1 - c)
            if act == "start":
                _remote(base + 3 * w + j, mine, mine, (x, y, 1 - c), ssem, rsem).start()
            else:
                _remote(base + 3 * w + j, theirs, theirs, (x, y, 1 - c), ssem, rsem).wait_recv()
                _remote(base + 3 * w + j, mine, mine, (x, y, 1 - c), ssem, rsem).wait_send()


def _gather_comm(forward_names, fulls, ici_names, shards):
    fwd = [GEOM[nm] for nm in forward_names]
    ici = [GEOM[nm] for nm in ici_names]
    nf, base = len(fwd), 3 * len(fwd)

    def run(act):
        def go(i, o, ss, rs):
            _forward_to_sibling(fwd, o[:nf], ss, rs, act)
            _gather_over_ici(ici, i[nf:], o[nf:], ss, rs, act, base)
        return go

    return _Comm(list(fulls) + list(shards),
                 [jax.ShapeDtypeStruct(f.shape, f.dtype) for f in fulls]
                 + [jax.ShapeDtypeStruct(g.full_shape, BF16) for g in ici],
                 base + 4 * len(ici), run("start"), run("wait"), aliases={w: w for w in range(nf)})


def _allgather_kernel(*refs, geoms):
    n = len(geoms)
    srcs, outs, (ssem, rsem, fsem, gsem) = refs[:n], refs[n:2 * n], refs[2 * n:]
    _gather_over_ici(geoms, srcs, outs, ssem, rsem, "start")
    _gather_over_ici(geoms, srcs, outs, ssem, rsem, "wait")
    _forward_to_sibling(geoms, outs, fsem, gsem, "start")
    _forward_to_sibling(geoms, outs, fsem, gsem, "wait")


def _allgather_weights(names, shards):
    geoms = [GEOM[nm] for nm in names]
    n = len(geoms)
    return pl.pallas_call(
        functools.partial(_allgather_kernel, geoms=geoms),
        name="allgather_weights",
        in_specs=[HBM] * n,
        out_specs=[HBM] * n,
        out_shape=[jax.ShapeDtypeStruct(g.full_shape, BF16) for g in geoms],
        scratch_shapes=[pltpu.SemaphoreType.DMA((4 * n,)), pltpu.SemaphoreType.DMA((4 * n,)),
                        pltpu.SemaphoreType.DMA((3 * n,)), pltpu.SemaphoreType.DMA((3 * n,))],
    )(*shards)


def _swap_halves_kernel(*refs, geoms):
    n = len(geoms)
    srcs, dsts, (ssem, rsem) = refs[:n], refs[n:2 * n], refs[2 * n:]
    x, y, c = _place()
    cps = []
    for w, g in enumerate(geoms):
        for s in range(4):
            cps.append(_remote(4 * w + s, g.view(srcs[w], s, 1 - c), dsts[w].at[s], (x, y, 1 - c), ssem, rsem))
    for cp in cps:
        cp.start()
    for cp in cps:
        cp.wait()


def _swap_halves(names, gb, *, name):
    geoms = [GEOM[nm] for nm in names]
    n = len(geoms)
    return pl.pallas_call(
        functools.partial(_swap_halves_kernel, geoms=geoms),
        name=name,
        in_specs=[HBM] * n,
        out_specs=[HBM] * n,
        out_shape=[jax.ShapeDtypeStruct((4, g.hr, g.cols), BF16) for g in geoms],
        scratch_shapes=[pltpu.SemaphoreType.DMA((4 * n,)), pltpu.SemaphoreType.DMA((4 * n,))],
    )(*gb)


def _exchange_with_chips(srcs, dsts, ssem, rsem, act):
    x, y, c = _place()
    for w in range(len(srcs)):
        for j, chip in enumerate(_other_chips(x, y)):
            cp = _remote(3 * w + j, srcs[w].at[2 * chip[0] + chip[1]], dsts[w].at[j], (*chip, c), ssem, rsem)
            if act == "start":
                cp.start()
            else:
                cp.wait()


def _exchange_comm(pairs):
    return _Comm(pairs, [jax.ShapeDtypeStruct((3,) + p.shape[1:], p.dtype) for p in pairs], 3 * len(pairs),
                 lambda i, o, ss, rs: _exchange_with_chips(i, o, ss, rs, "start"),
                 lambda i, o, ss, rs: _exchange_with_chips(i, o, ss, rs, "wait"))


def _share_kernel(*refs):
    srcs, dsts, (ssem, rsem) = refs[:N_BIG], refs[N_BIG:2 * N_BIG], refs[2 * N_BIG:]
    x, y, c = _place()
    cps = [_remote(w, srcs[w], dsts[w], (x, y, 1 - c), ssem, rsem) for w in range(N_BIG)]
    for cp in cps:
        cp.start()
    for cp in cps:
        cp.wait()


def _share_with_sibling(halves):
    return pl.pallas_call(
        functools.partial(_share_kernel),
        name="rs_share_halves",
        in_specs=[HBM] * N_BIG,
        out_specs=[HBM] * N_BIG,
        out_shape=[jax.ShapeDtypeStruct(h.shape, h.dtype) for h in halves],
        scratch_shapes=[pltpu.SemaphoreType.DMA((N_BIG,)), pltpu.SemaphoreType.DMA((N_BIG,))],
    )(*halves)


def _gather_sum_kernel(src, gat, tot, ssem, rsem):
    x, y, c = _place()
    me = 4 * x + 2 * y + c
    gat[me] = src[...]
    flips = [(dx, dy, dc) for dx in (0, 1) for dy in (0, 1) for dc in (0, 1)][1:]
    cps = []
    for k, (dx, dy, dc) in enumerate(flips):
        peer = (1 - x if dx else x, 1 - y if dy else y, 1 - c if dc else c)
        cp = pltpu.make_async_remote_copy(src_ref=src, dst_ref=gat.at[me], send_sem=ssem.at[k], recv_sem=rsem.at[k],
                                          device_id=peer, device_id_type=MESH)
        cp.start()
        cps.append((cp, 4 * peer[0] + 2 * peer[1] + peer[2], peer))
    for k, (cp, idx, peer) in enumerate(cps):
        pltpu.make_async_remote_copy(src_ref=src, dst_ref=gat.at[idx], send_sem=ssem.at[k], recv_sem=rsem.at[k],
                                     device_id=peer, device_id_type=MESH).wait_recv()
    for cp, _, _ in cps:
        cp.wait_send()
    acc = gat[0]
    for d in range(1, 8):
        acc = acc + gat[d]
    tot[...] = acc


def _gather_sum_small(a, *, name):
    vm = pl.BlockSpec(memory_space=pltpu.VMEM)
    return pl.pallas_call(
        functools.partial(_gather_sum_kernel),
        name=name,
        in_specs=[vm],
        out_specs=[vm, vm],
        out_shape=[jax.ShapeDtypeStruct((8,) + a.shape, a.dtype), jax.ShapeDtypeStruct(a.shape, a.dtype)],
        scratch_shapes=[pltpu.SemaphoreType.DMA((7,)), pltpu.SemaphoreType.DMA((7,))],
    )(a)


def _add_pairs_kernel(c_ref, g_ref, r_ref, o_ref):
    o_ref[...] = (g_ref[...] + r_ref[...].astype(F32)).astype(o_ref.dtype)


def _add_pairs(geom, core, g, recv, *, name):
    half = pl.BlockSpec((None, ADD_TILE, geom.cols), lambda s, i, cr: (s, i, 0))
    return pl.pallas_call(
        functools.partial(_add_pairs_kernel),
        name=name,
        grid_spec=pltpu.PrefetchScalarGridSpec(
            num_scalar_prefetch=1,
            grid=(4, geom.nt),
            in_specs=[geom.tile_spec(lambda s, i, cr: (s, cr[0], i)), half],
            out_specs=half,
        ),
        out_shape=jax.ShapeDtypeStruct(recv.shape, BF16),
        compiler_params=_cparams(("parallel", "parallel")),
    )(core, g, recv)


def _add_final_kernel(sc_ref, g_ref, r_ref, e_ref, o_ref):
    acc = g_ref[...] + r_ref[...].astype(F32)
    for j in range(3):
        acc = acc + e_ref[j].astype(F32)
    o_ref[...] = acc


def _add_final(geom, chip_core, g, recv, exch, *, name):
    return pl.pallas_call(
        functools.partial(_add_final_kernel),
        name=name,
        grid_spec=pltpu.PrefetchScalarGridSpec(
            num_scalar_prefetch=1,
            grid=(geom.nt,),
            in_specs=[geom.tile_spec(lambda i, sc: (sc[0], sc[1], i)),
                      pl.BlockSpec((None, ADD_TILE, geom.cols), lambda i, sc: (sc[0], i, 0)),
                      pl.BlockSpec((3, ADD_TILE, geom.cols), lambda i, sc: (0, i, 0))],
            out_specs=pl.BlockSpec((ADD_TILE, geom.cols), lambda i, sc: (i, 0)),
        ),
        out_shape=jax.ShapeDtypeStruct((geom.hr, geom.cols), F32),
        compiler_params=_cparams(("parallel",)),
    )(chip_core, g, recv, exch)


def _chip_major(w):
    return w.reshape(w.shape[0], 4, w.shape[1] // 4).transpose(1, 0, 2)


def _from_chip_major(w):
    return w.transpose(1, 0, 2).reshape(w.shape[1], 4 * w.shape[2])


def _mla_in_to_internal(w):
    return jnp.concatenate([w[:, 1344:], w[:, :512], w[:, 832:1344], w[:, 512:768], w[:, 768:832],
                            jnp.zeros((w.shape[0], 64), w.dtype)], axis=1)


def _mla_in_from_internal(w):
    return jnp.concatenate([w[:, P1_CQ:P1_QM], w[:, P1_CKV:P1_KR], w[:, P1_KR:P1_KR + 64], w[:, P1_QM:P1_CKV],
                            w[:, :MIX]], axis=1)


def _uq_to_internal(w):
    w = w.reshape(w.shape[0], MLA_HEADS, HEAD + ROPE)
    return jnp.pad(w, ((0, 0), (0, 0), (0, QK_PAD - HEAD - ROPE))).reshape(w.shape[0], MLA_HEADS * QK_PAD)


def _uq_from_internal(w):
    return w.reshape(w.shape[0], MLA_HEADS, QK_PAD)[:, :, :HEAD + ROPE].reshape(w.shape[0], MLA_HEADS * (HEAD + ROPE))


def _rows128(a, rows):
    flat = a.reshape(-1)
    return jnp.pad(flat, (0, rows * 128 - flat.shape[0])).reshape(rows, 128)


def kernel(x, mem, positions, norm_g, mem_norm_g, w_mem_kv, w_out, conv_w_in, conv_dw, conv_dw_b, conv_ln_g, conv_ln_b, mla_w_in, mla_q_norm_g, mla_w_uq, mla_kv_norm_g, mla_w_ukv, final_norm_g, loss_target, m_norm_g, m_mem_norm_g, m_w_mem_kv, m_w_out, m_conv_w_in, m_conv_dw, m_conv_dw_b, m_conv_ln_g, m_conv_ln_b, m_mla_w_in, m_mla_q_norm_g, m_mla_w_uq, m_mla_kv_norm_g, m_mla_w_ukv, m_final_norm_g, v_norm_g, v_mem_norm_g, v_w_mem_kv, v_w_out, v_conv_w_in, v_conv_dw, v_conv_dw_b, v_conv_ln_g, v_conv_ln_b, v_mla_w_in, v_mla_q_norm_g, v_mla_w_uq, v_mla_kv_norm_g, v_mla_w_ukv, v_final_norm_g):
    bsz, seq, d = x.shape
    n = bsz * seq
    mlen = mem.shape[1]
    ax, ay, ac = _place()
    chip = 2 * ax + ay

    shards = dict(w_mem_kv0=w_mem_kv[0], w_mem_kv1=w_mem_kv[1], w_out0=w_out[0], w_out1=w_out[1],
                  conv_w_in=conv_w_in[0], mla_w_ukv=mla_w_ukv[0], mla_w_in=mla_w_in[0], mla_w_uq=mla_w_uq[0])
    bshard = {nm: sh.astype(BF16) for nm, sh in shards.items()}
    wf = dict(zip(GATHER_0, _allgather_weights(GATHER_0, [bshard[nm] for nm in GATHER_0])))
    w_conv_in = wf["conv_w_in"]
    w_conv_in_t = w_conv_in.T

    small_in = jnp.concatenate([_rows128(conv_dw[0], 96), _rows128(mla_q_norm_g, 8), _rows128(mla_kv_norm_g, 8)],
                               axis=0)
    small_all, _ = _gather_sum_small(small_in, name="gather_small_params")
    small_all = small_all[0::2]
    dw_full = small_all[:, :93].reshape(4, -1)[:, :CONV_K * 384].reshape(4, CONV_K, 384)
    dw_full = dw_full.transpose(1, 0, 2).reshape(CONV_K, MAIN)
    qg_full = small_all[:, 96].reshape(Q_RANK)
    kvg_full = small_all[:, 104, :64].reshape(KV_RANK)

    inv_freq = 1.0 / (ROPE_THETA ** (jnp.arange(0, ROPE, 2, dtype=F32) / ROPE))
    ang = positions.astype(F32).reshape(n, 1) * inv_freq
    cos, sin, zer = jnp.cos(ang), jnp.sin(ang), jnp.zeros((n, 64), F32)
    rope_c = jnp.concatenate([cos, cos, zer], axis=1)
    rope_s = jnp.concatenate([-sin, sin, zer], axis=1)

    x2 = x.reshape(n, d)
    mem2 = mem.reshape(bsz * mlen, d)
    tgt2 = loss_target.reshape(n, d)

    memn = [_rms_fwd(mem2, mem_norm_g[i], name=f"mem_norm{i}") for i in range(2)]
    u0 = _rms_fwd(x2, norm_g[0], name="norm0")
    p0, landed1 = _matmul(u0, w_conv_in, name="conv_in_proj",
                          comm=_gather_comm((), (), GATHER_1, [bshard[nm] for nm in GATHER_1]))
    hc, carried = _conv_fwd(p0, dw_full, conv_dw_b, bsz, seq,
                            comm=_gather_comm(GATHER_1, landed1, GATHER_2, [bshard[nm] for nm in GATHER_2]))
    hc = hc.reshape(n, MAIN)
    wf.update(zip(GATHER_1, carried[:len(GATHER_1)]))
    kvm = [_matmul(memn[0], wf["w_mem_kv0"], out_dtype=BF16, name="mem_kv0").reshape(bsz, mlen, 2 * MEMW)]
    ymem0 = _mem_fwd(p0, kvm[0], P0_QM // MEMW, bsz, seq, name="mem_attn0").reshape(n, MEMW)
    y0 = _gate_fwd(hc, ymem0, p0, P0_Z // GATE_CHUNK, conv_ln_g, conv_ln_b, name="gate0")
    h1, gathered2 = _matmul(y0, wf["w_out0"], x2, name="out_proj0",
                            comm=_gather_comm(GATHER_2, carried[len(GATHER_1):], (), ()))
    wf.update(zip(GATHER_2, gathered2))
    w_mla_in = _mla_in_to_internal(_from_chip_major(wf["mla_w_in"]))
    w_uq = _uq_to_internal(_from_chip_major(wf["mla_w_uq"]))
    w_ukv = wf["mla_w_ukv"]
    w_memkv = [wf["w_mem_kv0"], wf["w_mem_kv1"]]
    w_o = [wf["w_out0"], wf["w_out1"]]
    w_mla_in_t, w_uq_t, w_ukv_t = w_mla_in.T, w_uq.T, w_ukv.T
    w_memkv_t = [w.T for w in w_memkv]
    w_o_t = [w.T for w in w_o]
    kvm.append(_matmul(memn[1], w_memkv[1], out_dtype=BF16, name="mem_kv1").reshape(bsz, mlen, 2 * MEMW))

    u1 = _rms_fwd(h1, norm_g[1], name="norm1")
    p1 = _matmul(u1, w_mla_in, name="mla_in_proj")
    cqn = _rms_fwd(p1, qg_full, width=Q_RANK, col_block=P1_CQ // Q_RANK, name="q_norm")
    ckvn = _rms_fwd(p1, kvg_full, width=KV_RANK, col_block=P1_CKV // KV_RANK, name="kv_norm")
    qb = _q_up(cqn, w_uq, rope_c, rope_s)
    kb, vb = _kv_up(ckvn, w_ukv, p1, rope_c, rope_s)
    o1, lse = _attn_fwd(qb, kb, vb, bsz, seq)
    o1 = o1.reshape(n, MAIN)
    ymem1 = _mem_fwd(p1, kvm[1], P1_QM // MEMW, bsz, seq, name="mem_attn1").reshape(n, MEMW)
    y1 = _gate_fwd(o1, ymem1, p1, P1_Z // GATE_CHUNK, name="gate1")
    h2 = _matmul(y1, w_o[1], h1, name="out_proj1")

    dh2, d_final_g, loss_part = _final_loss(h2, final_norm_g, tgt2)
    loss = lax.psum(loss_part[0, 0], ("x", "y", "c"))

    gbig = {}
    gbig["w_out1"] = _matmul(y1, dh2, ta=True, also_bf16=True, name="d_w_out1")
    dy1 = _matmul(dh2, w_o_t[1], name="d_y1")
    do1, dymem1, dz1 = _gate_bwd(dy1, o1, ymem1, p1, P1_Z // GATE_CHUNK, name="gate1_bwd")
    dqm1, dkvm1 = _mem_bwd(p1, kvm[1], dymem1, P1_QM // MEMW, bsz, seq, name="mem_attn1_bwd")
    dqb, dkv, dkrr = _attn_bwd(qb, kb, vb, o1.reshape(bsz, seq, MAIN), do1, lse, rope_c, rope_s, bsz, seq)
    dkr = _kr_bwd(dkrr, rope_c, rope_s)
    g_w_uq = _matmul(cqn, dqb, ta=True, name="d_w_uq")
    dcqn = _matmul(dqb, w_uq_t, name="d_cqn")
    gbig["mla_w_ukv"] = _matmul(ckvn, dkv, ta=True, also_bf16=True, name="d_w_ukv")
    dckvn = _matmul(dkv, w_ukv_t, name="d_ckvn")
    dcq, g_qg = _rms_bwd(p1, qg_full, dcqn, width=Q_RANK, col_block=P1_CQ // Q_RANK, out_dtype=BF16,
                         name="q_norm_bwd")
    dckv, g_kvg = _rms_bwd(p1, kvg_full, dckvn, width=KV_RANK, col_block=P1_CKV // KV_RANK, out_dtype=BF16,
                           name="kv_norm_bwd")
    dp1 = jnp.concatenate([dz1, dcq, dqm1.reshape(n, MEMW), dckv, dkr], axis=1)
    g_w_mla_in = _matmul(u1, dp1, ta=True, name="d_w_mla_in")
    du1 = _matmul(dp1, w_mla_in_t, name="d_u1")
    dh1, g_norm1 = _rms_bwd(h1, norm_g[1], du1, dh2, name="norm1_bwd")

    def mem_kv_bwd(i, dkvm):
        dk2 = dkvm.reshape(bsz * mlen, 2 * MEMW)
        gbig[f"w_mem_kv{i}"] = _matmul(memn[i], dk2, ta=True, also_bf16=True, name=f"d_w_mem_kv{i}")
        dmemn = _matmul(dk2, w_memkv_t[i], name=f"d_memn{i}")
        return _rms_bwd(mem2, mem_norm_g[i], dmemn, name=f"mem_norm{i}_bwd")[1]

    g_mem_g1 = mem_kv_bwd(1, dkvm1)
    for nm, g_int, back in (("mla_w_in", g_w_mla_in, _mla_in_from_internal), ("mla_w_uq", g_w_uq, _uq_from_internal)):
        g_cm = _chip_major(back(g_int))
        gbig[nm] = (g_cm, g_cm.astype(BF16))

    core = jnp.reshape(ac, (1,)).astype(jnp.int32)
    chip_core = jnp.stack([chip, ac]).astype(jnp.int32)

    def pair_sums(names, tag):
        from_sib = _swap_halves(names, [gbig[nm][1] for nm in names], name=f"rs_sibling_swap_{tag}")
        return from_sib, [_add_pairs(GEOM[nm], core, gbig[nm][0], fs, name=f"rs_add_pairs_{nm}")
                          for nm, fs in zip(names, from_sib)]

    from_sib_b, pairs_b = pair_sums(GROUP_B, "b")

    gbig["w_out0"] = _matmul(y0, dh1, ta=True, also_bf16=True, name="d_w_out0")
    dy0 = _matmul(dh1, w_o_t[0], name="d_y0")
    dhc, dymem0, dz0, g_ln_g, g_ln_b = _gate_bwd(dy0, hc, ymem0, p0, P0_Z // GATE_CHUNK, conv_ln_g, conv_ln_b,
                                                 name="gate0_bwd")
    dqm0, dkvm0 = _mem_bwd(p0, kvm[0], dymem0, P0_QM // MEMW, bsz, seq, name="mem_attn0_bwd")
    (da, dg, g_dw32), exch_b = _conv_bwd(p0, dhc.reshape(bsz, seq, MAIN), dw_full, bsz, seq,
                                         comm=_exchange_comm(pairs_b))
    dp0 = jnp.concatenate([da.reshape(n, MAIN), dg.reshape(n, MAIN), dqm0.reshape(n, MEMW), dz0], axis=1)
    gbig["conv_w_in"] = _matmul(u0, dp0, ta=True, also_bf16=True, name="d_w_conv_in")
    g_mem_g = [mem_kv_bwd(0, dkvm0), g_mem_g1]

    from_sib_a, pairs_a = pair_sums(GROUP_A, "a")
    du0, exch_a = _matmul(dp0, w_conv_in_t, name="d_u0", comm=_exchange_comm(pairs_a))
    grad_x, g_norm0 = _rms_bwd(x2, norm_g[0], du0, dh1, name="norm0_bwd")

    from_sibling = dict(zip(GROUP_A + GROUP_B, list(from_sib_a) + list(from_sib_b)))
    exch = dict(zip(GROUP_A + GROUP_B, list(exch_a) + list(exch_b)))
    mine = [_add_final(g, chip_core, gbig[nm][0], from_sibling[nm], exch[nm], name=f"rs_add_final_{nm}")
            for nm, g in GEOMS]
    theirs = _share_with_sibling(mine)
    red = {}
    for (nm, _), a, b in zip(GEOMS, mine, theirs):
        red[nm] = jnp.concatenate([jnp.where(ac == 0, a, b), jnp.where(ac == 0, b, a)], axis=0)
    red["w_mem_kv"] = jnp.stack([red["w_mem_kv0"], red["w_mem_kv1"]])
    red["w_out"] = jnp.stack([red["w_out0"], red["w_out1"]])

    small_g = jnp.concatenate([
        _rows128(jnp.concatenate([g_norm0, g_norm1], axis=0), 16), _rows128(jnp.concatenate(g_mem_g, axis=0), 16),
        _rows128(g_dw32[CONV_K], 16), _rows128(g_ln_g, 16), _rows128(g_ln_b, 16), _rows128(d_final_g, 8),
        _rows128(g_dw32[:CONV_K], 376), _rows128(g_qg, 8), _rows128(g_kvg, 8)], axis=0)
    _, small_sum = _gather_sum_small(small_g, name="allreduce_small_grads")
    flat = small_sum.reshape(-1)

    def take(off, shape):
        size = 1
        for s_ in shape:
            size *= s_
        return flat[off * 128:off * 128 + size].reshape(shape)

    grads = dict(red)
    grads["conv_w_in"] = red["conv_w_in"][None]
    grads["mla_w_in"] = red["mla_w_in"][None]
    grads["mla_w_uq"] = red["mla_w_uq"][None]
    grads["mla_w_ukv"] = red["mla_w_ukv"][None]
    grads["norm_g"] = take(0, (2, D_MODEL))
    grads["mem_norm_g"] = take(16, (2, D_MODEL))
    grads["conv_dw_b"] = take(32, (1, MAIN))
    grads["conv_ln_g"] = take(48, (1, MAIN))
    grads["conv_ln_b"] = take(64, (1, MAIN))
    grads["final_norm_g"] = take(80, (D_MODEL,))
    grads["conv_dw"] = lax.dynamic_slice_in_dim(take(88, (CONV_K, MAIN)), chip * 384, 384, axis=1)[None]
    grads["mla_q_norm_g"] = lax.dynamic_slice_in_dim(take(464, (Q_RANK,)), chip * 128, 128, axis=0)[None]
    grads["mla_kv_norm_g"] = lax.dynamic_slice_in_dim(take(472, (KV_RANK,)), chip * 64, 64, axis=0)[None]

    params = dict(norm_g=norm_g, mem_norm_g=mem_norm_g, w_mem_kv=w_mem_kv, w_out=w_out, conv_w_in=conv_w_in,
                  conv_dw=conv_dw, conv_dw_b=conv_dw_b, conv_ln_g=conv_ln_g, conv_ln_b=conv_ln_b, mla_w_in=mla_w_in,
                  mla_q_norm_g=mla_q_norm_g, mla_w_uq=mla_w_uq, mla_kv_norm_g=mla_kv_norm_g, mla_w_ukv=mla_w_ukv,
                  final_norm_g=final_norm_g)
    mom1 = dict(norm_g=m_norm_g, mem_norm_g=m_mem_norm_g, w_mem_kv=m_w_mem_kv, w_out=m_w_out, conv_w_in=m_conv_w_in,
                conv_dw=m_conv_dw, conv_dw_b=m_conv_dw_b, conv_ln_g=m_conv_ln_g, conv_ln_b=m_conv_ln_b,
                mla_w_in=m_mla_w_in, mla_q_norm_g=m_mla_q_norm_g, mla_w_uq=m_mla_w_uq,
                mla_kv_norm_g=m_mla_kv_norm_g, mla_w_ukv=m_mla_w_ukv, final_norm_g=m_final_norm_g)
    mom2 = dict(norm_g=v_norm_g, mem_norm_g=v_mem_norm_g, w_mem_kv=v_w_mem_kv, w_out=v_w_out, conv_w_in=v_conv_w_in,
                conv_dw=v_conv_dw, conv_dw_b=v_conv_dw_b, conv_ln_g=v_conv_ln_g, conv_ln_b=v_conv_ln_b,
                mla_w_in=v_mla_w_in, mla_q_norm_g=v_mla_q_norm_g, mla_w_uq=v_mla_w_uq,
                mla_kv_norm_g=v_mla_kv_norm_g, mla_w_ukv=v_mla_w_ukv, final_norm_g=v_final_norm_g)
    names = list(params)
    g_out, deltas, new_m, new_v = [], [], [], []
    for nm in names:
        w = params[nm]
        g = grads[nm].reshape(w.shape)
        w2 = w.reshape(1, -1) if w.ndim == 1 else w
        dlt, m_new, v_new = _adamw(w2, g.reshape(w2.shape), mom1[nm].reshape(w2.shape), mom2[nm].reshape(w2.shape),
                                   name=f"adamw_{nm}")
        g_out.append(g)
        deltas.append(dlt.reshape(w.shape))
        new_m.append(m_new.reshape(w.shape))
        new_v.append(v_new.reshape(w.shape))

    return (loss, grad_x.reshape(bsz, seq, d), *g_out, *deltas, *new_m, *new_v)
```

```python
import functools

import jax
import jax.numpy as jnp
from jax import lax
from jax.experimental import pallas as pl
from jax.experimental.pallas import tpu as pltpu

F32 = jnp.float32
BF16 = jnp.bfloat16
MESH = pl.DeviceIdType.MESH

D_MODEL = 1024
MIX = 2048
MAIN = 1536
MEMW = 512
MEM_HEADS = 4
HEAD = 128
CONV_K = 31
CONV_PAD = 32
MLA_HEADS = 12
ROPE = 64
QK_PAD = 256
Q_RANK = 512
KV_RANK = 256
ROPE_THETA = 10000.0
RMS_EPS = 1e-6
LN_EPS = 1e-5
MEM_SCALE = HEAD ** -0.5
MLA_SCALE = (HEAD + ROPE) ** -0.5
NEG = -1e30

P0_COLS = 5632
P0_A, P0_G, P0_QM, P0_Z = 0, 1536, 3072, 3584
P1_COLS = 3456
P1_Z, P1_CQ, P1_QM, P1_CKV, P1_KR = 0, 2048, 2560, 3072, 3328

ADAM_LR = 0.001
ADAM_B1 = 0.9
ADAM_B2 = 0.999
ADAM_EPS = 1e-08
ADAM_WD = 0.01
ADAM_STEP = 10

VMEM_LIMIT = 56 * 1024 * 1024

ADD_TILE = 128
LOCAL_CHUNKS = 4


def _cparams(sem=None):
    return pltpu.CompilerParams(dimension_semantics=sem, vmem_limit_bytes=VMEM_LIMIT)


HBM = pl.BlockSpec(memory_space=pl.ANY)


class _Comm:
    def __init__(self, ins, out_shapes, n_sem, start, finish, aliases=None):
        self.ins, self.out_shapes, self.n_sem = list(ins), list(out_shapes), n_sem
        self.start, self.finish, self.aliases = start, finish, dict(aliases or {})


def _call(kernel_fn, comm, *, name, grid, in_specs, out_specs, out_shape, scratch_shapes, semantics, args):
    if comm is None:
        outs = pl.pallas_call(kernel_fn, name=name, grid=grid, in_specs=in_specs, out_specs=out_specs,
                              out_shape=out_shape, scratch_shapes=scratch_shapes,
                              compiler_params=_cparams(semantics))(*args)
        return list(outs), []
    n_in, n_out, nci, nco = len(in_specs), len(out_shape), len(comm.ins), len(comm.out_shapes)

    def body(*refs):
        ins, cins = refs[:n_in], refs[n_in:n_in + nci]
        outs = refs[n_in + nci:n_in + nci + n_out]
        couts = refs[n_in + nci + n_out:n_in + nci + n_out + nco]
        scratch, (ssem, rsem) = refs[n_in + nci + n_out + nco:-2], refs[-2:]
        ids = [pl.program_id(ax) for ax in range(len(grid))]
        first = functools.reduce(jnp.logical_and, [i == 0 for i in ids])
        last = functools.reduce(jnp.logical_and, [i == g - 1 for i, g in zip(ids, grid)])

        @pl.when(first)
        def _():
            comm.start(cins, couts, ssem, rsem)

        kernel_fn(*ins, *outs, *scratch)

        @pl.when(last)
        def _():
            comm.finish(cins, couts, ssem, rsem)

    outs = pl.pallas_call(
        body, name=name, grid=grid,
        in_specs=list(in_specs) + [HBM] * nci,
        out_specs=list(out_specs) + [HBM] * nco,
        out_shape=list(out_shape) + comm.out_shapes,
        scratch_shapes=list(scratch_shapes) + [pltpu.SemaphoreType.DMA((comm.n_sem,))] * 2,
        input_output_aliases={n_in + i: n_out + o for i, o in comm.aliases.items()},
        compiler_params=_cparams(("arbitrary",) * len(grid)),
    )(*args, *comm.ins)
    return list(outs[:n_out]), list(outs[n_out:])


def _tile(n, pref):
    if n <= pref:
        return n
    t = (pref // 128) * 128
    while t > 128 and n % t:
        t -= 128
    assert n % t == 0, (n, pref)
    return t


def _mm_kernel(*refs, nk, ta, has_res, a_parts, b_parts):
    a_refs, b_refs = refs[:len(a_parts)], refs[len(a_parts):len(a_parts) + len(b_parts)]
    rest = refs[len(a_parts) + len(b_parts):]
    if has_res:
        r_ref, *o_refs, acc_ref = rest
    else:
        *o_refs, acc_ref = rest
    j, k = pl.program_id(1), pl.program_id(2)

    def finish(acc):
        if has_res:
            acc = r_ref[...] + acc
        for o in o_refs:
            o[...] = acc.astype(o.dtype)

    def step(a_ref, b_ref):
        dn = (((0 if ta else 1,), (0,)), ((), ()))
        p = lax.dot_general(a_ref[...].astype(BF16), b_ref[...].astype(BF16), dn, preferred_element_type=F32)
        if nk == 1:
            finish(p)
            return

        @pl.when(k == 0)
        def _():
            acc_ref[...] = p

        @pl.when(jnp.logical_and(k > 0, k < nk - 1))
        def _():
            acc_ref[...] += p

        @pl.when(k == nk - 1)
        def _():
            finish(acc_ref[...] + p)

    for a_ref, (k0, kn) in zip(a_refs, a_parts):
        for b_ref, (j0, jn) in zip(b_refs, b_parts):
            conds = []
            if len(a_parts) > 1:
                conds.append(jnp.logical_and(k >= k0, k < k0 + kn))
            if len(b_parts) > 1:
                conds.append(jnp.logical_and(j >= j0, j < j0 + jn))
            if conds:
                pl.when(functools.reduce(jnp.logical_and, conds))(functools.partial(step, a_ref, b_ref))
            else:
                step(a_ref, b_ref)


def _matmul(a, b, res=None, *, name, ta=False, out_dtype=F32, also_bf16=False, comm=None, tm=1024, tn=512, tk=2048):
    a_list = list(a) if isinstance(a, (list, tuple)) else [a]
    b_list = list(b) if isinstance(b, (list, tuple)) else [b]
    assert not (ta and len(a_list) > 1)
    m = a_list[0].shape[1] if ta else a_list[0].shape[0]
    kd = a_list[0].shape[0] if ta else sum(p.shape[1] for p in a_list)
    n = sum(p.shape[1] for p in b_list)
    assert all(p.shape[0] == kd for p in b_list)
    tm = _tile(m, tm)
    tn = 512 if len(b_list) > 1 else _tile(n, tn)
    tk = 512 if len(a_list) > 1 else _tile(kd, tk)
    nk = kd // tk

    def parts(widths, t):
        out, start = [], 0
        for w in widths:
            assert w % t == 0
            out.append((start, w // t))
            start += w // t
        return out

    a_parts = parts([p.shape[1] for p in a_list], tk) if len(a_list) > 1 else [(0, nk)]
    b_parts = parts([p.shape[1] for p in b_list], tn) if len(b_list) > 1 else [(0, n // tn)]
    if ta:
        a_specs = [pl.BlockSpec((tk, tm), lambda i, j, k: (k, i))]
    else:
        a_specs = [pl.BlockSpec((tm, tk), lambda i, j, k, k0=k0, kn=kn: (i, jnp.clip(k - k0, 0, kn - 1)))
                   for k0, kn in a_parts]
    b_specs = [pl.BlockSpec((tk, tn), lambda i, j, k, j0=j0, jn=jn: (k, jnp.clip(j - j0, 0, jn - 1)))
               for j0, jn in b_parts]
    out_spec = pl.BlockSpec((tm, tn), lambda i, j, k: (i, j))
    in_specs = a_specs + b_specs
    args = a_list + b_list
    if res is not None:
        in_specs.append(out_spec)
        args.append(res)
    out_shape = [jax.ShapeDtypeStruct((m, n), out_dtype)]
    if also_bf16:
        out_shape.append(jax.ShapeDtypeStruct((m, n), BF16))
    outs, carried = _call(
        functools.partial(_mm_kernel, nk=nk, ta=ta, has_res=res is not None, a_parts=a_parts, b_parts=b_parts), comm,
        name=name,
        grid=(m // tm, n // tn, nk),
        in_specs=in_specs,
        out_specs=[out_spec] * len(out_shape),
        out_shape=out_shape,
        scratch_shapes=[pltpu.VMEM((tm, tn), F32)],
        semantics=("parallel", "parallel", "arbitrary"),
        args=args)
    result = tuple(outs) if also_bf16 else outs[0]
    return result if comm is None else (result, carried)


def _rms_fwd_kernel(h_ref, g_ref, o_ref):
    h = h_ref[...]
    rstd = lax.rsqrt(jnp.mean(h * h, axis=-1, keepdims=True) + RMS_EPS)
    o_ref[...] = (h * rstd * g_ref[...]).astype(o_ref.dtype)


def _rms_fwd(h, g, *, name, width=None, col_block=0, tm=512):
    t = h.shape[0]
    width = width or h.shape[1]
    tm = _tile(t, tm)
    return pl.pallas_call(
        functools.partial(_rms_fwd_kernel),
        name=name,
        grid=(t // tm,),
        in_specs=[pl.BlockSpec((tm, width), lambda i: (i, col_block)),
                  pl.BlockSpec((1, width), lambda i: (0, 0))],
        out_specs=pl.BlockSpec((tm, width), lambda i: (i, 0)),
        out_shape=jax.ShapeDtypeStruct((t, width), BF16),
        compiler_params=_cparams(("parallel",)),
    )(h, g.reshape(1, width))


def _rms_bwd_math(h, g, du):
    rstd = lax.rsqrt(jnp.mean(h * h, axis=-1, keepdims=True) + RMS_EPS)
    dug = du * g
    dh = rstd * dug - h * (rstd * rstd * rstd) * jnp.mean(dug * h, axis=-1, keepdims=True)
    dg = jnp.sum(du * h * rstd, axis=0, keepdims=True)
    return dh, dg


def _rms_bwd_kernel(*refs, has_res):
    if has_res:
        h_ref, g_ref, du_ref, res_ref, dh_ref, dg_ref = refs
    else:
        h_ref, g_ref, du_ref, dh_ref, dg_ref = refs
    dh, dg = _rms_bwd_math(h_ref[...], g_ref[...], du_ref[...].astype(F32))
    if has_res:
        dh = dh + res_ref[...]
    dh_ref[...] = dh.astype(dh_ref.dtype)

    @pl.when(pl.program_id(0) == 0)
    def _():
        dg_ref[...] = dg

    @pl.when(pl.program_id(0) > 0)
    def _():
        dg_ref[...] += dg


def _rms_bwd(h, g, du, res=None, *, name, width=None, col_block=0, out_dtype=F32, tm=512):
    t = h.shape[0]
    width = width or h.shape[1]
    tm = _tile(t, tm)
    row = pl.BlockSpec((tm, width), lambda i: (i, 0))
    in_specs = [pl.BlockSpec((tm, width), lambda i: (i, col_block)),
                pl.BlockSpec((1, width), lambda i: (0, 0)), row]
    args = [h, g.reshape(1, width), du]
    if res is not None:
        in_specs.append(row)
        args.append(res)
    return pl.pallas_call(
        functools.partial(_rms_bwd_kernel, has_res=res is not None),
        name=name,
        grid=(t // tm,),
        in_specs=in_specs,
        out_specs=[row, pl.BlockSpec((1, width), lambda i: (0, 0))],
        out_shape=[jax.ShapeDtypeStruct((t, width), out_dtype), jax.ShapeDtypeStruct((1, width), F32)],
        compiler_params=_cparams(("arbitrary",)),
    )(*args)


def _final_kernel(h_ref, g_ref, t_ref, dh_ref, dg_ref, loss_ref):
    h = h_ref[...]
    g = g_ref[...]
    rstd = lax.rsqrt(jnp.mean(h * h, axis=-1, keepdims=True) + RMS_EPS)
    e = h * rstd * g - t_ref[...]
    part = 0.5 * jnp.sum(jnp.mean(e * e, axis=-1, keepdims=True), axis=0, keepdims=True)
    dh, dg = _rms_bwd_math(h, g, e * (1.0 / D_MODEL))
    dh_ref[...] = dh
    part = jnp.broadcast_to(part, loss_ref.shape)

    @pl.when(pl.program_id(0) == 0)
    def _():
        dg_ref[...] = dg
        loss_ref[...] = part

    @pl.when(pl.program_id(0) > 0)
    def _():
        dg_ref[...] += dg
        loss_ref[...] += part


def _final_loss(h, g, target, *, tm=512):
    t, d = h.shape
    tm = _tile(t, tm)
    row = pl.BlockSpec((tm, d), lambda i: (i, 0))
    return pl.pallas_call(
        functools.partial(_final_kernel),
        name="final_loss",
        grid=(t // tm,),
        in_specs=[row, pl.BlockSpec((1, d), lambda i: (0, 0)), row],
        out_specs=[row, pl.BlockSpec((1, d), lambda i: (0, 0)), pl.BlockSpec((1, 128), lambda i: (0, 0))],
        out_shape=[jax.ShapeDtypeStruct((t, d), F32), jax.ShapeDtypeStruct((1, d), F32),
                   jax.ShapeDtypeStruct((1, 128), F32)],
        compiler_params=_cparams(("arbitrary",)),
    )(h, g.reshape(1, d), target)


CONV_CT = 128
CONV_TC = 256


def _glu_into(pad_ref, a_ref, g_ref, seq, tc):
    ct = pad_ref.shape[1]
    pad_ref[0:CONV_PAD, :] = jnp.zeros((CONV_PAD, ct), F32)
    for r in range(0, seq, tc):
        a = a_ref[0, r:r + tc, :]
        g = g_ref[0, r:r + tc, :]
        pad_ref[CONV_PAD + r:CONV_PAD + r + tc, :] = a * jax.nn.sigmoid(g)


def _conv_fwd_kernel(a_ref, g_ref, dw_ref, dwb_ref, hc_ref, pad_ref, *, seq, tc):
    ct = pad_ref.shape[1]
    _glu_into(pad_ref, a_ref, g_ref, seq, tc)
    for r in range(0, seq, tc):
        acc = jnp.broadcast_to(dwb_ref[...], (tc, ct))
        for k in range(CONV_K):
            o = CONV_PAD + r - (CONV_K - 1) + k
            acc = acc + dw_ref[k:k + 1, :] * pad_ref[o:o + tc, :]
        hc_ref[0, r:r + tc, :] = acc


def _conv_fwd(p0, dw, dwb, bsz, seq, comm=None):
    ct = CONV_CT
    tc = min(CONV_TC, seq)
    p3 = p0.reshape(bsz, seq, P0_COLS)
    outs, carried = _call(
        functools.partial(_conv_fwd_kernel, seq=seq, tc=tc), comm,
        name="conv_fwd",
        grid=(MAIN // ct, bsz),
        in_specs=[pl.BlockSpec((1, seq, ct), lambda j, b: (b, 0, P0_A // ct + j)),
                  pl.BlockSpec((1, seq, ct), lambda j, b: (b, 0, P0_G // ct + j)),
                  pl.BlockSpec((CONV_K, ct), lambda j, b: (0, j)),
                  pl.BlockSpec((1, ct), lambda j, b: (0, j))],
        out_specs=[pl.BlockSpec((1, seq, ct), lambda j, b: (b, 0, j))],
        out_shape=[jax.ShapeDtypeStruct((bsz, seq, MAIN), F32)],
        scratch_shapes=[pltpu.VMEM((seq + CONV_PAD, ct), F32)],
        semantics=("parallel", "parallel"),
        args=[p3, p3, dw, dwb])
    return outs[0], carried


def _conv_bwd_kernel(a_ref, g_ref, dhc_ref, dw_ref, da_ref, dg_ref, ddw_ref, pad_ref, dpad_ref, acc_ref,
                     *, seq, tc):
    ct = pad_ref.shape[1]
    b = pl.program_id(1)
    _glu_into(pad_ref, a_ref, g_ref, seq, tc)
    dpad_ref[seq:seq + CONV_PAD, :] = jnp.zeros((CONV_PAD, ct), F32)
    for r in range(0, seq, tc):
        dpad_ref[r:r + tc, :] = dhc_ref[0, r:r + tc, :]
    acc_ref[...] = jnp.zeros(acc_ref.shape, F32)
    for r in range(0, seq, tc):
        dh = dhc_ref[0, r:r + tc, :]
        dglu = jnp.zeros((tc, ct), F32)
        for k in range(CONV_K):
            o = r + (CONV_K - 1) - k
            dglu = dglu + dw_ref[k:k + 1, :] * dpad_ref[o:o + tc, :]
            o = CONV_PAD + r - (CONV_K - 1) + k
            prod = pad_ref[o:o + tc, :] * dh
            acc_ref[k] += jnp.sum(prod.reshape(tc // 8, 8, ct), axis=0)
        acc_ref[CONV_K] += jnp.sum(dh.reshape(tc // 8, 8, ct), axis=0)
        a = a_ref[0, r:r + tc, :]
        sg = jax.nn.sigmoid(g_ref[0, r:r + tc, :])
        da_ref[0, r:r + tc, :] = (dglu * sg).astype(da_ref.dtype)
        dg_ref[0, r:r + tc, :] = (dglu * a * sg * (1.0 - sg)).astype(dg_ref.dtype)
    tot = jnp.sum(acc_ref[...], axis=1)

    @pl.when(b == 0)
    def _():
        ddw_ref[...] = tot

    @pl.when(b > 0)
    def _():
        ddw_ref[...] += tot


def _conv_bwd(p0, dhc, dw, bsz, seq, comm=None):
    ct = CONV_CT
    tc = min(CONV_TC, seq)
    p3 = p0.reshape(bsz, seq, P0_COLS)
    blk = pl.BlockSpec((1, seq, ct), lambda j, b: (b, 0, j))
    return _call(
        functools.partial(_conv_bwd_kernel, seq=seq, tc=tc), comm,
        name="conv_bwd",
        grid=(MAIN // ct, bsz),
        in_specs=[pl.BlockSpec((1, seq, ct), lambda j, b: (b, 0, P0_A // ct + j)),
                  pl.BlockSpec((1, seq, ct), lambda j, b: (b, 0, P0_G // ct + j)),
                  blk,
                  pl.BlockSpec((CONV_K, ct), lambda j, b: (0, j))],
        out_specs=[blk, blk, pl.BlockSpec((CONV_K + 1, ct), lambda j, b: (0, j))],
        out_shape=[jax.ShapeDtypeStruct((bsz, seq, MAIN), BF16), jax.ShapeDtypeStruct((bsz, seq, MAIN), BF16),
                   jax.ShapeDtypeStruct((CONV_K + 1, MAIN), F32)],
        scratch_shapes=[pltpu.VMEM((seq + CONV_PAD, ct), F32), pltpu.VMEM((seq + CONV_PAD, ct), F32),
                        pltpu.VMEM((CONV_K + 1, 8, ct), F32)],
        semantics=("parallel", "arbitrary"),
        args=[p3, p3, dhc, dw])


def _ln_parts(x, lng, lnb):
    mu = jnp.mean(x, axis=-1, keepdims=True)
    xc = x - mu
    rstd = lax.rsqrt(jnp.mean(xc * xc, axis=-1, keepdims=True) + LN_EPS)
    xh = xc * rstd
    hl = xh * lng + lnb
    return rstd, xh, hl


GATE_CHUNK = 512
GATE_NZ = MIX // GATE_CHUNK


def _z_specs(tm, zb):
    return [pl.BlockSpec((tm, GATE_CHUNK), lambda i, j=j: (i, zb + j)) for j in range(GATE_NZ)]


def _gate_fwd_kernel(*refs, ln):
    main_ref, ymem_ref, *z_refs = refs[:2 + GATE_NZ]
    y_ref = refs[-1]
    x = main_ref[...]
    if ln:
        _, _, hl = _ln_parts(x, refs[-3][...], refs[-2][...])
        x = hl * jax.nn.sigmoid(hl)
    for j, z_ref in enumerate(z_refs):
        cols = slice(j * GATE_CHUNK, (j + 1) * GATE_CHUNK)
        z = z_ref[...]
        src = x[:, cols] if j < MAIN // GATE_CHUNK else ymem_ref[...]
        y_ref[:, cols] = (src * (z * jax.nn.sigmoid(z))).astype(y_ref.dtype)


def _gate_fwd(main, ymem, p, zb, ln_g=None, ln_b=None, *, name, tm=256):
    t = main.shape[0]
    tm = _tile(t, tm)
    ln = ln_g is not None
    in_specs = [pl.BlockSpec((tm, MAIN), lambda i: (i, 0)), pl.BlockSpec((tm, MEMW), lambda i: (i, 0))]
    in_specs += _z_specs(tm, zb)
    args = [main, ymem] + [p] * GATE_NZ
    if ln:
        in_specs += [pl.BlockSpec((1, MAIN), lambda i: (0, 0))] * 2
        args += [ln_g.reshape(1, MAIN), ln_b.reshape(1, MAIN)]
    return pl.pallas_call(
        functools.partial(_gate_fwd_kernel, ln=ln),
        name=name,
        grid=(t // tm,),
        in_specs=in_specs,
        out_specs=pl.BlockSpec((tm, MIX), lambda i: (i, 0)),
        out_shape=jax.ShapeDtypeStruct((t, MIX), BF16),
        compiler_params=_cparams(("parallel",)),
    )(*args)


def _gate_bwd_kernel(*refs, ln):
    dy_ref, main_ref, ymem_ref, *z_refs = refs[:3 + GATE_NZ]
    if ln:
        lng_ref, lnb_ref, dmain_ref, dymem_ref, dz_ref, dlng_ref, dlnb_ref = refs[3 + GATE_NZ:]
    else:
        dmain_ref, dymem_ref, dz_ref = refs[3 + GATE_NZ:]
    x = main_ref[...]
    if ln:
        lng = lng_ref[...]
        rstd, xh, hl = _ln_parts(x, lng, lnb_ref[...])
        sh = jax.nn.sigmoid(hl)
        ymain = hl * sh
    else:
        ymain = x
    for j, z_ref in enumerate(z_refs):
        cols = slice(j * GATE_CHUNK, (j + 1) * GATE_CHUNK)
        dy = dy_ref[:, cols]
        z = z_ref[...]
        sg = jax.nn.sigmoid(z)
        dsz = sg * (1.0 + z * (1.0 - sg))
        if j < MAIN // GATE_CHUNK:
            src = ymain[:, cols]
            dmain_ref[:, cols] = dy * (z * sg)
        else:
            src = ymem_ref[...]
            dymem_ref[...] = dy * (z * sg)
        dz_ref[:, cols] = (dy * src * dsz).astype(dz_ref.dtype)
    if not ln:
        return
    dym = dmain_ref[...]
    dhl = dym * (sh * (1.0 + hl * (1.0 - sh)))
    dxh = dhl * lng
    dmain_ref[...] = rstd * (dxh - jnp.mean(dxh, axis=-1, keepdims=True)
                             - xh * jnp.mean(dxh * xh, axis=-1, keepdims=True))
    dlng = jnp.sum(dhl * xh, axis=0, keepdims=True)
    dlnb = jnp.sum(dhl, axis=0, keepdims=True)

    @pl.when(pl.program_id(0) == 0)
    def _():
        dlng_ref[...] = dlng
        dlnb_ref[...] = dlnb

    @pl.when(pl.program_id(0) > 0)
    def _():
        dlng_ref[...] += dlng
        dlnb_ref[...] += dlnb


def _gate_bwd(dy, main, ymem, p, zb, ln_g=None, ln_b=None, *, name, tm=256):
    t = main.shape[0]
    tm = _tile(t, tm)
    ln = ln_g is not None
    r_main = pl.BlockSpec((tm, MAIN), lambda i: (i, 0))
    r_mem = pl.BlockSpec((tm, MEMW), lambda i: (i, 0))
    r_mix = pl.BlockSpec((tm, MIX), lambda i: (i, 0))
    vec = pl.BlockSpec((1, MAIN), lambda i: (0, 0))
    in_specs = [r_mix, r_main, r_mem] + _z_specs(tm, zb)
    args = [dy, main, ymem] + [p] * GATE_NZ
    out_specs = [r_main, r_mem, r_mix]
    out_shape = [jax.ShapeDtypeStruct((t, MAIN), F32), jax.ShapeDtypeStruct((t, MEMW), F32),
                 jax.ShapeDtypeStruct((t, MIX), BF16)]
    if ln:
        in_specs += [vec, vec]
        args += [ln_g.reshape(1, MAIN), ln_b.reshape(1, MAIN)]
        out_specs += [vec, vec]
        out_shape += [jax.ShapeDtypeStruct((1, MAIN), F32)] * 2
    return pl.pallas_call(
        functools.partial(_gate_bwd_kernel, ln=ln),
        name=name,
        grid=(t // tm,),
        in_specs=in_specs,
        out_specs=out_specs,
        out_shape=out_shape,
        compiler_params=_cparams(("arbitrary",)),
    )(*args)


def _dot_nt(a, b):
    return lax.dot_general(a, b, (((1,), (1,)), ((), ())), preferred_element_type=F32)


def _dot_tn(a, b):
    return lax.dot_general(a, b, (((0,), (0,)), ((), ())), preferred_element_type=F32)


def _dot(a, b):
    return jnp.dot(a, b, preferred_element_type=F32)


def _mem_probs(q, k):
    s = _dot_nt(q, k) * MEM_SCALE
    p = jnp.exp(s - jnp.max(s, axis=-1, keepdims=True))
    return p / jnp.sum(p, axis=-1, keepdims=True)


def _mem_fwd_kernel(q_ref, kv_ref, o_ref):
    for h in range(MEM_HEADS):
        c = slice(h * HEAD, (h + 1) * HEAD)
        cv = slice(MEMW + h * HEAD, MEMW + (h + 1) * HEAD)
        p = _mem_probs(q_ref[0, :, c].astype(BF16), kv_ref[0, :, c])
        o_ref[0, :, c] = _dot(p.astype(BF16), kv_ref[0, :, cv])


def _mem_fwd(p, kvm, col_block, bsz, seq, *, name, tq=512):
    tq = _tile(seq, tq)
    p3 = p.reshape(bsz, seq, p.shape[1])
    mlen = kvm.shape[1]
    return pl.pallas_call(
        functools.partial(_mem_fwd_kernel),
        name=name,
        grid=(bsz, seq // tq),
        in_specs=[pl.BlockSpec((1, tq, MEMW), lambda b, i: (b, i, col_block)),
                  pl.BlockSpec((1, mlen, 2 * MEMW), lambda b, i: (b, 0, 0))],
        out_specs=pl.BlockSpec((1, tq, MEMW), lambda b, i: (b, i, 0)),
        out_shape=jax.ShapeDtypeStruct((bsz, seq, MEMW), F32),
        compiler_params=_cparams(("parallel", "parallel")),
    )(p3, kvm)


def _mem_bwd_kernel(q_ref, kv_ref, do_ref, dq_ref, dkv_ref):
    @pl.when(pl.program_id(1) == 0)
    def _():
        dkv_ref[...] = jnp.zeros(dkv_ref.shape, F32)

    for h in range(MEM_HEADS):
        c = slice(h * HEAD, (h + 1) * HEAD)
        cv = slice(MEMW + h * HEAD, MEMW + (h + 1) * HEAD)
        q = q_ref[0, :, c].astype(BF16)
        k = kv_ref[0, :, c]
        v = kv_ref[0, :, cv]
        do = do_ref[0, :, c].astype(BF16)
        p = _mem_probs(q, k)
        dp = _dot_nt(do, v)
        ds = (p * (dp - jnp.sum(p * dp, axis=-1, keepdims=True)) * MEM_SCALE).astype(BF16)
        dq_ref[0, :, c] = _dot(ds, k).astype(dq_ref.dtype)
        dkv_ref[0, :, c] += _dot_tn(ds, q)
        dkv_ref[0, :, cv] += _dot_tn(p.astype(BF16), do)


def _mem_bwd(p, kvm, dymem, col_block, bsz, seq, *, name, tq=512):
    tq = _tile(seq, tq)
    p3 = p.reshape(bsz, seq, p.shape[1])
    mlen = kvm.shape[1]
    return pl.pallas_call(
        functools.partial(_mem_bwd_kernel),
        name=name,
        grid=(bsz, seq // tq),
        in_specs=[pl.BlockSpec((1, tq, MEMW), lambda b, i: (b, i, col_block)),
                  pl.BlockSpec((1, mlen, 2 * MEMW), lambda b, i: (b, 0, 0)),
                  pl.BlockSpec((1, tq, MEMW), lambda b, i: (b, i, 0))],
        out_specs=[pl.BlockSpec((1, tq, MEMW), lambda b, i: (b, i, 0)),
                   pl.BlockSpec((1, mlen, 2 * MEMW), lambda b, i: (b, 0, 0))],
        out_shape=[jax.ShapeDtypeStruct((bsz, seq, MEMW), BF16),
                   jax.ShapeDtypeStruct((bsz, mlen, 2 * MEMW), F32)],
        compiler_params=_cparams(("parallel", "arbitrary")),
    )(p3, kvm, dymem.reshape(bsz, seq, MEMW))


def _swap32(x):
    lane = lax.broadcasted_iota(jnp.int32, x.shape, 1)
    return jnp.where(lane < 32, pltpu.roll(x, 96, 1), pltpu.roll(x, 32, 1))


def _rope(x, cs, sn):
    return x * cs + _swap32(x) * sn


def _rope_t(d, cs, sn):
    return d * cs + _swap32(d * sn)


UP_HEADS = 2


def _q_up_kernel(a_ref, b_ref, cs_ref, sn_ref, o_ref):
    acc = _dot(a_ref[...], b_ref[...])
    cs = cs_ref[...]
    sn = sn_ref[...]
    for h in range(UP_HEADS):
        c0 = slice(h * QK_PAD, h * QK_PAD + HEAD)
        c1 = slice(h * QK_PAD + HEAD, (h + 1) * QK_PAD)
        o_ref[:, c0] = acc[:, c0].astype(o_ref.dtype)
        o_ref[:, c1] = _rope(acc[:, c1], cs, sn).astype(o_ref.dtype)


def _q_up(cqn, w_uq, cs, sn, *, tm=512):
    t, kd = cqn.shape
    tm = _tile(t, tm)
    tn = UP_HEADS * QK_PAD
    tab = pl.BlockSpec((tm, 128), lambda i, j: (i, 0))
    return pl.pallas_call(
        functools.partial(_q_up_kernel),
        name="q_up_rope",
        grid=(t // tm, MLA_HEADS // UP_HEADS),
        in_specs=[pl.BlockSpec((tm, kd), lambda i, j: (i, 0)), pl.BlockSpec((kd, tn), lambda i, j: (0, j)), tab, tab],
        out_specs=pl.BlockSpec((tm, tn), lambda i, j: (i, j)),
        out_shape=jax.ShapeDtypeStruct((t, MLA_HEADS * QK_PAD), BF16),
        compiler_params=_cparams(("parallel", "parallel")),
    )(cqn, w_uq, cs, sn)


def _kv_up_kernel(a_ref, b_ref, kr_ref, cs_ref, sn_ref, k_ref, v_ref):
    acc = _dot(a_ref[...], b_ref[...])
    krr = _rope(kr_ref[...], cs_ref[...], sn_ref[...]).astype(k_ref.dtype)
    for h in range(UP_HEADS):
        k_ref[:, h * QK_PAD:h * QK_PAD + HEAD] = acc[:, h * 2 * HEAD:h * 2 * HEAD + HEAD].astype(k_ref.dtype)
        k_ref[:, h * QK_PAD + HEAD:(h + 1) * QK_PAD] = krr
        v_ref[:, h * HEAD:(h + 1) * HEAD] = acc[:, h * 2 * HEAD + HEAD:(h + 1) * 2 * HEAD].astype(v_ref.dtype)


def _kv_up(ckvn, w_ukv, p1, cs, sn, *, tm=512):
    t, kd = ckvn.shape
    tm = _tile(t, tm)
    tab = pl.BlockSpec((tm, 128), lambda i, j: (i, 0))
    return pl.pallas_call(
        functools.partial(_kv_up_kernel),
        name="kv_up_pack",
        grid=(t // tm, MLA_HEADS // UP_HEADS),
        in_specs=[pl.BlockSpec((tm, kd), lambda i, j: (i, 0)),
                  pl.BlockSpec((kd, UP_HEADS * 2 * HEAD), lambda i, j: (0, j)),
                  pl.BlockSpec((tm, 128), lambda i, j: (i, P1_KR // 128)), tab, tab],
        out_specs=[pl.BlockSpec((tm, UP_HEADS * QK_PAD), lambda i, j: (i, j)),
                   pl.BlockSpec((tm, UP_HEADS * HEAD), lambda i, j: (i, j))],
        out_shape=[jax.ShapeDtypeStruct((t, MLA_HEADS * QK_PAD), BF16), jax.ShapeDtypeStruct((t, MAIN), BF16)],
        compiler_params=_cparams(("parallel", "parallel")),
    )(ckvn, w_ukv, p1, cs, sn)


def _kr_bwd_kernel(d_ref, cs_ref, sn_ref, o_ref):
    acc = d_ref[:, :HEAD]
    for h in range(1, MLA_HEADS):
        acc = acc + d_ref[:, h * HEAD:(h + 1) * HEAD]
    o_ref[...] = _rope_t(acc, cs_ref[...], sn_ref[...]).astype(o_ref.dtype)


def _kr_bwd(dkrr, cs, sn, *, tm=512):
    t = dkrr.shape[0]
    tm = _tile(t, tm)
    tab = pl.BlockSpec((tm, 128), lambda i: (i, 0))
    return pl.pallas_call(
        functools.partial(_kr_bwd_kernel),
        name="kr_bwd",
        grid=(t // tm,),
        in_specs=[pl.BlockSpec((tm, MAIN), lambda i: (i, 0)), tab, tab],
        out_specs=tab,
        out_shape=jax.ShapeDtypeStruct((t, 128), BF16),
        compiler_params=_cparams(("parallel",)),
    )(dkrr, cs, sn)


ATT_T = 256


def _causal(s, t):
    row = lax.broadcasted_iota(jnp.int32, (t, t), 0)
    col = lax.broadcasted_iota(jnp.int32, (t, t), 1)
    return jnp.where(col <= row, s, NEG)


def _attn_fwd_kernel(q_ref, k_ref, v_ref, o_ref, lse_ref, *, seq, t):
    for i in range(seq // t):
        own = slice(i * t, (i + 1) * t)
        q = q_ref[0, own, :]
        sd = _causal(_dot_nt(q, k_ref[0, own, :]) * MLA_SCALE, t)
        m = jnp.max(sd, axis=-1, keepdims=True)
        if i:
            so = _dot_nt(q, k_ref[0, :i * t, :]) * MLA_SCALE
            m = jnp.maximum(m, jnp.max(so, axis=-1, keepdims=True))
        pd = jnp.exp(sd - m)
        l = jnp.sum(pd, axis=-1, keepdims=True)
        acc = _dot(pd.astype(BF16), v_ref[0, own, :])
        if i:
            po = jnp.exp(so - m)
            l = l + jnp.sum(po, axis=-1, keepdims=True)
            acc = acc + _dot(po.astype(BF16), v_ref[0, :i * t, :])
        o_ref[0, own, :] = acc / l
        lse_ref[0, 0, own, :] = m + jnp.log(l)


def _attn_fwd(qb, kb, vb, bsz, seq):
    t = min(ATT_T, seq)
    q3 = qb.reshape(bsz, seq, MLA_HEADS * QK_PAD)
    k3 = kb.reshape(bsz, seq, MLA_HEADS * QK_PAD)
    v3 = vb.reshape(bsz, seq, MAIN)
    qk = pl.BlockSpec((1, seq, QK_PAD), lambda b, h: (b, 0, h))
    vv = pl.BlockSpec((1, seq, HEAD), lambda b, h: (b, 0, h))
    return pl.pallas_call(
        functools.partial(_attn_fwd_kernel, seq=seq, t=t),
        name="attn_fwd",
        grid=(bsz, MLA_HEADS),
        in_specs=[qk, qk, vv],
        out_specs=[vv, pl.BlockSpec((1, 1, seq, 1), lambda b, h: (b, h, 0, 0))],
        out_shape=[jax.ShapeDtypeStruct((bsz, seq, MAIN), F32),
                   jax.ShapeDtypeStruct((bsz, MLA_HEADS, seq, 1), F32)],
        compiler_params=_cparams(("parallel", "parallel")),
    )(q3, k3, v3)


def _attn_bwd_kernel(q_ref, k_ref, v_ref, o_ref, do_ref, lse_ref, cs_ref, sn_ref, dq_ref, dkv_ref, dkr_ref,
                     delta_ref, dqacc_ref, *, seq, t):
    for r in range(0, seq, t):
        rows = slice(r, r + t)
        delta_ref[rows, :] = jnp.sum(do_ref[0, rows, :] * o_ref[0, rows, :], axis=-1, keepdims=True)
    dqacc_ref[...] = jnp.zeros(dqacc_ref.shape, F32)

    def piece(rows, k, v, masked):
        q = q_ref[0, rows, :]
        do = do_ref[0, rows, :].astype(BF16)
        s = _dot_nt(q, k) * MLA_SCALE
        if masked:
            s = _causal(s, t)
        p = jnp.exp(s - lse_ref[0, 0, rows, :])
        dp = _dot_nt(do, v)
        ds = (p * (dp - delta_ref[rows, :]) * MLA_SCALE).astype(BF16)
        dqacc_ref[rows, :] += _dot(ds, k)
        return _dot_tn(ds, q), _dot_tn(p.astype(BF16), do)

    for j in range(seq // t):
        own = slice(j * t, (j + 1) * t)
        k = k_ref[0, own, :]
        v = v_ref[0, own, :]
        dk, dv = piece(own, k, v, True)
        if (j + 1) * t < seq:
            dk2, dv2 = piece(slice((j + 1) * t, seq), k, v, False)
            dk, dv = dk + dk2, dv + dv2
        dkv_ref[0, own, :HEAD] = dk[:, :HEAD].astype(dkv_ref.dtype)
        dkv_ref[0, own, HEAD:] = dv.astype(dkv_ref.dtype)
        dkr_ref[0, own, :] = dk[:, HEAD:]
    for r in range(0, seq, t):
        rows = slice(r, r + t)
        dq = dqacc_ref[rows, :]
        dq_ref[0, rows, :HEAD] = dq[:, :HEAD].astype(dq_ref.dtype)
        dq_ref[0, rows, HEAD:] = _rope_t(dq[:, HEAD:], cs_ref[0, rows, :], sn_ref[0, rows, :]).astype(dq_ref.dtype)


def _attn_bwd(qb, kb, vb, o, do, lse, cs, sn, bsz, seq):
    t = min(ATT_T, seq)
    q3 = qb.reshape(bsz, seq, MLA_HEADS * QK_PAD)
    k3 = kb.reshape(bsz, seq, MLA_HEADS * QK_PAD)
    v3 = vb.reshape(bsz, seq, MAIN)
    qk = pl.BlockSpec((1, seq, QK_PAD), lambda b, h: (b, 0, h))
    vv = pl.BlockSpec((1, seq, HEAD), lambda b, h: (b, 0, h))
    tab = pl.BlockSpec((1, seq, 128), lambda b, h: (b, 0, 0))
    dq, dkv, dkr = pl.pallas_call(
        functools.partial(_attn_bwd_kernel, seq=seq, t=t),
        name="attn_bwd",
        grid=(bsz, MLA_HEADS),
        in_specs=[qk, qk, vv, vv, vv, pl.BlockSpec((1, 1, seq, 1), lambda b, h: (b, h, 0, 0)), tab, tab],
        out_specs=[qk, qk, vv],
        out_shape=[jax.ShapeDtypeStruct((bsz, seq, MLA_HEADS * QK_PAD), BF16),
                   jax.ShapeDtypeStruct((bsz, seq, MLA_HEADS * 2 * HEAD), BF16),
                   jax.ShapeDtypeStruct((bsz, seq, MAIN), F32)],
        scratch_shapes=[pltpu.VMEM((seq, 1), F32), pltpu.VMEM((seq, QK_PAD), F32)],
        compiler_params=_cparams(("parallel", "parallel")),
    )(q3, k3, v3, o, do.reshape(bsz, seq, MAIN), lse, cs.reshape(bsz, seq, 128), sn.reshape(bsz, seq, 128))
    n = bsz * seq
    return dq.reshape(n, -1), dkv.reshape(n, -1), dkr.reshape(n, -1)


def _adamw_kernel(w_ref, g_ref, m_ref, v_ref, d_ref, nm_ref, nv_ref):
    g = g_ref[...]
    m = ADAM_B1 * m_ref[...] + (1.0 - ADAM_B1) * g
    v = ADAM_B2 * v_ref[...] + (1.0 - ADAM_B2) * (g * g)
    m_hat = m / (1.0 - ADAM_B1 ** ADAM_STEP)
    v_hat = v / (1.0 - ADAM_B2 ** ADAM_STEP)
    d_ref[...] = -ADAM_LR * (m_hat / (jnp.sqrt(v_hat) + ADAM_EPS) + ADAM_WD * w_ref[...])
    nm_ref[...] = m
    nv_ref[...] = v


def _adamw(w, g, m, v, *, name):
    shape = w.shape
    c = shape[-1]
    r = w.size // c
    tr = r
    for cand in (512, 256, 128, 64, 32, 16, 8):
        if r % cand == 0 and cand * c * 4 <= 2 * 1024 * 1024:
            tr = cand
            break
    blk = pl.BlockSpec((tr, c), lambda i: (i, 0))
    outs = pl.pallas_call(
        functools.partial(_adamw_kernel),
        name=name,
        grid=(r // tr,),
        in_specs=[blk] * 4,
        out_specs=[blk] * 3,
        out_shape=[jax.ShapeDtypeStruct((r, c), F32)] * 3,
        compiler_params=_cparams(("parallel",)),
    )(w.reshape(r, c), g.reshape(r, c), m.reshape(r, c), v.reshape(r, c))
    return tuple(o.reshape(shape) for o in outs)


def _place():
    return lax.axis_index("x"), lax.axis_index("y"), lax.axis_index("c")


def _other_chips(x, y):
    return [(1 - x, y), (x, 1 - y), (1 - x, 1 - y)]


class _Geom:
    def __init__(self, kind, rows, cols):
        self.kind, self.rows, self.cols, self.hr = kind, rows, cols, rows // 2
        self.full_shape = {"rows": (4 * rows, cols), "cols": (rows, 4 * cols), "chips": (4, rows, cols)}[kind]
        self.nt = self.hr // ADD_TILE

    def view(self, ref, s, h):
        if self.kind == "rows":
            return ref.at[pl.ds(s * self.rows + h * self.hr, self.hr), :]
        if self.kind == "cols":
            return ref.at[pl.ds(h * self.hr, self.hr), pl.ds(s * self.cols, self.cols)]
        return ref.at[s, pl.ds(h * self.hr, self.hr), :]

    def shard_half(self, ref, h):
        return ref.at[pl.ds(h * self.hr, self.hr), :]

    def rows_view(self, ref, s, r0, nr):
        if self.kind == "rows":
            return ref.at[pl.ds(s * self.rows + r0, nr), :]
        if self.kind == "cols":
            return ref.at[pl.ds(r0, nr), pl.ds(s * self.cols, self.cols)]
        return ref.at[s, pl.ds(r0, nr), :]

    def tile_spec(self, chip_half_of):
        if self.kind == "rows":
            def index(*a):
                s, h, i = chip_half_of(*a)
                return (s * (self.rows // ADD_TILE) + h * self.nt + i, 0)
            return pl.BlockSpec((ADD_TILE, self.cols), index)
        if self.kind == "cols":
            def index(*a):
                s, h, i = chip_half_of(*a)
                return (h * self.nt + i, s)
            return pl.BlockSpec((ADD_TILE, self.cols), index)

        def index(*a):
            s, h, i = chip_half_of(*a)
            return (s, h * self.nt + i, 0)
        return pl.BlockSpec((None, ADD_TILE, self.cols), index)


GEOMS = (("w_mem_kv0", _Geom("rows", 256, 1024)), ("w_mem_kv1", _Geom("rows", 256, 1024)),
         ("w_out0", _Geom("rows", 512, 1024)), ("w_out1", _Geom("rows", 512, 1024)),
         ("conv_w_in", _Geom("cols", 1024, 1408)), ("mla_w_ukv", _Geom("cols", 256, 768)),
         ("mla_w_in", _Geom("chips", 1024, 848)), ("mla_w_uq", _Geom("chips", 512, 576)))
N_BIG = len(GEOMS)


def _remote(k, src_ref, dst_ref, to, ssem, rsem):
    return pltpu.make_async_remote_copy(src_ref=src_ref, dst_ref=dst_ref, send_sem=ssem.at[k], recv_sem=rsem.at[k],
                                        device_id=to, device_id_type=MESH)


GROUP_A = ("w_mem_kv0", "w_out0", "conv_w_in")
GROUP_B = ("w_mem_kv1", "w_out1", "mla_w_ukv", "mla_w_in", "mla_w_uq")
GATHER_0 = ("conv_w_in",)
GATHER_1 = ("w_mem_kv0", "w_out0", "mla_w_in")
GATHER_2 = ("w_mem_kv1", "w_out1", "mla_w_ukv", "mla_w_uq")
GEOM = dict(GEOMS)


def _gather_over_ici(geoms, srcs, outs, ssem, rsem, act, base=0):
    x, y, c = _place()
    s = 2 * x + y
    n = len(geoms)
    locals_ = []
    for w, g in enumerate(geoms):
        nr = g.rows // LOCAL_CHUNKS
        for q in range(LOCAL_CHUNKS):
            locals_.append(pltpu.make_async_copy(srcs[w].at[pl.ds(q * nr, nr), :], g.rows_view(outs[w], s, q * nr, nr),
                                                 ssem.at[base + 3 * n + LOCAL_CHUNKS * w + q]))
    sends = [_remote(base + 3 * w + j, g.shard_half(srcs[w], c), g.view(outs[w], s, c), (*chip, c), ssem, rsem)
             for w, g in enumerate(geoms) for j, chip in enumerate(_other_chips(x, y))]
    if act == "start":
        for cp in sends + locals_:
            cp.start()
        return
    for w, g in enumerate(geoms):
        for j, chip in enumerate(_other_chips(x, y)):
            blk = g.view(outs[w], 2 * chip[0] + chip[1], c)
            _remote(base + 3 * w + j, blk, blk, (*chip, c), ssem, rsem).wait_recv()
    for cp in sends:
        cp.wait_send()
    for cp in locals_:
        cp.wait()


def _forward_to_sibling(geoms, fulls, ssem, rsem, act, base=0):
    x, y, c = _place()
    for w, g in enumerate(geoms):
        for j, chip in enumerate(_other_chips(x, y)):
            s = 2 * chip[0] + chip[1]
            mine, theirs = g.view(fulls[w], s, c), g.view(fulls[w], s, 1 - c)
            if act == "start":
                _remote(base + 3 * w + j, mine, mine, (x, y, 1 - c), ssem, rsem).start()
            else:
                _remote(base + 3 * w + j, theirs, theirs, (x, y, 1 - c), ssem, rsem).wait_recv()
                _remote(base + 3 * w + j, mine, mine, (x, y, 1 - c), ssem, rsem).wait_send()


def _gather_comm(forward_names, fulls, ici_names, shards):
    fwd = [GEOM[nm] for nm in forward_names]
    ici = [GEOM[nm] for nm in ici_names]
    nf, base = len(fwd), 3 * len(fwd)

    def run(act):
        def go(i, o, ss, rs):
            _forward_to_sibling(fwd, o[:nf], ss, rs, act)
            _gather_over_ici(ici, i[nf:], o[nf:], ss, rs, act, base)
        return go

    return _Comm(list(fulls) + list(shards),
                 [jax.ShapeDtypeStruct(f.shape, f.dtype) for f in fulls]
                 + [jax.ShapeDtypeStruct(g.full_shape, BF16) for g in ici],
                 base + (3 + LOCAL_CHUNKS) * len(ici), run("start"), run("wait"), aliases={w: w for w in range(nf)})


def _allgather_kernel(*refs, geoms):
    n = len(geoms)
    srcs, outs, (ssem, rsem, fsem, gsem) = refs[:n], refs[n:2 * n], refs[2 * n:]
    _gather_over_ici(geoms, srcs, outs, ssem, rsem, "start")
    _gather_over_ici(geoms, srcs, outs, ssem, rsem, "wait")
    _forward_to_sibling(geoms, outs, fsem, gsem, "start")
    _forward_to_sibling(geoms, outs, fsem, gsem, "wait")


def _allgather_weights(names, shards):
    geoms = [GEOM[nm] for nm in names]
    n = len(geoms)
    return pl.pallas_call(
        functools.partial(_allgather_kernel, geoms=geoms),
        name="allgather_weights",
        in_specs=[HBM] * n,
        out_specs=[HBM] * n,
        out_shape=[jax.ShapeDtypeStruct(g.full_shape, BF16) for g in geoms],
        scratch_shapes=[pltpu.SemaphoreType.DMA(((3 + LOCAL_CHUNKS) * n,)), pltpu.SemaphoreType.DMA((3 * n,)),
                        pltpu.SemaphoreType.DMA((3 * n,)), pltpu.SemaphoreType.DMA((3 * n,))],
    )(*shards)


def _swap_halves_kernel(*refs, geoms):
    n = len(geoms)
    srcs, dsts, (ssem, rsem) = refs[:n], refs[n:2 * n], refs[2 * n:]
    x, y, c = _place()
    cps = []
    for w, g in enumerate(geoms):
        for s in range(4):
            cps.append(_remote(4 * w + s, g.view(srcs[w], s, 1 - c), dsts[w].at[s], (x, y, 1 - c), ssem, rsem))
    for cp in cps:
        cp.start()
    for cp in cps:
        cp.wait()


def _swap_halves(names, gb, *, name):
    geoms = [GEOM[nm] for nm in names]
    n = len(geoms)
    return pl.pallas_call(
        functools.partial(_swap_halves_kernel, geoms=geoms),
        name=name,
        in_specs=[HBM] * n,
        out_specs=[HBM] * n,
        out_shape=[jax.ShapeDtypeStruct((4, g.hr, g.cols), BF16) for g in geoms],
        scratch_shapes=[pltpu.SemaphoreType.DMA((4 * n,)), pltpu.SemaphoreType.DMA((4 * n,))],
    )(*gb)


def _exchange_with_chips(srcs, dsts, ssem, rsem, act):
    x, y, c = _place()
    for w in range(len(srcs)):
        for j, chip in enumerate(_other_chips(x, y)):
            cp = _remote(3 * w + j, srcs[w].at[2 * chip[0] + chip[1]], dsts[w].at[j], (*chip, c), ssem, rsem)
            if act == "start":
                cp.start()
            else:
                cp.wait()


def _exchange_comm(pairs):
    return _Comm(pairs, [jax.ShapeDtypeStruct((3,) + p.shape[1:], p.dtype) for p in pairs], 3 * len(pairs),
                 lambda i, o, ss, rs: _exchange_with_chips(i, o, ss, rs, "start"),
                 lambda i, o, ss, rs: _exchange_with_chips(i, o, ss, rs, "wait"))


def _share_kernel(*refs):
    srcs, dsts, (ssem, rsem) = refs[:N_BIG], refs[N_BIG:2 * N_BIG], refs[2 * N_BIG:]
    x, y, c = _place()
    cps = [_remote(w, srcs[w], dsts[w], (x, y, 1 - c), ssem, rsem) for w in range(N_BIG)]
    for cp in cps:
        cp.start()
    for cp in cps:
        cp.wait()


def _share_with_sibling(halves):
    return pl.pallas_call(
        functools.partial(_share_kernel),
        name="rs_share_halves",
        in_specs=[HBM] * N_BIG,
        out_specs=[HBM] * N_BIG,
        out_shape=[jax.ShapeDtypeStruct(h.shape, h.dtype) for h in halves],
        scratch_shapes=[pltpu.SemaphoreType.DMA((N_BIG,)), pltpu.SemaphoreType.DMA((N_BIG,))],
    )(*halves)


def _gather_sum_kernel(src, gat, tot, ssem, rsem):
    x, y, c = _place()
    me = 4 * x + 2 * y + c
    gat[me] = src[...]
    flips = [(dx, dy, dc) for dx in (0, 1) for dy in (0, 1) for dc in (0, 1)][1:]
    cps = []
    for k, (dx, dy, dc) in enumerate(flips):
        peer = (1 - x if dx else x, 1 - y if dy else y, 1 - c if dc else c)
        cp = pltpu.make_async_remote_copy(src_ref=src, dst_ref=gat.at[me], send_sem=ssem.at[k], recv_sem=rsem.at[k],
                                          device_id=peer, device_id_type=MESH)
        cp.start()
        cps.append((cp, 4 * peer[0] + 2 * peer[1] + peer[2], peer))
    for k, (cp, idx, peer) in enumerate(cps):
        pltpu.make_async_remote_copy(src_ref=src, dst_ref=gat.at[idx], send_sem=ssem.at[k], recv_sem=rsem.at[k],
                                     device_id=peer, device_id_type=MESH).wait_recv()
    for cp, _, _ in cps:
        cp.wait_send()
    acc = gat[0]
    for d in range(1, 8):
        acc = acc + gat[d]
    tot[...] = acc


def _gather_sum_small(a, *, name):
    vm = pl.BlockSpec(memory_space=pltpu.VMEM)
    return pl.pallas_call(
        functools.partial(_gather_sum_kernel),
        name=name,
        in_specs=[vm],
        out_specs=[vm, vm],
        out_shape=[jax.ShapeDtypeStruct((8,) + a.shape, a.dtype), jax.ShapeDtypeStruct(a.shape, a.dtype)],
        scratch_shapes=[pltpu.SemaphoreType.DMA((7,)), pltpu.SemaphoreType.DMA((7,))],
    )(a)


def _add_pairs_kernel(c_ref, g_ref, r_ref, o_ref):
    o_ref[...] = (g_ref[...] + r_ref[...].astype(F32)).astype(o_ref.dtype)


def _add_pairs(geom, core, g, recv, *, name):
    half = pl.BlockSpec((None, ADD_TILE, geom.cols), lambda s, i, cr: (s, i, 0))
    return pl.pallas_call(
        functools.partial(_add_pairs_kernel),
        name=name,
        grid_spec=pltpu.PrefetchScalarGridSpec(
            num_scalar_prefetch=1,
            grid=(4, geom.nt),
            in_specs=[geom.tile_spec(lambda s, i, cr: (s, cr[0], i)), half],
            out_specs=half,
        ),
        out_shape=jax.ShapeDtypeStruct(recv.shape, BF16),
        compiler_params=_cparams(("parallel", "parallel")),
    )(core, g, recv)


def _add_final_kernel(sc_ref, g_ref, r_ref, e_ref, o_ref):
    acc = g_ref[...] + r_ref[...].astype(F32)
    for j in range(3):
        acc = acc + e_ref[j].astype(F32)
    o_ref[...] = acc


def _add_final(geom, chip_core, g, recv, exch, *, name):
    return pl.pallas_call(
        functools.partial(_add_final_kernel),
        name=name,
        grid_spec=pltpu.PrefetchScalarGridSpec(
            num_scalar_prefetch=1,
            grid=(geom.nt,),
            in_specs=[geom.tile_spec(lambda i, sc: (sc[0], sc[1], i)),
                      pl.BlockSpec((None, ADD_TILE, geom.cols), lambda i, sc: (sc[0], i, 0)),
                      pl.BlockSpec((3, ADD_TILE, geom.cols), lambda i, sc: (0, i, 0))],
            out_specs=pl.BlockSpec((ADD_TILE, geom.cols), lambda i, sc: (i, 0)),
        ),
        out_shape=jax.ShapeDtypeStruct((geom.hr, geom.cols), F32),
        compiler_params=_cparams(("parallel",)),
    )(chip_core, g, recv, exch)


def _chip_major(w):
    return w.reshape(w.shape[0], 4, w.shape[1] // 4).transpose(1, 0, 2)


def _from_chip_major(w):
    return w.transpose(1, 0, 2).reshape(w.shape[1], 4 * w.shape[2])


def _mla_in_to_internal(w):
    return jnp.concatenate([w[:, 1344:], w[:, :512], w[:, 832:1344], w[:, 512:768], w[:, 768:832],
                            jnp.zeros((w.shape[0], 64), w.dtype)], axis=1)


def _mla_in_from_internal(w):
    return jnp.concatenate([w[:, P1_CQ:P1_QM], w[:, P1_CKV:P1_KR], w[:, P1_KR:P1_KR + 64], w[:, P1_QM:P1_CKV],
                            w[:, :MIX]], axis=1)


def _uq_to_internal(w):
    w = w.reshape(w.shape[0], MLA_HEADS, HEAD + ROPE)
    return jnp.pad(w, ((0, 0), (0, 0), (0, QK_PAD - HEAD - ROPE))).reshape(w.shape[0], MLA_HEADS * QK_PAD)


def _uq_from_internal(w):
    return w.reshape(w.shape[0], MLA_HEADS, QK_PAD)[:, :, :HEAD + ROPE].reshape(w.shape[0], MLA_HEADS * (HEAD + ROPE))


def _rows128(a, rows):
    flat = a.reshape(-1)
    return jnp.pad(flat, (0, rows * 128 - flat.shape[0])).reshape(rows, 128)


def kernel(x, mem, positions, norm_g, mem_norm_g, w_mem_kv, w_out, conv_w_in, conv_dw, conv_dw_b, conv_ln_g, conv_ln_b, mla_w_in, mla_q_norm_g, mla_w_uq, mla_kv_norm_g, mla_w_ukv, final_norm_g, loss_target, m_norm_g, m_mem_norm_g, m_w_mem_kv, m_w_out, m_conv_w_in, m_conv_dw, m_conv_dw_b, m_conv_ln_g, m_conv_ln_b, m_mla_w_in, m_mla_q_norm_g, m_mla_w_uq, m_mla_kv_norm_g, m_mla_w_ukv, m_final_norm_g, v_norm_g, v_mem_norm_g, v_w_mem_kv, v_w_out, v_conv_w_in, v_conv_dw, v_conv_dw_b, v_conv_ln_g, v_conv_ln_b, v_mla_w_in, v_mla_q_norm_g, v_mla_w_uq, v_mla_kv_norm_g, v_mla_w_ukv, v_final_norm_g):
    bsz, seq, d = x.shape
    n = bsz * seq
    mlen = mem.shape[1]
    ax, ay, ac = _place()
    chip = 2 * ax + ay

    shards = dict(w_mem_kv0=w_mem_kv[0], w_mem_kv1=w_mem_kv[1], w_out0=w_out[0], w_out1=w_out[1],
                  conv_w_in=conv_w_in[0], mla_w_ukv=mla_w_ukv[0], mla_w_in=mla_w_in[0], mla_w_uq=mla_w_uq[0])
    bshard = {nm: sh.astype(BF16) for nm, sh in shards.items()}
    wf = dict(zip(GATHER_0, _allgather_weights(GATHER_0, [bshard[nm] for nm in GATHER_0])))
    w_conv_in = wf["conv_w_in"]
    w_conv_in_t = w_conv_in.T

    small_in = jnp.concatenate([_rows128(conv_dw[0], 96), _rows128(mla_q_norm_g, 8), _rows128(mla_kv_norm_g, 8)],
                               axis=0)
    small_all, _ = _gather_sum_small(small_in, name="gather_small_params")
    small_all = small_all[0::2]
    dw_full = small_all[:, :93].reshape(4, -1)[:, :CONV_K * 384].reshape(4, CONV_K, 384)
    dw_full = dw_full.transpose(1, 0, 2).reshape(CONV_K, MAIN)
    qg_full = small_all[:, 96].reshape(Q_RANK)
    kvg_full = small_all[:, 104, :64].reshape(KV_RANK)

    inv_freq = 1.0 / (ROPE_THETA ** (jnp.arange(0, ROPE, 2, dtype=F32) / ROPE))
    ang = positions.astype(F32).reshape(n, 1) * inv_freq
    cos, sin, zer = jnp.cos(ang), jnp.sin(ang), jnp.zeros((n, 64), F32)
    rope_c = jnp.concatenate([cos, cos, zer], axis=1)
    rope_s = jnp.concatenate([-sin, sin, zer], axis=1)

    x2 = x.reshape(n, d)
    mem2 = mem.reshape(bsz * mlen, d)
    tgt2 = loss_target.reshape(n, d)

    memn = [_rms_fwd(mem2, mem_norm_g[i], name=f"mem_norm{i}") for i in range(2)]
    u0 = _rms_fwd(x2, norm_g[0], name="norm0")
    p0, landed1 = _matmul(u0, w_conv_in, name="conv_in_proj",
                          comm=_gather_comm((), (), GATHER_1, [bshard[nm] for nm in GATHER_1]))
    hc, carried = _conv_fwd(p0, dw_full, conv_dw_b, bsz, seq,
                            comm=_gather_comm(GATHER_1, landed1, GATHER_2, [bshard[nm] for nm in GATHER_2]))
    hc = hc.reshape(n, MAIN)
    wf.update(zip(GATHER_1, carried[:len(GATHER_1)]))
    kvm = [_matmul(memn[0], wf["w_mem_kv0"], out_dtype=BF16, name="mem_kv0").reshape(bsz, mlen, 2 * MEMW)]
    ymem0 = _mem_fwd(p0, kvm[0], P0_QM // MEMW, bsz, seq, name="mem_attn0").reshape(n, MEMW)
    y0 = _gate_fwd(hc, ymem0, p0, P0_Z // GATE_CHUNK, conv_ln_g, conv_ln_b, name="gate0")
    h1, gathered2 = _matmul(y0, wf["w_out0"], x2, name="out_proj0",
                            comm=_gather_comm(GATHER_2, carried[len(GATHER_1):], (), ()))
    wf.update(zip(GATHER_2, gathered2))
    w_mla_in = _mla_in_to_internal(_from_chip_major(wf["mla_w_in"]))
    w_uq = _uq_to_internal(_from_chip_major(wf["mla_w_uq"]))
    w_ukv = wf["mla_w_ukv"]
    w_memkv = [wf["w_mem_kv0"], wf["w_mem_kv1"]]
    w_o = [wf["w_out0"], wf["w_out1"]]
    w_mla_in_t, w_uq_t, w_ukv_t = w_mla_in.T, w_uq.T, w_ukv.T
    w_memkv_t = [w.T for w in w_memkv]
    w_o_t = [w.T for w in w_o]
    kvm.append(_matmul(memn[1], w_memkv[1], out_dtype=BF16, name="mem_kv1").reshape(bsz, mlen, 2 * MEMW))

    u1 = _rms_fwd(h1, norm_g[1], name="norm1")
    p1 = _matmul(u1, w_mla_in, name="mla_in_proj")
    cqn = _rms_fwd(p1, qg_full, width=Q_RANK, col_block=P1_CQ // Q_RANK, name="q_norm")
    ckvn = _rms_fwd(p1, kvg_full, width=KV_RANK, col_block=P1_CKV // KV_RANK, name="kv_norm")
    qb = _q_up(cqn, w_uq, rope_c, rope_s)
    kb, vb = _kv_up(ckvn, w_ukv, p1, rope_c, rope_s)
    o1, lse = _attn_fwd(qb, kb, vb, bsz, seq)
    o1 = o1.reshape(n, MAIN)
    ymem1 = _mem_fwd(p1, kvm[1], P1_QM // MEMW, bsz, seq, name="mem_attn1").reshape(n, MEMW)
    y1 = _gate_fwd(o1, ymem1, p1, P1_Z // GATE_CHUNK, name="gate1")
    h2 = _matmul(y1, w_o[1], h1, name="out_proj1")

    dh2, d_final_g, loss_part = _final_loss(h2, final_norm_g, tgt2)
    loss = lax.psum(loss_part[0, 0], ("x", "y", "c"))

    gbig = {}
    gbig["w_out1"] = _matmul(y1, dh2, ta=True, also_bf16=True, name="d_w_out1")
    dy1 = _matmul(dh2, w_o_t[1], name="d_y1")
    do1, dymem1, dz1 = _gate_bwd(dy1, o1, ymem1, p1, P1_Z // GATE_CHUNK, name="gate1_bwd")
    dqm1, dkvm1 = _mem_bwd(p1, kvm[1], dymem1, P1_QM // MEMW, bsz, seq, name="mem_attn1_bwd")
    dqb, dkv, dkrr = _attn_bwd(qb, kb, vb, o1.reshape(bsz, seq, MAIN), do1, lse, rope_c, rope_s, bsz, seq)
    dkr = _kr_bwd(dkrr, rope_c, rope_s)
    g_w_uq = _matmul(cqn, dqb, ta=True, name="d_w_uq")
    dcqn = _matmul(dqb, w_uq_t, name="d_cqn")
    gbig["mla_w_ukv"] = _matmul(ckvn, dkv, ta=True, also_bf16=True, name="d_w_ukv")
    dckvn = _matmul(dkv, w_ukv_t, name="d_ckvn")
    dcq, g_qg = _rms_bwd(p1, qg_full, dcqn, width=Q_RANK, col_block=P1_CQ // Q_RANK, out_dtype=BF16,
                         name="q_norm_bwd")
    dckv, g_kvg = _rms_bwd(p1, kvg_full, dckvn, width=KV_RANK, col_block=P1_CKV // KV_RANK, out_dtype=BF16,
                           name="kv_norm_bwd")
    dp1 = jnp.concatenate([dz1, dcq, dqm1.reshape(n, MEMW), dckv, dkr], axis=1)
    g_w_mla_in = _matmul(u1, dp1, ta=True, name="d_w_mla_in")
    du1 = _matmul(dp1, w_mla_in_t, name="d_u1")
    dh1, g_norm1 = _rms_bwd(h1, norm_g[1], du1, dh2, name="norm1_bwd")

    def mem_kv_bwd(i, dkvm):
        dk2 = dkvm.reshape(bsz * mlen, 2 * MEMW)
        gbig[f"w_mem_kv{i}"] = _matmul(memn[i], dk2, ta=True, also_bf16=True, name=f"d_w_mem_kv{i}")
        dmemn = _matmul(dk2, w_memkv_t[i], name=f"d_memn{i}")
        return _rms_bwd(mem2, mem_norm_g[i], dmemn, name=f"mem_norm{i}_bwd")[1]

    g_mem_g1 = mem_kv_bwd(1, dkvm1)
    for nm, g_int, back in (("mla_w_in", g_w_mla_in, _mla_in_from_internal), ("mla_w_uq", g_w_uq, _uq_from_internal)):
        g_cm = _chip_major(back(g_int))
        gbig[nm] = (g_cm, g_cm.astype(BF16))

    core = jnp.reshape(ac, (1,)).astype(jnp.int32)
    chip_core = jnp.stack([chip, ac]).astype(jnp.int32)

    def pair_sums(names, tag):
        from_sib = _swap_halves(names, [gbig[nm][1] for nm in names], name=f"rs_sibling_swap_{tag}")
        return from_sib, [_add_pairs(GEOM[nm], core, gbig[nm][0], fs, name=f"rs_add_pairs_{nm}")
                          for nm, fs in zip(names, from_sib)]

    from_sib_b, pairs_b = pair_sums(GROUP_B, "b")

    gbig["w_out0"] = _matmul(y0, dh1, ta=True, also_bf16=True, name="d_w_out0")
    dy0 = _matmul(dh1, w_o_t[0], name="d_y0")
    dhc, dymem0, dz0, g_ln_g, g_ln_b = _gate_bwd(dy0, hc, ymem0, p0, P0_Z // GATE_CHUNK, conv_ln_g, conv_ln_b,
                                                 name="gate0_bwd")
    dqm0, dkvm0 = _mem_bwd(p0, kvm[0], dymem0, P0_QM // MEMW, bsz, seq, name="mem_attn0_bwd")
    (da, dg, g_dw32), exch_b = _conv_bwd(p0, dhc.reshape(bsz, seq, MAIN), dw_full, bsz, seq,
                                         comm=_exchange_comm(pairs_b))
    dp0 = [da.reshape(n, MAIN), dg.reshape(n, MAIN), dqm0.reshape(n, MEMW), dz0]
    gbig["conv_w_in"] = _matmul(u0, dp0, ta=True, also_bf16=True, name="d_w_conv_in")
    g_mem_g = [mem_kv_bwd(0, dkvm0), g_mem_g1]

    from_sib_a, pairs_a = pair_sums(GROUP_A, "a")
    du0, exch_a = _matmul(dp0, w_conv_in_t, name="d_u0", comm=_exchange_comm(pairs_a))
    grad_x, g_norm0 = _rms_bwd(x2, norm_g[0], du0, dh1, name="norm0_bwd")

    from_sibling = dict(zip(GROUP_A + GROUP_B, list(from_sib_a) + list(from_sib_b)))
    exch = dict(zip(GROUP_A + GROUP_B, list(exch_a) + list(exch_b)))
    mine = [_add_final(g, chip_core, gbig[nm][0], from_sibling[nm], exch[nm], name=f"rs_add_final_{nm}")
            for nm, g in GEOMS]
    theirs = _share_with_sibling(mine)
    red = {}
    for (nm, _), a, b in zip(GEOMS, mine, theirs):
        red[nm] = jnp.concatenate([jnp.where(ac == 0, a, b), jnp.where(ac == 0, b, a)], axis=0)
    red["w_mem_kv"] = jnp.stack([red["w_mem_kv0"], red["w_mem_kv1"]])
    red["w_out"] = jnp.stack([red["w_out0"], red["w_out1"]])

    small_g = jnp.concatenate([
        _rows128(jnp.concatenate([g_norm0, g_norm1], axis=0), 16), _rows128(jnp.concatenate(g_mem_g, axis=0), 16),
        _rows128(g_dw32[CONV_K], 16), _rows128(g_ln_g, 16), _rows128(g_ln_b, 16), _rows128(d_final_g, 8),
        _rows128(g_dw32[:CONV_K], 376), _rows128(g_qg, 8), _rows128(g_kvg, 8)], axis=0)
    _, small_sum = _gather_sum_small(small_g, name="allreduce_small_grads")
    flat = small_sum.reshape(-1)

    def take(off, shape):
        size = 1
        for s_ in shape:
            size *= s_
        return flat[off * 128:off * 128 + size].reshape(shape)

    grads = dict(red)
    grads["conv_w_in"] = red["conv_w_in"][None]
    grads["mla_w_in"] = red["mla_w_in"][None]
    grads["mla_w_uq"] = red["mla_w_uq"][None]
    grads["mla_w_ukv"] = red["mla_w_ukv"][None]
    grads["norm_g"] = take(0, (2, D_MODEL))
    grads["mem_norm_g"] = take(16, (2, D_MODEL))
    grads["conv_dw_b"] = take(32, (1, MAIN))
    grads["conv_ln_g"] = take(48, (1, MAIN))
    grads["conv_ln_b"] = take(64, (1, MAIN))
    grads["final_norm_g"] = take(80, (D_MODEL,))
    grads["conv_dw"] = lax.dynamic_slice_in_dim(take(88, (CONV_K, MAIN)), chip * 384, 384, axis=1)[None]
    grads["mla_q_norm_g"] = lax.dynamic_slice_in_dim(take(464, (Q_RANK,)), chip * 128, 128, axis=0)[None]
    grads["mla_kv_norm_g"] = lax.dynamic_slice_in_dim(take(472, (KV_RANK,)), chip * 64, 64, axis=0)[None]

    params = dict(norm_g=norm_g, mem_norm_g=mem_norm_g, w_mem_kv=w_mem_kv, w_out=w_out, conv_w_in=conv_w_in,
                  conv_dw=conv_dw, conv_dw_b=conv_dw_b, conv_ln_g=conv_ln_g, conv_ln_b=conv_ln_b, mla_w_in=mla_w_in,
                  mla_q_norm_g=mla_q_norm_g, mla_w_uq=mla_w_uq, mla_kv_norm_g=mla_kv_norm_g, mla_w_ukv=mla_w_ukv,
                  final_norm_g=final_norm_g)
    mom1 = dict(norm_g=m_norm_g, mem_norm_g=m_mem_norm_g, w_mem_kv=m_w_mem_kv, w_out=m_w_out, conv_w_in=m_conv_w_in,
                conv_dw=m_conv_dw, conv_dw_b=m_conv_dw_b, conv_ln_g=m_conv_ln_g, conv_ln_b=m_conv_ln_b,
                mla_w_in=m_mla_w_in, mla_q_norm_g=m_mla_q_norm_g, mla_w_uq=m_mla_w_uq,
                mla_kv_norm_g=m_mla_kv_norm_g, mla_w_ukv=m_mla_w_ukv, final_norm_g=m_final_norm_g)
    mom2 = dict(norm_g=v_norm_g, mem_norm_g=v_mem_norm_g, w_mem_kv=v_w_mem_kv, w_out=v_w_out, conv_w_in=v_conv_w_in,
                conv_dw=v_conv_dw, conv_dw_b=v_conv_dw_b, conv_ln_g=v_conv_ln_g, conv_ln_b=v_conv_ln_b,
                mla_w_in=v_mla_w_in, mla_q_norm_g=v_mla_q_norm_g, mla_w_uq=v_mla_w_uq,
                mla_kv_norm_g=v_mla_kv_norm_g, mla_w_ukv=v_mla_w_ukv, final_norm_g=v_final_norm_g)
    names = list(params)
    g_out, deltas, new_m, new_v = [], [], [], []
    for nm in names:
        w = params[nm]
        g = grads[nm].reshape(w.shape)
        w2 = w.reshape(1, -1) if w.ndim == 1 else w
        dlt, m_new, v_new = _adamw(w2, g.reshape(w2.shape), mom1[nm].reshape(w2.shape), mom2[nm].reshape(w2.shape),
                                   name=f"adamw_{nm}")
        g_out.append(g)
        deltas.append(dlt.reshape(w.shape))
        new_m.append(m_new.reshape(w.shape))
        new_v.append(v_new.reshape(w.shape))

    return (loss, grad_x.reshape(bsz, seq, d), *g_out, *deltas, *new_m, *new_v)
```

```python
import functools

import jax
import jax.numpy as jnp
from jax import lax
from jax.experimental import pallas as pl
from jax.experimental.pallas import tpu as pltpu

F32 = jnp.float32
BF16 = jnp.bfloat16
MESH = pl.DeviceIdType.MESH

D_MODEL = 1024
MIX = 2048
MAIN = 1536
MEMW = 512
MEM_HEADS = 4
HEAD = 128
CONV_K = 31
CONV_PAD = 32
MLA_HEADS = 12
ROPE = 64
QK_PAD = 256
Q_RANK = 512
KV_RANK = 256
ROPE_THETA = 10000.0
RMS_EPS = 1e-6
LN_EPS = 1e-5
MEM_SCALE = HEAD ** -0.5
MLA_SCALE = (HEAD + ROPE) ** -0.5
NEG = -1e30

P0_COLS = 5632
P0_A, P0_G, P0_QM, P0_Z = 0, 1536, 3072, 3584
P1_COLS = 3456
P1_Z, P1_CQ, P1_QM, P1_CKV, P1_KR = 0, 2048, 2560, 3072, 3328

ADAM_LR = 0.001
ADAM_B1 = 0.9
ADAM_B2 = 0.999
ADAM_EPS = 1e-08
ADAM_WD = 0.01
ADAM_STEP = 10

VMEM_LIMIT = 56 * 1024 * 1024

ADD_TILE = 128
LOCAL_CHUNKS = 4


def _cparams(sem=None):
    return pltpu.CompilerParams(dimension_semantics=sem, vmem_limit_bytes=VMEM_LIMIT)


HBM = pl.BlockSpec(memory_space=pl.ANY)


class _Comm:
    def __init__(self, ins, out_shapes, n_sem, start, finish, aliases=None):
        self.ins, self.out_shapes, self.n_sem = list(ins), list(out_shapes), n_sem
        self.start, self.finish, self.aliases = start, finish, dict(aliases or {})


def _call(kernel_fn, comm, *, name, grid, in_specs, out_specs, out_shape, scratch_shapes, semantics, args):
    if comm is None:
        outs = pl.pallas_call(kernel_fn, name=name, grid=grid, in_specs=in_specs, out_specs=out_specs,
                              out_shape=out_shape, scratch_shapes=scratch_shapes,
                              compiler_params=_cparams(semantics))(*args)
        return list(outs), []
    n_in, n_out, nci, nco = len(in_specs), len(out_shape), len(comm.ins), len(comm.out_shapes)

    def body(*refs):
        ins, cins = refs[:n_in], refs[n_in:n_in + nci]
        outs = refs[n_in + nci:n_in + nci + n_out]
        couts = refs[n_in + nci + n_out:n_in + nci + n_out + nco]
        scratch, (ssem, rsem) = refs[n_in + nci + n_out + nco:-2], refs[-2:]
        ids = [pl.program_id(ax) for ax in range(len(grid))]
        first = functools.reduce(jnp.logical_and, [i == 0 for i in ids])
        last = functools.reduce(jnp.logical_and, [i == g - 1 for i, g in zip(ids, grid)])

        @pl.when(first)
        def _():
            comm.start(cins, couts, ssem, rsem)

        kernel_fn(*ins, *outs, *scratch)

        @pl.when(last)
        def _():
            comm.finish(cins, couts, ssem, rsem)

    outs = pl.pallas_call(
        body, name=name, grid=grid,
        in_specs=list(in_specs) + [HBM] * nci,
        out_specs=list(out_specs) + [HBM] * nco,
        out_shape=list(out_shape) + comm.out_shapes,
        scratch_shapes=list(scratch_shapes) + [pltpu.SemaphoreType.DMA((comm.n_sem,))] * 2,
        input_output_aliases={n_in + i: n_out + o for i, o in comm.aliases.items()},
        compiler_params=_cparams(("arbitrary",) * len(grid)),
    )(*args, *comm.ins)
    return list(outs[:n_out]), list(outs[n_out:])


def _tile(n, pref):
    if n <= pref:
        return n
    t = (pref // 128) * 128
    while t > 128 and n % t:
        t -= 128
    assert n % t == 0, (n, pref)
    return t


def _mm_kernel(*refs, nk, ta, has_res, a_parts, b_parts):
    a_refs, b_refs = refs[:len(a_parts)], refs[len(a_parts):len(a_parts) + len(b_parts)]
    rest = refs[len(a_parts) + len(b_parts):]
    if has_res:
        r_ref, *o_refs, acc_ref = rest
    else:
        *o_refs, acc_ref = rest
    j, k = pl.program_id(1), pl.program_id(2)

    def finish(acc):
        if has_res:
            acc = r_ref[...] + acc
        for o in o_refs:
            o[...] = acc.astype(o.dtype)

    def step(a_ref, b_ref):
        dn = (((0 if ta else 1,), (0,)), ((), ()))
        p = lax.dot_general(a_ref[...].astype(BF16), b_ref[...].astype(BF16), dn, preferred_element_type=F32)
        if nk == 1:
            finish(p)
            return

        @pl.when(k == 0)
        def _():
            acc_ref[...] = p

        @pl.when(jnp.logical_and(k > 0, k < nk - 1))
        def _():
            acc_ref[...] += p

        @pl.when(k == nk - 1)
        def _():
            finish(acc_ref[...] + p)

    for a_ref, (k0, kn) in zip(a_refs, a_parts):
        for b_ref, (j0, jn) in zip(b_refs, b_parts):
            conds = []
            if len(a_parts) > 1:
                conds.append(jnp.logical_and(k >= k0, k < k0 + kn))
            if len(b_parts) > 1:
                conds.append(jnp.logical_and(j >= j0, j < j0 + jn))
            if conds:
                pl.when(functools.reduce(jnp.logical_and, conds))(functools.partial(step, a_ref, b_ref))
            else:
                step(a_ref, b_ref)


def _matmul(a, b, res=None, *, name, ta=False, out_dtype=F32, also_bf16=False, comm=None, tm=1024, tn=512, tk=2048):
    a_list = list(a) if isinstance(a, (list, tuple)) else [a]
    b_list = list(b) if isinstance(b, (list, tuple)) else [b]
    assert not (ta and len(a_list) > 1)
    m = a_list[0].shape[1] if ta else a_list[0].shape[0]
    kd = a_list[0].shape[0] if ta else sum(p.shape[1] for p in a_list)
    n = sum(p.shape[1] for p in b_list)
    assert all(p.shape[0] == kd for p in b_list)
    tm = _tile(m, tm)
    tn = 512 if len(b_list) > 1 else _tile(n, tn)
    tk = 512 if len(a_list) > 1 else _tile(kd, tk)
    nk = kd // tk

    def parts(widths, t):
        out, start = [], 0
        for w in widths:
            assert w % t == 0
            out.append((start, w // t))
            start += w // t
        return out

    a_parts = parts([p.shape[1] for p in a_list], tk) if len(a_list) > 1 else [(0, nk)]
    b_parts = parts([p.shape[1] for p in b_list], tn) if len(b_list) > 1 else [(0, n // tn)]
    if ta:
        a_specs = [pl.BlockSpec((tk, tm), lambda i, j, k: (k, i))]
    else:
        a_specs = [pl.BlockSpec((tm, tk), lambda i, j, k, k0=k0, kn=kn: (i, jnp.clip(k - k0, 0, kn - 1)))
                   for k0, kn in a_parts]
    b_specs = [pl.BlockSpec((tk, tn), lambda i, j, k, j0=j0, jn=jn: (
                   jnp.where(jnp.logical_and(j >= j0, j < j0 + jn), k, 0), jnp.clip(j - j0, 0, jn - 1)))
               for j0, jn in b_parts]
    out_spec = pl.BlockSpec((tm, tn), lambda i, j, k: (i, j))
    in_specs = a_specs + b_specs
    args = a_list + b_list
    if res is not None:
        in_specs.append(out_spec)
        args.append(res)
    out_shape = [jax.ShapeDtypeStruct((m, n), out_dtype)]
    if also_bf16:
        out_shape.append(jax.ShapeDtypeStruct((m, n), BF16))
    outs, carried = _call(
        functools.partial(_mm_kernel, nk=nk, ta=ta, has_res=res is not None, a_parts=a_parts, b_parts=b_parts), comm,
        name=name,
        grid=(m // tm, n // tn, nk),
        in_specs=in_specs,
        out_specs=[out_spec] * len(out_shape),
        out_shape=out_shape,
        scratch_shapes=[pltpu.VMEM((tm, tn), F32)],
        semantics=("parallel", "parallel", "arbitrary"),
        args=args)
    result = tuple(outs) if also_bf16 else outs[0]
    return result if comm is None else (result, carried)


def _rms_fwd_kernel(h_ref, g_ref, o_ref):
    h = h_ref[...]
    rstd = lax.rsqrt(jnp.mean(h * h, axis=-1, keepdims=True) + RMS_EPS)
    o_ref[...] = (h * rstd * g_ref[...]).astype(o_ref.dtype)


def _rms_fwd(h, g, *, name, width=None, col_block=0, tm=512):
    t = h.shape[0]
    width = width or h.shape[1]
    tm = _tile(t, tm)
    return pl.pallas_call(
        functools.partial(_rms_fwd_kernel),
        name=name,
        grid=(t // tm,),
        in_specs=[pl.BlockSpec((tm, width), lambda i: (i, col_block)),
                  pl.BlockSpec((1, width), lambda i: (0, 0))],
        out_specs=pl.BlockSpec((tm, width), lambda i: (i, 0)),
        out_shape=jax.ShapeDtypeStruct((t, width), BF16),
        compiler_params=_cparams(("parallel",)),
    )(h, g.reshape(1, width))


def _rms_bwd_math(h, g, du):
    rstd = lax.rsqrt(jnp.mean(h * h, axis=-1, keepdims=True) + RMS_EPS)
    dug = du * g
    dh = rstd * dug - h * (rstd * rstd * rstd) * jnp.mean(dug * h, axis=-1, keepdims=True)
    dg = jnp.sum(du * h * rstd, axis=0, keepdims=True)
    return dh, dg


def _rms_bwd_kernel(*refs, has_res):
    if has_res:
        h_ref, g_ref, du_ref, res_ref, dh_ref, dg_ref = refs
    else:
        h_ref, g_ref, du_ref, dh_ref, dg_ref = refs
    dh, dg = _rms_bwd_math(h_ref[...], g_ref[...], du_ref[...].astype(F32))
    if has_res:
        dh = dh + res_ref[...]
    dh_ref[...] = dh.astype(dh_ref.dtype)

    @pl.when(pl.program_id(0) == 0)
    def _():
        dg_ref[...] = dg

    @pl.when(pl.program_id(0) > 0)
    def _():
        dg_ref[...] += dg


def _rms_bwd(h, g, du, res=None, *, name, width=None, col_block=0, out_dtype=F32, tm=512):
    t = h.shape[0]
    width = width or h.shape[1]
    tm = _tile(t, tm)
    row = pl.BlockSpec((tm, width), lambda i: (i, 0))
    in_specs = [pl.BlockSpec((tm, width), lambda i: (i, col_block)),
                pl.BlockSpec((1, width), lambda i: (0, 0)), row]
    args = [h, g.reshape(1, width), du]
    if res is not None:
        in_specs.append(row)
        args.append(res)
    return pl.pallas_call(
        functools.partial(_rms_bwd_kernel, has_res=res is not None),
        name=name,
        grid=(t // tm,),
        in_specs=in_specs,
        out_specs=[row, pl.BlockSpec((1, width), lambda i: (0, 0))],
        out_shape=[jax.ShapeDtypeStruct((t, width), out_dtype), jax.ShapeDtypeStruct((1, width), F32)],
        compiler_params=_cparams(("arbitrary",)),
    )(*args)


def _final_kernel(h_ref, g_ref, t_ref, dh_ref, dg_ref, loss_ref):
    h = h_ref[...]
    g = g_ref[...]
    rstd = lax.rsqrt(jnp.mean(h * h, axis=-1, keepdims=True) + RMS_EPS)
    e = h * rstd * g - t_ref[...]
    part = 0.5 * jnp.sum(jnp.mean(e * e, axis=-1, keepdims=True), axis=0, keepdims=True)
    dh, dg = _rms_bwd_math(h, g, e * (1.0 / D_MODEL))
    dh_ref[...] = dh
    part = jnp.broadcast_to(part, loss_ref.shape)

    @pl.when(pl.program_id(0) == 0)
    def _():
        dg_ref[...] = dg
        loss_ref[...] = part

    @pl.when(pl.program_id(0) > 0)
    def _():
        dg_ref[...] += dg
        loss_ref[...] += part


def _final_loss(h, g, target, *, tm=512):
    t, d = h.shape
    tm = _tile(t, tm)
    row = pl.BlockSpec((tm, d), lambda i: (i, 0))
    return pl.pallas_call(
        functools.partial(_final_kernel),
        name="final_loss",
        grid=(t // tm,),
        in_specs=[row, pl.BlockSpec((1, d), lambda i: (0, 0)), row],
        out_specs=[row, pl.BlockSpec((1, d), lambda i: (0, 0)), pl.BlockSpec((1, 128), lambda i: (0, 0))],
        out_shape=[jax.ShapeDtypeStruct((t, d), F32), jax.ShapeDtypeStruct((1, d), F32),
                   jax.ShapeDtypeStruct((1, 128), F32)],
        compiler_params=_cparams(("arbitrary",)),
    )(h, g.reshape(1, d), target)


CONV_CT = 128
CONV_TC = 256


def _glu_into(pad_ref, a_ref, g_ref, seq, tc):
    ct = pad_ref.shape[1]
    pad_ref[0:CONV_PAD, :] = jnp.zeros((CONV_PAD, ct), F32)
    for r in range(0, seq, tc):
        a = a_ref[0, r:r + tc, :]
        g = g_ref[0, r:r + tc, :]
        pad_ref[CONV_PAD + r:CONV_PAD + r + tc, :] = a * jax.nn.sigmoid(g)


def _conv_fwd_kernel(a_ref, g_ref, dw_ref, dwb_ref, hc_ref, pad_ref, *, seq, tc):
    ct = pad_ref.shape[1]
    _glu_into(pad_ref, a_ref, g_ref, seq, tc)
    for r in range(0, seq, tc):
        acc = jnp.broadcast_to(dwb_ref[...], (tc, ct))
        for k in range(CONV_K):
            o = CONV_PAD + r - (CONV_K - 1) + k
            acc = acc + dw_ref[k:k + 1, :] * pad_ref[o:o + tc, :]
        hc_ref[0, r:r + tc, :] = acc


def _conv_fwd(p0, dw, dwb, bsz, seq, comm=None):
    ct = CONV_CT
    tc = min(CONV_TC, seq)
    p3 = p0.reshape(bsz, seq, P0_COLS)
    outs, carried = _call(
        functools.partial(_conv_fwd_kernel, seq=seq, tc=tc), comm,
        name="conv_fwd",
        grid=(MAIN // ct, bsz),
        in_specs=[pl.BlockSpec((1, seq, ct), lambda j, b: (b, 0, P0_A // ct + j)),
                  pl.BlockSpec((1, seq, ct), lambda j, b: (b, 0, P0_G // ct + j)),
                  pl.BlockSpec((CONV_K, ct), lambda j, b: (0, j)),
                  pl.BlockSpec((1, ct), lambda j, b: (0, j))],
        out_specs=[pl.BlockSpec((1, seq, ct), lambda j, b: (b, 0, j))],
        out_shape=[jax.ShapeDtypeStruct((bsz, seq, MAIN), F32)],
        scratch_shapes=[pltpu.VMEM((seq + CONV_PAD, ct), F32)],
        semantics=("parallel", "parallel"),
        args=[p3, p3, dw, dwb])
    return outs[0], carried


def _conv_bwd_kernel(a_ref, g_ref, dhc_ref, dw_ref, da_ref, dg_ref, ddw_ref, pad_ref, dpad_ref, acc_ref,
                     *, seq, tc):
    ct = pad_ref.shape[1]
    b = pl.program_id(1)
    _glu_into(pad_ref, a_ref, g_ref, seq, tc)
    dpad_ref[seq:seq + CONV_PAD, :] = jnp.zeros((CONV_PAD, ct), F32)
    for r in range(0, seq, tc):
        dpad_ref[r:r + tc, :] = dhc_ref[0, r:r + tc, :]
    acc_ref[...] = jnp.zeros(acc_ref.shape, F32)
    for r in range(0, seq, tc):
        dh = dhc_ref[0, r:r + tc, :]
        dglu = jnp.zeros((tc, ct), F32)
        for k in range(CONV_K):
            o = r + (CONV_K - 1) - k
            dglu = dglu + dw_ref[k:k + 1, :] * dpad_ref[o:o + tc, :]
            o = CONV_PAD + r - (CONV_K - 1) + k
            prod = pad_ref[o:o + tc, :] * dh
            acc_ref[k] += jnp.sum(prod.reshape(tc // 8, 8, ct), axis=0)
        acc_ref[CONV_K] += jnp.sum(dh.reshape(tc // 8, 8, ct), axis=0)
        a = a_ref[0, r:r + tc, :]
        sg = jax.nn.sigmoid(g_ref[0, r:r + tc, :])
        da_ref[0, r:r + tc, :] = (dglu * sg).astype(da_ref.dtype)
        dg_ref[0, r:r + tc, :] = (dglu * a * sg * (1.0 - sg)).astype(dg_ref.dtype)
    tot = jnp.sum(acc_ref[...], axis=1)

    @pl.when(b == 0)
    def _():
        ddw_ref[...] = tot

    @pl.when(b > 0)
    def _():
        ddw_ref[...] += tot


def _conv_bwd(p0, dhc, dw, bsz, seq, comm=None):
    ct = CONV_CT
    tc = min(CONV_TC, seq)
    p3 = p0.reshape(bsz, seq, P0_COLS)
    blk = pl.BlockSpec((1, seq, ct), lambda j, b: (b, 0, j))
    return _call(
        functools.partial(_conv_bwd_kernel, seq=seq, tc=tc), comm,
        name="conv_bwd",
        grid=(MAIN // ct, bsz),
        in_specs=[pl.BlockSpec((1, seq, ct), lambda j, b: (b, 0, P0_A // ct + j)),
                  pl.BlockSpec((1, seq, ct), lambda j, b: (b, 0, P0_G // ct + j)),
                  blk,
                  pl.BlockSpec((CONV_K, ct), lambda j, b: (0, j))],
        out_specs=[blk, blk, pl.BlockSpec((CONV_K + 1, ct), lambda j, b: (0, j))],
        out_shape=[jax.ShapeDtypeStruct((bsz, seq, MAIN), BF16), jax.ShapeDtypeStruct((bsz, seq, MAIN), BF16),
                   jax.ShapeDtypeStruct((CONV_K + 1, MAIN), F32)],
        scratch_shapes=[pltpu.VMEM((seq + CONV_PAD, ct), F32), pltpu.VMEM((seq + CONV_PAD, ct), F32),
                        pltpu.VMEM((CONV_K + 1, 8, ct), F32)],
        semantics=("parallel", "arbitrary"),
        args=[p3, p3, dhc, dw])


def _ln_parts(x, lng, lnb):
    mu = jnp.mean(x, axis=-1, keepdims=True)
    xc = x - mu
    rstd = lax.rsqrt(jnp.mean(xc * xc, axis=-1, keepdims=True) + LN_EPS)
    xh = xc * rstd
    hl = xh * lng + lnb
    return rstd, xh, hl


GATE_CHUNK = 512
GATE_NZ = MIX // GATE_CHUNK


def _z_specs(tm, zb):
    return [pl.BlockSpec((tm, GATE_CHUNK), lambda i, j=j: (i, zb + j)) for j in range(GATE_NZ)]


def _gate_fwd_kernel(*refs, ln):
    main_ref, ymem_ref, *z_refs = refs[:2 + GATE_NZ]
    y_ref = refs[-1]
    x = main_ref[...]
    if ln:
        _, _, hl = _ln_parts(x, refs[-3][...], refs[-2][...])
        x = hl * jax.nn.sigmoid(hl)
    for j, z_ref in enumerate(z_refs):
        cols = slice(j * GATE_CHUNK, (j + 1) * GATE_CHUNK)
        z = z_ref[...]
        src = x[:, cols] if j < MAIN // GATE_CHUNK else ymem_ref[...]
        y_ref[:, cols] = (src * (z * jax.nn.sigmoid(z))).astype(y_ref.dtype)


def _gate_fwd(main, ymem, p, zb, ln_g=None, ln_b=None, *, name, tm=256):
    t = main.shape[0]
    tm = _tile(t, tm)
    ln = ln_g is not None
    in_specs = [pl.BlockSpec((tm, MAIN), lambda i: (i, 0)), pl.BlockSpec((tm, MEMW), lambda i: (i, 0))]
    in_specs += _z_specs(tm, zb)
    args = [main, ymem] + [p] * GATE_NZ
    if ln:
        in_specs += [pl.BlockSpec((1, MAIN), lambda i: (0, 0))] * 2
        args += [ln_g.reshape(1, MAIN), ln_b.reshape(1, MAIN)]
    return pl.pallas_call(
        functools.partial(_gate_fwd_kernel, ln=ln),
        name=name,
        grid=(t // tm,),
        in_specs=in_specs,
        out_specs=pl.BlockSpec((tm, MIX), lambda i: (i, 0)),
        out_shape=jax.ShapeDtypeStruct((t, MIX), BF16),
        compiler_params=_cparams(("parallel",)),
    )(*args)


def _gate_bwd_kernel(*refs, ln):
    dy_ref, main_ref, ymem_ref, *z_refs = refs[:3 + GATE_NZ]
    if ln:
        lng_ref, lnb_ref, dmain_ref, dymem_ref, dz_ref, dlng_ref, dlnb_ref = refs[3 + GATE_NZ:]
    else:
        dmain_ref, dymem_ref, dz_ref = refs[3 + GATE_NZ:]
    x = main_ref[...]
    if ln:
        lng = lng_ref[...]
        rstd, xh, hl = _ln_parts(x, lng, lnb_ref[...])
        sh = jax.nn.sigmoid(hl)
        ymain = hl * sh
    else:
        ymain = x
    for j, z_ref in enumerate(z_refs):
        cols = slice(j * GATE_CHUNK, (j + 1) * GATE_CHUNK)
        dy = dy_ref[:, cols]
        z = z_ref[...]
        sg = jax.nn.sigmoid(z)
        dsz = sg * (1.0 + z * (1.0 - sg))
        if j < MAIN // GATE_CHUNK:
            src = ymain[:, cols]
            dmain_ref[:, cols] = dy * (z * sg)
        else:
            src = ymem_ref[...]
            dymem_ref[...] = dy * (z * sg)
        dz_ref[:, cols] = (dy * src * dsz).astype(dz_ref.dtype)
    if not ln:
        return
    dym = dmain_ref[...]
    dhl = dym * (sh * (1.0 + hl * (1.0 - sh)))
    dxh = dhl * lng
    dmain_ref[...] = rstd * (dxh - jnp.mean(dxh, axis=-1, keepdims=True)
                             - xh * jnp.mean(dxh * xh, axis=-1, keepdims=True))
    dlng = jnp.sum(dhl * xh, axis=0, keepdims=True)
    dlnb = jnp.sum(dhl, axis=0, keepdims=True)

    @pl.when(pl.program_id(0) == 0)
    def _():
        dlng_ref[...] = dlng
        dlnb_ref[...] = dlnb

    @pl.when(pl.program_id(0) > 0)
    def _():
        dlng_ref[...] += dlng
        dlnb_ref[...] += dlnb


def _gate_bwd(dy, main, ymem, p, zb, ln_g=None, ln_b=None, *, name, tm=256):
    t = main.shape[0]
    tm = _tile(t, tm)
    ln = ln_g is not None
    r_main = pl.BlockSpec((tm, MAIN), lambda i: (i, 0))
    r_mem = pl.BlockSpec((tm, MEMW), lambda i: (i, 0))
    r_mix = pl.BlockSpec((tm, MIX), lambda i: (i, 0))
    vec = pl.BlockSpec((1, MAIN), lambda i: (0, 0))
    in_specs = [r_mix, r_main, r_mem] + _z_specs(tm, zb)
    args = [dy, main, ymem] + [p] * GATE_NZ
    out_specs = [r_main, r_mem, r_mix]
    out_shape = [jax.ShapeDtypeStruct((t, MAIN), F32), jax.ShapeDtypeStruct((t, MEMW), F32),
                 jax.ShapeDtypeStruct((t, MIX), BF16)]
    if ln:
        in_specs += [vec, vec]
        args += [ln_g.reshape(1, MAIN), ln_b.reshape(1, MAIN)]
        out_specs += [vec, vec]
        out_shape += [jax.ShapeDtypeStruct((1, MAIN), F32)] * 2
    return pl.pallas_call(
        functools.partial(_gate_bwd_kernel, ln=ln),
        name=name,
        grid=(t // tm,),
        in_specs=in_specs,
        out_specs=out_specs,
        out_shape=out_shape,
        compiler_params=_cparams(("arbitrary",)),
    )(*args)


def _dot_nt(a, b):
    return lax.dot_general(a, b, (((1,), (1,)), ((), ())), preferred_element_type=F32)


def _dot_tn(a, b):
    return lax.dot_general(a, b, (((0,), (0,)), ((), ())), preferred_element_type=F32)


def _dot(a, b):
    return jnp.dot(a, b, preferred_element_type=F32)


def _mem_probs(q, k):
    s = _dot_nt(q, k) * MEM_SCALE
    p = jnp.exp(s - jnp.max(s, axis=-1, keepdims=True))
    return p / jnp.sum(p, axis=-1, keepdims=True)


def _mem_fwd_kernel(q_ref, kv_ref, o_ref):
    for h in range(MEM_HEADS):
        c = slice(h * HEAD, (h + 1) * HEAD)
        cv = slice(MEMW + h * HEAD, MEMW + (h + 1) * HEAD)
        p = _mem_probs(q_ref[0, :, c].astype(BF16), kv_ref[0, :, c])
        o_ref[0, :, c] = _dot(p.astype(BF16), kv_ref[0, :, cv])


def _mem_fwd(p, kvm, col_block, bsz, seq, *, name, tq=512):
    tq = _tile(seq, tq)
    p3 = p.reshape(bsz, seq, p.shape[1])
    mlen = kvm.shape[1]
    return pl.pallas_call(
        functools.partial(_mem_fwd_kernel),
        name=name,
        grid=(bsz, seq // tq),
        in_specs=[pl.BlockSpec((1, tq, MEMW), lambda b, i: (b, i, col_block)),
                  pl.BlockSpec((1, mlen, 2 * MEMW), lambda b, i: (b, 0, 0))],
        out_specs=pl.BlockSpec((1, tq, MEMW), lambda b, i: (b, i, 0)),
        out_shape=jax.ShapeDtypeStruct((bsz, seq, MEMW), F32),
        compiler_params=_cparams(("parallel", "parallel")),
    )(p3, kvm)


def _mem_bwd_kernel(q_ref, kv_ref, do_ref, dq_ref, dkv_ref):
    @pl.when(pl.program_id(1) == 0)
    def _():
        dkv_ref[...] = jnp.zeros(dkv_ref.shape, F32)

    for h in range(MEM_HEADS):
        c = slice(h * HEAD, (h + 1) * HEAD)
        cv = slice(MEMW + h * HEAD, MEMW + (h + 1) * HEAD)
        q = q_ref[0, :, c].astype(BF16)
        k = kv_ref[0, :, c]
        v = kv_ref[0, :, cv]
        do = do_ref[0, :, c].astype(BF16)
        p = _mem_probs(q, k)
        dp = _dot_nt(do, v)
        ds = (p * (dp - jnp.sum(p * dp, axis=-1, keepdims=True)) * MEM_SCALE).astype(BF16)
        dq_ref[0, :, c] = _dot(ds, k).astype(dq_ref.dtype)
        dkv_ref[0, :, c] += _dot_tn(ds, q)
        dkv_ref[0, :, cv] += _dot_tn(p.astype(BF16), do)


def _mem_bwd(p, kvm, dymem, col_block, bsz, seq, *, name, tq=512):
    tq = _tile(seq, tq)
    p3 = p.reshape(bsz, seq, p.shape[1])
    mlen = kvm.shape[1]
    return pl.pallas_call(
        functools.partial(_mem_bwd_kernel),
        name=name,
        grid=(bsz, seq // tq),
        in_specs=[pl.BlockSpec((1, tq, MEMW), lambda b, i: (b, i, col_block)),
                  pl.BlockSpec((1, mlen, 2 * MEMW), lambda b, i: (b, 0, 0)),
                  pl.BlockSpec((1, tq, MEMW), lambda b, i: (b, i, 0))],
        out_specs=[pl.BlockSpec((1, tq, MEMW), lambda b, i: (b, i, 0)),
                   pl.BlockSpec((1, mlen, 2 * MEMW), lambda b, i: (b, 0, 0))],
        out_shape=[jax.ShapeDtypeStruct((bsz, seq, MEMW), BF16),
                   jax.ShapeDtypeStruct((bsz, mlen, 2 * MEMW), F32)],
        compiler_params=_cparams(("parallel", "arbitrary")),
    )(p3, kvm, dymem.reshape(bsz, seq, MEMW))


def _swap32(x):
    lane = lax.broadcasted_iota(jnp.int32, x.shape, 1)
    return jnp.where(lane < 32, pltpu.roll(x, 96, 1), pltpu.roll(x, 32, 1))


def _rope(x, cs, sn):
    return x * cs + _swap32(x) * sn


def _rope_t(d, cs, sn):
    return d * cs + _swap32(d * sn)


UP_HEADS = 2


def _q_up_kernel(a_ref, b_ref, cs_ref, sn_ref, o_ref):
    acc = _dot(a_ref[...], b_ref[...])
    cs = cs_ref[...]
    sn = sn_ref[...]
    for h in range(UP_HEADS):
        c0 = slice(h * QK_PAD, h * QK_PAD + HEAD)
        c1 = slice(h * QK_PAD + HEAD, (h + 1) * QK_PAD)
        o_ref[:, c0] = acc[:, c0].astype(o_ref.dtype)
        o_ref[:, c1] = _rope(acc[:, c1], cs, sn).astype(o_ref.dtype)


def _q_up(cqn, w_uq, cs, sn, *, tm=512):
    t, kd = cqn.shape
    tm = _tile(t, tm)
    tn = UP_HEADS * QK_PAD
    tab = pl.BlockSpec((tm, 128), lambda i, j: (i, 0))
    return pl.pallas_call(
        functools.partial(_q_up_kernel),
        name="q_up_rope",
        grid=(t // tm, MLA_HEADS // UP_HEADS),
        in_specs=[pl.BlockSpec((tm, kd), lambda i, j: (i, 0)), pl.BlockSpec((kd, tn), lambda i, j: (0, j)), tab, tab],
        out_specs=pl.BlockSpec((tm, tn), lambda i, j: (i, j)),
        out_shape=jax.ShapeDtypeStruct((t, MLA_HEADS * QK_PAD), BF16),
        compiler_params=_cparams(("parallel", "parallel")),
    )(cqn, w_uq, cs, sn)


def _kv_up_kernel(a_ref, b_ref, kr_ref, cs_ref, sn_ref, k_ref, v_ref):
    acc = _dot(a_ref[...], b_ref[...])
    krr = _rope(kr_ref[...], cs_ref[...], sn_ref[...]).astype(k_ref.dtype)
    for h in range(UP_HEADS):
        k_ref[:, h * QK_PAD:h * QK_PAD + HEAD] = acc[:, h * 2 * HEAD:h * 2 * HEAD + HEAD].astype(k_ref.dtype)
        k_ref[:, h * QK_PAD + HEAD:(h + 1) * QK_PAD] = krr
        v_ref[:, h * HEAD:(h + 1) * HEAD] = acc[:, h * 2 * HEAD + HEAD:(h + 1) * 2 * HEAD].astype(v_ref.dtype)


def _kv_up(ckvn, w_ukv, p1, cs, sn, *, tm=512):
    t, kd = ckvn.shape
    tm = _tile(t, tm)
    tab = pl.BlockSpec((tm, 128), lambda i, j: (i, 0))
    return pl.pallas_call(
        functools.partial(_kv_up_kernel),
        name="kv_up_pack",
        grid=(t // tm, MLA_HEADS // UP_HEADS),
        in_specs=[pl.BlockSpec((tm, kd), lambda i, j: (i, 0)),
                  pl.BlockSpec((kd, UP_HEADS * 2 * HEAD), lambda i, j: (0, j)),
                  pl.BlockSpec((tm, 128), lambda i, j: (i, P1_KR // 128)), tab, tab],
        out_specs=[pl.BlockSpec((tm, UP_HEADS * QK_PAD), lambda i, j: (i, j)),
                   pl.BlockSpec((tm, UP_HEADS * HEAD), lambda i, j: (i, j))],
        out_shape=[jax.ShapeDtypeStruct((t, MLA_HEADS * QK_PAD), BF16), jax.ShapeDtypeStruct((t, MAIN), BF16)],
        compiler_params=_cparams(("parallel", "parallel")),
    )(ckvn, w_ukv, p1, cs, sn)


def _kr_bwd_kernel(d_ref, cs_ref, sn_ref, o_ref):
    acc = d_ref[:, :HEAD]
    for h in range(1, MLA_HEADS):
        acc = acc + d_ref[:, h * HEAD:(h + 1) * HEAD]
    o_ref[...] = _rope_t(acc, cs_ref[...], sn_ref[...]).astype(o_ref.dtype)


def _kr_bwd(dkrr, cs, sn, *, tm=512):
    t = dkrr.shape[0]
    tm = _tile(t, tm)
    tab = pl.BlockSpec((tm, 128), lambda i: (i, 0))
    return pl.pallas_call(
        functools.partial(_kr_bwd_kernel),
        name="kr_bwd",
        grid=(t // tm,),
        in_specs=[pl.BlockSpec((tm, MAIN), lambda i: (i, 0)), tab, tab],
        out_specs=tab,
        out_shape=jax.ShapeDtypeStruct((t, 128), BF16),
        compiler_params=_cparams(("parallel",)),
    )(dkrr, cs, sn)


ATT_T = 256


def _causal(s, t):
    row = lax.broadcasted_iota(jnp.int32, (t, t), 0)
    col = lax.broadcasted_iota(jnp.int32, (t, t), 1)
    return jnp.where(col <= row, s, NEG)


def _attn_fwd_kernel(q_ref, k_ref, v_ref, o_ref, lse_ref, *, seq, t):
    for i in range(seq // t):
        own = slice(i * t, (i + 1) * t)
        q = q_ref[0, own, :]
        sd = _causal(_dot_nt(q, k_ref[0, own, :]) * MLA_SCALE, t)
        m = jnp.max(sd, axis=-1, keepdims=True)
        if i:
            so = _dot_nt(q, k_ref[0, :i * t, :]) * MLA_SCALE
            m = jnp.maximum(m, jnp.max(so, axis=-1, keepdims=True))
        pd = jnp.exp(sd - m)
        l = jnp.sum(pd, axis=-1, keepdims=True)
        acc = _dot(pd.astype(BF16), v_ref[0, own, :])
        if i:
            po = jnp.exp(so - m)
            l = l + jnp.sum(po, axis=-1, keepdims=True)
            acc = acc + _dot(po.astype(BF16), v_ref[0, :i * t, :])
        o_ref[0, own, :] = acc / l
        lse_ref[0, 0, own, :] = m + jnp.log(l)


def _attn_fwd(qb, kb, vb, bsz, seq):
    t = min(ATT_T, seq)
    q3 = qb.reshape(bsz, seq, MLA_HEADS * QK_PAD)
    k3 = kb.reshape(bsz, seq, MLA_HEADS * QK_PAD)
    v3 = vb.reshape(bsz, seq, MAIN)
    qk = pl.BlockSpec((1, seq, QK_PAD), lambda b, h: (b, 0, h))
    vv = pl.BlockSpec((1, seq, HEAD), lambda b, h: (b, 0, h))
    return pl.pallas_call(
        functools.partial(_attn_fwd_kernel, seq=seq, t=t),
        name="attn_fwd",
        grid=(bsz, MLA_HEADS),
        in_specs=[qk, qk, vv],
        out_specs=[vv, pl.BlockSpec((1, 1, seq, 1), lambda b, h: (b, h, 0, 0))],
        out_shape=[jax.ShapeDtypeStruct((bsz, seq, MAIN), F32),
                   jax.ShapeDtypeStruct((bsz, MLA_HEADS, seq, 1), F32)],
        compiler_params=_cparams(("parallel", "parallel")),
    )(q3, k3, v3)


def _attn_bwd_kernel(q_ref, k_ref, v_ref, o_ref, do_ref, lse_ref, cs_ref, sn_ref, dq_ref, dkv_ref, dkr_ref,
                     delta_ref, dqacc_ref, *, seq, t):
    for r in range(0, seq, t):
        rows = slice(r, r + t)
        delta_ref[rows, :] = jnp.sum(do_ref[0, rows, :] * o_ref[0, rows, :], axis=-1, keepdims=True)
    dqacc_ref[...] = jnp.zeros(dqacc_ref.shape, F32)

    def piece(rows, k, v, masked):
        q = q_ref[0, rows, :]
        do = do_ref[0, rows, :].astype(BF16)
        s = _dot_nt(q, k) * MLA_SCALE
        if masked:
            s = _causal(s, t)
        p = jnp.exp(s - lse_ref[0, 0, rows, :])
        dp = _dot_nt(do, v)
        ds = (p * (dp - delta_ref[rows, :]) * MLA_SCALE).astype(BF16)
        dqacc_ref[rows, :] += _dot(ds, k)
        return _dot_tn(ds, q), _dot_tn(p.astype(BF16), do)

    for j in range(seq // t):
        own = slice(j * t, (j + 1) * t)
        k = k_ref[0, own, :]
        v = v_ref[0, own, :]
        dk, dv = piece(own, k, v, True)
        if (j + 1) * t < seq:
            dk2, dv2 = piece(slice((j + 1) * t, seq), k, v, False)
            dk, dv = dk + dk2, dv + dv2
        dkv_ref[0, own, :HEAD] = dk[:, :HEAD].astype(dkv_ref.dtype)
        dkv_ref[0, own, HEAD:] = dv.astype(dkv_ref.dtype)
        dkr_ref[0, own, :] = dk[:, HEAD:]
    for r in range(0, seq, t):
        rows = slice(r, r + t)
        dq = dqacc_ref[rows, :]
        dq_ref[0, rows, :HEAD] = dq[:, :HEAD].astype(dq_ref.dtype)
        dq_ref[0, rows, HEAD:] = _rope_t(dq[:, HEAD:], cs_ref[0, rows, :], sn_ref[0, rows, :]).astype(dq_ref.dtype)


def _attn_bwd(qb, kb, vb, o, do, lse, cs, sn, bsz, seq):
    t = min(ATT_T, seq)
    q3 = qb.reshape(bsz, seq, MLA_HEADS * QK_PAD)
    k3 = kb.reshape(bsz, seq, MLA_HEADS * QK_PAD)
    v3 = vb.reshape(bsz, seq, MAIN)
    qk = pl.BlockSpec((1, seq, QK_PAD), lambda b, h: (b, 0, h))
    vv = pl.BlockSpec((1, seq, HEAD), lambda b, h: (b, 0, h))
    tab = pl.BlockSpec((1, seq, 128), lambda b, h: (b, 0, 0))
    dq, dkv, dkr = pl.pallas_call(
        functools.partial(_attn_bwd_kernel, seq=seq, t=t),
        name="attn_bwd",
        grid=(bsz, MLA_HEADS),
        in_specs=[qk, qk, vv, vv, vv, pl.BlockSpec((1, 1, seq, 1), lambda b, h: (b, h, 0, 0)), tab, tab],
        out_specs=[qk, qk, vv],
        out_shape=[jax.ShapeDtypeStruct((bsz, seq, MLA_HEADS * QK_PAD), BF16),
                   jax.ShapeDtypeStruct((bsz, seq, MLA_HEADS * 2 * HEAD), BF16),
                   jax.ShapeDtypeStruct((bsz, seq, MAIN), F32)],
        scratch_shapes=[pltpu.VMEM((seq, 1), F32), pltpu.VMEM((seq, QK_PAD), F32)],
        compiler_params=_cparams(("parallel", "parallel")),
    )(q3, k3, v3, o, do.reshape(bsz, seq, MAIN), lse, cs.reshape(bsz, seq, 128), sn.reshape(bsz, seq, 128))
    n = bsz * seq
    return dq.reshape(n, -1), dkv.reshape(n, -1), dkr.reshape(n, -1)


def _adamw_kernel(w_ref, g_ref, m_ref, v_ref, d_ref, nm_ref, nv_ref):
    g = g_ref[...]
    m = ADAM_B1 * m_ref[...] + (1.0 - ADAM_B1) * g
    v = ADAM_B2 * v_ref[...] + (1.0 - ADAM_B2) * (g * g)
    m_hat = m / (1.0 - ADAM_B1 ** ADAM_STEP)
    v_hat = v / (1.0 - ADAM_B2 ** ADAM_STEP)
    d_ref[...] = -ADAM_LR * (m_hat / (jnp.sqrt(v_hat) + ADAM_EPS) + ADAM_WD * w_ref[...])
    nm_ref[...] = m
    nv_ref[...] = v


def _adamw(w, g, m, v, *, name):
    shape = w.shape
    c = shape[-1]
    r = w.size // c
    tr = r
    for cand in (512, 256, 128, 64, 32, 16, 8):
        if r % cand == 0 and cand * c * 4 <= 2 * 1024 * 1024:
            tr = cand
            break
    blk = pl.BlockSpec((tr, c), lambda i: (i, 0))
    outs = pl.pallas_call(
        functools.partial(_adamw_kernel),
        name=name,
        grid=(r // tr,),
        in_specs=[blk] * 4,
        out_specs=[blk] * 3,
        out_shape=[jax.ShapeDtypeStruct((r, c), F32)] * 3,
        compiler_params=_cparams(("parallel",)),
    )(w.reshape(r, c), g.reshape(r, c), m.reshape(r, c), v.reshape(r, c))
    return tuple(o.reshape(shape) for o in outs)


def _place():
    return lax.axis_index("x"), lax.axis_index("y"), lax.axis_index("c")


def _other_chips(x, y):
    return [(1 - x, y), (x, 1 - y), (1 - x, 1 - y)]


class _Geom:
    def __init__(self, kind, rows, cols):
        self.kind, self.rows, self.cols, self.hr = kind, rows, cols, rows // 2
        self.full_shape = {"rows": (4 * rows, cols), "cols": (rows, 4 * cols), "chips": (4, rows, cols)}[kind]
        self.nt = self.hr // ADD_TILE

    def view(self, ref, s, h):
        if self.kind == "rows":
            return ref.at[pl.ds(s * self.rows + h * self.hr, self.hr), :]
        if self.kind == "cols":
            return ref.at[pl.ds(h * self.hr, self.hr), pl.ds(s * self.cols, self.cols)]
        return ref.at[s, pl.ds(h * self.hr, self.hr), :]

    def shard_half(self, ref, h):
        return ref.at[pl.ds(h * self.hr, self.hr), :]

    def rows_view(self, ref, s, r0, nr):
        if self.kind == "rows":
            return ref.at[pl.ds(s * self.rows + r0, nr), :]
        if self.kind == "cols":
            return ref.at[pl.ds(r0, nr), pl.ds(s * self.cols, self.cols)]
        return ref.at[s, pl.ds(r0, nr), :]

    def tile_spec(self, chip_half_of):
        if self.kind == "rows":
            def index(*a):
                s, h, i = chip_half_of(*a)
                return (s * (self.rows // ADD_TILE) + h * self.nt + i, 0)
            return pl.BlockSpec((ADD_TILE, self.cols), index)
        if self.kind == "cols":
            def index(*a):
                s, h, i = chip_half_of(*a)
                return (h * self.nt + i, s)
            return pl.BlockSpec((ADD_TILE, self.cols), index)

        def index(*a):
            s, h, i = chip_half_of(*a)
            return (s, h * self.nt + i, 0)
        return pl.BlockSpec((None, ADD_TILE, self.cols), index)


GEOMS = (("w_mem_kv0", _Geom("rows", 256, 1024)), ("w_mem_kv1", _Geom("rows", 256, 1024)),
         ("w_out0", _Geom("rows", 512, 1024)), ("w_out1", _Geom("rows", 512, 1024)),
         ("conv_w_in", _Geom("cols", 1024, 1408)), ("mla_w_ukv", _Geom("cols", 256, 768)),
         ("mla_w_in", _Geom("chips", 1024, 848)), ("mla_w_uq", _Geom("chips", 512, 576)))
N_BIG = len(GEOMS)


def _remote(k, src_ref, dst_ref, to, ssem, rsem):
    return pltpu.make_async_remote_copy(src_ref=src_ref, dst_ref=dst_ref, send_sem=ssem.at[k], recv_sem=rsem.at[k],
                                        device_id=to, device_id_type=MESH)


GROUP_A = ("w_mem_kv0", "w_out0", "conv_w_in")
GROUP_B = ("w_mem_kv1", "w_out1", "mla_w_ukv", "mla_w_in", "mla_w_uq")
GATHER_0 = ("conv_w_in",)
GATHER_1 = ("w_mem_kv0", "w_out0", "mla_w_ukv", "mla_w_uq")
GATHER_2 = ("w_mem_kv1", "mla_w_in")
GATHER_3 = ("w_out1",)
GEOM = dict(GEOMS)


def _gather_over_ici(geoms, srcs, outs, ssem, rsem, act, base=0):
    x, y, c = _place()
    s = 2 * x + y
    n = len(geoms)
    locals_ = []
    for w, g in enumerate(geoms):
        nr = g.rows // LOCAL_CHUNKS
        for q in range(LOCAL_CHUNKS):
            locals_.append(pltpu.make_async_copy(srcs[w].at[pl.ds(q * nr, nr), :], g.rows_view(outs[w], s, q * nr, nr),
                                                 ssem.at[base + 3 * n + LOCAL_CHUNKS * w + q]))
    sends = [_remote(base + 3 * w + j, g.shard_half(srcs[w], c), g.view(outs[w], s, c), (*chip, c), ssem, rsem)
             for w, g in enumerate(geoms) for j, chip in enumerate(_other_chips(x, y))]
    if act == "start":
        for cp in sends + locals_:
            cp.start()
        return
    for w, g in enumerate(geoms):
        for j, chip in enumerate(_other_chips(x, y)):
            blk = g.view(outs[w], 2 * chip[0] + chip[1], c)
            _remote(base + 3 * w + j, blk, blk, (*chip, c), ssem, rsem).wait_recv()
    for cp in sends:
        cp.wait_send()
    for cp in locals_:
        cp.wait()


def _forward_to_sibling(geoms, fulls, ssem, rsem, act, base=0):
    x, y, c = _place()
    for w, g in enumerate(geoms):
        for j, chip in enumerate(_other_chips(x, y)):
            s = 2 * chip[0] + chip[1]
            mine, theirs = g.view(fulls[w], s, c), g.view(fulls[w], s, 1 - c)
            if act == "start":
                _remote(base + 3 * w + j, mine, mine, (x, y, 1 - c), ssem, rsem).start()
            else:
                _remote(base + 3 * w + j, theirs, theirs, (x, y, 1 - c), ssem, rsem).wait_recv()
                _remote(base + 3 * w + j, mine, mine, (x, y, 1 - c), ssem, rsem).wait_send()


def _gather_comm(forward_names, fulls, ici_names, shards):
    fwd = [GEOM[nm] for nm in forward_names]
    ici = [GEOM[nm] for nm in ici_names]
    nf, base = len(fwd), 3 * len(fwd)

    def run(act):
        def go(i, o, ss, rs):
            _forward_to_sibling(fwd, o[:nf], ss, rs, act)
            _gather_over_ici(ici, i[nf:], o[nf:], ss, rs, act, base)
        return go

    return _Comm(list(fulls) + list(shards),
                 [jax.ShapeDtypeStruct(f.shape, f.dtype) for f in fulls]
                 + [jax.ShapeDtypeStruct(g.full_shape, BF16) for g in ici],
                 base + (3 + LOCAL_CHUNKS) * len(ici), run("start"), run("wait"), aliases={w: w for w in range(nf)})


def _allgather_kernel(*refs, geoms):
    n = len(geoms)
    srcs, outs, (ssem, rsem, fsem, gsem) = refs[:n], refs[n:2 * n], refs[2 * n:]
    _gather_over_ici(geoms, srcs, outs, ssem, rsem, "start")
    _gather_over_ici(geoms, srcs, outs, ssem, rsem, "wait")
    _forward_to_sibling(geoms, outs, fsem, gsem, "start")
    _forward_to_sibling(geoms, outs, fsem, gsem, "wait")


def _allgather_weights(names, shards):
    geoms = [GEOM[nm] for nm in names]
    n = len(geoms)
    return pl.pallas_call(
        functools.partial(_allgather_kernel, geoms=geoms),
        name="allgather_weights",
        in_specs=[HBM] * n,
        out_specs=[HBM] * n,
        out_shape=[jax.ShapeDtypeStruct(g.full_shape, BF16) for g in geoms],
        scratch_shapes=[pltpu.SemaphoreType.DMA(((3 + LOCAL_CHUNKS) * n,)), pltpu.SemaphoreType.DMA((3 * n,)),
                        pltpu.SemaphoreType.DMA((3 * n,)), pltpu.SemaphoreType.DMA((3 * n,))],
    )(*shards)


def _swap_halves_kernel(*refs, geoms):
    n = len(geoms)
    srcs, dsts, (ssem, rsem) = refs[:n], refs[n:2 * n], refs[2 * n:]
    x, y, c = _place()
    cps = []
    for w, g in enumerate(geoms):
        for s in range(4):
            cps.append(_remote(4 * w + s, g.view(srcs[w], s, 1 - c), dsts[w].at[s], (x, y, 1 - c), ssem, rsem))
    for cp in cps:
        cp.start()
    for cp in cps:
        cp.wait()


def _swap_halves(names, gb, *, name):
    geoms = [GEOM[nm] for nm in names]
    n = len(geoms)
    return pl.pallas_call(
        functools.partial(_swap_halves_kernel, geoms=geoms),
        name=name,
        in_specs=[HBM] * n,
        out_specs=[HBM] * n,
        out_shape=[jax.ShapeDtypeStruct((4, g.hr, g.cols), BF16) for g in geoms],
        scratch_shapes=[pltpu.SemaphoreType.DMA((4 * n,)), pltpu.SemaphoreType.DMA((4 * n,))],
    )(*gb)


def _exchange_with_chips(srcs, dsts, ssem, rsem, act):
    x, y, c = _place()
    for w in range(len(srcs)):
        for j, chip in enumerate(_other_chips(x, y)):
            cp = _remote(3 * w + j, srcs[w].at[2 * chip[0] + chip[1]], dsts[w].at[j], (*chip, c), ssem, rsem)
            if act == "start":
                cp.start()
            else:
                cp.wait()


def _exchange_comm(pairs):
    return _Comm(pairs, [jax.ShapeDtypeStruct((3,) + p.shape[1:], p.dtype) for p in pairs], 3 * len(pairs),
                 lambda i, o, ss, rs: _exchange_with_chips(i, o, ss, rs, "start"),
                 lambda i, o, ss, rs: _exchange_with_chips(i, o, ss, rs, "wait"))


def _share_kernel(*refs):
    srcs, dsts, (ssem, rsem) = refs[:N_BIG], refs[N_BIG:2 * N_BIG], refs[2 * N_BIG:]
    x, y, c = _place()
    cps = [_remote(w, srcs[w], dsts[w], (x, y, 1 - c), ssem, rsem) for w in range(N_BIG)]
    for cp in cps:
        cp.start()
    for cp in cps:
        cp.wait()


def _share_with_sibling(halves):
    return pl.pallas_call(
        functools.partial(_share_kernel),
        name="rs_share_halves",
        in_specs=[HBM] * N_BIG,
        out_specs=[HBM] * N_BIG,
        out_shape=[jax.ShapeDtypeStruct(h.shape, h.dtype) for h in halves],
        scratch_shapes=[pltpu.SemaphoreType.DMA((N_BIG,)), pltpu.SemaphoreType.DMA((N_BIG,))],
    )(*halves)


def _gather_sum_kernel(src, gat, tot, ssem, rsem):
    x, y, c = _place()
    me = 4 * x + 2 * y + c
    gat[me] = src[...]
    flips = [(dx, dy, dc) for dx in (0, 1) for dy in (0, 1) for dc in (0, 1)][1:]
    cps = []
    for k, (dx, dy, dc) in enumerate(flips):
        peer = (1 - x if dx else x, 1 - y if dy else y, 1 - c if dc else c)
        cp = pltpu.make_async_remote_copy(src_ref=src, dst_ref=gat.at[me], send_sem=ssem.at[k], recv_sem=rsem.at[k],
                                          device_id=peer, device_id_type=MESH)
        cp.start()
        cps.append((cp, 4 * peer[0] + 2 * peer[1] + peer[2], peer))
    for k, (cp, idx, peer) in enumerate(cps):
        pltpu.make_async_remote_copy(src_ref=src, dst_ref=gat.at[idx], send_sem=ssem.at[k], recv_sem=rsem.at[k],
                                     device_id=peer, device_id_type=MESH).wait_recv()
    for cp, _, _ in cps:
        cp.wait_send()
    acc = gat[0]
    for d in range(1, 8):
        acc = acc + gat[d]
    tot[...] = acc


def _gather_sum_small(a, *, name):
    vm = pl.BlockSpec(memory_space=pltpu.VMEM)
    return pl.pallas_call(
        functools.partial(_gather_sum_kernel),
        name=name,
        in_specs=[vm],
        out_specs=[vm, vm],
        out_shape=[jax.ShapeDtypeStruct((8,) + a.shape, a.dtype), jax.ShapeDtypeStruct(a.shape, a.dtype)],
        scratch_shapes=[pltpu.SemaphoreType.DMA((7,)), pltpu.SemaphoreType.DMA((7,))],
    )(a)


def _add_pairs_kernel(c_ref, g_ref, r_ref, o_ref):
    o_ref[...] = (g_ref[...] + r_ref[...].astype(F32)).astype(o_ref.dtype)


def _add_pairs(geom, core, g, recv, *, name):
    half = pl.BlockSpec((None, ADD_TILE, geom.cols), lambda s, i, cr: (s, i, 0))
    return pl.pallas_call(
        functools.partial(_add_pairs_kernel),
        name=name,
        grid_spec=pltpu.PrefetchScalarGridSpec(
            num_scalar_prefetch=1,
            grid=(4, geom.nt),
            in_specs=[geom.tile_spec(lambda s, i, cr: (s, cr[0], i)), half],
            out_specs=half,
        ),
        out_shape=jax.ShapeDtypeStruct(recv.shape, BF16),
        compiler_params=_cparams(("parallel", "parallel")),
    )(core, g, recv)


def _add_final_kernel(sc_ref, g_ref, r_ref, e_ref, o_ref):
    acc = g_ref[...] + r_ref[...].astype(F32)
    for j in range(3):
        acc = acc + e_ref[j].astype(F32)
    o_ref[...] = acc


def _add_final(geom, chip_core, g, recv, exch, *, name):
    return pl.pallas_call(
        functools.partial(_add_final_kernel),
        name=name,
        grid_spec=pltpu.PrefetchScalarGridSpec(
            num_scalar_prefetch=1,
            grid=(geom.nt,),
            in_specs=[geom.tile_spec(lambda i, sc: (sc[0], sc[1], i)),
                      pl.BlockSpec((None, ADD_TILE, geom.cols), lambda i, sc: (sc[0], i, 0)),
                      pl.BlockSpec((3, ADD_TILE, geom.cols), lambda i, sc: (0, i, 0))],
            out_specs=pl.BlockSpec((ADD_TILE, geom.cols), lambda i, sc: (i, 0)),
        ),
        out_shape=jax.ShapeDtypeStruct((geom.hr, geom.cols), F32),
        compiler_params=_cparams(("parallel",)),
    )(chip_core, g, recv, exch)


def _chip_major(w):
    return w.reshape(w.shape[0], 4, w.shape[1] // 4).transpose(1, 0, 2)


def _from_chip_major(w):
    return w.transpose(1, 0, 2).reshape(w.shape[1], 4 * w.shape[2])


def _mla_in_to_internal(w):
    return jnp.concatenate([w[:, 1344:], w[:, :512], w[:, 832:1344], w[:, 512:768], w[:, 768:832],
                            jnp.zeros((w.shape[0], 64), w.dtype)], axis=1)


def _mla_in_from_internal(w):
    return jnp.concatenate([w[:, P1_CQ:P1_QM], w[:, P1_CKV:P1_KR], w[:, P1_KR:P1_KR + 64], w[:, P1_QM:P1_CKV],
                            w[:, :MIX]], axis=1)


def _uq_to_internal(w):
    w = w.reshape(w.shape[0], MLA_HEADS, HEAD + ROPE)
    return jnp.pad(w, ((0, 0), (0, 0), (0, QK_PAD - HEAD - ROPE))).reshape(w.shape[0], MLA_HEADS * QK_PAD)


def _uq_from_internal(w):
    return w.reshape(w.shape[0], MLA_HEADS, QK_PAD)[:, :, :HEAD + ROPE].reshape(w.shape[0], MLA_HEADS * (HEAD + ROPE))


def _rows128(a, rows):
    flat = a.reshape(-1)
    return jnp.pad(flat, (0, rows * 128 - flat.shape[0])).reshape(rows, 128)


def kernel(x, mem, positions, norm_g, mem_norm_g, w_mem_kv, w_out, conv_w_in, conv_dw, conv_dw_b, conv_ln_g, conv_ln_b, mla_w_in, mla_q_norm_g, mla_w_uq, mla_kv_norm_g, mla_w_ukv, final_norm_g, loss_target, m_norm_g, m_mem_norm_g, m_w_mem_kv, m_w_out, m_conv_w_in, m_conv_dw, m_conv_dw_b, m_conv_ln_g, m_conv_ln_b, m_mla_w_in, m_mla_q_norm_g, m_mla_w_uq, m_mla_kv_norm_g, m_mla_w_ukv, m_final_norm_g, v_norm_g, v_mem_norm_g, v_w_mem_kv, v_w_out, v_conv_w_in, v_conv_dw, v_conv_dw_b, v_conv_ln_g, v_conv_ln_b, v_mla_w_in, v_mla_q_norm_g, v_mla_w_uq, v_mla_kv_norm_g, v_mla_w_ukv, v_final_norm_g):
    bsz, seq, d = x.shape
    n = bsz * seq
    mlen = mem.shape[1]
    ax, ay, ac = _place()
    chip = 2 * ax + ay

    shards = dict(w_mem_kv0=w_mem_kv[0], w_mem_kv1=w_mem_kv[1], w_out0=w_out[0], w_out1=w_out[1],
                  conv_w_in=conv_w_in[0], mla_w_ukv=mla_w_ukv[0], mla_w_in=mla_w_in[0], mla_w_uq=mla_w_uq[0])
    bshard = {nm: sh.astype(BF16) for nm, sh in shards.items()}
    wf = dict(zip(GATHER_0, _allgather_weights(GATHER_0, [bshard[nm] for nm in GATHER_0])))
    w_conv_in = wf["conv_w_in"]
    w_conv_in_t = w_conv_in.T

    small_in = jnp.concatenate([_rows128(conv_dw[0], 96), _rows128(mla_q_norm_g, 8), _rows128(mla_kv_norm_g, 8)],
                               axis=0)
    small_all, _ = _gather_sum_small(small_in, name="gather_small_params")
    small_all = small_all[0::2]
    dw_full = small_all[:, :93].reshape(4, -1)[:, :CONV_K * 384].reshape(4, CONV_K, 384)
    dw_full = dw_full.transpose(1, 0, 2).reshape(CONV_K, MAIN)
    qg_full = small_all[:, 96].reshape(Q_RANK)
    kvg_full = small_all[:, 104, :64].reshape(KV_RANK)

    inv_freq = 1.0 / (ROPE_THETA ** (jnp.arange(0, ROPE, 2, dtype=F32) / ROPE))
    ang = positions.astype(F32).reshape(n, 1) * inv_freq
    cos, sin, zer = jnp.cos(ang), jnp.sin(ang), jnp.zeros((n, 64), F32)
    rope_c = jnp.concatenate([cos, cos, zer], axis=1)
    rope_s = jnp.concatenate([-sin, sin, zer], axis=1)

    x2 = x.reshape(n, d)
    mem2 = mem.reshape(bsz * mlen, d)
    tgt2 = loss_target.reshape(n, d)

    memn = [_rms_fwd(mem2, mem_norm_g[i], name=f"mem_norm{i}") for i in range(2)]
    u0 = _rms_fwd(x2, norm_g[0], name="norm0")
    p0, landed1 = _matmul(u0, w_conv_in, name="conv_in_proj",
                          comm=_gather_comm((), (), GATHER_1, [bshard[nm] for nm in GATHER_1]))
    hc, carried = _conv_fwd(p0, dw_full, conv_dw_b, bsz, seq,
                            comm=_gather_comm(GATHER_1, landed1, GATHER_2, [bshard[nm] for nm in GATHER_2]))
    hc = hc.reshape(n, MAIN)
    wf.update(zip(GATHER_1, carried[:len(GATHER_1)]))
    kvm = [_matmul(memn[0], wf["w_mem_kv0"], out_dtype=BF16, name="mem_kv0").reshape(bsz, mlen, 2 * MEMW)]
    ymem0 = _mem_fwd(p0, kvm[0], P0_QM // MEMW, bsz, seq, name="mem_attn0").reshape(n, MEMW)
    y0 = _gate_fwd(hc, ymem0, p0, P0_Z // GATE_CHUNK, conv_ln_g, conv_ln_b, name="gate0")
    h1, gathered2 = _matmul(y0, wf["w_out0"], x2, name="out_proj0",
                            comm=_gather_comm(GATHER_2, carried[len(GATHER_1):], (), ()))
    wf.update(zip(GATHER_2, gathered2))
    w_mla_in = _mla_in_to_internal(_from_chip_major(wf["mla_w_in"]))
    w_uq = _uq_to_internal(_from_chip_major(wf["mla_w_uq"]))
    w_ukv = wf["mla_w_ukv"]
    w_mla_in_t, w_uq_t, w_ukv_t = w_mla_in.T, w_uq.T, w_ukv.T

    u1 = _rms_fwd(h1, norm_g[1], name="norm1")
    p1, landed3 = _matmul(u1, w_mla_in, name="mla_in_proj",
                          comm=_gather_comm((), (), GATHER_3, [bshard[nm] for nm in GATHER_3]))
    kvm1, gathered3 = _matmul(memn[1], wf["w_mem_kv1"], out_dtype=BF16, name="mem_kv1",
                              comm=_gather_comm(GATHER_3, landed3, (), ()))
    kvm.append(kvm1.reshape(bsz, mlen, 2 * MEMW))
    wf.update(zip(GATHER_3, gathered3))
    w_memkv = [wf["w_mem_kv0"], wf["w_mem_kv1"]]
    w_o = [wf["w_out0"], wf["w_out1"]]
    w_memkv_t = [w.T for w in w_memkv]
    w_o_t = [w.T for w in w_o]
    cqn = _rms_fwd(p1, qg_full, width=Q_RANK, col_block=P1_CQ // Q_RANK, name="q_norm")
    ckvn = _rms_fwd(p1, kvg_full, width=KV_RANK, col_block=P1_CKV // KV_RANK, name="kv_norm")
    qb = _q_up(cqn, w_uq, rope_c, rope_s)
    kb, vb = _kv_up(ckvn, w_ukv, p1, rope_c, rope_s)
    o1, lse = _attn_fwd(qb, kb, vb, bsz, seq)
    o1 = o1.reshape(n, MAIN)
    ymem1 = _mem_fwd(p1, kvm[1], P1_QM // MEMW, bsz, seq, name="mem_attn1").reshape(n, MEMW)
    y1 = _gate_fwd(o1, ymem1, p1, P1_Z // GATE_CHUNK, name="gate1")
    h2 = _matmul(y1, w_o[1], h1, name="out_proj1")

    dh2, d_final_g, loss_part = _final_loss(h2, final_norm_g, tgt2)
    loss = lax.psum(loss_part[0, 0], ("x", "y", "c"))

    gbig = {}
    gbig["w_out1"] = _matmul(y1, dh2, ta=True, also_bf16=True, name="d_w_out1")
    dy1 = _matmul(dh2, w_o_t[1], name="d_y1")
    do1, dymem1, dz1 = _gate_bwd(dy1, o1, ymem1, p1, P1_Z // GATE_CHUNK, name="gate1_bwd")
    dqm1, dkvm1 = _mem_bwd(p1, kvm[1], dymem1, P1_QM // MEMW, bsz, seq, name="mem_attn1_bwd")
    dqb, dkv, dkrr = _attn_bwd(qb, kb, vb, o1.reshape(bsz, seq, MAIN), do1, lse, rope_c, rope_s, bsz, seq)
    dkr = _kr_bwd(dkrr, rope_c, rope_s)
    g_w_uq = _matmul(cqn, dqb, ta=True, name="d_w_uq")
    dcqn = _matmul(dqb, w_uq_t, name="d_cqn")
    gbig["mla_w_ukv"] = _matmul(ckvn, dkv, ta=True, also_bf16=True, name="d_w_ukv")
    dckvn = _matmul(dkv, w_ukv_t, name="d_ckvn")
    dcq, g_qg = _rms_bwd(p1, qg_full, dcqn, width=Q_RANK, col_block=P1_CQ // Q_RANK, out_dtype=BF16,
                         name="q_norm_bwd")
    dckv, g_kvg = _rms_bwd(p1, kvg_full, dckvn, width=KV_RANK, col_block=P1_CKV // KV_RANK, out_dtype=BF16,
                           name="kv_norm_bwd")
    dp1 = jnp.concatenate([dz1, dcq, dqm1.reshape(n, MEMW), dckv, dkr], axis=1)
    g_w_mla_in = _matmul(u1, dp1, ta=True, name="d_w_mla_in")
    du1 = _matmul(dp1, w_mla_in_t, name="d_u1")
    dh1, g_norm1 = _rms_bwd(h1, norm_g[1], du1, dh2, name="norm1_bwd")

    def mem_kv_bwd(i, dkvm):
        dk2 = dkvm.reshape(bsz * mlen, 2 * MEMW)
        gbig[f"w_mem_kv{i}"] = _matmul(memn[i], dk2, ta=True, also_bf16=True, name=f"d_w_mem_kv{i}")
        dmemn = _matmul(dk2, w_memkv_t[i], name=f"d_memn{i}")
        return _rms_bwd(mem2, mem_norm_g[i], dmemn, name=f"mem_norm{i}_bwd")[1]

    g_mem_g1 = mem_kv_bwd(1, dkvm1)
    for nm, g_int, back in (("mla_w_in", g_w_mla_in, _mla_in_from_internal), ("mla_w_uq", g_w_uq, _uq_from_internal)):
        g_cm = _chip_major(back(g_int))
        gbig[nm] = (g_cm, g_cm.astype(BF16))

    core = jnp.reshape(ac, (1,)).astype(jnp.int32)
    chip_core = jnp.stack([chip, ac]).astype(jnp.int32)

    def pair_sums(names, tag):
        from_sib = _swap_halves(names, [gbig[nm][1] for nm in names], name=f"rs_sibling_swap_{tag}")
        return from_sib, [_add_pairs(GEOM[nm], core, gbig[nm][0], fs, name=f"rs_add_pairs_{nm}")
                          for nm, fs in zip(names, from_sib)]

    from_sib_b, pairs_b = pair_sums(GROUP_B, "b")

    gbig["w_out0"] = _matmul(y0, dh1, ta=True, also_bf16=True, name="d_w_out0")
    dy0 = _matmul(dh1, w_o_t[0], name="d_y0")
    dhc, dymem0, dz0, g_ln_g, g_ln_b = _gate_bwd(dy0, hc, ymem0, p0, P0_Z // GATE_CHUNK, conv_ln_g, conv_ln_b,
                                                 name="gate0_bwd")
    dqm0, dkvm0 = _mem_bwd(p0, kvm[0], dymem0, P0_QM // MEMW, bsz, seq, name="mem_attn0_bwd")
    (da, dg, g_dw32), exch_b = _conv_bwd(p0, dhc.reshape(bsz, seq, MAIN), dw_full, bsz, seq,
                                         comm=_exchange_comm(pairs_b))
    dp0 = [da.reshape(n, MAIN), dg.reshape(n, MAIN), dqm0.reshape(n, MEMW), dz0]
    gbig["conv_w_in"] = _matmul(u0, dp0, ta=True, also_bf16=True, name="d_w_conv_in")
    g_mem_g = [mem_kv_bwd(0, dkvm0), g_mem_g1]

    from_sib_a, pairs_a = pair_sums(GROUP_A, "a")
    du0, exch_a = _matmul(dp0, w_conv_in_t, name="d_u0", comm=_exchange_comm(pairs_a))
    grad_x, g_norm0 = _rms_bwd(x2, norm_g[0], du0, dh1, name="norm0_bwd")

    from_sibling = dict(zip(GROUP_A + GROUP_B, list(from_sib_a) + list(from_sib_b)))
    exch = dict(zip(GROUP_A + GROUP_B, list(exch_a) + list(exch_b)))
    mine = [_add_final(g, chip_core, gbig[nm][0], from_sibling[nm], exch[nm], name=f"rs_add_final_{nm}")
            for nm, g in GEOMS]
    theirs = _share_with_sibling(mine)
    red = {}
    for (nm, _), a, b in zip(GEOMS, mine, theirs):
        red[nm] = jnp.concatenate([jnp.where(ac == 0, a, b), jnp.where(ac == 0, b, a)], axis=0)
    red["w_mem_kv"] = jnp.stack([red["w_mem_kv0"], red["w_mem_kv1"]])
    red["w_out"] = jnp.stack([red["w_out0"], red["w_out1"]])

    small_g = jnp.concatenate([
        _rows128(jnp.concatenate([g_norm0, g_norm1], axis=0), 16), _rows128(jnp.concatenate(g_mem_g, axis=0), 16),
        _rows128(g_dw32[CONV_K], 16), _rows128(g_ln_g, 16), _rows128(g_ln_b, 16), _rows128(d_final_g, 8),
        _rows128(g_dw32[:CONV_K], 376), _rows128(g_qg, 8), _rows128(g_kvg, 8)], axis=0)
    _, small_sum = _gather_sum_small(small_g, name="allreduce_small_grads")
    flat = small_sum.reshape(-1)

    def take(off, shape):
        size = 1
        for s_ in shape:
            size *= s_
        return flat[off * 128:off * 128 + size].reshape(shape)

    grads = dict(red)
    grads["conv_w_in"] = red["conv_w_in"][None]
    grads["mla_w_in"] = red["mla_w_in"][None]
    grads["mla_w_uq"] = red["mla_w_uq"][None]
    grads["mla_w_ukv"] = red["mla_w_ukv"][None]
    grads["norm_g"] = take(0, (2, D_MODEL))
    grads["mem_norm_g"] = take(16, (2, D_MODEL))
    grads["conv_dw_b"] = take(32, (1, MAIN))
    grads["conv_ln_g"] = take(48, (1, MAIN))
    grads["conv_ln_b"] = take(64, (1, MAIN))
    grads["final_norm_g"] = take(80, (D_MODEL,))
    grads["conv_dw"] = lax.dynamic_slice_in_dim(take(88, (CONV_K, MAIN)), chip * 384, 384, axis=1)[None]
    grads["mla_q_norm_g"] = lax.dynamic_slice_in_dim(take(464, (Q_RANK,)), chip * 128, 128, axis=0)[None]
    grads["mla_kv_norm_g"] = lax.dynamic_slice_in_dim(take(472, (KV_RANK,)), chip * 64, 64, axis=0)[None]

    params = dict(norm_g=norm_g, mem_norm_g=mem_norm_g, w_mem_kv=w_mem_kv, w_out=w_out, conv_w_in=conv_w_in,
                  conv_dw=conv_dw, conv_dw_b=conv_dw_b, conv_ln_g=conv_ln_g, conv_ln_b=conv_ln_b, mla_w_in=mla_w_in,
                  mla_q_norm_g=mla_q_norm_g, mla_w_uq=mla_w_uq, mla_kv_norm_g=mla_kv_norm_g, mla_w_ukv=mla_w_ukv,
                  final_norm_g=final_norm_g)
    mom1 = dict(norm_g=m_norm_g, mem_norm_g=m_mem_norm_g, w_mem_kv=m_w_mem_kv, w_out=m_w_out, conv_w_in=m_conv_w_in,
                conv_dw=m_conv_dw, conv_dw_b=m_conv_dw_b, conv_ln_g=m_conv_ln_g, conv_ln_b=m_conv_ln_b,
                mla_w_in=m_mla_w_in, mla_q_norm_g=m_mla_q_norm_g, mla_w_uq=m_mla_w_uq,
                mla_kv_norm_g=m_mla_kv_norm_g, mla_w_ukv=m_mla_w_ukv, final_norm_g=m_final_norm_g)
    mom2 = dict(norm_g=v_norm_g, mem_norm_g=v_mem_norm_g, w_mem_kv=v_w_mem_kv, w_out=v_w_out, conv_w_in=v_conv_w_in,
                conv_dw=v_conv_dw, conv_dw_b=v_conv_dw_b, conv_ln_g=v_conv_ln_g, conv_ln_b=v_conv_ln_b,
                mla_w_in=v_mla_w_in, mla_q_norm_g=v_mla_q_norm_g, mla_w_uq=v_mla_w_uq,
                mla_kv_norm_g=v_mla_kv_norm_g, mla_w_ukv=v_mla_w_ukv, final_norm_g=v_final_norm_g)
    names = list(params)
    g_out, deltas, new_m, new_v = [], [], [], []
    for nm in names:
        w = params[nm]
        g = grads[nm].reshape(w.shape)
        w2 = w.reshape(1, -1) if w.ndim == 1 else w
        dlt, m_new, v_new = _adamw(w2, g.reshape(w2.shape), mom1[nm].reshape(w2.shape), mom2[nm].reshape(w2.shape),
                                   name=f"adamw_{nm}")
        g_out.append(g)
        deltas.append(dlt.reshape(w.shape))
        new_m.append(m_new.reshape(w.shape))
        new_v.append(v_new.reshape(w.shape))

    return (loss, grad_x.reshape(bsz, seq, d), *g_out, *deltas, *new_m, *new_v)
```

```python
import functools

import jax
import jax.numpy as jnp
from jax import lax
from jax.experimental import pallas as pl
from jax.experimental.pallas import tpu as pltpu

F32 = jnp.float32
BF16 = jnp.bfloat16
MESH = pl.DeviceIdType.MESH

D_MODEL = 1024
MIX = 2048
MAIN = 1536
MEMW = 512
MEM_HEADS = 4
HEAD = 128
CONV_K = 31
CONV_PAD = 32
MLA_HEADS = 12
ROPE = 64
QK_PAD = 256
Q_RANK = 512
KV_RANK = 256
ROPE_THETA = 10000.0
RMS_EPS = 1e-6
LN_EPS = 1e-5
MEM_SCALE = HEAD ** -0.5
MLA_SCALE = (HEAD + ROPE) ** -0.5
NEG = -1e30

P0_COLS = 5632
P0_A, P0_G, P0_QM, P0_Z = 0, 1536, 3072, 3584
P1_COLS = 3456
P1_Z, P1_CQ, P1_QM, P1_CKV, P1_KR = 0, 2048, 2560, 3072, 3328

ADAM_LR = 0.001
ADAM_B1 = 0.9
ADAM_B2 = 0.999
ADAM_EPS = 1e-08
ADAM_WD = 0.01
ADAM_STEP = 10

VMEM_LIMIT = 56 * 1024 * 1024

ADD_TILE = 128
LOCAL_CHUNKS = 4


def _cparams(sem=None):
    return pltpu.CompilerParams(dimension_semantics=sem, vmem_limit_bytes=VMEM_LIMIT)


HBM = pl.BlockSpec(memory_space=pl.ANY)


class _Comm:
    def __init__(self, ins, out_shapes, n_sem, start, finish, aliases=None):
        self.ins, self.out_shapes, self.n_sem = list(ins), list(out_shapes), n_sem
        self.start, self.finish, self.aliases = start, finish, dict(aliases or {})


def _call(kernel_fn, comm, *, name, grid, in_specs, out_specs, out_shape, scratch_shapes, semantics, args):
    if comm is None:
        outs = pl.pallas_call(kernel_fn, name=name, grid=grid, in_specs=in_specs, out_specs=out_specs,
                              out_shape=out_shape, scratch_shapes=scratch_shapes,
                              compiler_params=_cparams(semantics))(*args)
        return list(outs), []
    n_in, n_out, nci, nco = len(in_specs), len(out_shape), len(comm.ins), len(comm.out_shapes)

    def body(*refs):
        ins, cins = refs[:n_in], refs[n_in:n_in + nci]
        outs = refs[n_in + nci:n_in + nci + n_out]
        couts = refs[n_in + nci + n_out:n_in + nci + n_out + nco]
        scratch, (ssem, rsem) = refs[n_in + nci + n_out + nco:-2], refs[-2:]
        ids = [pl.program_id(ax) for ax in range(len(grid))]
        first = functools.reduce(jnp.logical_and, [i == 0 for i in ids])
        last = functools.reduce(jnp.logical_and, [i == g - 1 for i, g in zip(ids, grid)])

        @pl.when(first)
        def _():
            comm.start(cins, couts, ssem, rsem)

        kernel_fn(*ins, *outs, *scratch)

        @pl.when(last)
        def _():
            comm.finish(cins, couts, ssem, rsem)

    outs = pl.pallas_call(
        body, name=name, grid=grid,
        in_specs=list(in_specs) + [HBM] * nci,
        out_specs=list(out_specs) + [HBM] * nco,
        out_shape=list(out_shape) + comm.out_shapes,
        scratch_shapes=list(scratch_shapes) + [pltpu.SemaphoreType.DMA((comm.n_sem,))] * 2,
        input_output_aliases={n_in + i: n_out + o for i, o in comm.aliases.items()},
        compiler_params=_cparams(("arbitrary",) * len(grid)),
    )(*args, *comm.ins)
    return list(outs[:n_out]), list(outs[n_out:])


def _tile(n, pref):
    if n <= pref:
        return n
    t = (pref // 128) * 128
    while t > 128 and n % t:
        t -= 128
    assert n % t == 0, (n, pref)
    return t


def _mm_kernel(*refs, nk, ta, has_res, a_parts, b_parts):
    a_refs, b_refs = refs[:len(a_parts)], refs[len(a_parts):len(a_parts) + len(b_parts)]
    rest = refs[len(a_parts) + len(b_parts):]
    if has_res:
        r_ref, *o_refs, acc_ref = rest
    else:
        *o_refs, acc_ref = rest
    j, k = pl.program_id(1), pl.program_id(2)

    def finish(acc):
        if has_res:
            acc = r_ref[...] + acc
        for o in o_refs:
            o[...] = acc.astype(o.dtype)

    def step(a_ref, b_ref):
        dn = (((0 if ta else 1,), (0,)), ((), ()))
        p = lax.dot_general(a_ref[...].astype(BF16), b_ref[...].astype(BF16), dn, preferred_element_type=F32)
        if nk == 1:
            finish(p)
            return

        @pl.when(k == 0)
        def _():
            acc_ref[...] = p

        @pl.when(jnp.logical_and(k > 0, k < nk - 1))
        def _():
            acc_ref[...] += p

        @pl.when(k == nk - 1)
        def _():
            finish(acc_ref[...] + p)

    for a_ref, (k0, kn) in zip(a_refs, a_parts):
        for b_ref, (j0, jn) in zip(b_refs, b_parts):
            conds = []
            if len(a_parts) > 1:
                conds.append(jnp.logical_and(k >= k0, k < k0 + kn))
            if len(b_parts) > 1:
                conds.append(jnp.logical_and(j >= j0, j < j0 + jn))
            if conds:
                pl.when(functools.reduce(jnp.logical_and, conds))(functools.partial(step, a_ref, b_ref))
            else:
                step(a_ref, b_ref)


def _matmul(a, b, res=None, *, name, ta=False, out_dtype=F32, also_bf16=False, comm=None, tm=1024, tn=512, tk=2048):
    a_list = list(a) if isinstance(a, (list, tuple)) else [a]
    b_list = list(b) if isinstance(b, (list, tuple)) else [b]
    assert not (ta and len(a_list) > 1)
    m = a_list[0].shape[1] if ta else a_list[0].shape[0]
    kd = a_list[0].shape[0] if ta else sum(p.shape[1] for p in a_list)
    n = sum(p.shape[1] for p in b_list)
    assert all(p.shape[0] == kd for p in b_list)
    tm = _tile(m, tm)
    tn = 512 if len(b_list) > 1 else _tile(n, tn)
    tk = 512 if len(a_list) > 1 else _tile(kd, tk)
    nk = kd // tk

    def parts(widths, t):
        out, start = [], 0
        for w in widths:
            assert w % t == 0
            out.append((start, w // t))
            start += w // t
        return out

    a_parts = parts([p.shape[1] for p in a_list], tk) if len(a_list) > 1 else [(0, nk)]
    b_parts = parts([p.shape[1] for p in b_list], tn) if len(b_list) > 1 else [(0, n // tn)]
    if ta:
        a_specs = [pl.BlockSpec((tk, tm), lambda i, j, k: (k, i))]
    else:
        a_specs = [pl.BlockSpec((tm, tk), lambda i, j, k, k0=k0, kn=kn: (i, jnp.clip(k - k0, 0, kn - 1)))
                   for k0, kn in a_parts]
    b_specs = [pl.BlockSpec((tk, tn), lambda i, j, k, j0=j0, jn=jn: (
                   jnp.where(jnp.logical_and(j >= j0, j < j0 + jn), k, 0), jnp.clip(j - j0, 0, jn - 1)))
               for j0, jn in b_parts]
    out_spec = pl.BlockSpec((tm, tn), lambda i, j, k: (i, j))
    in_specs = a_specs + b_specs
    args = a_list + b_list
    if res is not None:
        in_specs.append(out_spec)
        args.append(res)
    out_shape = [jax.ShapeDtypeStruct((m, n), out_dtype)]
    if also_bf16:
        out_shape.append(jax.ShapeDtypeStruct((m, n), BF16))
    outs, carried = _call(
        functools.partial(_mm_kernel, nk=nk, ta=ta, has_res=res is not None, a_parts=a_parts, b_parts=b_parts), comm,
        name=name,
        grid=(m // tm, n // tn, nk),
        in_specs=in_specs,
        out_specs=[out_spec] * len(out_shape),
        out_shape=out_shape,
        scratch_shapes=[pltpu.VMEM((tm, tn), F32)],
        semantics=("parallel", "parallel", "arbitrary"),
        args=args)
    result = tuple(outs) if also_bf16 else outs[0]
    return result if comm is None else (result, carried)


def _rms_fwd_kernel(h_ref, g_ref, o_ref):
    h = h_ref[...]
    rstd = lax.rsqrt(jnp.mean(h * h, axis=-1, keepdims=True) + RMS_EPS)
    o_ref[...] = (h * rstd * g_ref[...]).astype(o_ref.dtype)


def _rms_fwd(h, g, *, name, width=None, col_block=0, tm=512):
    t = h.shape[0]
    width = width or h.shape[1]
    tm = _tile(t, tm)
    return pl.pallas_call(
        functools.partial(_rms_fwd_kernel),
        name=name,
        grid=(t // tm,),
        in_specs=[pl.BlockSpec((tm, width), lambda i: (i, col_block)),
                  pl.BlockSpec((1, width), lambda i: (0, 0))],
        out_specs=pl.BlockSpec((tm, width), lambda i: (i, 0)),
        out_shape=jax.ShapeDtypeStruct((t, width), BF16),
        compiler_params=_cparams(("parallel",)),
    )(h, g.reshape(1, width))


def _rms_bwd_math(h, g, du):
    rstd = lax.rsqrt(jnp.mean(h * h, axis=-1, keepdims=True) + RMS_EPS)
    dug = du * g
    dh = rstd * dug - h * (rstd * rstd * rstd) * jnp.mean(dug * h, axis=-1, keepdims=True)
    dg = jnp.sum(du * h * rstd, axis=0, keepdims=True)
    return dh, dg


def _rms_bwd_kernel(*refs, has_res):
    if has_res:
        h_ref, g_ref, du_ref, res_ref, dh_ref, dg_ref = refs
    else:
        h_ref, g_ref, du_ref, dh_ref, dg_ref = refs
    dh, dg = _rms_bwd_math(h_ref[...], g_ref[...], du_ref[...].astype(F32))
    if has_res:
        dh = dh + res_ref[...]
    dh_ref[...] = dh.astype(dh_ref.dtype)

    @pl.when(pl.program_id(0) == 0)
    def _():
        dg_ref[...] = dg

    @pl.when(pl.program_id(0) > 0)
    def _():
        dg_ref[...] += dg


def _rms_bwd(h, g, du, res=None, *, name, width=None, col_block=0, out_dtype=F32, tm=512):
    t = h.shape[0]
    width = width or h.shape[1]
    tm = _tile(t, tm)
    row = pl.BlockSpec((tm, width), lambda i: (i, 0))
    in_specs = [pl.BlockSpec((tm, width), lambda i: (i, col_block)),
                pl.BlockSpec((1, width), lambda i: (0, 0)), row]
    args = [h, g.reshape(1, width), du]
    if res is not None:
        in_specs.append(row)
        args.append(res)
    return pl.pallas_call(
        functools.partial(_rms_bwd_kernel, has_res=res is not None),
        name=name,
        grid=(t // tm,),
        in_specs=in_specs,
        out_specs=[row, pl.BlockSpec((1, width), lambda i: (0, 0))],
        out_shape=[jax.ShapeDtypeStruct((t, width), out_dtype), jax.ShapeDtypeStruct((1, width), F32)],
        compiler_params=_cparams(("arbitrary",)),
    )(*args)


def _final_kernel(h_ref, g_ref, t_ref, dh_ref, dg_ref, loss_ref):
    h = h_ref[...]
    g = g_ref[...]
    rstd = lax.rsqrt(jnp.mean(h * h, axis=-1, keepdims=True) + RMS_EPS)
    e = h * rstd * g - t_ref[...]
    part = 0.5 * jnp.sum(jnp.mean(e * e, axis=-1, keepdims=True), axis=0, keepdims=True)
    dh, dg = _rms_bwd_math(h, g, e * (1.0 / D_MODEL))
    dh_ref[...] = dh
    part = jnp.broadcast_to(part, loss_ref.shape)

    @pl.when(pl.program_id(0) == 0)
    def _():
        dg_ref[...] = dg
        loss_ref[...] = part

    @pl.when(pl.program_id(0) > 0)
    def _():
        dg_ref[...] += dg
        loss_ref[...] += part


def _final_loss(h, g, target, *, tm=512):
    t, d = h.shape
    tm = _tile(t, tm)
    row = pl.BlockSpec((tm, d), lambda i: (i, 0))
    return pl.pallas_call(
        functools.partial(_final_kernel),
        name="final_loss",
        grid=(t // tm,),
        in_specs=[row, pl.BlockSpec((1, d), lambda i: (0, 0)), row],
        out_specs=[row, pl.BlockSpec((1, d), lambda i: (0, 0)), pl.BlockSpec((1, 128), lambda i: (0, 0))],
        out_shape=[jax.ShapeDtypeStruct((t, d), F32), jax.ShapeDtypeStruct((1, d), F32),
                   jax.ShapeDtypeStruct((1, 128), F32)],
        compiler_params=_cparams(("arbitrary",)),
    )(h, g.reshape(1, d), target)


CONV_CT = 128
CONV_TC = 256
CONV_TC_BWD = 128


def _glu_into(pad_ref, a_ref, g_ref, seq, tc):
    ct = pad_ref.shape[1]
    pad_ref[0:CONV_PAD, :] = jnp.zeros((CONV_PAD, ct), F32)
    for r in range(0, seq, tc):
        a = a_ref[0, r:r + tc, :]
        g = g_ref[0, r:r + tc, :]
        pad_ref[CONV_PAD + r:CONV_PAD + r + tc, :] = a * jax.nn.sigmoid(g)


def _conv_fwd_kernel(a_ref, g_ref, dw_ref, dwb_ref, hc_ref, pad_ref, *, seq, tc):
    ct = pad_ref.shape[1]
    _glu_into(pad_ref, a_ref, g_ref, seq, tc)
    for r in range(0, seq, tc):
        acc = jnp.broadcast_to(dwb_ref[...], (tc, ct))
        for k in range(CONV_K):
            o = CONV_PAD + r - (CONV_K - 1) + k
            acc = acc + dw_ref[k:k + 1, :] * pad_ref[o:o + tc, :]
        hc_ref[0, r:r + tc, :] = acc


def _conv_fwd(p0, dw, dwb, bsz, seq, comm=None):
    ct = CONV_CT
    tc = min(CONV_TC, seq)
    p3 = p0.reshape(bsz, seq, P0_COLS)
    outs, carried = _call(
        functools.partial(_conv_fwd_kernel, seq=seq, tc=tc), comm,
        name="conv_fwd",
        grid=(MAIN // ct, bsz),
        in_specs=[pl.BlockSpec((1, seq, ct), lambda j, b: (b, 0, P0_A // ct + j)),
                  pl.BlockSpec((1, seq, ct), lambda j, b: (b, 0, P0_G // ct + j)),
                  pl.BlockSpec((CONV_K, ct), lambda j, b: (0, j)),
                  pl.BlockSpec((1, ct), lambda j, b: (0, j))],
        out_specs=[pl.BlockSpec((1, seq, ct), lambda j, b: (b, 0, j))],
        out_shape=[jax.ShapeDtypeStruct((bsz, seq, MAIN), F32)],
        scratch_shapes=[pltpu.VMEM((seq + CONV_PAD, ct), F32)],
        semantics=("parallel", "parallel"),
        args=[p3, p3, dw, dwb])
    return outs[0], carried


def _fold_rows(x):
    parts = [x[r:r + 8] for r in range(0, x.shape[0], 8)]
    while len(parts) > 1:
        parts = [a + b for a, b in zip(parts[::2], parts[1::2])]
    return parts[0]


def _conv_bwd_kernel(a_ref, g_ref, dhc_ref, dw_ref, da_ref, dg_ref, ddw_ref, pad_ref, dpad_ref, acc_ref,
                     *, seq, tc):
    ct = pad_ref.shape[1]
    b = pl.program_id(1)
    _glu_into(pad_ref, a_ref, g_ref, seq, tc)
    dpad_ref[seq:seq + CONV_PAD, :] = jnp.zeros((CONV_PAD, ct), F32)
    for r in range(0, seq, tc):
        dpad_ref[r:r + tc, :] = dhc_ref[0, r:r + tc, :]
    acc_ref[...] = jnp.zeros(acc_ref.shape, F32)
    for r in range(0, seq, tc):
        dh = dhc_ref[0, r:r + tc, :]
        dglu = jnp.zeros((tc, ct), F32)
        for k in range(CONV_K):
            o = r + (CONV_K - 1) - k
            dglu = dglu + dw_ref[k:k + 1, :] * dpad_ref[o:o + tc, :]
            o = CONV_PAD + r - (CONV_K - 1) + k
            prod = pad_ref[o:o + tc, :] * dh
            acc_ref[k] += _fold_rows(prod)
        acc_ref[CONV_K] += _fold_rows(dh)
        a = a_ref[0, r:r + tc, :]
        sg = jax.nn.sigmoid(g_ref[0, r:r + tc, :])
        da_ref[0, r:r + tc, :] = (dglu * sg).astype(da_ref.dtype)
        dg_ref[0, r:r + tc, :] = (dglu * a * sg * (1.0 - sg)).astype(dg_ref.dtype)
    tot = jnp.sum(acc_ref[...], axis=1)

    @pl.when(b == 0)
    def _():
        ddw_ref[...] = tot

    @pl.when(b > 0)
    def _():
        ddw_ref[...] += tot


def _conv_bwd(p0, dhc, dw, bsz, seq, comm=None):
    ct = CONV_CT
    tc = min(CONV_TC_BWD, seq)
    p3 = p0.reshape(bsz, seq, P0_COLS)
    blk = pl.BlockSpec((1, seq, ct), lambda j, b: (b, 0, j))
    return _call(
        functools.partial(_conv_bwd_kernel, seq=seq, tc=tc), comm,
        name="conv_bwd",
        grid=(MAIN // ct, bsz),
        in_specs=[pl.BlockSpec((1, seq, ct), lambda j, b: (b, 0, P0_A // ct + j)),
                  pl.BlockSpec((1, seq, ct), lambda j, b: (b, 0, P0_G // ct + j)),
                  blk,
                  pl.BlockSpec((CONV_K, ct), lambda j, b: (0, j))],
        out_specs=[blk, blk, pl.BlockSpec((CONV_K + 1, ct), lambda j, b: (0, j))],
        out_shape=[jax.ShapeDtypeStruct((bsz, seq, MAIN), BF16), jax.ShapeDtypeStruct((bsz, seq, MAIN), BF16),
                   jax.ShapeDtypeStruct((CONV_K + 1, MAIN), F32)],
        scratch_shapes=[pltpu.VMEM((seq + CONV_PAD, ct), F32), pltpu.VMEM((seq + CONV_PAD, ct), F32),
                        pltpu.VMEM((CONV_K + 1, 8, ct), F32)],
        semantics=("parallel", "arbitrary"),
        args=[p3, p3, dhc, dw])


def _ln_parts(x, lng, lnb):
    mu = jnp.mean(x, axis=-1, keepdims=True)
    xc = x - mu
    rstd = lax.rsqrt(jnp.mean(xc * xc, axis=-1, keepdims=True) + LN_EPS)
    xh = xc * rstd
    hl = xh * lng + lnb
    return rstd, xh, hl


GATE_CHUNK = 512
GATE_NZ = MIX // GATE_CHUNK


def _z_specs(tm, zb):
    return [pl.BlockSpec((tm, GATE_CHUNK), lambda i, j=j: (i, zb + j)) for j in range(GATE_NZ)]


def _gate_fwd_kernel(*refs, ln):
    main_ref, ymem_ref, *z_refs = refs[:2 + GATE_NZ]
    y_ref = refs[-1]
    x = main_ref[...]
    if ln:
        _, _, hl = _ln_parts(x, refs[-3][...], refs[-2][...])
        x = hl * jax.nn.sigmoid(hl)
    for j, z_ref in enumerate(z_refs):
        cols = slice(j * GATE_CHUNK, (j + 1) * GATE_CHUNK)
        z = z_ref[...]
        src = x[:, cols] if j < MAIN // GATE_CHUNK else ymem_ref[...]
        y_ref[:, cols] = (src * (z * jax.nn.sigmoid(z))).astype(y_ref.dtype)


def _gate_fwd(main, ymem, p, zb, ln_g=None, ln_b=None, *, name, tm=256):
    t = main.shape[0]
    tm = _tile(t, tm)
    ln = ln_g is not None
    in_specs = [pl.BlockSpec((tm, MAIN), lambda i: (i, 0)), pl.BlockSpec((tm, MEMW), lambda i: (i, 0))]
    in_specs += _z_specs(tm, zb)
    args = [main, ymem] + [p] * GATE_NZ
    if ln:
        in_specs += [pl.BlockSpec((1, MAIN), lambda i: (0, 0))] * 2
        args += [ln_g.reshape(1, MAIN), ln_b.reshape(1, MAIN)]
    return pl.pallas_call(
        functools.partial(_gate_fwd_kernel, ln=ln),
        name=name,
        grid=(t // tm,),
        in_specs=in_specs,
        out_specs=pl.BlockSpec((tm, MIX), lambda i: (i, 0)),
        out_shape=jax.ShapeDtypeStruct((t, MIX), BF16),
        compiler_params=_cparams(("parallel",)),
    )(*args)


def _gate_bwd_kernel(*refs, ln):
    dy_ref, main_ref, ymem_ref, *z_refs = refs[:3 + GATE_NZ]
    if ln:
        lng_ref, lnb_ref, dmain_ref, dymem_ref, dz_ref, dlng_ref, dlnb_ref = refs[3 + GATE_NZ:]
    else:
        dmain_ref, dymem_ref, dz_ref = refs[3 + GATE_NZ:]
    x = main_ref[...]
    if ln:
        lng = lng_ref[...]
        rstd, xh, hl = _ln_parts(x, lng, lnb_ref[...])
        sh = jax.nn.sigmoid(hl)
        ymain = hl * sh
    else:
        ymain = x
    for j, z_ref in enumerate(z_refs):
        cols = slice(j * GATE_CHUNK, (j + 1) * GATE_CHUNK)
        dy = dy_ref[:, cols]
        z = z_ref[...]
        sg = jax.nn.sigmoid(z)
        dsz = sg * (1.0 + z * (1.0 - sg))
        if j < MAIN // GATE_CHUNK:
            src = ymain[:, cols]
            dmain_ref[:, cols] = dy * (z * sg)
        else:
            src = ymem_ref[...]
            dymem_ref[...] = dy * (z * sg)
        dz_ref[:, cols] = (dy * src * dsz).astype(dz_ref.dtype)
    if not ln:
        return
    dym = dmain_ref[...]
    dhl = dym * (sh * (1.0 + hl * (1.0 - sh)))
    dxh = dhl * lng
    dmain_ref[...] = rstd * (dxh - jnp.mean(dxh, axis=-1, keepdims=True)
                             - xh * jnp.mean(dxh * xh, axis=-1, keepdims=True))
    dlng = jnp.sum(dhl * xh, axis=0, keepdims=True)
    dlnb = jnp.sum(dhl, axis=0, keepdims=True)

    @pl.when(pl.program_id(0) == 0)
    def _():
        dlng_ref[...] = dlng
        dlnb_ref[...] = dlnb

    @pl.when(pl.program_id(0) > 0)
    def _():
        dlng_ref[...] += dlng
        dlnb_ref[...] += dlnb


def _gate_bwd(dy, main, ymem, p, zb, ln_g=None, ln_b=None, *, name, tm=256):
    t = main.shape[0]
    tm = _tile(t, tm)
    ln = ln_g is not None
    r_main = pl.BlockSpec((tm, MAIN), lambda i: (i, 0))
    r_mem = pl.BlockSpec((tm, MEMW), lambda i: (i, 0))
    r_mix = pl.BlockSpec((tm, MIX), lambda i: (i, 0))
    vec = pl.BlockSpec((1, MAIN), lambda i: (0, 0))
    in_specs = [r_mix, r_main, r_mem] + _z_specs(tm, zb)
    args = [dy, main, ymem] + [p] * GATE_NZ
    out_specs = [r_main, r_mem, r_mix]
    out_shape = [jax.ShapeDtypeStruct((t, MAIN), F32), jax.ShapeDtypeStruct((t, MEMW), F32),
                 jax.ShapeDtypeStruct((t, MIX), BF16)]
    if ln:
        in_specs += [vec, vec]
        args += [ln_g.reshape(1, MAIN), ln_b.reshape(1, MAIN)]
        out_specs += [vec, vec]
        out_shape += [jax.ShapeDtypeStruct((1, MAIN), F32)] * 2
    return pl.pallas_call(
        functools.partial(_gate_bwd_kernel, ln=ln),
        name=name,
        grid=(t // tm,),
        in_specs=in_specs,
        out_specs=out_specs,
        out_shape=out_shape,
        compiler_params=_cparams(("arbitrary",)),
    )(*args)


def _dot_nt(a, b):
    return lax.dot_general(a, b, (((1,), (1,)), ((), ())), preferred_element_type=F32)


def _dot_tn(a, b):
    return lax.dot_general(a, b, (((0,), (0,)), ((), ())), preferred_element_type=F32)


def _dot(a, b):
    return jnp.dot(a, b, preferred_element_type=F32)


def _mem_probs(q, k):
    s = _dot_nt(q, k) * MEM_SCALE
    p = jnp.exp(s - jnp.max(s, axis=-1, keepdims=True))
    return p / jnp.sum(p, axis=-1, keepdims=True)


def _mem_fwd_kernel(q_ref, kv_ref, o_ref):
    for h in range(MEM_HEADS):
        c = slice(h * HEAD, (h + 1) * HEAD)
        cv = slice(MEMW + h * HEAD, MEMW + (h + 1) * HEAD)
        p = _mem_probs(q_ref[0, :, c].astype(BF16), kv_ref[0, :, c])
        o_ref[0, :, c] = _dot(p.astype(BF16), kv_ref[0, :, cv])


def _mem_fwd(p, kvm, col_block, bsz, seq, *, name, tq=512):
    tq = _tile(seq, tq)
    p3 = p.reshape(bsz, seq, p.shape[1])
    mlen = kvm.shape[1]
    return pl.pallas_call(
        functools.partial(_mem_fwd_kernel),
        name=name,
        grid=(bsz, seq // tq),
        in_specs=[pl.BlockSpec((1, tq, MEMW), lambda b, i: (b, i, col_block)),
                  pl.BlockSpec((1, mlen, 2 * MEMW), lambda b, i: (b, 0, 0))],
        out_specs=pl.BlockSpec((1, tq, MEMW), lambda b, i: (b, i, 0)),
        out_shape=jax.ShapeDtypeStruct((bsz, seq, MEMW), F32),
        compiler_params=_cparams(("parallel", "parallel")),
    )(p3, kvm)


def _mem_bwd_kernel(q_ref, kv_ref, do_ref, dq_ref, dkv_ref):
    @pl.when(pl.program_id(1) == 0)
    def _():
        dkv_ref[...] = jnp.zeros(dkv_ref.shape, F32)

    for h in range(MEM_HEADS):
        c = slice(h * HEAD, (h + 1) * HEAD)
        cv = slice(MEMW + h * HEAD, MEMW + (h + 1) * HEAD)
        q = q_ref[0, :, c].astype(BF16)
        k = kv_ref[0, :, c]
        v = kv_ref[0, :, cv]
        do = do_ref[0, :, c].astype(BF16)
        p = _mem_probs(q, k)
        dp = _dot_nt(do, v)
        ds = (p * (dp - jnp.sum(p * dp, axis=-1, keepdims=True)) * MEM_SCALE).astype(BF16)
        dq_ref[0, :, c] = _dot(ds, k).astype(dq_ref.dtype)
        dkv_ref[0, :, c] += _dot_tn(ds, q)
        dkv_ref[0, :, cv] += _dot_tn(p.astype(BF16), do)


def _mem_bwd(p, kvm, dymem, col_block, bsz, seq, *, name, tq=512):
    tq = _tile(seq, tq)
    p3 = p.reshape(bsz, seq, p.shape[1])
    mlen = kvm.shape[1]
    return pl.pallas_call(
        functools.partial(_mem_bwd_kernel),
        name=name,
        grid=(bsz, seq // tq),
        in_specs=[pl.BlockSpec((1, tq, MEMW), lambda b, i: (b, i, col_block)),
                  pl.BlockSpec((1, mlen, 2 * MEMW), lambda b, i: (b, 0, 0)),
                  pl.BlockSpec((1, tq, MEMW), lambda b, i: (b, i, 0))],
        out_specs=[pl.BlockSpec((1, tq, MEMW), lambda b, i: (b, i, 0)),
                   pl.BlockSpec((1, mlen, 2 * MEMW), lambda b, i: (b, 0, 0))],
        out_shape=[jax.ShapeDtypeStruct((bsz, seq, MEMW), BF16),
                   jax.ShapeDtypeStruct((bsz, mlen, 2 * MEMW), F32)],
        compiler_params=_cparams(("parallel", "arbitrary")),
    )(p3, kvm, dymem.reshape(bsz, seq, MEMW))


def _swap32(x):
    lane = lax.broadcasted_iota(jnp.int32, x.shape, 1)
    return jnp.where(lane < 32, pltpu.roll(x, 96, 1), pltpu.roll(x, 32, 1))


def _rope(x, cs, sn):
    return x * cs + _swap32(x) * sn


def _rope_t(d, cs, sn):
    return d * cs + _swap32(d * sn)


UP_HEADS = 2


def _q_up_kernel(a_ref, b_ref, cs_ref, sn_ref, o_ref):
    acc = _dot(a_ref[...], b_ref[...])
    cs = cs_ref[...]
    sn = sn_ref[...]
    for h in range(UP_HEADS):
        c0 = slice(h * QK_PAD, h * QK_PAD + HEAD)
        c1 = slice(h * QK_PAD + HEAD, (h + 1) * QK_PAD)
        o_ref[:, c0] = acc[:, c0].astype(o_ref.dtype)
        o_ref[:, c1] = _rope(acc[:, c1], cs, sn).astype(o_ref.dtype)


def _q_up(cqn, w_uq, cs, sn, *, tm=512):
    t, kd = cqn.shape
    tm = _tile(t, tm)
    tn = UP_HEADS * QK_PAD
    tab = pl.BlockSpec((tm, 128), lambda i, j: (i, 0))
    return pl.pallas_call(
        functools.partial(_q_up_kernel),
        name="q_up_rope",
        grid=(t // tm, MLA_HEADS // UP_HEADS),
        in_specs=[pl.BlockSpec((tm, kd), lambda i, j: (i, 0)), pl.BlockSpec((kd, tn), lambda i, j: (0, j)), tab, tab],
        out_specs=pl.BlockSpec((tm, tn), lambda i, j: (i, j)),
        out_shape=jax.ShapeDtypeStruct((t, MLA_HEADS * QK_PAD), BF16),
        compiler_params=_cparams(("parallel", "parallel")),
    )(cqn, w_uq, cs, sn)


def _kv_up_kernel(a_ref, b_ref, kr_ref, cs_ref, sn_ref, k_ref, v_ref):
    acc = _dot(a_ref[...], b_ref[...])
    krr = _rope(kr_ref[...], cs_ref[...], sn_ref[...]).astype(k_ref.dtype)
    for h in range(UP_HEADS):
        k_ref[:, h * QK_PAD:h * QK_PAD + HEAD] = acc[:, h * 2 * HEAD:h * 2 * HEAD + HEAD].astype(k_ref.dtype)
        k_ref[:, h * QK_PAD + HEAD:(h + 1) * QK_PAD] = krr
        v_ref[:, h * HEAD:(h + 1) * HEAD] = acc[:, h * 2 * HEAD + HEAD:(h + 1) * 2 * HEAD].astype(v_ref.dtype)


def _kv_up(ckvn, w_ukv, p1, cs, sn, *, tm=512):
    t, kd = ckvn.shape
    tm = _tile(t, tm)
    tab = pl.BlockSpec((tm, 128), lambda i, j: (i, 0))
    return pl.pallas_call(
        functools.partial(_kv_up_kernel),
        name="kv_up_pack",
        grid=(t // tm, MLA_HEADS // UP_HEADS),
        in_specs=[pl.BlockSpec((tm, kd), lambda i, j: (i, 0)),
                  pl.BlockSpec((kd, UP_HEADS * 2 * HEAD), lambda i, j: (0, j)),
                  pl.BlockSpec((tm, 128), lambda i, j: (i, P1_KR // 128)), tab, tab],
        out_specs=[pl.BlockSpec((tm, UP_HEADS * QK_PAD), lambda i, j: (i, j)),
                   pl.BlockSpec((tm, UP_HEADS * HEAD), lambda i, j: (i, j))],
        out_shape=[jax.ShapeDtypeStruct((t, MLA_HEADS * QK_PAD), BF16), jax.ShapeDtypeStruct((t, MAIN), BF16)],
        compiler_params=_cparams(("parallel", "parallel")),
    )(ckvn, w_ukv, p1, cs, sn)


def _kr_bwd_kernel(d_ref, cs_ref, sn_ref, o_ref):
    acc = d_ref[:, :HEAD]
    for h in range(1, MLA_HEADS):
        acc = acc + d_ref[:, h * HEAD:(h + 1) * HEAD]
    o_ref[...] = _rope_t(acc, cs_ref[...], sn_ref[...]).astype(o_ref.dtype)


def _kr_bwd(dkrr, cs, sn, *, tm=512):
    t = dkrr.shape[0]
    tm = _tile(t, tm)
    tab = pl.BlockSpec((tm, 128), lambda i: (i, 0))
    return pl.pallas_call(
        functools.partial(_kr_bwd_kernel),
        name="kr_bwd",
        grid=(t // tm,),
        in_specs=[pl.BlockSpec((tm, MAIN), lambda i: (i, 0)), tab, tab],
        out_specs=tab,
        out_shape=jax.ShapeDtypeStruct((t, 128), BF16),
        compiler_params=_cparams(("parallel",)),
    )(dkrr, cs, sn)


ATT_T = 256


def _causal(s, t):
    row = lax.broadcasted_iota(jnp.int32, (t, t), 0)
    col = lax.broadcasted_iota(jnp.int32, (t, t), 1)
    return jnp.where(col <= row, s, NEG)


def _attn_fwd_kernel(q_ref, k_ref, v_ref, o_ref, lse_ref, *, seq, t):
    for i in range(seq // t):
        own = slice(i * t, (i + 1) * t)
        q = q_ref[0, own, :]
        sd = _causal(_dot_nt(q, k_ref[0, own, :]) * MLA_SCALE, t)
        m = jnp.max(sd, axis=-1, keepdims=True)
        if i:
            so = _dot_nt(q, k_ref[0, :i * t, :]) * MLA_SCALE
            m = jnp.maximum(m, jnp.max(so, axis=-1, keepdims=True))
        pd = jnp.exp(sd - m)
        l = jnp.sum(pd, axis=-1, keepdims=True)
        acc = _dot(pd.astype(BF16), v_ref[0, own, :])
        if i:
            po = jnp.exp(so - m)
            l = l + jnp.sum(po, axis=-1, keepdims=True)
            acc = acc + _dot(po.astype(BF16), v_ref[0, :i * t, :])
        o_ref[0, own, :] = acc / l
        lse_ref[0, 0, own, :] = m + jnp.log(l)


def _attn_fwd(qb, kb, vb, bsz, seq):
    t = min(ATT_T, seq)
    q3 = qb.reshape(bsz, seq, MLA_HEADS * QK_PAD)
    k3 = kb.reshape(bsz, seq, MLA_HEADS * QK_PAD)
    v3 = vb.reshape(bsz, seq, MAIN)
    qk = pl.BlockSpec((1, seq, QK_PAD), lambda b, h: (b, 0, h))
    vv = pl.BlockSpec((1, seq, HEAD), lambda b, h: (b, 0, h))
    return pl.pallas_call(
        functools.partial(_attn_fwd_kernel, seq=seq, t=t),
        name="attn_fwd",
        grid=(bsz, MLA_HEADS),
        in_specs=[qk, qk, vv],
        out_specs=[vv, pl.BlockSpec((1, 1, seq, 1), lambda b, h: (b, h, 0, 0))],
        out_shape=[jax.ShapeDtypeStruct((bsz, seq, MAIN), F32),
                   jax.ShapeDtypeStruct((bsz, MLA_HEADS, seq, 1), F32)],
        compiler_params=_cparams(("parallel", "parallel")),
    )(q3, k3, v3)


def _attn_bwd_kernel(q_ref, k_ref, v_ref, o_ref, do_ref, lse_ref, cs_ref, sn_ref, dq_ref, dkv_ref, dkr_ref,
                     delta_ref, dqacc_ref, *, seq, t):
    for r in range(0, seq, t):
        rows = slice(r, r + t)
        delta_ref[rows, :] = jnp.sum(do_ref[0, rows, :] * o_ref[0, rows, :], axis=-1, keepdims=True)
    dqacc_ref[...] = jnp.zeros(dqacc_ref.shape, F32)

    def piece(rows, k, v, masked):
        q = q_ref[0, rows, :]
        do = do_ref[0, rows, :].astype(BF16)
        s = _dot_nt(q, k) * MLA_SCALE
        if masked:
            s = _causal(s, t)
        p = jnp.exp(s - lse_ref[0, 0, rows, :])
        dp = _dot_nt(do, v)
        ds = (p * (dp - delta_ref[rows, :]) * MLA_SCALE).astype(BF16)
        dqacc_ref[rows, :] += _dot(ds, k)
        return _dot_tn(ds, q), _dot_tn(p.astype(BF16), do)

    for j in range(seq // t):
        own = slice(j * t, (j + 1) * t)
        k = k_ref[0, own, :]
        v = v_ref[0, own, :]
        dk, dv = piece(own, k, v, True)
        if (j + 1) * t < seq:
            dk2, dv2 = piece(slice((j + 1) * t, seq), k, v, False)
            dk, dv = dk + dk2, dv + dv2
        dkv_ref[0, own, :HEAD] = dk[:, :HEAD].astype(dkv_ref.dtype)
        dkv_ref[0, own, HEAD:] = dv.astype(dkv_ref.dtype)
        dkr_ref[0, own, :] = dk[:, HEAD:]
    for r in range(0, seq, t):
        rows = slice(r, r + t)
        dq = dqacc_ref[rows, :]
        dq_ref[0, rows, :HEAD] = dq[:, :HEAD].astype(dq_ref.dtype)
        dq_ref[0, rows, HEAD:] = _rope_t(dq[:, HEAD:], cs_ref[0, rows, :], sn_ref[0, rows, :]).astype(dq_ref.dtype)


def _attn_bwd(qb, kb, vb, o, do, lse, cs, sn, bsz, seq):
    t = min(ATT_T, seq)
    q3 = qb.reshape(bsz, seq, MLA_HEADS * QK_PAD)
    k3 = kb.reshape(bsz, seq, MLA_HEADS * QK_PAD)
    v3 = vb.reshape(bsz, seq, MAIN)
    qk = pl.BlockSpec((1, seq, QK_PAD), lambda b, h: (b, 0, h))
    vv = pl.BlockSpec((1, seq, HEAD), lambda b, h: (b, 0, h))
    tab = pl.BlockSpec((1, seq, 128), lambda b, h: (b, 0, 0))
    dq, dkv, dkr = pl.pallas_call(
        functools.partial(_attn_bwd_kernel, seq=seq, t=t),
        name="attn_bwd",
        grid=(bsz, MLA_HEADS),
        in_specs=[qk, qk, vv, vv, vv, pl.BlockSpec((1, 1, seq, 1), lambda b, h: (b, h, 0, 0)), tab, tab],
        out_specs=[qk, qk, vv],
        out_shape=[jax.ShapeDtypeStruct((bsz, seq, MLA_HEADS * QK_PAD), BF16),
                   jax.ShapeDtypeStruct((bsz, seq, MLA_HEADS * 2 * HEAD), BF16),
                   jax.ShapeDtypeStruct((bsz, seq, MAIN), F32)],
        scratch_shapes=[pltpu.VMEM((seq, 1), F32), pltpu.VMEM((seq, QK_PAD), F32)],
        compiler_params=_cparams(("parallel", "parallel")),
    )(q3, k3, v3, o, do.reshape(bsz, seq, MAIN), lse, cs.reshape(bsz, seq, 128), sn.reshape(bsz, seq, 128))
    n = bsz * seq
    return dq.reshape(n, -1), dkv.reshape(n, -1), dkr.reshape(n, -1)


def _adamw_kernel(w_ref, g_ref, m_ref, v_ref, d_ref, nm_ref, nv_ref):
    g = g_ref[...]
    m = ADAM_B1 * m_ref[...] + (1.0 - ADAM_B1) * g
    v = ADAM_B2 * v_ref[...] + (1.0 - ADAM_B2) * (g * g)
    m_hat = m / (1.0 - ADAM_B1 ** ADAM_STEP)
    v_hat = v / (1.0 - ADAM_B2 ** ADAM_STEP)
    d_ref[...] = -ADAM_LR * (m_hat / (jnp.sqrt(v_hat) + ADAM_EPS) + ADAM_WD * w_ref[...])
    nm_ref[...] = m
    nv_ref[...] = v


def _adamw(w, g, m, v, *, name):
    shape = w.shape
    c = shape[-1]
    r = w.size // c
    tr = r
    for cand in (512, 256, 128, 64, 32, 16, 8):
        if r % cand == 0 and cand * c * 4 <= 2 * 1024 * 1024:
            tr = cand
            break
    blk = pl.BlockSpec((tr, c), lambda i: (i, 0))
    outs = pl.pallas_call(
        functools.partial(_adamw_kernel),
        name=name,
        grid=(r // tr,),
        in_specs=[blk] * 4,
        out_specs=[blk] * 3,
        out_shape=[jax.ShapeDtypeStruct((r, c), F32)] * 3,
        compiler_params=_cparams(("parallel",)),
    )(w.reshape(r, c), g.reshape(r, c), m.reshape(r, c), v.reshape(r, c))
    return tuple(o.reshape(shape) for o in outs)


def _place():
    return lax.axis_index("x"), lax.axis_index("y"), lax.axis_index("c")


def _other_chips(x, y):
    return [(1 - x, y), (x, 1 - y), (1 - x, 1 - y)]


class _Geom:
    def __init__(self, kind, rows, cols):
        self.kind, self.rows, self.cols, self.hr = kind, rows, cols, rows // 2
        self.full_shape = {"rows": (4 * rows, cols), "cols": (rows, 4 * cols), "chips": (4, rows, cols)}[kind]
        self.nt = self.hr // ADD_TILE

    def view(self, ref, s, h):
        if self.kind == "rows":
            return ref.at[pl.ds(s * self.rows + h * self.hr, self.hr), :]
        if self.kind == "cols":
            return ref.at[pl.ds(h * self.hr, self.hr), pl.ds(s * self.cols, self.cols)]
        return ref.at[s, pl.ds(h * self.hr, self.hr), :]

    def shard_half(self, ref, h):
        return ref.at[pl.ds(h * self.hr, self.hr), :]

    def rows_view(self, ref, s, r0, nr):
        if self.kind == "rows":
            return ref.at[pl.ds(s * self.rows + r0, nr), :]
        if self.kind == "cols":
            return ref.at[pl.ds(r0, nr), pl.ds(s * self.cols, self.cols)]
        return ref.at[s, pl.ds(r0, nr), :]

    def tile_spec(self, chip_half_of):
        if self.kind == "rows":
            def index(*a):
                s, h, i = chip_half_of(*a)
                return (s * (self.rows // ADD_TILE) + h * self.nt + i, 0)
            return pl.BlockSpec((ADD_TILE, self.cols), index)
        if self.kind == "cols":
            def index(*a):
                s, h, i = chip_half_of(*a)
                return (h * self.nt + i, s)
            return pl.BlockSpec((ADD_TILE, self.cols), index)

        def index(*a):
            s, h, i = chip_half_of(*a)
            return (s, h * self.nt + i, 0)
        return pl.BlockSpec((None, ADD_TILE, self.cols), index)


GEOMS = (("w_mem_kv0", _Geom("rows", 256, 1024)), ("w_mem_kv1", _Geom("rows", 256, 1024)),
         ("w_out0", _Geom("rows", 512, 1024)), ("w_out1", _Geom("rows", 512, 1024)),
         ("conv_w_in", _Geom("cols", 1024, 1408)), ("mla_w_ukv", _Geom("cols", 256, 768)),
         ("mla_w_in", _Geom("chips", 1024, 848)), ("mla_w_uq", _Geom("chips", 512, 576)))
N_BIG = len(GEOMS)


def _remote(k, src_ref, dst_ref, to, ssem, rsem):
    return pltpu.make_async_remote_copy(src_ref=src_ref, dst_ref=dst_ref, send_sem=ssem.at[k], recv_sem=rsem.at[k],
                                        device_id=to, device_id_type=MESH)


GROUP_A = ("w_mem_kv0", "w_out0", "conv_w_in")
GROUP_B = ("w_mem_kv1", "w_out1", "mla_w_ukv", "mla_w_in", "mla_w_uq")
GATHER_0 = ("conv_w_in",)
GATHER_1 = ("w_mem_kv0", "w_out0", "mla_w_ukv", "mla_w_uq")
GATHER_2 = ("w_mem_kv1", "mla_w_in")
GATHER_3 = ("w_out1",)
GEOM = dict(GEOMS)


def _gather_over_ici(geoms, srcs, outs, ssem, rsem, act, base=0):
    x, y, c = _place()
    s = 2 * x + y
    n = len(geoms)
    locals_ = []
    for w, g in enumerate(geoms):
        nr = g.rows // LOCAL_CHUNKS
        for q in range(LOCAL_CHUNKS):
            locals_.append(pltpu.make_async_copy(srcs[w].at[pl.ds(q * nr, nr), :], g.rows_view(outs[w], s, q * nr, nr),
                                                 ssem.at[base + 3 * n + LOCAL_CHUNKS * w + q]))
    sends = [_remote(base + 3 * w + j, g.shard_half(srcs[w], c), g.view(outs[w], s, c), (*chip, c), ssem, rsem)
             for w, g in enumerate(geoms) for j, chip in enumerate(_other_chips(x, y))]
    if act == "start":
        for cp in sends + locals_:
            cp.start()
        return
    for w, g in enumerate(geoms):
        for j, chip in enumerate(_other_chips(x, y)):
            blk = g.view(outs[w], 2 * chip[0] + chip[1], c)
            _remote(base + 3 * w + j, blk, blk, (*chip, c), ssem, rsem).wait_recv()
    for cp in sends:
        cp.wait_send()
    for cp in locals_:
        cp.wait()


def _forward_to_sibling(geoms, fulls, ssem, rsem, act, base=0):
    x, y, c = _place()
    for w, g in enumerate(geoms):
        for j, chip in enumerate(_other_chips(x, y)):
            s = 2 * chip[0] + chip[1]
            mine, theirs = g.view(fulls[w], s, c), g.view(fulls[w], s, 1 - c)
            if act == "start":
                _remote(base + 3 * w + j, mine, mine, (x, y, 1 - c), ssem, rsem).start()
            else:
                _remote(base + 3 * w + j, theirs, theirs, (x, y, 1 - c), ssem, rsem).wait_recv()
                _remote(base + 3 * w + j, mine, mine, (x, y, 1 - c), ssem, rsem).wait_send()


def _gather_comm(forward_names, fulls, ici_names, shards):
    fwd = [GEOM[nm] for nm in forward_names]
    ici = [GEOM[nm] for nm in ici_names]
    nf, base = len(fwd), 3 * len(fwd)

    def run(act):
        def go(i, o, ss, rs):
            _forward_to_sibling(fwd, o[:nf], ss, rs, act)
            _gather_over_ici(ici, i[nf:], o[nf:], ss, rs, act, base)
        return go

    return _Comm(list(fulls) + list(shards),
                 [jax.ShapeDtypeStruct(f.shape, f.dtype) for f in fulls]
                 + [jax.ShapeDtypeStruct(g.full_shape, BF16) for g in ici],
                 base + (3 + LOCAL_CHUNKS) * len(ici), run("start"), run("wait"), aliases={w: w for w in range(nf)})


def _allgather_kernel(*refs, geoms):
    n = len(geoms)
    srcs, outs, (ssem, rsem, fsem, gsem) = refs[:n], refs[n:2 * n], refs[2 * n:]
    _gather_over_ici(geoms, srcs, outs, ssem, rsem, "start")
    _gather_over_ici(geoms, srcs, outs, ssem, rsem, "wait")
    _forward_to_sibling(geoms, outs, fsem, gsem, "start")
    _forward_to_sibling(geoms, outs, fsem, gsem, "wait")


def _allgather_weights(names, shards):
    geoms = [GEOM[nm] for nm in names]
    n = len(geoms)
    return pl.pallas_call(
        functools.partial(_allgather_kernel, geoms=geoms),
        name="allgather_weights",
        in_specs=[HBM] * n,
        out_specs=[HBM] * n,
        out_shape=[jax.ShapeDtypeStruct(g.full_shape, BF16) for g in geoms],
        scratch_shapes=[pltpu.SemaphoreType.DMA(((3 + LOCAL_CHUNKS) * n,)), pltpu.SemaphoreType.DMA((3 * n,)),
                        pltpu.SemaphoreType.DMA((3 * n,)), pltpu.SemaphoreType.DMA((3 * n,))],
    )(*shards)


def _swap_halves_kernel(*refs, geoms):
    n = len(geoms)
    srcs, dsts, (ssem, rsem) = refs[:n], refs[n:2 * n], refs[2 * n:]
    x, y, c = _place()
    cps = []
    for w, g in enumerate(geoms):
        for s in range(4):
            cps.append(_remote(4 * w + s, g.view(srcs[w], s, 1 - c), dsts[w].at[s], (x, y, 1 - c), ssem, rsem))
    for cp in cps:
        cp.start()
    for cp in cps:
        cp.wait()


def _swap_halves(names, gb, *, name):
    geoms = [GEOM[nm] for nm in names]
    n = len(geoms)
    return pl.pallas_call(
        functools.partial(_swap_halves_kernel, geoms=geoms),
        name=name,
        in_specs=[HBM] * n,
        out_specs=[HBM] * n,
        out_shape=[jax.ShapeDtypeStruct((4, g.hr, g.cols), BF16) for g in geoms],
        scratch_shapes=[pltpu.SemaphoreType.DMA((4 * n,)), pltpu.SemaphoreType.DMA((4 * n,))],
    )(*gb)


def _exchange_with_chips(srcs, dsts, ssem, rsem, act):
    x, y, c = _place()
    for w in range(len(srcs)):
        for j, chip in enumerate(_other_chips(x, y)):
            cp = _remote(3 * w + j, srcs[w].at[2 * chip[0] + chip[1]], dsts[w].at[j], (*chip, c), ssem, rsem)
            if act == "start":
                cp.start()
            else:
                cp.wait()


def _exchange_comm(pairs):
    return _Comm(pairs, [jax.ShapeDtypeStruct((3,) + p.shape[1:], p.dtype) for p in pairs], 3 * len(pairs),
                 lambda i, o, ss, rs: _exchange_with_chips(i, o, ss, rs, "start"),
                 lambda i, o, ss, rs: _exchange_with_chips(i, o, ss, rs, "wait"))


def _share_kernel(*refs):
    srcs, dsts, (ssem, rsem) = refs[:N_BIG], refs[N_BIG:2 * N_BIG], refs[2 * N_BIG:]
    x, y, c = _place()
    cps = [_remote(w, srcs[w], dsts[w], (x, y, 1 - c), ssem, rsem) for w in range(N_BIG)]
    for cp in cps:
        cp.start()
    for cp in cps:
        cp.wait()


def _share_with_sibling(halves):
    return pl.pallas_call(
        functools.partial(_share_kernel),
        name="rs_share_halves",
        in_specs=[HBM] * N_BIG,
        out_specs=[HBM] * N_BIG,
        out_shape=[jax.ShapeDtypeStruct(h.shape, h.dtype) for h in halves],
        scratch_shapes=[pltpu.SemaphoreType.DMA((N_BIG,)), pltpu.SemaphoreType.DMA((N_BIG,))],
    )(*halves)


def _gather_sum_kernel(src, gat, tot, ssem, rsem):
    x, y, c = _place()
    me = 4 * x + 2 * y + c
    gat[me] = src[...]
    flips = [(dx, dy, dc) for dx in (0, 1) for dy in (0, 1) for dc in (0, 1)][1:]
    cps = []
    for k, (dx, dy, dc) in enumerate(flips):
        peer = (1 - x if dx else x, 1 - y if dy else y, 1 - c if dc else c)
        cp = pltpu.make_async_remote_copy(src_ref=src, dst_ref=gat.at[me], send_sem=ssem.at[k], recv_sem=rsem.at[k],
                                          device_id=peer, device_id_type=MESH)
        cp.start()
        cps.append((cp, 4 * peer[0] + 2 * peer[1] + peer[2], peer))
    for k, (cp, idx, peer) in enumerate(cps):
        pltpu.make_async_remote_copy(src_ref=src, dst_ref=gat.at[idx], send_sem=ssem.at[k], recv_sem=rsem.at[k],
                                     device_id=peer, device_id_type=MESH).wait_recv()
    for cp, _, _ in cps:
        cp.wait_send()
    acc = gat[0]
    for d in range(1, 8):
        acc = acc + gat[d]
    tot[...] = acc


def _gather_sum_small(a, *, name):
    vm = pl.BlockSpec(memory_space=pltpu.VMEM)
    return pl.pallas_call(
        functools.partial(_gather_sum_kernel),
        name=name,
        in_specs=[vm],
        out_specs=[vm, vm],
        out_shape=[jax.ShapeDtypeStruct((8,) + a.shape, a.dtype), jax.ShapeDtypeStruct(a.shape, a.dtype)],
        scratch_shapes=[pltpu.SemaphoreType.DMA((7,)), pltpu.SemaphoreType.DMA((7,))],
    )(a)


def _add_pairs_kernel(c_ref, g_ref, r_ref, o_ref):
    o_ref[...] = (g_ref[...] + r_ref[...].astype(F32)).astype(o_ref.dtype)


def _add_pairs(geom, core, g, recv, *, name):
    half = pl.BlockSpec((None, ADD_TILE, geom.cols), lambda s, i, cr: (s, i, 0))
    return pl.pallas_call(
        functools.partial(_add_pairs_kernel),
        name=name,
        grid_spec=pltpu.PrefetchScalarGridSpec(
            num_scalar_prefetch=1,
            grid=(4, geom.nt),
            in_specs=[geom.tile_spec(lambda s, i, cr: (s, cr[0], i)), half],
            out_specs=half,
        ),
        out_shape=jax.ShapeDtypeStruct(recv.shape, BF16),
        compiler_params=_cparams(("parallel", "parallel")),
    )(core, g, recv)


def _add_final_kernel(sc_ref, g_ref, r_ref, e_ref, o_ref):
    acc = g_ref[...] + r_ref[...].astype(F32)
    for j in range(3):
        acc = acc + e_ref[j].astype(F32)
    o_ref[...] = acc


def _add_final(geom, chip_core, g, recv, exch, *, name):
    return pl.pallas_call(
        functools.partial(_add_final_kernel),
        name=name,
        grid_spec=pltpu.PrefetchScalarGridSpec(
            num_scalar_prefetch=1,
            grid=(geom.nt,),
            in_specs=[geom.tile_spec(lambda i, sc: (sc[0], sc[1], i)),
                      pl.BlockSpec((None, ADD_TILE, geom.cols), lambda i, sc: (sc[0], i, 0)),
                      pl.BlockSpec((3, ADD_TILE, geom.cols), lambda i, sc: (0, i, 0))],
            out_specs=pl.BlockSpec((ADD_TILE, geom.cols), lambda i, sc: (i, 0)),
        ),
        out_shape=jax.ShapeDtypeStruct((geom.hr, geom.cols), F32),
        compiler_params=_cparams(("parallel",)),
    )(chip_core, g, recv, exch)


def _chip_major(w):
    return w.reshape(w.shape[0], 4, w.shape[1] // 4).transpose(1, 0, 2)


def _from_chip_major(w):
    return w.transpose(1, 0, 2).reshape(w.shape[1], 4 * w.shape[2])


def _mla_in_to_internal(w):
    return jnp.concatenate([w[:, 1344:], w[:, :512], w[:, 832:1344], w[:, 512:768], w[:, 768:832],
                            jnp.zeros((w.shape[0], 64), w.dtype)], axis=1)


def _mla_in_from_internal(w):
    return jnp.concatenate([w[:, P1_CQ:P1_QM], w[:, P1_CKV:P1_KR], w[:, P1_KR:P1_KR + 64], w[:, P1_QM:P1_CKV],
                            w[:, :MIX]], axis=1)


def _uq_to_internal(w):
    w = w.reshape(w.shape[0], MLA_HEADS, HEAD + ROPE)
    return jnp.pad(w, ((0, 0), (0, 0), (0, QK_PAD - HEAD - ROPE))).reshape(w.shape[0], MLA_HEADS * QK_PAD)


def _uq_from_internal(w):
    return w.reshape(w.shape[0], MLA_HEADS, QK_PAD)[:, :, :HEAD + ROPE].reshape(w.shape[0], MLA_HEADS * (HEAD + ROPE))


def _rows128(a, rows):
    flat = a.reshape(-1)
    return jnp.pad(flat, (0, rows * 128 - flat.shape[0])).reshape(rows, 128)


def kernel(x, mem, positions, norm_g, mem_norm_g, w_mem_kv, w_out, conv_w_in, conv_dw, conv_dw_b, conv_ln_g, conv_ln_b, mla_w_in, mla_q_norm_g, mla_w_uq, mla_kv_norm_g, mla_w_ukv, final_norm_g, loss_target, m_norm_g, m_mem_norm_g, m_w_mem_kv, m_w_out, m_conv_w_in, m_conv_dw, m_conv_dw_b, m_conv_ln_g, m_conv_ln_b, m_mla_w_in, m_mla_q_norm_g, m_mla_w_uq, m_mla_kv_norm_g, m_mla_w_ukv, m_final_norm_g, v_norm_g, v_mem_norm_g, v_w_mem_kv, v_w_out, v_conv_w_in, v_conv_dw, v_conv_dw_b, v_conv_ln_g, v_conv_ln_b, v_mla_w_in, v_mla_q_norm_g, v_mla_w_uq, v_mla_kv_norm_g, v_mla_w_ukv, v_final_norm_g):
    bsz, seq, d = x.shape
    n = bsz * seq
    mlen = mem.shape[1]
    ax, ay, ac = _place()
    chip = 2 * ax + ay

    shards = dict(w_mem_kv0=w_mem_kv[0], w_mem_kv1=w_mem_kv[1], w_out0=w_out[0], w_out1=w_out[1],
                  conv_w_in=conv_w_in[0], mla_w_ukv=mla_w_ukv[0], mla_w_in=mla_w_in[0], mla_w_uq=mla_w_uq[0])
    bshard = {nm: sh.astype(BF16) for nm, sh in shards.items()}
    wf = dict(zip(GATHER_0, _allgather_weights(GATHER_0, [bshard[nm] for nm in GATHER_0])))
    w_conv_in = wf["conv_w_in"]
    w_conv_in_t = w_conv_in.T

    small_in = jnp.concatenate([_rows128(conv_dw[0], 96), _rows128(mla_q_norm_g, 8), _rows128(mla_kv_norm_g, 8)],
                               axis=0)
    small_all, _ = _gather_sum_small(small_in, name="gather_small_params")
    small_all = small_all[0::2]
    dw_full = small_all[:, :93].reshape(4, -1)[:, :CONV_K * 384].reshape(4, CONV_K, 384)
    dw_full = dw_full.transpose(1, 0, 2).reshape(CONV_K, MAIN)
    qg_full = small_all[:, 96].reshape(Q_RANK)
    kvg_full = small_all[:, 104, :64].reshape(KV_RANK)

    inv_freq = 1.0 / (ROPE_THETA ** (jnp.arange(0, ROPE, 2, dtype=F32) / ROPE))
    ang = positions.astype(F32).reshape(n, 1) * inv_freq
    cos, sin, zer = jnp.cos(ang), jnp.sin(ang), jnp.zeros((n, 64), F32)
    rope_c = jnp.concatenate([cos, cos, zer], axis=1)
    rope_s = jnp.concatenate([-sin, sin, zer], axis=1)

    x2 = x.reshape(n, d)
    mem2 = mem.reshape(bsz * mlen, d)
    tgt2 = loss_target.reshape(n, d)

    memn = [_rms_fwd(mem2, mem_norm_g[i], name=f"mem_norm{i}") for i in range(2)]
    u0 = _rms_fwd(x2, norm_g[0], name="norm0")
    p0, landed1 = _matmul(u0, w_conv_in, name="conv_in_proj",
                          comm=_gather_comm((), (), GATHER_1, [bshard[nm] for nm in GATHER_1]))
    hc, carried = _conv_fwd(p0, dw_full, conv_dw_b, bsz, seq,
                            comm=_gather_comm(GATHER_1, landed1, GATHER_2, [bshard[nm] for nm in GATHER_2]))
    hc = hc.reshape(n, MAIN)
    wf.update(zip(GATHER_1, carried[:len(GATHER_1)]))
    kvm = [_matmul(memn[0], wf["w_mem_kv0"], out_dtype=BF16, name="mem_kv0").reshape(bsz, mlen, 2 * MEMW)]
    ymem0 = _mem_fwd(p0, kvm[0], P0_QM // MEMW, bsz, seq, name="mem_attn0").reshape(n, MEMW)
    y0 = _gate_fwd(hc, ymem0, p0, P0_Z // GATE_CHUNK, conv_ln_g, conv_ln_b, name="gate0")
    h1, gathered2 = _matmul(y0, wf["w_out0"], x2, name="out_proj0",
                            comm=_gather_comm(GATHER_2, carried[len(GATHER_1):], (), ()))
    wf.update(zip(GATHER_2, gathered2))
    w_mla_in = _mla_in_to_internal(_from_chip_major(wf["mla_w_in"]))
    w_uq = _uq_to_internal(_from_chip_major(wf["mla_w_uq"]))
    w_ukv = wf["mla_w_ukv"]
    w_mla_in_t, w_uq_t, w_ukv_t = w_mla_in.T, w_uq.T, w_ukv.T

    u1 = _rms_fwd(h1, norm_g[1], name="norm1")
    p1, landed3 = _matmul(u1, w_mla_in, name="mla_in_proj",
                          comm=_gather_comm((), (), GATHER_3, [bshard[nm] for nm in GATHER_3]))
    kvm1, gathered3 = _matmul(memn[1], wf["w_mem_kv1"], out_dtype=BF16, name="mem_kv1",
                              comm=_gather_comm(GATHER_3, landed3, (), ()))
    kvm.append(kvm1.reshape(bsz, mlen, 2 * MEMW))
    wf.update(zip(GATHER_3, gathered3))
    w_memkv = [wf["w_mem_kv0"], wf["w_mem_kv1"]]
    w_o = [wf["w_out0"], wf["w_out1"]]
    w_memkv_t = [w.T for w in w_memkv]
    w_o_t = [w.T for w in w_o]
    cqn = _rms_fwd(p1, qg_full, width=Q_RANK, col_block=P1_CQ // Q_RANK, name="q_norm")
    ckvn = _rms_fwd(p1, kvg_full, width=KV_RANK, col_block=P1_CKV // KV_RANK, name="kv_norm")
    qb = _q_up(cqn, w_uq, rope_c, rope_s)
    kb, vb = _kv_up(ckvn, w_ukv, p1, rope_c, rope_s)
    o1, lse = _attn_fwd(qb, kb, vb, bsz, seq)
    o1 = o1.reshape(n, MAIN)
    ymem1 = _mem_fwd(p1, kvm[1], P1_QM // MEMW, bsz, seq, name="mem_attn1").reshape(n, MEMW)
    y1 = _gate_fwd(o1, ymem1, p1, P1_Z // GATE_CHUNK, name="gate1")
    h2 = _matmul(y1, w_o[1], h1, name="out_proj1")

    dh2, d_final_g, loss_part = _final_loss(h2, final_norm_g, tgt2)
    loss = lax.psum(loss_part[0, 0], ("x", "y", "c"))

    gbig = {}
    gbig["w_out1"] = _matmul(y1, dh2, ta=True, also_bf16=True, name="d_w_out1")
    dy1 = _matmul(dh2, w_o_t[1], name="d_y1")
    do1, dymem1, dz1 = _gate_bwd(dy1, o1, ymem1, p1, P1_Z // GATE_CHUNK, name="gate1_bwd")
    dqm1, dkvm1 = _mem_bwd(p1, kvm[1], dymem1, P1_QM // MEMW, bsz, seq, name="mem_attn1_bwd")
    dqb, dkv, dkrr = _attn_bwd(qb, kb, vb, o1.reshape(bsz, seq, MAIN), do1, lse, rope_c, rope_s, bsz, seq)
    dkr = _kr_bwd(dkrr, rope_c, rope_s)
    g_w_uq = _matmul(cqn, dqb, ta=True, name="d_w_uq")
    dcqn = _matmul(dqb, w_uq_t, name="d_cqn")
    gbig["mla_w_ukv"] = _matmul(ckvn, dkv, ta=True, also_bf16=True, name="d_w_ukv")
    dckvn = _matmul(dkv, w_ukv_t, name="d_ckvn")
    dcq, g_qg = _rms_bwd(p1, qg_full, dcqn, width=Q_RANK, col_block=P1_CQ // Q_RANK, out_dtype=BF16,
                         name="q_norm_bwd")
    dckv, g_kvg = _rms_bwd(p1, kvg_full, dckvn, width=KV_RANK, col_block=P1_CKV // KV_RANK, out_dtype=BF16,
                           name="kv_norm_bwd")
    dp1 = jnp.concatenate([dz1, dcq, dqm1.reshape(n, MEMW), dckv, dkr], axis=1)
    g_w_mla_in = _matmul(u1, dp1, ta=True, name="d_w_mla_in")
    du1 = _matmul(dp1, w_mla_in_t, name="d_u1")
    dh1, g_norm1 = _rms_bwd(h1, norm_g[1], du1, dh2, name="norm1_bwd")

    def mem_kv_bwd(i, dkvm):
        dk2 = dkvm.reshape(bsz * mlen, 2 * MEMW)
        gbig[f"w_mem_kv{i}"] = _matmul(memn[i], dk2, ta=True, also_bf16=True, name=f"d_w_mem_kv{i}")
        dmemn = _matmul(dk2, w_memkv_t[i], name=f"d_memn{i}")
        return _rms_bwd(mem2, mem_norm_g[i], dmemn, name=f"mem_norm{i}_bwd")[1]

    g_mem_g1 = mem_kv_bwd(1, dkvm1)
    for nm, g_int, back in (("mla_w_in", g_w_mla_in, _mla_in_from_internal), ("mla_w_uq", g_w_uq, _uq_from_internal)):
        g_cm = _chip_major(back(g_int))
        gbig[nm] = (g_cm, g_cm.astype(BF16))

    core = jnp.reshape(ac, (1,)).astype(jnp.int32)
    chip_core = jnp.stack([chip, ac]).astype(jnp.int32)

    def pair_sums(names, tag):
        from_sib = _swap_halves(names, [gbig[nm][1] for nm in names], name=f"rs_sibling_swap_{tag}")
        return from_sib, [_add_pairs(GEOM[nm], core, gbig[nm][0], fs, name=f"rs_add_pairs_{nm}")
                          for nm, fs in zip(names, from_sib)]

    from_sib_b, pairs_b = pair_sums(GROUP_B, "b")

    gbig["w_out0"] = _matmul(y0, dh1, ta=True, also_bf16=True, name="d_w_out0")
    dy0 = _matmul(dh1, w_o_t[0], name="d_y0")
    dhc, dymem0, dz0, g_ln_g, g_ln_b = _gate_bwd(dy0, hc, ymem0, p0, P0_Z // GATE_CHUNK, conv_ln_g, conv_ln_b,
                                                 name="gate0_bwd")
    dqm0, dkvm0 = _mem_bwd(p0, kvm[0], dymem0, P0_QM // MEMW, bsz, seq, name="mem_attn0_bwd")
    (da, dg, g_dw32), exch_b = _conv_bwd(p0, dhc.reshape(bsz, seq, MAIN), dw_full, bsz, seq,
                                         comm=_exchange_comm(pairs_b))
    dp0 = [da.reshape(n, MAIN), dg.reshape(n, MAIN), dqm0.reshape(n, MEMW), dz0]
    gbig["conv_w_in"] = _matmul(u0, dp0, ta=True, also_bf16=True, name="d_w_conv_in")
    g_mem_g = [mem_kv_bwd(0, dkvm0), g_mem_g1]

    from_sib_a, pairs_a = pair_sums(GROUP_A, "a")
    du0, exch_a = _matmul(dp0, w_conv_in_t, name="d_u0", comm=_exchange_comm(pairs_a))
    grad_x, g_norm0 = _rms_bwd(x2, norm_g[0], du0, dh1, name="norm0_bwd")

    from_sibling = dict(zip(GROUP_A + GROUP_B, list(from_sib_a) + list(from_sib_b)))
    exch = dict(zip(GROUP_A + GROUP_B, list(exch_a) + list(exch_b)))
    mine = [_add_final(g, chip_core, gbig[nm][0], from_sibling[nm], exch[nm], name=f"rs_add_final_{nm}")
            for nm, g in GEOMS]
    theirs = _share_with_sibling(mine)
    red = {}
    for (nm, _), a, b in zip(GEOMS, mine, theirs):
        red[nm] = jnp.concatenate([jnp.where(ac == 0, a, b), jnp.where(ac == 0, b, a)], axis=0)
    red["w_mem_kv"] = jnp.stack([red["w_mem_kv0"], red["w_mem_kv1"]])
    red["w_out"] = jnp.stack([red["w_out0"], red["w_out1"]])

    small_g = jnp.concatenate([
        _rows128(jnp.concatenate([g_norm0, g_norm1], axis=0), 16), _rows128(jnp.concatenate(g_mem_g, axis=0), 16),
        _rows128(g_dw32[CONV_K], 16), _rows128(g_ln_g, 16), _rows128(g_ln_b, 16), _rows128(d_final_g, 8),
        _rows128(g_dw32[:CONV_K], 376), _rows128(g_qg, 8), _rows128(g_kvg, 8)], axis=0)
    _, small_sum = _gather_sum_small(small_g, name="allreduce_small_grads")
    flat = small_sum.reshape(-1)

    def take(off, shape):
        size = 1
        for s_ in shape:
            size *= s_
        return flat[off * 128:off * 128 + size].reshape(shape)

    grads = dict(red)
    grads["conv_w_in"] = red["conv_w_in"][None]
    grads["mla_w_in"] = red["mla_w_in"][None]
    grads["mla_w_uq"] = red["mla_w_uq"][None]
    grads["mla_w_ukv"] = red["mla_w_ukv"][None]
    grads["norm_g"] = take(0, (2, D_MODEL))
    grads["mem_norm_g"] = take(16, (2, D_MODEL))
    grads["conv_dw_b"] = take(32, (1, MAIN))
    grads["conv_ln_g"] = take(48, (1, MAIN))
    grads["conv_ln_b"] = take(64, (1, MAIN))
    grads["final_norm_g"] = take(80, (D_MODEL,))
    grads["conv_dw"] = lax.dynamic_slice_in_dim(take(88, (CONV_K, MAIN)), chip * 384, 384, axis=1)[None]
    grads["mla_q_norm_g"] = lax.dynamic_slice_in_dim(take(464, (Q_RANK,)), chip * 128, 128, axis=0)[None]
    grads["mla_kv_norm_g"] = lax.dynamic_slice_in_dim(take(472, (KV_RANK,)), chip * 64, 64, axis=0)[None]

    params = dict(norm_g=norm_g, mem_norm_g=mem_norm_g, w_mem_kv=w_mem_kv, w_out=w_out, conv_w_in=conv_w_in,
                  conv_dw=conv_dw, conv_dw_b=conv_dw_b, conv_ln_g=conv_ln_g, conv_ln_b=conv_ln_b, mla_w_in=mla_w_in,
                  mla_q_norm_g=mla_q_norm_g, mla_w_uq=mla_w_uq, mla_kv_norm_g=mla_kv_norm_g, mla_w_ukv=mla_w_ukv,
                  final_norm_g=final_norm_g)
    mom1 = dict(norm_g=m_norm_g, mem_norm_g=m_mem_norm_g, w_mem_kv=m_w_mem_kv, w_out=m_w_out, conv_w_in=m_conv_w_in,
                conv_dw=m_conv_dw, conv_dw_b=m_conv_dw_b, conv_ln_g=m_conv_ln_g, conv_ln_b=m_conv_ln_b,
                mla_w_in=m_mla_w_in, mla_q_norm_g=m_mla_q_norm_g, mla_w_uq=m_mla_w_uq,
                mla_kv_norm_g=m_mla_kv_norm_g, mla_w_ukv=m_mla_w_ukv, final_norm_g=m_final_norm_g)
    mom2 = dict(norm_g=v_norm_g, mem_norm_g=v_mem_norm_g, w_mem_kv=v_w_mem_kv, w_out=v_w_out, conv_w_in=v_conv_w_in,
                conv_dw=v_conv_dw, conv_dw_b=v_conv_dw_b, conv_ln_g=v_conv_ln_g, conv_ln_b=v_conv_ln_b,
                mla_w_in=v_mla_w_in, mla_q_norm_g=v_mla_q_norm_g, mla_w_uq=v_mla_w_uq,
                mla_kv_norm_g=v_mla_kv_norm_g, mla_w_ukv=v_mla_w_ukv, final_norm_g=v_final_norm_g)
    names = list(params)
    g_out, deltas, new_m, new_v = [], [], [], []
    for nm in names:
        w = params[nm]
        g = grads[nm].reshape(w.shape)
        w2 = w.reshape(1, -1) if w.ndim == 1 else w
        dlt, m_new, v_new = _adamw(w2, g.reshape(w2.shape), mom1[nm].reshape(w2.shape), mom2[nm].reshape(w2.shape),
                                   name=f"adamw_{nm}")
        g_out.append(g)
        deltas.append(dlt.reshape(w.shape))
        new_m.append(m_new.reshape(w.shape))
        new_v.append(v_new.reshape(w.shape))

    return (loss, grad_x.reshape(bsz, seq, d), *g_out, *deltas, *new_m, *new_v)
```

```python
import functools

import jax
import jax.numpy as jnp
from jax import lax
from jax.experimental import pallas as pl
from jax.experimental.pallas import tpu as pltpu

F32 = jnp.float32
BF16 = jnp.bfloat16
MESH = pl.DeviceIdType.MESH

D_MODEL = 1024
MIX = 2048
MAIN = 1536
MEMW = 512
MEM_HEADS = 4
HEAD = 128
CONV_K = 31
CONV_PAD = 32
MLA_HEADS = 12
ROPE = 64
QK_PAD = 256
Q_RANK = 512
KV_RANK = 256
ROPE_THETA = 10000.0
RMS_EPS = 1e-6
LN_EPS = 1e-5
MEM_SCALE = HEAD ** -0.5
MLA_SCALE = (HEAD + ROPE) ** -0.5
NEG = -1e30

P0_COLS = 5632
P0_A, P0_G, P0_QM, P0_Z = 0, 1536, 3072, 3584
P1_COLS = 3456
P1_Z, P1_CQ, P1_QM, P1_CKV, P1_KR = 0, 2048, 2560, 3072, 3328

ADAM_LR = 0.001
ADAM_B1 = 0.9
ADAM_B2 = 0.999
ADAM_EPS = 1e-08
ADAM_WD = 0.01
ADAM_STEP = 10

VMEM_LIMIT = 56 * 1024 * 1024

ADD_TILE = 128
LOCAL_CHUNKS = 4


def _cparams(sem=None):
    return pltpu.CompilerParams(dimension_semantics=sem, vmem_limit_bytes=VMEM_LIMIT)


HBM = pl.BlockSpec(memory_space=pl.ANY)


class _Comm:
    def __init__(self, ins, out_shapes, n_sem, start, finish, aliases=None):
        self.ins, self.out_shapes, self.n_sem = list(ins), list(out_shapes), n_sem
        self.start, self.finish, self.aliases = start, finish, dict(aliases or {})


def _call(kernel_fn, comm, *, name, grid, in_specs, out_specs, out_shape, scratch_shapes, semantics, args):
    if comm is None:
        outs = pl.pallas_call(kernel_fn, name=name, grid=grid, in_specs=in_specs, out_specs=out_specs,
                              out_shape=out_shape, scratch_shapes=scratch_shapes,
                              compiler_params=_cparams(semantics))(*args)
        return list(outs), []
    n_in, n_out, nci, nco = len(in_specs), len(out_shape), len(comm.ins), len(comm.out_shapes)

    def body(*refs):
        ins, cins = refs[:n_in], refs[n_in:n_in + nci]
        outs = refs[n_in + nci:n_in + nci + n_out]
        couts = refs[n_in + nci + n_out:n_in + nci + n_out + nco]
        scratch, (ssem, rsem) = refs[n_in + nci + n_out + nco:-2], refs[-2:]
        ids = [pl.program_id(ax) for ax in range(len(grid))]
        first = functools.reduce(jnp.logical_and, [i == 0 for i in ids])
        last = functools.reduce(jnp.logical_and, [i == g - 1 for i, g in zip(ids, grid)])

        @pl.when(first)
        def _():
            comm.start(cins, couts, ssem, rsem)

        kernel_fn(*ins, *outs, *scratch)

        @pl.when(last)
        def _():
            comm.finish(cins, couts, ssem, rsem)

    outs = pl.pallas_call(
        body, name=name, grid=grid,
        in_specs=list(in_specs) + [HBM] * nci,
        out_specs=list(out_specs) + [HBM] * nco,
        out_shape=list(out_shape) + comm.out_shapes,
        scratch_shapes=list(scratch_shapes) + [pltpu.SemaphoreType.DMA((comm.n_sem,))] * 2,
        input_output_aliases={n_in + i: n_out + o for i, o in comm.aliases.items()},
        compiler_params=_cparams(("arbitrary",) * len(grid)),
    )(*args, *comm.ins)
    return list(outs[:n_out]), list(outs[n_out:])


def _tile(n, pref):
    if n <= pref:
        return n
    t = (pref // 128) * 128
    while t > 128 and n % t:
        t -= 128
    assert n % t == 0, (n, pref)
    return t


def _mm_kernel(*refs, nk, ta, has_res, a_parts, b_parts):
    a_refs, b_refs = refs[:len(a_parts)], refs[len(a_parts):len(a_parts) + len(b_parts)]
    rest = refs[len(a_parts) + len(b_parts):]
    if has_res:
        r_ref, *o_refs, acc_ref = rest
    else:
        *o_refs, acc_ref = rest
    j, k = pl.program_id(1), pl.program_id(2)

    def finish(acc):
        if has_res:
            acc = r_ref[...] + acc
        for o in o_refs:
            o[...] = acc.astype(o.dtype)

    def step(a_ref, b_ref):
        dn = (((0 if ta else 1,), (0,)), ((), ()))
        p = lax.dot_general(a_ref[...].astype(BF16), b_ref[...].astype(BF16), dn, preferred_element_type=F32)
        if nk == 1:
            finish(p)
            return

        @pl.when(k == 0)
        def _():
            acc_ref[...] = p

        @pl.when(jnp.logical_and(k > 0, k < nk - 1))
        def _():
            acc_ref[...] += p

        @pl.when(k == nk - 1)
        def _():
            finish(acc_ref[...] + p)

    for a_ref, (k0, kn) in zip(a_refs, a_parts):
        for b_ref, (j0, jn) in zip(b_refs, b_parts):
            conds = []
            if len(a_parts) > 1:
                conds.append(jnp.logical_and(k >= k0, k < k0 + kn))
            if len(b_parts) > 1:
                conds.append(jnp.logical_and(j >= j0, j < j0 + jn))
            if conds:
                pl.when(functools.reduce(jnp.logical_and, conds))(functools.partial(step, a_ref, b_ref))
            else:
                step(a_ref, b_ref)


def _matmul(a, b, res=None, *, name, ta=False, out_dtype=F32, also_bf16=False, comm=None, tm=1024, tn=512, tk=2048):
    a_list = list(a) if isinstance(a, (list, tuple)) else [a]
    b_list = list(b) if isinstance(b, (list, tuple)) else [b]
    assert not (ta and len(a_list) > 1)
    m = a_list[0].shape[1] if ta else a_list[0].shape[0]
    kd = a_list[0].shape[0] if ta else sum(p.shape[1] for p in a_list)
    n = sum(p.shape[1] for p in b_list)
    assert all(p.shape[0] == kd for p in b_list)
    tm = _tile(m, tm)
    tn = 512 if len(b_list) > 1 else _tile(n, tn)
    tk = 512 if len(a_list) > 1 else _tile(kd, tk)
    nk = kd // tk

    def parts(widths, t):
        out, start = [], 0
        for w in widths:
            assert w % t == 0
            out.append((start, w // t))
            start += w // t
        return out

    a_parts = parts([p.shape[1] for p in a_list], tk) if len(a_list) > 1 else [(0, nk)]
    b_parts = parts([p.shape[1] for p in b_list], tn) if len(b_list) > 1 else [(0, n // tn)]
    if ta:
        a_specs = [pl.BlockSpec((tk, tm), lambda i, j, k: (k, i))]
    else:
        a_specs = [pl.BlockSpec((tm, tk), lambda i, j, k, k0=k0, kn=kn: (i, jnp.clip(k - k0, 0, kn - 1)))
                   for k0, kn in a_parts]
    b_specs = [pl.BlockSpec((tk, tn), lambda i, j, k, j0=j0, jn=jn: (
                   jnp.where(jnp.logical_and(j >= j0, j < j0 + jn), k, 0), jnp.clip(j - j0, 0, jn - 1)))
               for j0, jn in b_parts]
    out_spec = pl.BlockSpec((tm, tn), lambda i, j, k: (i, j))
    in_specs = a_specs + b_specs
    args = a_list + b_list
    if res is not None:
        in_specs.append(out_spec)
        args.append(res)
    out_shape = [jax.ShapeDtypeStruct((m, n), out_dtype)]
    if also_bf16:
        out_shape.append(jax.ShapeDtypeStruct((m, n), BF16))
    outs, carried = _call(
        functools.partial(_mm_kernel, nk=nk, ta=ta, has_res=res is not None, a_parts=a_parts, b_parts=b_parts), comm,
        name=name,
        grid=(m // tm, n // tn, nk),
        in_specs=in_specs,
        out_specs=[out_spec] * len(out_shape),
        out_shape=out_shape,
        scratch_shapes=[pltpu.VMEM((tm, tn), F32)],
        semantics=("parallel", "parallel", "arbitrary"),
        args=args)
    result = tuple(outs) if also_bf16 else outs[0]
    return result if comm is None else (result, carried)


def _rms_fwd_kernel(h_ref, g_ref, o_ref):
    h = h_ref[...]
    rstd = lax.rsqrt(jnp.mean(h * h, axis=-1, keepdims=True) + RMS_EPS)
    o_ref[...] = (h * rstd * g_ref[...]).astype(o_ref.dtype)


def _rms_fwd(h, g, *, name, width=None, col_block=0, tm=512):
    t = h.shape[0]
    width = width or h.shape[1]
    tm = _tile(t, tm)
    return pl.pallas_call(
        functools.partial(_rms_fwd_kernel),
        name=name,
        grid=(t // tm,),
        in_specs=[pl.BlockSpec((tm, width), lambda i: (i, col_block)),
                  pl.BlockSpec((1, width), lambda i: (0, 0))],
        out_specs=pl.BlockSpec((tm, width), lambda i: (i, 0)),
        out_shape=jax.ShapeDtypeStruct((t, width), BF16),
        compiler_params=_cparams(("parallel",)),
    )(h, g.reshape(1, width))


def _rms_bwd_math(h, g, du):
    rstd = lax.rsqrt(jnp.mean(h * h, axis=-1, keepdims=True) + RMS_EPS)
    dug = du * g
    dh = rstd * dug - h * (rstd * rstd * rstd) * jnp.mean(dug * h, axis=-1, keepdims=True)
    dg = jnp.sum(du * h * rstd, axis=0, keepdims=True)
    return dh, dg


def _rms_bwd_kernel(*refs, has_res):
    if has_res:
        h_ref, g_ref, du_ref, res_ref, dh_ref, dg_ref = refs
    else:
        h_ref, g_ref, du_ref, dh_ref, dg_ref = refs
    dh, dg = _rms_bwd_math(h_ref[...], g_ref[...], du_ref[...].astype(F32))
    if has_res:
        dh = dh + res_ref[...]
    dh_ref[...] = dh.astype(dh_ref.dtype)

    @pl.when(pl.program_id(0) == 0)
    def _():
        dg_ref[...] = dg

    @pl.when(pl.program_id(0) > 0)
    def _():
        dg_ref[...] += dg


def _rms_bwd(h, g, du, res=None, *, name, width=None, col_block=0, out_dtype=F32, tm=512):
    t = h.shape[0]
    width = width or h.shape[1]
    tm = _tile(t, tm)
    row = pl.BlockSpec((tm, width), lambda i: (i, 0))
    in_specs = [pl.BlockSpec((tm, width), lambda i: (i, col_block)),
                pl.BlockSpec((1, width), lambda i: (0, 0)), row]
    args = [h, g.reshape(1, width), du]
    if res is not None:
        in_specs.append(row)
        args.append(res)
    return pl.pallas_call(
        functools.partial(_rms_bwd_kernel, has_res=res is not None),
        name=name,
        grid=(t // tm,),
        in_specs=in_specs,
        out_specs=[row, pl.BlockSpec((1, width), lambda i: (0, 0))],
        out_shape=[jax.ShapeDtypeStruct((t, width), out_dtype), jax.ShapeDtypeStruct((1, width), F32)],
        compiler_params=_cparams(("arbitrary",)),
    )(*args)


def _final_kernel(h_ref, g_ref, t_ref, dh_ref, dg_ref, loss_ref):
    h = h_ref[...]
    g = g_ref[...]
    rstd = lax.rsqrt(jnp.mean(h * h, axis=-1, keepdims=True) + RMS_EPS)
    e = h * rstd * g - t_ref[...]
    part = 0.5 * jnp.sum(jnp.mean(e * e, axis=-1, keepdims=True), axis=0, keepdims=True)
    dh, dg = _rms_bwd_math(h, g, e * (1.0 / D_MODEL))
    dh_ref[...] = dh
    part = jnp.broadcast_to(part, loss_ref.shape)

    @pl.when(pl.program_id(0) == 0)
    def _():
        dg_ref[...] = dg
        loss_ref[...] = part

    @pl.when(pl.program_id(0) > 0)
    def _():
        dg_ref[...] += dg
        loss_ref[...] += part


def _final_loss(h, g, target, *, tm=512):
    t, d = h.shape
    tm = _tile(t, tm)
    row = pl.BlockSpec((tm, d), lambda i: (i, 0))
    return pl.pallas_call(
        functools.partial(_final_kernel),
        name="final_loss",
        grid=(t // tm,),
        in_specs=[row, pl.BlockSpec((1, d), lambda i: (0, 0)), row],
        out_specs=[row, pl.BlockSpec((1, d), lambda i: (0, 0)), pl.BlockSpec((1, 128), lambda i: (0, 0))],
        out_shape=[jax.ShapeDtypeStruct((t, d), F32), jax.ShapeDtypeStruct((1, d), F32),
                   jax.ShapeDtypeStruct((1, 128), F32)],
        compiler_params=_cparams(("arbitrary",)),
    )(h, g.reshape(1, d), target)


CONV_CT = 128
CONV_TC = 256
CONV_TC_BWD = 128


def _glu_into(pad_ref, a_ref, g_ref, seq, tc):
    ct = pad_ref.shape[1]
    pad_ref[0:CONV_PAD, :] = jnp.zeros((CONV_PAD, ct), F32)
    for r in range(0, seq, tc):
        a = a_ref[0, r:r + tc, :]
        g = g_ref[0, r:r + tc, :]
        pad_ref[CONV_PAD + r:CONV_PAD + r + tc, :] = a * jax.nn.sigmoid(g)


def _conv_fwd_kernel(a_ref, g_ref, dw_ref, dwb_ref, hc_ref, pad_ref, *, seq, tc):
    ct = pad_ref.shape[1]
    _glu_into(pad_ref, a_ref, g_ref, seq, tc)
    for r in range(0, seq, tc):
        acc = jnp.broadcast_to(dwb_ref[...], (tc, ct))
        for k in range(CONV_K):
            o = CONV_PAD + r - (CONV_K - 1) + k
            acc = acc + dw_ref[k:k + 1, :] * pad_ref[o:o + tc, :]
        hc_ref[0, r:r + tc, :] = acc


def _conv_fwd(p0, dw, dwb, bsz, seq, comm=None):
    ct = CONV_CT
    tc = min(CONV_TC, seq)
    p3 = p0.reshape(bsz, seq, P0_COLS)
    outs, carried = _call(
        functools.partial(_conv_fwd_kernel, seq=seq, tc=tc), comm,
        name="conv_fwd",
        grid=(MAIN // ct, bsz),
        in_specs=[pl.BlockSpec((1, seq, ct), lambda j, b: (b, 0, P0_A // ct + j)),
                  pl.BlockSpec((1, seq, ct), lambda j, b: (b, 0, P0_G // ct + j)),
                  pl.BlockSpec((CONV_K, ct), lambda j, b: (0, j)),
                  pl.BlockSpec((1, ct), lambda j, b: (0, j))],
        out_specs=[pl.BlockSpec((1, seq, ct), lambda j, b: (b, 0, j))],
        out_shape=[jax.ShapeDtypeStruct((bsz, seq, MAIN), F32)],
        scratch_shapes=[pltpu.VMEM((seq + CONV_PAD, ct), F32)],
        semantics=("parallel", "parallel"),
        args=[p3, p3, dw, dwb])
    return outs[0], carried


def _fold_rows(x):
    parts = [x[r:r + 8] for r in range(0, x.shape[0], 8)]
    while len(parts) > 1:
        parts = [a + b for a, b in zip(parts[::2], parts[1::2])]
    return parts[0]


def _conv_bwd_kernel(a_ref, g_ref, dhc_ref, dw_ref, da_ref, dg_ref, ddw_ref, pad_ref, dpad_ref, acc_ref,
                     *, seq, tc):
    ct = pad_ref.shape[1]
    b = pl.program_id(1)
    _glu_into(pad_ref, a_ref, g_ref, seq, tc)
    dpad_ref[seq:seq + CONV_PAD, :] = jnp.zeros((CONV_PAD, ct), F32)
    for r in range(0, seq, tc):
        dpad_ref[r:r + tc, :] = dhc_ref[0, r:r + tc, :]
    acc_ref[...] = jnp.zeros(acc_ref.shape, F32)
    for r in range(0, seq, tc):
        dh = dhc_ref[0, r:r + tc, :]
        dglu = jnp.zeros((tc, ct), F32)
        for k in range(CONV_K):
            o = r + (CONV_K - 1) - k
            dglu = dglu + dw_ref[k:k + 1, :] * dpad_ref[o:o + tc, :]
            o = CONV_PAD + r - (CONV_K - 1) + k
            prod = pad_ref[o:o + tc, :] * dh
            acc_ref[k] += _fold_rows(prod)
        acc_ref[CONV_K] += _fold_rows(dh)
        a = a_ref[0, r:r + tc, :]
        sg = jax.nn.sigmoid(g_ref[0, r:r + tc, :])
        da_ref[0, r:r + tc, :] = (dglu * sg).astype(da_ref.dtype)
        dg_ref[0, r:r + tc, :] = (dglu * a * sg * (1.0 - sg)).astype(dg_ref.dtype)
    tot = jnp.sum(acc_ref[...], axis=1)

    @pl.when(b == 0)
    def _():
        ddw_ref[...] = tot

    @pl.when(b > 0)
    def _():
        ddw_ref[...] += tot


def _conv_bwd(p0, dhc, dw, bsz, seq, comm=None):
    ct = CONV_CT
    tc = min(CONV_TC_BWD, seq)
    p3 = p0.reshape(bsz, seq, P0_COLS)
    blk = pl.BlockSpec((1, seq, ct), lambda j, b: (b, 0, j))
    return _call(
        functools.partial(_conv_bwd_kernel, seq=seq, tc=tc), comm,
        name="conv_bwd",
        grid=(MAIN // ct, bsz),
        in_specs=[pl.BlockSpec((1, seq, ct), lambda j, b: (b, 0, P0_A // ct + j)),
                  pl.BlockSpec((1, seq, ct), lambda j, b: (b, 0, P0_G // ct + j)),
                  blk,
                  pl.BlockSpec((CONV_K, ct), lambda j, b: (0, j))],
        out_specs=[blk, blk, pl.BlockSpec((CONV_K + 1, ct), lambda j, b: (0, j))],
        out_shape=[jax.ShapeDtypeStruct((bsz, seq, MAIN), BF16), jax.ShapeDtypeStruct((bsz, seq, MAIN), BF16),
                   jax.ShapeDtypeStruct((CONV_K + 1, MAIN), F32)],
        scratch_shapes=[pltpu.VMEM((seq + CONV_PAD, ct), F32), pltpu.VMEM((seq + CONV_PAD, ct), F32),
                        pltpu.VMEM((CONV_K + 1, 8, ct), F32)],
        semantics=("parallel", "arbitrary"),
        args=[p3, p3, dhc, dw])


def _ln_parts(x, lng, lnb):
    mu = jnp.mean(x, axis=-1, keepdims=True)
    xc = x - mu
    rstd = lax.rsqrt(jnp.mean(xc * xc, axis=-1, keepdims=True) + LN_EPS)
    xh = xc * rstd
    hl = xh * lng + lnb
    return rstd, xh, hl


GATE_CHUNK = 512
GATE_NZ = MIX // GATE_CHUNK


def _z_specs(tm, zb):
    return [pl.BlockSpec((tm, GATE_CHUNK), lambda i, j=j: (i, zb + j)) for j in range(GATE_NZ)]


def _gate_fwd_kernel(*refs, ln):
    main_ref, ymem_ref, *z_refs = refs[:2 + GATE_NZ]
    y_ref = refs[-1]
    x = main_ref[...]
    if ln:
        _, _, hl = _ln_parts(x, refs[-3][...], refs[-2][...])
        x = hl * jax.nn.sigmoid(hl)
    for j, z_ref in enumerate(z_refs):
        cols = slice(j * GATE_CHUNK, (j + 1) * GATE_CHUNK)
        z = z_ref[...]
        src = x[:, cols] if j < MAIN // GATE_CHUNK else ymem_ref[...]
        y_ref[:, cols] = (src * (z * jax.nn.sigmoid(z))).astype(y_ref.dtype)


def _gate_fwd(main, ymem, p, zb, ln_g=None, ln_b=None, *, name, tm=256):
    t = main.shape[0]
    tm = _tile(t, tm)
    ln = ln_g is not None
    in_specs = [pl.BlockSpec((tm, MAIN), lambda i: (i, 0)), pl.BlockSpec((tm, MEMW), lambda i: (i, 0))]
    in_specs += _z_specs(tm, zb)
    args = [main, ymem] + [p] * GATE_NZ
    if ln:
        in_specs += [pl.BlockSpec((1, MAIN), lambda i: (0, 0))] * 2
        args += [ln_g.reshape(1, MAIN), ln_b.reshape(1, MAIN)]
    return pl.pallas_call(
        functools.partial(_gate_fwd_kernel, ln=ln),
        name=name,
        grid=(t // tm,),
        in_specs=in_specs,
        out_specs=pl.BlockSpec((tm, MIX), lambda i: (i, 0)),
        out_shape=jax.ShapeDtypeStruct((t, MIX), BF16),
        compiler_params=_cparams(("parallel",)),
    )(*args)


def _gate_bwd_kernel(*refs, ln):
    dy_ref, main_ref, ymem_ref, *z_refs = refs[:3 + GATE_NZ]
    if ln:
        lng_ref, lnb_ref, dmain_ref, dymem_ref, dz_ref, dlng_ref, dlnb_ref = refs[3 + GATE_NZ:]
    else:
        dmain_ref, dymem_ref, dz_ref = refs[3 + GATE_NZ:]
    x = main_ref[...]
    if ln:
        lng = lng_ref[...]
        rstd, xh, hl = _ln_parts(x, lng, lnb_ref[...])
        sh = jax.nn.sigmoid(hl)
        ymain = hl * sh
    else:
        ymain = x
    for j, z_ref in enumerate(z_refs):
        cols = slice(j * GATE_CHUNK, (j + 1) * GATE_CHUNK)
        dy = dy_ref[:, cols]
        z = z_ref[...]
        sg = jax.nn.sigmoid(z)
        dsz = sg * (1.0 + z * (1.0 - sg))
        if j < MAIN // GATE_CHUNK:
            src = ymain[:, cols]
            dmain_ref[:, cols] = dy * (z * sg)
        else:
            src = ymem_ref[...]
            dymem_ref[...] = dy * (z * sg)
        dz_ref[:, cols] = (dy * src * dsz).astype(dz_ref.dtype)
    if not ln:
        return
    dym = dmain_ref[...]
    dhl = dym * (sh * (1.0 + hl * (1.0 - sh)))
    dxh = dhl * lng
    dmain_ref[...] = rstd * (dxh - jnp.mean(dxh, axis=-1, keepdims=True)
                             - xh * jnp.mean(dxh * xh, axis=-1, keepdims=True))
    dlng = jnp.sum(dhl * xh, axis=0, keepdims=True)
    dlnb = jnp.sum(dhl, axis=0, keepdims=True)

    @pl.when(pl.program_id(0) == 0)
    def _():
        dlng_ref[...] = dlng
        dlnb_ref[...] = dlnb

    @pl.when(pl.program_id(0) > 0)
    def _():
        dlng_ref[...] += dlng
        dlnb_ref[...] += dlnb


def _gate_bwd(dy, main, ymem, p, zb, ln_g=None, ln_b=None, *, name, tm=256):
    t = main.shape[0]
    tm = _tile(t, tm)
    ln = ln_g is not None
    r_main = pl.BlockSpec((tm, MAIN), lambda i: (i, 0))
    r_mem = pl.BlockSpec((tm, MEMW), lambda i: (i, 0))
    r_mix = pl.BlockSpec((tm, MIX), lambda i: (i, 0))
    vec = pl.BlockSpec((1, MAIN), lambda i: (0, 0))
    in_specs = [r_mix, r_main, r_mem] + _z_specs(tm, zb)
    args = [dy, main, ymem] + [p] * GATE_NZ
    out_specs = [r_main, r_mem, r_mix]
    out_shape = [jax.ShapeDtypeStruct((t, MAIN), F32), jax.ShapeDtypeStruct((t, MEMW), F32),
                 jax.ShapeDtypeStruct((t, MIX), BF16)]
    if ln:
        in_specs += [vec, vec]
        args += [ln_g.reshape(1, MAIN), ln_b.reshape(1, MAIN)]
        out_specs += [vec, vec]
        out_shape += [jax.ShapeDtypeStruct((1, MAIN), F32)] * 2
    return pl.pallas_call(
        functools.partial(_gate_bwd_kernel, ln=ln),
        name=name,
        grid=(t // tm,),
        in_specs=in_specs,
        out_specs=out_specs,
        out_shape=out_shape,
        compiler_params=_cparams(("arbitrary",)),
    )(*args)


def _dot_nt(a, b):
    return lax.dot_general(a, b, (((1,), (1,)), ((), ())), preferred_element_type=F32)


def _dot_tn(a, b):
    return lax.dot_general(a, b, (((0,), (0,)), ((), ())), preferred_element_type=F32)


def _dot(a, b):
    return jnp.dot(a, b, preferred_element_type=F32)


def _mem_probs(q, k):
    s = _dot_nt(q, k) * MEM_SCALE
    p = jnp.exp(s - jnp.max(s, axis=-1, keepdims=True))
    return p / jnp.sum(p, axis=-1, keepdims=True)


def _mem_fwd_kernel(q_ref, kv_ref, o_ref):
    for h in range(MEM_HEADS):
        c = slice(h * HEAD, (h + 1) * HEAD)
        cv = slice(MEMW + h * HEAD, MEMW + (h + 1) * HEAD)
        p = _mem_probs(q_ref[0, :, c].astype(BF16), kv_ref[0, :, c])
        o_ref[0, :, c] = _dot(p.astype(BF16), kv_ref[0, :, cv])


def _mem_fwd(p, kvm, col_block, bsz, seq, *, name, tq=512):
    tq = _tile(seq, tq)
    p3 = p.reshape(bsz, seq, p.shape[1])
    mlen = kvm.shape[1]
    return pl.pallas_call(
        functools.partial(_mem_fwd_kernel),
        name=name,
        grid=(bsz, seq // tq),
        in_specs=[pl.BlockSpec((1, tq, MEMW), lambda b, i: (b, i, col_block)),
                  pl.BlockSpec((1, mlen, 2 * MEMW), lambda b, i: (b, 0, 0))],
        out_specs=pl.BlockSpec((1, tq, MEMW), lambda b, i: (b, i, 0)),
        out_shape=jax.ShapeDtypeStruct((bsz, seq, MEMW), F32),
        compiler_params=_cparams(("parallel", "parallel")),
    )(p3, kvm)


def _mem_bwd_kernel(q_ref, kv_ref, do_ref, dq_ref, dkv_ref):
    @pl.when(pl.program_id(1) == 0)
    def _():
        dkv_ref[...] = jnp.zeros(dkv_ref.shape, F32)

    for h in range(MEM_HEADS):
        c = slice(h * HEAD, (h + 1) * HEAD)
        cv = slice(MEMW + h * HEAD, MEMW + (h + 1) * HEAD)
        q = q_ref[0, :, c].astype(BF16)
        k = kv_ref[0, :, c]
        v = kv_ref[0, :, cv]
        do = do_ref[0, :, c].astype(BF16)
        p = _mem_probs(q, k)
        dp = _dot_nt(do, v)
        ds = (p * (dp - jnp.sum(p * dp, axis=-1, keepdims=True)) * MEM_SCALE).astype(BF16)
        dq_ref[0, :, c] = _dot(ds, k).astype(dq_ref.dtype)
        dkv_ref[0, :, c] += _dot_tn(ds, q)
        dkv_ref[0, :, cv] += _dot_tn(p.astype(BF16), do)


def _mem_bwd(p, kvm, dymem, col_block, bsz, seq, *, name, tq=512):
    tq = _tile(seq, tq)
    p3 = p.reshape(bsz, seq, p.shape[1])
    mlen = kvm.shape[1]
    return pl.pallas_call(
        functools.partial(_mem_bwd_kernel),
        name=name,
        grid=(bsz, seq // tq),
        in_specs=[pl.BlockSpec((1, tq, MEMW), lambda b, i: (b, i, col_block)),
                  pl.BlockSpec((1, mlen, 2 * MEMW), lambda b, i: (b, 0, 0)),
                  pl.BlockSpec((1, tq, MEMW), lambda b, i: (b, i, 0))],
        out_specs=[pl.BlockSpec((1, tq, MEMW), lambda b, i: (b, i, 0)),
                   pl.BlockSpec((1, mlen, 2 * MEMW), lambda b, i: (b, 0, 0))],
        out_shape=[jax.ShapeDtypeStruct((bsz, seq, MEMW), BF16),
                   jax.ShapeDtypeStruct((bsz, mlen, 2 * MEMW), F32)],
        compiler_params=_cparams(("parallel", "arbitrary")),
    )(p3, kvm, dymem.reshape(bsz, seq, MEMW))


def _swap32(x):
    lane = lax.broadcasted_iota(jnp.int32, x.shape, 1)
    return jnp.where(lane < 32, pltpu.roll(x, 96, 1), pltpu.roll(x, 32, 1))


def _rope(x, cs, sn):
    return x * cs + _swap32(x) * sn


def _rope_t(d, cs, sn):
    return d * cs + _swap32(d * sn)


UP_HEADS = 2


def _q_up_kernel(a_ref, b_ref, cs_ref, sn_ref, o_ref):
    acc = _dot(a_ref[...], b_ref[...])
    cs = cs_ref[...]
    sn = sn_ref[...]
    for h in range(UP_HEADS):
        c0 = slice(h * QK_PAD, h * QK_PAD + HEAD)
        c1 = slice(h * QK_PAD + HEAD, (h + 1) * QK_PAD)
        o_ref[:, c0] = acc[:, c0].astype(o_ref.dtype)
        o_ref[:, c1] = _rope(acc[:, c1], cs, sn).astype(o_ref.dtype)


def _q_up(cqn, w_uq, cs, sn, *, tm=512):
    t, kd = cqn.shape
    tm = _tile(t, tm)
    tn = UP_HEADS * QK_PAD
    tab = pl.BlockSpec((tm, 128), lambda i, j: (i, 0))
    return pl.pallas_call(
        functools.partial(_q_up_kernel),
        name="q_up_rope",
        grid=(t // tm, MLA_HEADS // UP_HEADS),
        in_specs=[pl.BlockSpec((tm, kd), lambda i, j: (i, 0)), pl.BlockSpec((kd, tn), lambda i, j: (0, j)), tab, tab],
        out_specs=pl.BlockSpec((tm, tn), lambda i, j: (i, j)),
        out_shape=jax.ShapeDtypeStruct((t, MLA_HEADS * QK_PAD), BF16),
        compiler_params=_cparams(("parallel", "parallel")),
    )(cqn, w_uq, cs, sn)


def _kv_up_kernel(a_ref, b_ref, kr_ref, cs_ref, sn_ref, k_ref, v_ref):
    acc = _dot(a_ref[...], b_ref[...])
    krr = _rope(kr_ref[...], cs_ref[...], sn_ref[...]).astype(k_ref.dtype)
    for h in range(UP_HEADS):
        k_ref[:, h * QK_PAD:h * QK_PAD + HEAD] = acc[:, h * 2 * HEAD:h * 2 * HEAD + HEAD].astype(k_ref.dtype)
        k_ref[:, h * QK_PAD + HEAD:(h + 1) * QK_PAD] = krr
        v_ref[:, h * HEAD:(h + 1) * HEAD] = acc[:, h * 2 * HEAD + HEAD:(h + 1) * 2 * HEAD].astype(v_ref.dtype)


def _kv_up(ckvn, w_ukv, p1, cs, sn, *, tm=512):
    t, kd = ckvn.shape
    tm = _tile(t, tm)
    tab = pl.BlockSpec((tm, 128), lambda i, j: (i, 0))
    return pl.pallas_call(
        functools.partial(_kv_up_kernel),
        name="kv_up_pack",
        grid=(t // tm, MLA_HEADS // UP_HEADS),
        in_specs=[pl.BlockSpec((tm, kd), lambda i, j: (i, 0)),
                  pl.BlockSpec((kd, UP_HEADS * 2 * HEAD), lambda i, j: (0, j)),
                  pl.BlockSpec((tm, 128), lambda i, j: (i, P1_KR // 128)), tab, tab],
        out_specs=[pl.BlockSpec((tm, UP_HEADS * QK_PAD), lambda i, j: (i, j)),
                   pl.BlockSpec((tm, UP_HEADS * HEAD), lambda i, j: (i, j))],
        out_shape=[jax.ShapeDtypeStruct((t, MLA_HEADS * QK_PAD), BF16), jax.ShapeDtypeStruct((t, MAIN), BF16)],
        compiler_params=_cparams(("parallel", "parallel")),
    )(ckvn, w_ukv, p1, cs, sn)


def _kr_bwd_kernel(d_ref, cs_ref, sn_ref, o_ref):
    acc = d_ref[:, :HEAD]
    for h in range(1, MLA_HEADS):
        acc = acc + d_ref[:, h * HEAD:(h + 1) * HEAD]
    o_ref[...] = _rope_t(acc, cs_ref[...], sn_ref[...]).astype(o_ref.dtype)


def _kr_bwd(dkrr, cs, sn, *, tm=512):
    t = dkrr.shape[0]
    tm = _tile(t, tm)
    tab = pl.BlockSpec((tm, 128), lambda i: (i, 0))
    return pl.pallas_call(
        functools.partial(_kr_bwd_kernel),
        name="kr_bwd",
        grid=(t // tm,),
        in_specs=[pl.BlockSpec((tm, MAIN), lambda i: (i, 0)), tab, tab],
        out_specs=tab,
        out_shape=jax.ShapeDtypeStruct((t, 128), BF16),
        compiler_params=_cparams(("parallel",)),
    )(dkrr, cs, sn)


ATT_T = 256


def _causal(s, t):
    row = lax.broadcasted_iota(jnp.int32, (t, t), 0)
    col = lax.broadcasted_iota(jnp.int32, (t, t), 1)
    return jnp.where(col <= row, s, NEG)


def _attn_fwd_kernel(q_ref, k_ref, v_ref, o_ref, lse_ref, *, seq, t):
    for i in range(seq // t):
        own = slice(i * t, (i + 1) * t)
        q = q_ref[0, own, :]
        sd = _causal(_dot_nt(q, k_ref[0, own, :]) * MLA_SCALE, t)
        m = jnp.max(sd, axis=-1, keepdims=True)
        if i:
            so = _dot_nt(q, k_ref[0, :i * t, :]) * MLA_SCALE
            m = jnp.maximum(m, jnp.max(so, axis=-1, keepdims=True))
        pd = jnp.exp(sd - m)
        l = jnp.sum(pd, axis=-1, keepdims=True)
        acc = _dot(pd.astype(BF16), v_ref[0, own, :])
        if i:
            po = jnp.exp(so - m)
            l = l + jnp.sum(po, axis=-1, keepdims=True)
            acc = acc + _dot(po.astype(BF16), v_ref[0, :i * t, :])
        o_ref[0, own, :] = acc / l
        lse_ref[0, 0, own, :] = m + jnp.log(l)


def _attn_fwd(qb, kb, vb, bsz, seq):
    t = min(ATT_T, seq)
    q3 = qb.reshape(bsz, seq, MLA_HEADS * QK_PAD)
    k3 = kb.reshape(bsz, seq, MLA_HEADS * QK_PAD)
    v3 = vb.reshape(bsz, seq, MAIN)
    qk = pl.BlockSpec((1, seq, QK_PAD), lambda b, h: (b, 0, h))
    vv = pl.BlockSpec((1, seq, HEAD), lambda b, h: (b, 0, h))
    return pl.pallas_call(
        functools.partial(_attn_fwd_kernel, seq=seq, t=t),
        name="attn_fwd",
        grid=(bsz, MLA_HEADS),
        in_specs=[qk, qk, vv],
        out_specs=[vv, pl.BlockSpec((1, 1, seq, 1), lambda b, h: (b, h, 0, 0))],
        out_shape=[jax.ShapeDtypeStruct((bsz, seq, MAIN), F32),
                   jax.ShapeDtypeStruct((bsz, MLA_HEADS, seq, 1), F32)],
        compiler_params=_cparams(("parallel", "parallel")),
    )(q3, k3, v3)


def _attn_bwd_kernel(q_ref, k_ref, v_ref, o_ref, do_ref, lse_ref, cs_ref, sn_ref, dq_ref, dkv_ref, dkr_ref,
                     delta_ref, dqacc_ref, *, seq, t):
    for r in range(0, seq, t):
        rows = slice(r, r + t)
        delta_ref[rows, :] = jnp.sum(do_ref[0, rows, :] * o_ref[0, rows, :], axis=-1, keepdims=True)
    dqacc_ref[...] = jnp.zeros(dqacc_ref.shape, F32)

    def piece(rows, k, v, masked):
        q = q_ref[0, rows, :]
        do = do_ref[0, rows, :].astype(BF16)
        s = _dot_nt(q, k) * MLA_SCALE
        if masked:
            s = _causal(s, t)
        p = jnp.exp(s - lse_ref[0, 0, rows, :])
        dp = _dot_nt(do, v)
        ds = (p * (dp - delta_ref[rows, :]) * MLA_SCALE).astype(BF16)
        dqacc_ref[rows, :] += _dot(ds, k)
        return _dot_tn(ds, q), _dot_tn(p.astype(BF16), do)

    for j in range(seq // t):
        own = slice(j * t, (j + 1) * t)
        k = k_ref[0, own, :]
        v = v_ref[0, own, :]
        dk, dv = piece(own, k, v, True)
        if (j + 1) * t < seq:
            dk2, dv2 = piece(slice((j + 1) * t, seq), k, v, False)
            dk, dv = dk + dk2, dv + dv2
        dkv_ref[0, own, :HEAD] = dk[:, :HEAD].astype(dkv_ref.dtype)
        dkv_ref[0, own, HEAD:] = dv.astype(dkv_ref.dtype)
        dkr_ref[0, own, :] = dk[:, HEAD:]
    for r in range(0, seq, t):
        rows = slice(r, r + t)
        dq = dqacc_ref[rows, :]
        dq_ref[0, rows, :HEAD] = dq[:, :HEAD].astype(dq_ref.dtype)
        dq_ref[0, rows, HEAD:] = _rope_t(dq[:, HEAD:], cs_ref[0, rows, :], sn_ref[0, rows, :]).astype(dq_ref.dtype)


def _attn_bwd(qb, kb, vb, o, do, lse, cs, sn, bsz, seq):
    t = min(ATT_T, seq)
    q3 = qb.reshape(bsz, seq, MLA_HEADS * QK_PAD)
    k3 = kb.reshape(bsz, seq, MLA_HEADS * QK_PAD)
    v3 = vb.reshape(bsz, seq, MAIN)
    qk = pl.BlockSpec((1, seq, QK_PAD), lambda b, h: (b, 0, h))
    vv = pl.BlockSpec((1, seq, HEAD), lambda b, h: (b, 0, h))
    tab = pl.BlockSpec((1, seq, 128), lambda b, h: (b, 0, 0))
    dq, dkv, dkr = pl.pallas_call(
        functools.partial(_attn_bwd_kernel, seq=seq, t=t),
        name="attn_bwd",
        grid=(bsz, MLA_HEADS),
        in_specs=[qk, qk, vv, vv, vv, pl.BlockSpec((1, 1, seq, 1), lambda b, h: (b, h, 0, 0)), tab, tab],
        out_specs=[qk, qk, vv],
        out_shape=[jax.ShapeDtypeStruct((bsz, seq, MLA_HEADS * QK_PAD), BF16),
                   jax.ShapeDtypeStruct((bsz, seq, MLA_HEADS * 2 * HEAD), BF16),
                   jax.ShapeDtypeStruct((bsz, seq, MAIN), F32)],
        scratch_shapes=[pltpu.VMEM((seq, 1), F32), pltpu.VMEM((seq, QK_PAD), F32)],
        compiler_params=_cparams(("parallel", "parallel")),
    )(q3, k3, v3, o, do.reshape(bsz, seq, MAIN), lse, cs.reshape(bsz, seq, 128), sn.reshape(bsz, seq, 128))
    n = bsz * seq
    return dq.reshape(n, -1), dkv.reshape(n, -1), dkr.reshape(n, -1)


def _adamw_kernel(w_ref, g_ref, m_ref, v_ref, d_ref, nm_ref, nv_ref):
    g = g_ref[...]
    m = ADAM_B1 * m_ref[...] + (1.0 - ADAM_B1) * g
    v = ADAM_B2 * v_ref[...] + (1.0 - ADAM_B2) * (g * g)
    m_hat = m / (1.0 - ADAM_B1 ** ADAM_STEP)
    v_hat = v / (1.0 - ADAM_B2 ** ADAM_STEP)
    d_ref[...] = -ADAM_LR * (m_hat / (jnp.sqrt(v_hat) + ADAM_EPS) + ADAM_WD * w_ref[...])
    nm_ref[...] = m
    nv_ref[...] = v


def _adamw(w, g, m, v, *, name):
    shape = w.shape
    c = shape[-1]
    r = w.size // c
    tr = r
    for cand in (512, 256, 128, 64, 32, 16, 8):
        if r % cand == 0 and cand * c * 4 <= 2 * 1024 * 1024:
            tr = cand
            break
    blk = pl.BlockSpec((tr, c), lambda i: (i, 0))
    outs = pl.pallas_call(
        functools.partial(_adamw_kernel),
        name=name,
        grid=(r // tr,),
        in_specs=[blk] * 4,
        out_specs=[blk] * 3,
        out_shape=[jax.ShapeDtypeStruct((r, c), F32)] * 3,
        compiler_params=_cparams(("parallel",)),
    )(w.reshape(r, c), g.reshape(r, c), m.reshape(r, c), v.reshape(r, c))
    return tuple(o.reshape(shape) for o in outs)


def _place():
    return lax.axis_index("x"), lax.axis_index("y"), lax.axis_index("c")


def _other_chips(x, y):
    return [(1 - x, y), (x, 1 - y), (1 - x, 1 - y)]


class _Geom:
    def __init__(self, kind, rows, cols):
        self.kind, self.rows, self.cols, self.hr = kind, rows, cols, rows // 2
        self.full_shape = {"rows": (4 * rows, cols), "cols": (rows, 4 * cols), "chips": (4, rows, cols)}[kind]
        self.nt = self.hr // ADD_TILE

    def view(self, ref, s, h):
        if self.kind == "rows":
            return ref.at[pl.ds(s * self.rows + h * self.hr, self.hr), :]
        if self.kind == "cols":
            return ref.at[pl.ds(h * self.hr, self.hr), pl.ds(s * self.cols, self.cols)]
        return ref.at[s, pl.ds(h * self.hr, self.hr), :]

    def shard_half(self, ref, h):
        return ref.at[pl.ds(h * self.hr, self.hr), :]

    def rows_view(self, ref, s, r0, nr):
        if self.kind == "rows":
            return ref.at[pl.ds(s * self.rows + r0, nr), :]
        if self.kind == "cols":
            return ref.at[pl.ds(r0, nr), pl.ds(s * self.cols, self.cols)]
        return ref.at[s, pl.ds(r0, nr), :]

    def tile_spec(self, chip_half_of):
        if self.kind == "rows":
            def index(*a):
                s, h, i = chip_half_of(*a)
                return (s * (self.rows // ADD_TILE) + h * self.nt + i, 0)
            return pl.BlockSpec((ADD_TILE, self.cols), index)
        if self.kind == "cols":
            def index(*a):
                s, h, i = chip_half_of(*a)
                return (h * self.nt + i, s)
            return pl.BlockSpec((ADD_TILE, self.cols), index)

        def index(*a):
            s, h, i = chip_half_of(*a)
            return (s, h * self.nt + i, 0)
        return pl.BlockSpec((None, ADD_TILE, self.cols), index)


GEOMS = (("w_mem_kv0", _Geom("rows", 256, 1024)), ("w_mem_kv1", _Geom("rows", 256, 1024)),
         ("w_out0", _Geom("rows", 512, 1024)), ("w_out1", _Geom("rows", 512, 1024)),
         ("conv_w_in", _Geom("cols", 1024, 1408)), ("mla_w_ukv", _Geom("cols", 256, 768)),
         ("mla_w_in", _Geom("chips", 1024, 848)), ("mla_w_uq", _Geom("chips", 512, 576)))
N_BIG = len(GEOMS)


def _remote(k, src_ref, dst_ref, to, ssem, rsem):
    return pltpu.make_async_remote_copy(src_ref=src_ref, dst_ref=dst_ref, send_sem=ssem.at[k], recv_sem=rsem.at[k],
                                        device_id=to, device_id_type=MESH)


GROUP_A = ("w_mem_kv0", "w_out0", "conv_w_in")
GROUP_B = ("w_mem_kv1", "w_out1", "mla_w_ukv", "mla_w_in", "mla_w_uq")
GATHER_0 = ("conv_w_in",)
GATHER_1 = ("w_mem_kv0", "w_out0", "mla_w_ukv", "mla_w_uq")
GATHER_2 = ("w_mem_kv1", "mla_w_in")
GATHER_3 = ("w_out1",)
GEOM = dict(GEOMS)


def _gather_over_ici(geoms, srcs, outs, ssem, rsem, act, base=0):
    x, y, c = _place()
    s = 2 * x + y
    n = len(geoms)
    locals_ = []
    for w, g in enumerate(geoms):
        nr = g.rows // LOCAL_CHUNKS
        for q in range(LOCAL_CHUNKS):
            locals_.append(pltpu.make_async_copy(srcs[w].at[pl.ds(q * nr, nr), :], g.rows_view(outs[w], s, q * nr, nr),
                                                 ssem.at[base + 3 * n + LOCAL_CHUNKS * w + q]))
    sends = [_remote(base + 3 * w + j, g.shard_half(srcs[w], c), g.view(outs[w], s, c), (*chip, c), ssem, rsem)
             for w, g in enumerate(geoms) for j, chip in enumerate(_other_chips(x, y))]
    if act == "start":
        for cp in sends + locals_:
            cp.start()
        return
    for w, g in enumerate(geoms):
        for j, chip in enumerate(_other_chips(x, y)):
            blk = g.view(outs[w], 2 * chip[0] + chip[1], c)
            _remote(base + 3 * w + j, blk, blk, (*chip, c), ssem, rsem).wait_recv()
    for cp in sends:
        cp.wait_send()
    for cp in locals_:
        cp.wait()


def _forward_to_sibling(geoms, fulls, ssem, rsem, act, base=0):
    x, y, c = _place()
    for w, g in enumerate(geoms):
        for j, chip in enumerate(_other_chips(x, y)):
            s = 2 * chip[0] + chip[1]
            mine, theirs = g.view(fulls[w], s, c), g.view(fulls[w], s, 1 - c)
            if act == "start":
                _remote(base + 3 * w + j, mine, mine, (x, y, 1 - c), ssem, rsem).start()
            else:
                _remote(base + 3 * w + j, theirs, theirs, (x, y, 1 - c), ssem, rsem).wait_recv()
                _remote(base + 3 * w + j, mine, mine, (x, y, 1 - c), ssem, rsem).wait_send()


def _gather_comm(forward_names, fulls, ici_names, shards):
    fwd = [GEOM[nm] for nm in forward_names]
    ici = [GEOM[nm] for nm in ici_names]
    nf, base = len(fwd), 3 * len(fwd)

    def run(act):
        def go(i, o, ss, rs):
            _forward_to_sibling(fwd, o[:nf], ss, rs, act)
            _gather_over_ici(ici, i[nf:], o[nf:], ss, rs, act, base)
        return go

    return _Comm(list(fulls) + list(shards),
                 [jax.ShapeDtypeStruct(f.shape, f.dtype) for f in fulls]
                 + [jax.ShapeDtypeStruct(g.full_shape, BF16) for g in ici],
                 base + (3 + LOCAL_CHUNKS) * len(ici), run("start"), run("wait"), aliases={w: w for w in range(nf)})


def _allgather_kernel(*refs, geoms):
    n = len(geoms)
    srcs, outs, (ssem, rsem, fsem, gsem) = refs[:n], refs[n:2 * n], refs[2 * n:]
    _gather_over_ici(geoms, srcs, outs, ssem, rsem, "start")
    _gather_over_ici(geoms, srcs, outs, ssem, rsem, "wait")
    _forward_to_sibling(geoms, outs, fsem, gsem, "start")
    _forward_to_sibling(geoms, outs, fsem, gsem, "wait")


def _allgather_weights(names, shards):
    geoms = [GEOM[nm] for nm in names]
    n = len(geoms)
    return pl.pallas_call(
        functools.partial(_allgather_kernel, geoms=geoms),
        name="allgather_weights",
        in_specs=[HBM] * n,
        out_specs=[HBM] * n,
        out_shape=[jax.ShapeDtypeStruct(g.full_shape, BF16) for g in geoms],
        scratch_shapes=[pltpu.SemaphoreType.DMA(((3 + LOCAL_CHUNKS) * n,)), pltpu.SemaphoreType.DMA((3 * n,)),
                        pltpu.SemaphoreType.DMA((3 * n,)), pltpu.SemaphoreType.DMA((3 * n,))],
    )(*shards)


def _swap_halves_with_sibling(geoms, srcs, dsts, ssem, rsem, act):
    x, y, c = _place()
    for w, g in enumerate(geoms):
        for s in range(4):
            cp = _remote(4 * w + s, g.view(srcs[w], s, 1 - c), dsts[w].at[s], (x, y, 1 - c), ssem, rsem)
            if act == "start":
                cp.start()
            else:
                cp.wait()


def _swap_comm(names, gb):
    geoms = [GEOM[nm] for nm in names]
    return _Comm(gb, [jax.ShapeDtypeStruct((4, g.hr, g.cols), BF16) for g in geoms], 4 * len(geoms),
                 lambda i, o, ss, rs: _swap_halves_with_sibling(geoms, i, o, ss, rs, "start"),
                 lambda i, o, ss, rs: _swap_halves_with_sibling(geoms, i, o, ss, rs, "wait"))


def _swap_halves_kernel(*refs, geoms):
    n = len(geoms)
    srcs, dsts, (ssem, rsem) = refs[:n], refs[n:2 * n], refs[2 * n:]
    _swap_halves_with_sibling(geoms, srcs, dsts, ssem, rsem, "start")
    _swap_halves_with_sibling(geoms, srcs, dsts, ssem, rsem, "wait")


def _swap_halves(names, gb, *, name):
    geoms = [GEOM[nm] for nm in names]
    n = len(geoms)
    return pl.pallas_call(
        functools.partial(_swap_halves_kernel, geoms=geoms),
        name=name,
        in_specs=[HBM] * n,
        out_specs=[HBM] * n,
        out_shape=[jax.ShapeDtypeStruct((4, g.hr, g.cols), BF16) for g in geoms],
        scratch_shapes=[pltpu.SemaphoreType.DMA((4 * n,)), pltpu.SemaphoreType.DMA((4 * n,))],
    )(*gb)


def _exchange_with_chips(srcs, dsts, ssem, rsem, act):
    x, y, c = _place()
    for w in range(len(srcs)):
        for j, chip in enumerate(_other_chips(x, y)):
            cp = _remote(3 * w + j, srcs[w].at[2 * chip[0] + chip[1]], dsts[w].at[j], (*chip, c), ssem, rsem)
            if act == "start":
                cp.start()
            else:
                cp.wait()


def _exchange_comm(pairs):
    return _Comm(pairs, [jax.ShapeDtypeStruct((3,) + p.shape[1:], p.dtype) for p in pairs], 3 * len(pairs),
                 lambda i, o, ss, rs: _exchange_with_chips(i, o, ss, rs, "start"),
                 lambda i, o, ss, rs: _exchange_with_chips(i, o, ss, rs, "wait"))


def _share_kernel(*refs):
    srcs, dsts, (ssem, rsem) = refs[:N_BIG], refs[N_BIG:2 * N_BIG], refs[2 * N_BIG:]
    x, y, c = _place()
    cps = [_remote(w, srcs[w], dsts[w], (x, y, 1 - c), ssem, rsem) for w in range(N_BIG)]
    for cp in cps:
        cp.start()
    for cp in cps:
        cp.wait()


def _share_with_sibling(halves):
    return pl.pallas_call(
        functools.partial(_share_kernel),
        name="rs_share_halves",
        in_specs=[HBM] * N_BIG,
        out_specs=[HBM] * N_BIG,
        out_shape=[jax.ShapeDtypeStruct(h.shape, h.dtype) for h in halves],
        scratch_shapes=[pltpu.SemaphoreType.DMA((N_BIG,)), pltpu.SemaphoreType.DMA((N_BIG,))],
    )(*halves)


def _gather_sum_kernel(src, gat, tot, ssem, rsem):
    x, y, c = _place()
    me = 4 * x + 2 * y + c
    gat[me] = src[...]
    flips = [(dx, dy, dc) for dx in (0, 1) for dy in (0, 1) for dc in (0, 1)][1:]
    cps = []
    for k, (dx, dy, dc) in enumerate(flips):
        peer = (1 - x if dx else x, 1 - y if dy else y, 1 - c if dc else c)
        cp = pltpu.make_async_remote_copy(src_ref=src, dst_ref=gat.at[me], send_sem=ssem.at[k], recv_sem=rsem.at[k],
                                          device_id=peer, device_id_type=MESH)
        cp.start()
        cps.append((cp, 4 * peer[0] + 2 * peer[1] + peer[2], peer))
    for k, (cp, idx, peer) in enumerate(cps):
        pltpu.make_async_remote_copy(src_ref=src, dst_ref=gat.at[idx], send_sem=ssem.at[k], recv_sem=rsem.at[k],
                                     device_id=peer, device_id_type=MESH).wait_recv()
    for cp, _, _ in cps:
        cp.wait_send()
    acc = gat[0]
    for d in range(1, 8):
        acc = acc + gat[d]
    tot[...] = acc


def _gather_sum_small(a, *, name):
    vm = pl.BlockSpec(memory_space=pltpu.VMEM)
    return pl.pallas_call(
        functools.partial(_gather_sum_kernel),
        name=name,
        in_specs=[vm],
        out_specs=[vm, vm],
        out_shape=[jax.ShapeDtypeStruct((8,) + a.shape, a.dtype), jax.ShapeDtypeStruct(a.shape, a.dtype)],
        scratch_shapes=[pltpu.SemaphoreType.DMA((7,)), pltpu.SemaphoreType.DMA((7,))],
    )(a)


def _add_pairs_kernel(c_ref, g_ref, r_ref, o_ref):
    o_ref[...] = (g_ref[...] + r_ref[...].astype(F32)).astype(o_ref.dtype)


def _add_pairs(geom, core, g, recv, *, name):
    half = pl.BlockSpec((None, ADD_TILE, geom.cols), lambda s, i, cr: (s, i, 0))
    return pl.pallas_call(
        functools.partial(_add_pairs_kernel),
        name=name,
        grid_spec=pltpu.PrefetchScalarGridSpec(
            num_scalar_prefetch=1,
            grid=(4, geom.nt),
            in_specs=[geom.tile_spec(lambda s, i, cr: (s, cr[0], i)), half],
            out_specs=half,
        ),
        out_shape=jax.ShapeDtypeStruct(recv.shape, BF16),
        compiler_params=_cparams(("parallel", "parallel")),
    )(core, g, recv)


def _add_final_kernel(sc_ref, g_ref, r_ref, e_ref, o_ref):
    acc = g_ref[...] + r_ref[...].astype(F32)
    for j in range(3):
        acc = acc + e_ref[j].astype(F32)
    o_ref[...] = acc


def _add_final(geom, chip_core, g, recv, exch, *, name):
    return pl.pallas_call(
        functools.partial(_add_final_kernel),
        name=name,
        grid_spec=pltpu.PrefetchScalarGridSpec(
            num_scalar_prefetch=1,
            grid=(geom.nt,),
            in_specs=[geom.tile_spec(lambda i, sc: (sc[0], sc[1], i)),
                      pl.BlockSpec((None, ADD_TILE, geom.cols), lambda i, sc: (sc[0], i, 0)),
                      pl.BlockSpec((3, ADD_TILE, geom.cols), lambda i, sc: (0, i, 0))],
            out_specs=pl.BlockSpec((ADD_TILE, geom.cols), lambda i, sc: (i, 0)),
        ),
        out_shape=jax.ShapeDtypeStruct((geom.hr, geom.cols), F32),
        compiler_params=_cparams(("parallel",)),
    )(chip_core, g, recv, exch)


def _chip_major(w):
    return w.reshape(w.shape[0], 4, w.shape[1] // 4).transpose(1, 0, 2)


def _from_chip_major(w):
    return w.transpose(1, 0, 2).reshape(w.shape[1], 4 * w.shape[2])


def _mla_in_to_internal(w):
    return jnp.concatenate([w[:, 1344:], w[:, :512], w[:, 832:1344], w[:, 512:768], w[:, 768:832],
                            jnp.zeros((w.shape[0], 64), w.dtype)], axis=1)


def _mla_in_from_internal(w):
    return jnp.concatenate([w[:, P1_CQ:P1_QM], w[:, P1_CKV:P1_KR], w[:, P1_KR:P1_KR + 64], w[:, P1_QM:P1_CKV],
                            w[:, :MIX]], axis=1)


def _uq_to_internal(w):
    w = w.reshape(w.shape[0], MLA_HEADS, HEAD + ROPE)
    return jnp.pad(w, ((0, 0), (0, 0), (0, QK_PAD - HEAD - ROPE))).reshape(w.shape[0], MLA_HEADS * QK_PAD)


def _uq_from_internal(w):
    return w.reshape(w.shape[0], MLA_HEADS, QK_PAD)[:, :, :HEAD + ROPE].reshape(w.shape[0], MLA_HEADS * (HEAD + ROPE))


def _rows128(a, rows):
    flat = a.reshape(-1)
    return jnp.pad(flat, (0, rows * 128 - flat.shape[0])).reshape(rows, 128)


def kernel(x, mem, positions, norm_g, mem_norm_g, w_mem_kv, w_out, conv_w_in, conv_dw, conv_dw_b, conv_ln_g, conv_ln_b, mla_w_in, mla_q_norm_g, mla_w_uq, mla_kv_norm_g, mla_w_ukv, final_norm_g, loss_target, m_norm_g, m_mem_norm_g, m_w_mem_kv, m_w_out, m_conv_w_in, m_conv_dw, m_conv_dw_b, m_conv_ln_g, m_conv_ln_b, m_mla_w_in, m_mla_q_norm_g, m_mla_w_uq, m_mla_kv_norm_g, m_mla_w_ukv, m_final_norm_g, v_norm_g, v_mem_norm_g, v_w_mem_kv, v_w_out, v_conv_w_in, v_conv_dw, v_conv_dw_b, v_conv_ln_g, v_conv_ln_b, v_mla_w_in, v_mla_q_norm_g, v_mla_w_uq, v_mla_kv_norm_g, v_mla_w_ukv, v_final_norm_g):
    bsz, seq, d = x.shape
    n = bsz * seq
    mlen = mem.shape[1]
    ax, ay, ac = _place()
    chip = 2 * ax + ay

    shards = dict(w_mem_kv0=w_mem_kv[0], w_mem_kv1=w_mem_kv[1], w_out0=w_out[0], w_out1=w_out[1],
                  conv_w_in=conv_w_in[0], mla_w_ukv=mla_w_ukv[0], mla_w_in=mla_w_in[0], mla_w_uq=mla_w_uq[0])
    bshard = {nm: sh.astype(BF16) for nm, sh in shards.items()}
    wf = dict(zip(GATHER_0, _allgather_weights(GATHER_0, [bshard[nm] for nm in GATHER_0])))
    w_conv_in = wf["conv_w_in"]
    w_conv_in_t = w_conv_in.T

    small_in = jnp.concatenate([_rows128(conv_dw[0], 96), _rows128(mla_q_norm_g, 8), _rows128(mla_kv_norm_g, 8)],
                               axis=0)
    small_all, _ = _gather_sum_small(small_in, name="gather_small_params")
    small_all = small_all[0::2]
    dw_full = small_all[:, :93].reshape(4, -1)[:, :CONV_K * 384].reshape(4, CONV_K, 384)
    dw_full = dw_full.transpose(1, 0, 2).reshape(CONV_K, MAIN)
    qg_full = small_all[:, 96].reshape(Q_RANK)
    kvg_full = small_all[:, 104, :64].reshape(KV_RANK)

    inv_freq = 1.0 / (ROPE_THETA ** (jnp.arange(0, ROPE, 2, dtype=F32) / ROPE))
    ang = positions.astype(F32).reshape(n, 1) * inv_freq
    cos, sin, zer = jnp.cos(ang), jnp.sin(ang), jnp.zeros((n, 64), F32)
    rope_c = jnp.concatenate([cos, cos, zer], axis=1)
    rope_s = jnp.concatenate([-sin, sin, zer], axis=1)

    x2 = x.reshape(n, d)
    mem2 = mem.reshape(bsz * mlen, d)
    tgt2 = loss_target.reshape(n, d)

    memn = [_rms_fwd(mem2, mem_norm_g[i], name=f"mem_norm{i}") for i in range(2)]
    u0 = _rms_fwd(x2, norm_g[0], name="norm0")
    p0, landed1 = _matmul(u0, w_conv_in, name="conv_in_proj",
                          comm=_gather_comm((), (), GATHER_1, [bshard[nm] for nm in GATHER_1]))
    hc, carried = _conv_fwd(p0, dw_full, conv_dw_b, bsz, seq,
                            comm=_gather_comm(GATHER_1, landed1, GATHER_2, [bshard[nm] for nm in GATHER_2]))
    hc = hc.reshape(n, MAIN)
    wf.update(zip(GATHER_1, carried[:len(GATHER_1)]))
    kvm = [_matmul(memn[0], wf["w_mem_kv0"], out_dtype=BF16, name="mem_kv0").reshape(bsz, mlen, 2 * MEMW)]
    ymem0 = _mem_fwd(p0, kvm[0], P0_QM // MEMW, bsz, seq, name="mem_attn0").reshape(n, MEMW)
    y0 = _gate_fwd(hc, ymem0, p0, P0_Z // GATE_CHUNK, conv_ln_g, conv_ln_b, name="gate0")
    h1, gathered2 = _matmul(y0, wf["w_out0"], x2, name="out_proj0",
                            comm=_gather_comm(GATHER_2, carried[len(GATHER_1):], (), ()))
    wf.update(zip(GATHER_2, gathered2))
    w_mla_in = _mla_in_to_internal(_from_chip_major(wf["mla_w_in"]))
    w_uq = _uq_to_internal(_from_chip_major(wf["mla_w_uq"]))
    w_ukv = wf["mla_w_ukv"]
    w_mla_in_t, w_uq_t, w_ukv_t = w_mla_in.T, w_uq.T, w_ukv.T

    u1 = _rms_fwd(h1, norm_g[1], name="norm1")
    p1, landed3 = _matmul(u1, w_mla_in, name="mla_in_proj",
                          comm=_gather_comm((), (), GATHER_3, [bshard[nm] for nm in GATHER_3]))
    kvm1, gathered3 = _matmul(memn[1], wf["w_mem_kv1"], out_dtype=BF16, name="mem_kv1",
                              comm=_gather_comm(GATHER_3, landed3, (), ()))
    kvm.append(kvm1.reshape(bsz, mlen, 2 * MEMW))
    wf.update(zip(GATHER_3, gathered3))
    w_memkv = [wf["w_mem_kv0"], wf["w_mem_kv1"]]
    w_o = [wf["w_out0"], wf["w_out1"]]
    w_memkv_t = [w.T for w in w_memkv]
    w_o_t = [w.T for w in w_o]
    cqn = _rms_fwd(p1, qg_full, width=Q_RANK, col_block=P1_CQ // Q_RANK, name="q_norm")
    ckvn = _rms_fwd(p1, kvg_full, width=KV_RANK, col_block=P1_CKV // KV_RANK, name="kv_norm")
    qb = _q_up(cqn, w_uq, rope_c, rope_s)
    kb, vb = _kv_up(ckvn, w_ukv, p1, rope_c, rope_s)
    o1, lse = _attn_fwd(qb, kb, vb, bsz, seq)
    o1 = o1.reshape(n, MAIN)
    ymem1 = _mem_fwd(p1, kvm[1], P1_QM // MEMW, bsz, seq, name="mem_attn1").reshape(n, MEMW)
    y1 = _gate_fwd(o1, ymem1, p1, P1_Z // GATE_CHUNK, name="gate1")
    h2 = _matmul(y1, w_o[1], h1, name="out_proj1")

    dh2, d_final_g, loss_part = _final_loss(h2, final_norm_g, tgt2)
    loss = lax.psum(loss_part[0, 0], ("x", "y", "c"))

    gbig = {}
    gbig["w_out1"] = _matmul(y1, dh2, ta=True, also_bf16=True, name="d_w_out1")
    dy1 = _matmul(dh2, w_o_t[1], name="d_y1")
    do1, dymem1, dz1 = _gate_bwd(dy1, o1, ymem1, p1, P1_Z // GATE_CHUNK, name="gate1_bwd")
    dqm1, dkvm1 = _mem_bwd(p1, kvm[1], dymem1, P1_QM // MEMW, bsz, seq, name="mem_attn1_bwd")
    dqb, dkv, dkrr = _attn_bwd(qb, kb, vb, o1.reshape(bsz, seq, MAIN), do1, lse, rope_c, rope_s, bsz, seq)
    dkr = _kr_bwd(dkrr, rope_c, rope_s)
    g_w_uq = _matmul(cqn, dqb, ta=True, name="d_w_uq")
    dcqn = _matmul(dqb, w_uq_t, name="d_cqn")
    gbig["mla_w_ukv"] = _matmul(ckvn, dkv, ta=True, also_bf16=True, name="d_w_ukv")
    dckvn = _matmul(dkv, w_ukv_t, name="d_ckvn")
    dcq, g_qg = _rms_bwd(p1, qg_full, dcqn, width=Q_RANK, col_block=P1_CQ // Q_RANK, out_dtype=BF16,
                         name="q_norm_bwd")
    dckv, g_kvg = _rms_bwd(p1, kvg_full, dckvn, width=KV_RANK, col_block=P1_CKV // KV_RANK, out_dtype=BF16,
                           name="kv_norm_bwd")
    dp1 = jnp.concatenate([dz1, dcq, dqm1.reshape(n, MEMW), dckv, dkr], axis=1)
    g_w_mla_in = _matmul(u1, dp1, ta=True, name="d_w_mla_in")
    du1 = _matmul(dp1, w_mla_in_t, name="d_u1")
    dh1, g_norm1 = _rms_bwd(h1, norm_g[1], du1, dh2, name="norm1_bwd")

    def mem_kv_bwd(i, dkvm):
        dk2 = dkvm.reshape(bsz * mlen, 2 * MEMW)
        gbig[f"w_mem_kv{i}"] = _matmul(memn[i], dk2, ta=True, also_bf16=True, name=f"d_w_mem_kv{i}")
        dmemn = _matmul(dk2, w_memkv_t[i], name=f"d_memn{i}")
        return _rms_bwd(mem2, mem_norm_g[i], dmemn, name=f"mem_norm{i}_bwd")[1]

    g_mem_g1 = mem_kv_bwd(1, dkvm1)
    for nm, g_int, back in (("mla_w_in", g_w_mla_in, _mla_in_from_internal), ("mla_w_uq", g_w_uq, _uq_from_internal)):
        g_cm = _chip_major(back(g_int))
        gbig[nm] = (g_cm, g_cm.astype(BF16))

    core = jnp.reshape(ac, (1,)).astype(jnp.int32)
    chip_core = jnp.stack([chip, ac]).astype(jnp.int32)

    def add_pairs(names, from_sib):
        return [_add_pairs(GEOM[nm], core, gbig[nm][0], fs, name=f"rs_add_pairs_{nm}")
                for nm, fs in zip(names, from_sib)]

    def pair_sums(names, tag):
        from_sib = _swap_halves(names, [gbig[nm][1] for nm in names], name=f"rs_sibling_swap_{tag}")
        return from_sib, add_pairs(names, from_sib)

    gbig["w_out0"], from_sib_b = _matmul(y0, dh1, ta=True, also_bf16=True, name="d_w_out0",
                                         comm=_swap_comm(GROUP_B, [gbig[nm][1] for nm in GROUP_B]))
    pairs_b = add_pairs(GROUP_B, from_sib_b)
    dy0 = _matmul(dh1, w_o_t[0], name="d_y0")
    dhc, dymem0, dz0, g_ln_g, g_ln_b = _gate_bwd(dy0, hc, ymem0, p0, P0_Z // GATE_CHUNK, conv_ln_g, conv_ln_b,
                                                 name="gate0_bwd")
    dqm0, dkvm0 = _mem_bwd(p0, kvm[0], dymem0, P0_QM // MEMW, bsz, seq, name="mem_attn0_bwd")
    (da, dg, g_dw32), exch_b = _conv_bwd(p0, dhc.reshape(bsz, seq, MAIN), dw_full, bsz, seq,
                                         comm=_exchange_comm(pairs_b))
    dp0 = [da.reshape(n, MAIN), dg.reshape(n, MAIN), dqm0.reshape(n, MEMW), dz0]
    gbig["conv_w_in"] = _matmul(u0, dp0, ta=True, also_bf16=True, name="d_w_conv_in")
    g_mem_g = [mem_kv_bwd(0, dkvm0), g_mem_g1]

    from_sib_a, pairs_a = pair_sums(GROUP_A, "a")
    du0, exch_a = _matmul(dp0, w_conv_in_t, name="d_u0", comm=_exchange_comm(pairs_a))
    grad_x, g_norm0 = _rms_bwd(x2, norm_g[0], du0, dh1, name="norm0_bwd")

    from_sibling = dict(zip(GROUP_A + GROUP_B, list(from_sib_a) + list(from_sib_b)))
    exch = dict(zip(GROUP_A + GROUP_B, list(exch_a) + list(exch_b)))
    mine = [_add_final(g, chip_core, gbig[nm][0], from_sibling[nm], exch[nm], name=f"rs_add_final_{nm}")
            for nm, g in GEOMS]
    theirs = _share_with_sibling(mine)
    red = {}
    for (nm, _), a, b in zip(GEOMS, mine, theirs):
        red[nm] = jnp.concatenate([jnp.where(ac == 0, a, b), jnp.where(ac == 0, b, a)], axis=0)
    red["w_mem_kv"] = jnp.stack([red["w_mem_kv0"], red["w_mem_kv1"]])
    red["w_out"] = jnp.stack([red["w_out0"], red["w_out1"]])

    small_g = jnp.concatenate([
        _rows128(jnp.concatenate([g_norm0, g_norm1], axis=0), 16), _rows128(jnp.concatenate(g_mem_g, axis=0), 16),
        _rows128(g_dw32[CONV_K], 16), _rows128(g_ln_g, 16), _rows128(g_ln_b, 16), _rows128(d_final_g, 8),
        _rows128(g_dw32[:CONV_K], 376), _rows128(g_qg, 8), _rows128(g_kvg, 8)], axis=0)
    _, small_sum = _gather_sum_small(small_g, name="allreduce_small_grads")
    flat = small_sum.reshape(-1)

    def take(off, shape):
        size = 1
        for s_ in shape:
            size *= s_
        return flat[off * 128:off * 128 + size].reshape(shape)

    grads = dict(red)
    grads["conv_w_in"] = red["conv_w_in"][None]
    grads["mla_w_in"] = red["mla_w_in"][None]
    grads["mla_w_uq"] = red["mla_w_uq"][None]
    grads["mla_w_ukv"] = red["mla_w_ukv"][None]
    grads["norm_g"] = take(0, (2, D_MODEL))
    grads["mem_norm_g"] = take(16, (2, D_MODEL))
    grads["conv_dw_b"] = take(32, (1, MAIN))
    grads["conv_ln_g"] = take(48, (1, MAIN))
    grads["conv_ln_b"] = take(64, (1, MAIN))
    grads["final_norm_g"] = take(80, (D_MODEL,))
    grads["conv_dw"] = lax.dynamic_slice_in_dim(take(88, (CONV_K, MAIN)), chip * 384, 384, axis=1)[None]
    grads["mla_q_norm_g"] = lax.dynamic_slice_in_dim(take(464, (Q_RANK,)), chip * 128, 128, axis=0)[None]
    grads["mla_kv_norm_g"] = lax.dynamic_slice_in_dim(take(472, (KV_RANK,)), chip * 64, 64, axis=0)[None]

    params = dict(norm_g=norm_g, mem_norm_g=mem_norm_g, w_mem_kv=w_mem_kv, w_out=w_out, conv_w_in=conv_w_in,
                  conv_dw=conv_dw, conv_dw_b=conv_dw_b, conv_ln_g=conv_ln_g, conv_ln_b=conv_ln_b, mla_w_in=mla_w_in,
                  mla_q_norm_g=mla_q_norm_g, mla_w_uq=mla_w_uq, mla_kv_norm_g=mla_kv_norm_g, mla_w_ukv=mla_w_ukv,
                  final_norm_g=final_norm_g)
    mom1 = dict(norm_g=m_norm_g, mem_norm_g=m_mem_norm_g, w_mem_kv=m_w_mem_kv, w_out=m_w_out, conv_w_in=m_conv_w_in,
                conv_dw=m_conv_dw, conv_dw_b=m_conv_dw_b, conv_ln_g=m_conv_ln_g, conv_ln_b=m_conv_ln_b,
                mla_w_in=m_mla_w_in, mla_q_norm_g=m_mla_q_norm_g, mla_w_uq=m_mla_w_uq,
                mla_kv_norm_g=m_mla_kv_norm_g, mla_w_ukv=m_mla_w_ukv, final_norm_g=m_final_norm_g)
    mom2 = dict(norm_g=v_norm_g, mem_norm_g=v_mem_norm_g, w_mem_kv=v_w_mem_kv, w_out=v_w_out, conv_w_in=v_conv_w_in,
                conv_dw=v_conv_dw, conv_dw_b=v_conv_dw_b, conv_ln_g=v_conv_ln_g, conv_ln_b=v_conv_ln_b,
                mla_w_in=v_mla_w_in, mla_q_norm_g=v_mla_q_norm_g, mla_w_uq=v_mla_w_uq,
                mla_kv_norm_g=v_mla_kv_norm_g, mla_w_ukv=v_mla_w_ukv, final_norm_g=v_final_norm_g)
    names = list(params)
    g_out, deltas, new_m, new_v = [], [], [], []
    for nm in names:
        w = params[nm]
        g = grads[nm].reshape(w.shape)
        w2 = w.reshape(1, -1) if w.ndim == 1 else w
        dlt, m_new, v_new = _adamw(w2, g.reshape(w2.shape), mom1[nm].reshape(w2.shape), mom2[nm].reshape(w2.shape),
                                   name=f"adamw_{nm}")
        g_out.append(g)
        deltas.append(dlt.reshape(w.shape))
        new_m.append(m_new.reshape(w.shape))
        new_v.append(v_new.reshape(w.shape))

    return (loss, grad_x.reshape(bsz, seq, d), *g_out, *deltas, *new_m, *new_v)
```

```python
import functools

import jax
import jax.numpy as jnp
from jax import lax
from jax.experimental import pallas as pl
from jax.experimental.pallas import tpu as pltpu

F32 = jnp.float32
BF16 = jnp.bfloat16
MESH = pl.DeviceIdType.MESH

D_MODEL = 1024
MIX = 2048
MAIN = 1536
MEMW = 512
MEM_HEADS = 4
HEAD = 128
CONV_K = 31
CONV_PAD = 32
MLA_HEADS = 12
ROPE = 64
QK_PAD = 256
Q_RANK = 512
KV_RANK = 256
ROPE_THETA = 10000.0
RMS_EPS = 1e-6
LN_EPS = 1e-5
MEM_SCALE = HEAD ** -0.5
MLA_SCALE = (HEAD + ROPE) ** -0.5
NEG = -1e30

P0_COLS = 5632
P0_A, P0_G, P0_QM, P0_Z = 0, 1536, 3072, 3584
P1_COLS = 3456
P1_Z, P1_CQ, P1_QM, P1_CKV, P1_KR = 0, 2048, 2560, 3072, 3328

ADAM_LR = 0.001
ADAM_B1 = 0.9
ADAM_B2 = 0.999
ADAM_EPS = 1e-08
ADAM_WD = 0.01
ADAM_STEP = 10

VMEM_LIMIT = 56 * 1024 * 1024

ADD_TILE = 128
LOCAL_CHUNKS = 4


def _cparams(sem=None):
    return pltpu.CompilerParams(dimension_semantics=sem, vmem_limit_bytes=VMEM_LIMIT)


HBM = pl.BlockSpec(memory_space=pl.ANY)


class _Comm:
    def __init__(self, ins, out_shapes, n_sem, start, finish, aliases=None):
        self.ins, self.out_shapes, self.n_sem = list(ins), list(out_shapes), n_sem
        self.start, self.finish, self.aliases = start, finish, dict(aliases or {})


def _call(kernel_fn, comm, *, name, grid, in_specs, out_specs, out_shape, scratch_shapes, semantics, args):
    if comm is None:
        outs = pl.pallas_call(kernel_fn, name=name, grid=grid, in_specs=in_specs, out_specs=out_specs,
                              out_shape=out_shape, scratch_shapes=scratch_shapes,
                              compiler_params=_cparams(semantics))(*args)
        return list(outs), []
    n_in, n_out, nci, nco = len(in_specs), len(out_shape), len(comm.ins), len(comm.out_shapes)

    def body(*refs):
        ins, cins = refs[:n_in], refs[n_in:n_in + nci]
        outs = refs[n_in + nci:n_in + nci + n_out]
        couts = refs[n_in + nci + n_out:n_in + nci + n_out + nco]
        scratch, (ssem, rsem) = refs[n_in + nci + n_out + nco:-2], refs[-2:]
        ids = [pl.program_id(ax) for ax in range(len(grid))]
        first = functools.reduce(jnp.logical_and, [i == 0 for i in ids])
        last = functools.reduce(jnp.logical_and, [i == g - 1 for i, g in zip(ids, grid)])

        @pl.when(first)
        def _():
            comm.start(cins, couts, ssem, rsem)

        kernel_fn(*ins, *outs, *scratch)

        @pl.when(last)
        def _():
            comm.finish(cins, couts, ssem, rsem)

    outs = pl.pallas_call(
        body, name=name, grid=grid,
        in_specs=list(in_specs) + [HBM] * nci,
        out_specs=list(out_specs) + [HBM] * nco,
        out_shape=list(out_shape) + comm.out_shapes,
        scratch_shapes=list(scratch_shapes) + [pltpu.SemaphoreType.DMA((comm.n_sem,))] * 2,
        input_output_aliases={n_in + i: n_out + o for i, o in comm.aliases.items()},
        compiler_params=_cparams(("arbitrary",) * len(grid)),
    )(*args, *comm.ins)
    return list(outs[:n_out]), list(outs[n_out:])


def _tile(n, pref):
    if n <= pref:
        return n
    t = (pref // 128) * 128
    while t > 128 and n % t:
        t -= 128
    assert n % t == 0, (n, pref)
    return t


def _mm_kernel(*refs, nk, ta, has_res, a_parts, b_parts):
    a_refs, b_refs = refs[:len(a_parts)], refs[len(a_parts):len(a_parts) + len(b_parts)]
    rest = refs[len(a_parts) + len(b_parts):]
    if has_res:
        r_ref, *o_refs, acc_ref = rest
    else:
        *o_refs, acc_ref = rest
    j, k = pl.program_id(1), pl.program_id(2)

    def finish(acc):
        if has_res:
            acc = r_ref[...] + acc
        for o in o_refs:
            o[...] = acc.astype(o.dtype)

    def step(a_ref, b_ref):
        dn = (((0 if ta else 1,), (0,)), ((), ()))
        p = lax.dot_general(a_ref[...].astype(BF16), b_ref[...].astype(BF16), dn, preferred_element_type=F32)
        if nk == 1:
            finish(p)
            return

        @pl.when(k == 0)
        def _():
            acc_ref[...] = p

        @pl.when(jnp.logical_and(k > 0, k < nk - 1))
        def _():
            acc_ref[...] += p

        @pl.when(k == nk - 1)
        def _():
            finish(acc_ref[...] + p)

    for a_ref, (k0, kn) in zip(a_refs, a_parts):
        for b_ref, (j0, jn) in zip(b_refs, b_parts):
            conds = []
            if len(a_parts) > 1:
                conds.append(jnp.logical_and(k >= k0, k < k0 + kn))
            if len(b_parts) > 1:
                conds.append(jnp.logical_and(j >= j0, j < j0 + jn))
            if conds:
                pl.when(functools.reduce(jnp.logical_and, conds))(functools.partial(step, a_ref, b_ref))
            else:
                step(a_ref, b_ref)


def _matmul(a, b, res=None, *, name, ta=False, out_dtype=F32, also_bf16=False, comm=None, tm=1024, tn=512, tk=2048):
    a_list = list(a) if isinstance(a, (list, tuple)) else [a]
    b_list = list(b) if isinstance(b, (list, tuple)) else [b]
    assert not (ta and len(a_list) > 1)
    m = a_list[0].shape[1] if ta else a_list[0].shape[0]
    kd = a_list[0].shape[0] if ta else sum(p.shape[1] for p in a_list)
    n = sum(p.shape[1] for p in b_list)
    assert all(p.shape[0] == kd for p in b_list)
    tm = _tile(m, tm)
    tn = 512 if len(b_list) > 1 else _tile(n, tn)
    tk = 512 if len(a_list) > 1 else _tile(kd, tk)
    nk = kd // tk

    def parts(widths, t):
        out, start = [], 0
        for w in widths:
            assert w % t == 0
            out.append((start, w // t))
            start += w // t
        return out

    a_parts = parts([p.shape[1] for p in a_list], tk) if len(a_list) > 1 else [(0, nk)]
    b_parts = parts([p.shape[1] for p in b_list], tn) if len(b_list) > 1 else [(0, n // tn)]
    if ta:
        a_specs = [pl.BlockSpec((tk, tm), lambda i, j, k: (k, i))]
    else:
        a_specs = [pl.BlockSpec((tm, tk), lambda i, j, k, k0=k0, kn=kn: (i, jnp.clip(k - k0, 0, kn - 1)))
                   for k0, kn in a_parts]
    b_specs = [pl.BlockSpec((tk, tn), lambda i, j, k, j0=j0, jn=jn: (
                   jnp.where(jnp.logical_and(j >= j0, j < j0 + jn), k, 0), jnp.clip(j - j0, 0, jn - 1)))
               for j0, jn in b_parts]
    out_spec = pl.BlockSpec((tm, tn), lambda i, j, k: (i, j))
    in_specs = a_specs + b_specs
    args = a_list + b_list
    if res is not None:
        in_specs.append(out_spec)
        args.append(res)
    out_shape = [jax.ShapeDtypeStruct((m, n), out_dtype)]
    if also_bf16:
        out_shape.append(jax.ShapeDtypeStruct((m, n), BF16))
    outs, carried = _call(
        functools.partial(_mm_kernel, nk=nk, ta=ta, has_res=res is not None, a_parts=a_parts, b_parts=b_parts), comm,
        name=name,
        grid=(m // tm, n // tn, nk),
        in_specs=in_specs,
        out_specs=[out_spec] * len(out_shape),
        out_shape=out_shape,
        scratch_shapes=[pltpu.VMEM((tm, tn), F32)],
        semantics=("parallel", "parallel", "arbitrary"),
        args=args)
    result = tuple(outs) if also_bf16 else outs[0]
    return result if comm is None else (result, carried)


def _rms_fwd_kernel(h_ref, g_ref, o_ref):
    h = h_ref[...]
    rstd = lax.rsqrt(jnp.mean(h * h, axis=-1, keepdims=True) + RMS_EPS)
    o_ref[...] = (h * rstd * g_ref[...]).astype(o_ref.dtype)


def _rms_fwd(h, g, *, name, width=None, col_block=0, tm=512):
    t = h.shape[0]
    width = width or h.shape[1]
    tm = _tile(t, tm)
    return pl.pallas_call(
        functools.partial(_rms_fwd_kernel),
        name=name,
        grid=(t // tm,),
        in_specs=[pl.BlockSpec((tm, width), lambda i: (i, col_block)),
                  pl.BlockSpec((1, width), lambda i: (0, 0))],
        out_specs=pl.BlockSpec((tm, width), lambda i: (i, 0)),
        out_shape=jax.ShapeDtypeStruct((t, width), BF16),
        compiler_params=_cparams(("parallel",)),
    )(h, g.reshape(1, width))


def _rms_bwd_math(h, g, du):
    rstd = lax.rsqrt(jnp.mean(h * h, axis=-1, keepdims=True) + RMS_EPS)
    dug = du * g
    dh = rstd * dug - h * (rstd * rstd * rstd) * jnp.mean(dug * h, axis=-1, keepdims=True)
    dg = jnp.sum(du * h * rstd, axis=0, keepdims=True)
    return dh, dg


def _rms_bwd_kernel(*refs, has_res, also_bf16):
    refs = list(refs)
    h_ref, g_ref, du_ref = refs[:3]
    res_ref = refs[3] if has_res else None
    dh_ref, dg_ref = refs[-2 - also_bf16], refs[-1]
    dh, dg = _rms_bwd_math(h_ref[...], g_ref[...], du_ref[...].astype(F32))
    if has_res:
        dh = dh + res_ref[...]
    dh_ref[...] = dh.astype(dh_ref.dtype)
    if also_bf16:
        refs[-2][...] = dh.astype(BF16)

    @pl.when(pl.program_id(0) == 0)
    def _():
        dg_ref[...] = dg

    @pl.when(pl.program_id(0) > 0)
    def _():
        dg_ref[...] += dg


def _rms_bwd(h, g, du, res=None, *, name, width=None, col_block=0, out_dtype=F32, also_bf16=False, tm=512):
    t = h.shape[0]
    width = width or h.shape[1]
    tm = _tile(t, tm)
    row = pl.BlockSpec((tm, width), lambda i: (i, 0))
    in_specs = [pl.BlockSpec((tm, width), lambda i: (i, col_block)),
                pl.BlockSpec((1, width), lambda i: (0, 0)), row]
    args = [h, g.reshape(1, width), du]
    if res is not None:
        in_specs.append(row)
        args.append(res)
    rows = [jax.ShapeDtypeStruct((t, width), out_dtype)] + [jax.ShapeDtypeStruct((t, width), BF16)] * also_bf16
    return pl.pallas_call(
        functools.partial(_rms_bwd_kernel, has_res=res is not None, also_bf16=int(also_bf16)),
        name=name,
        grid=(t // tm,),
        in_specs=in_specs,
        out_specs=[row] * len(rows) + [pl.BlockSpec((1, width), lambda i: (0, 0))],
        out_shape=rows + [jax.ShapeDtypeStruct((1, width), F32)],
        compiler_params=_cparams(("arbitrary",)),
    )(*args)


def _final_kernel(h_ref, g_ref, t_ref, dh_ref, dhb_ref, dg_ref, loss_ref):
    h = h_ref[...]
    g = g_ref[...]
    rstd = lax.rsqrt(jnp.mean(h * h, axis=-1, keepdims=True) + RMS_EPS)
    e = h * rstd * g - t_ref[...]
    part = 0.5 * jnp.sum(jnp.mean(e * e, axis=-1, keepdims=True), axis=0, keepdims=True)
    dh, dg = _rms_bwd_math(h, g, e * (1.0 / D_MODEL))
    dh_ref[...] = dh
    dhb_ref[...] = dh.astype(dhb_ref.dtype)
    part = jnp.broadcast_to(part, loss_ref.shape)

    @pl.when(pl.program_id(0) == 0)
    def _():
        dg_ref[...] = dg
        loss_ref[...] = part

    @pl.when(pl.program_id(0) > 0)
    def _():
        dg_ref[...] += dg
        loss_ref[...] += part


def _final_loss(h, g, target, *, tm=512):
    t, d = h.shape
    tm = _tile(t, tm)
    row = pl.BlockSpec((tm, d), lambda i: (i, 0))
    return pl.pallas_call(
        functools.partial(_final_kernel),
        name="final_loss",
        grid=(t // tm,),
        in_specs=[row, pl.BlockSpec((1, d), lambda i: (0, 0)), row],
        out_specs=[row, row, pl.BlockSpec((1, d), lambda i: (0, 0)), pl.BlockSpec((1, 128), lambda i: (0, 0))],
        out_shape=[jax.ShapeDtypeStruct((t, d), F32), jax.ShapeDtypeStruct((t, d), BF16),
                   jax.ShapeDtypeStruct((1, d), F32), jax.ShapeDtypeStruct((1, 128), F32)],
        compiler_params=_cparams(("arbitrary",)),
    )(h, g.reshape(1, d), target)


CONV_CT = 128
CONV_TC = 256
CONV_TC_BWD = 128


def _glu_into(pad_ref, a_ref, g_ref, seq, tc):
    ct = pad_ref.shape[1]
    pad_ref[0:CONV_PAD, :] = jnp.zeros((CONV_PAD, ct), F32)
    for r in range(0, seq, tc):
        a = a_ref[0, r:r + tc, :]
        g = g_ref[0, r:r + tc, :]
        pad_ref[CONV_PAD + r:CONV_PAD + r + tc, :] = a * jax.nn.sigmoid(g)


def _conv_fwd_kernel(a_ref, g_ref, dw_ref, dwb_ref, hc_ref, pad_ref, *, seq, tc):
    ct = pad_ref.shape[1]
    _glu_into(pad_ref, a_ref, g_ref, seq, tc)
    for r in range(0, seq, tc):
        acc = jnp.broadcast_to(dwb_ref[...], (tc, ct))
        for k in range(CONV_K):
            o = CONV_PAD + r - (CONV_K - 1) + k
            acc = acc + dw_ref[k:k + 1, :] * pad_ref[o:o + tc, :]
        hc_ref[0, r:r + tc, :] = acc


def _conv_fwd(p0, dw, dwb, bsz, seq, comm=None):
    ct = CONV_CT
    tc = min(CONV_TC, seq)
    p3 = p0.reshape(bsz, seq, P0_COLS)
    outs, carried = _call(
        functools.partial(_conv_fwd_kernel, seq=seq, tc=tc), comm,
        name="conv_fwd",
        grid=(MAIN // ct, bsz),
        in_specs=[pl.BlockSpec((1, seq, ct), lambda j, b: (b, 0, P0_A // ct + j)),
                  pl.BlockSpec((1, seq, ct), lambda j, b: (b, 0, P0_G // ct + j)),
                  pl.BlockSpec((CONV_K, ct), lambda j, b: (0, j)),
                  pl.BlockSpec((1, ct), lambda j, b: (0, j))],
        out_specs=[pl.BlockSpec((1, seq, ct), lambda j, b: (b, 0, j))],
        out_shape=[jax.ShapeDtypeStruct((bsz, seq, MAIN), F32)],
        scratch_shapes=[pltpu.VMEM((seq + CONV_PAD, ct), F32)],
        semantics=("parallel", "parallel"),
        args=[p3, p3, dw, dwb])
    return outs[0], carried


def _fold_rows(x):
    parts = [x[r:r + 8] for r in range(0, x.shape[0], 8)]
    while len(parts) > 1:
        parts = [a + b for a, b in zip(parts[::2], parts[1::2])]
    return parts[0]


def _conv_bwd_kernel(a_ref, g_ref, dhc_ref, dw_ref, da_ref, dg_ref, ddw_ref, pad_ref, dpad_ref, acc_ref,
                     *, seq, tc):
    ct = pad_ref.shape[1]
    b = pl.program_id(1)
    _glu_into(pad_ref, a_ref, g_ref, seq, tc)
    dpad_ref[seq:seq + CONV_PAD, :] = jnp.zeros((CONV_PAD, ct), F32)
    for r in range(0, seq, tc):
        dpad_ref[r:r + tc, :] = dhc_ref[0, r:r + tc, :]
    acc_ref[...] = jnp.zeros(acc_ref.shape, F32)
    for r in range(0, seq, tc):
        dh = dhc_ref[0, r:r + tc, :]
        dglu = jnp.zeros((tc, ct), F32)
        for k in range(CONV_K):
            o = r + (CONV_K - 1) - k
            dglu = dglu + dw_ref[k:k + 1, :] * dpad_ref[o:o + tc, :]
            o = CONV_PAD + r - (CONV_K - 1) + k
            prod = pad_ref[o:o + tc, :] * dh
            acc_ref[k] += _fold_rows(prod)
        acc_ref[CONV_K] += _fold_rows(dh)
        a = a_ref[0, r:r + tc, :]
        sg = jax.nn.sigmoid(g_ref[0, r:r + tc, :])
        da_ref[0, r:r + tc, :] = (dglu * sg).astype(da_ref.dtype)
        dg_ref[0, r:r + tc, :] = (dglu * a * sg * (1.0 - sg)).astype(dg_ref.dtype)
    tot = jnp.sum(acc_ref[...], axis=1)

    @pl.when(b == 0)
    def _():
        ddw_ref[...] = tot

    @pl.when(b > 0)
    def _():
        ddw_ref[...] += tot


def _conv_bwd(p0, dhc, dw, bsz, seq, comm=None):
    ct = CONV_CT
    tc = min(CONV_TC_BWD, seq)
    p3 = p0.reshape(bsz, seq, P0_COLS)
    blk = pl.BlockSpec((1, seq, ct), lambda j, b: (b, 0, j))
    return _call(
        functools.partial(_conv_bwd_kernel, seq=seq, tc=tc), comm,
        name="conv_bwd",
        grid=(MAIN // ct, bsz),
        in_specs=[pl.BlockSpec((1, seq, ct), lambda j, b: (b, 0, P0_A // ct + j)),
                  pl.BlockSpec((1, seq, ct), lambda j, b: (b, 0, P0_G // ct + j)),
                  blk,
                  pl.BlockSpec((CONV_K, ct), lambda j, b: (0, j))],
        out_specs=[blk, blk, pl.BlockSpec((CONV_K + 1, ct), lambda j, b: (0, j))],
        out_shape=[jax.ShapeDtypeStruct((bsz, seq, MAIN), BF16), jax.ShapeDtypeStruct((bsz, seq, MAIN), BF16),
                   jax.ShapeDtypeStruct((CONV_K + 1, MAIN), F32)],
        scratch_shapes=[pltpu.VMEM((seq + CONV_PAD, ct), F32), pltpu.VMEM((seq + CONV_PAD, ct), F32),
                        pltpu.VMEM((CONV_K + 1, 8, ct), F32)],
        semantics=("parallel", "arbitrary"),
        args=[p3, p3, dhc, dw])


def _ln_parts(x, lng, lnb):
    mu = jnp.mean(x, axis=-1, keepdims=True)
    xc = x - mu
    rstd = lax.rsqrt(jnp.mean(xc * xc, axis=-1, keepdims=True) + LN_EPS)
    xh = xc * rstd
    hl = xh * lng + lnb
    return rstd, xh, hl


GATE_CHUNK = 512
GATE_NZ = MIX // GATE_CHUNK


def _z_specs(tm, zb):
    return [pl.BlockSpec((tm, GATE_CHUNK), lambda i, j=j: (i, zb + j)) for j in range(GATE_NZ)]


def _gate_fwd_kernel(*refs, ln):
    main_ref, ymem_ref, *z_refs = refs[:2 + GATE_NZ]
    y_ref = refs[-1]
    x = main_ref[...]
    if ln:
        _, _, hl = _ln_parts(x, refs[-3][...], refs[-2][...])
        x = hl * jax.nn.sigmoid(hl)
    for j, z_ref in enumerate(z_refs):
        cols = slice(j * GATE_CHUNK, (j + 1) * GATE_CHUNK)
        z = z_ref[...]
        src = x[:, cols] if j < MAIN // GATE_CHUNK else ymem_ref[...]
        y_ref[:, cols] = (src * (z * jax.nn.sigmoid(z))).astype(y_ref.dtype)


def _gate_fwd(main, ymem, p, zb, ln_g=None, ln_b=None, *, name, tm=256):
    t = main.shape[0]
    tm = _tile(t, tm)
    ln = ln_g is not None
    in_specs = [pl.BlockSpec((tm, MAIN), lambda i: (i, 0)), pl.BlockSpec((tm, MEMW), lambda i: (i, 0))]
    in_specs += _z_specs(tm, zb)
    args = [main, ymem] + [p] * GATE_NZ
    if ln:
        in_specs += [pl.BlockSpec((1, MAIN), lambda i: (0, 0))] * 2
        args += [ln_g.reshape(1, MAIN), ln_b.reshape(1, MAIN)]
    return pl.pallas_call(
        functools.partial(_gate_fwd_kernel, ln=ln),
        name=name,
        grid=(t // tm,),
        in_specs=in_specs,
        out_specs=pl.BlockSpec((tm, MIX), lambda i: (i, 0)),
        out_shape=jax.ShapeDtypeStruct((t, MIX), BF16),
        compiler_params=_cparams(("parallel",)),
    )(*args)


def _gate_bwd_kernel(*refs, ln):
    dy_ref, main_ref, ymem_ref, *z_refs = refs[:3 + GATE_NZ]
    if ln:
        lng_ref, lnb_ref, dmain_ref, dymem_ref, dz_ref, dlng_ref, dlnb_ref = refs[3 + GATE_NZ:]
    else:
        dmain_ref, dymem_ref, dz_ref = refs[3 + GATE_NZ:]
    x = main_ref[...]
    if ln:
        lng = lng_ref[...]
        rstd, xh, hl = _ln_parts(x, lng, lnb_ref[...])
        sh = jax.nn.sigmoid(hl)
        ymain = hl * sh
    else:
        ymain = x
    for j, z_ref in enumerate(z_refs):
        cols = slice(j * GATE_CHUNK, (j + 1) * GATE_CHUNK)
        dy = dy_ref[:, cols]
        z = z_ref[...]
        sg = jax.nn.sigmoid(z)
        dsz = sg * (1.0 + z * (1.0 - sg))
        if j < MAIN // GATE_CHUNK:
            src = ymain[:, cols]
            dmain_ref[:, cols] = dy * (z * sg)
        else:
            src = ymem_ref[...]
            dymem_ref[...] = dy * (z * sg)
        dz_ref[:, cols] = (dy * src * dsz).astype(dz_ref.dtype)
    if not ln:
        return
    dym = dmain_ref[...]
    dhl = dym * (sh * (1.0 + hl * (1.0 - sh)))
    dxh = dhl * lng
    dmain_ref[...] = rstd * (dxh - jnp.mean(dxh, axis=-1, keepdims=True)
                             - xh * jnp.mean(dxh * xh, axis=-1, keepdims=True))
    dlng = jnp.sum(dhl * xh, axis=0, keepdims=True)
    dlnb = jnp.sum(dhl, axis=0, keepdims=True)

    @pl.when(pl.program_id(0) == 0)
    def _():
        dlng_ref[...] = dlng
        dlnb_ref[...] = dlnb

    @pl.when(pl.program_id(0) > 0)
    def _():
        dlng_ref[...] += dlng
        dlnb_ref[...] += dlnb


def _gate_bwd(dy, main, ymem, p, zb, ln_g=None, ln_b=None, *, name, tm=256):
    t = main.shape[0]
    tm = _tile(t, tm)
    ln = ln_g is not None
    r_main = pl.BlockSpec((tm, MAIN), lambda i: (i, 0))
    r_mem = pl.BlockSpec((tm, MEMW), lambda i: (i, 0))
    r_mix = pl.BlockSpec((tm, MIX), lambda i: (i, 0))
    vec = pl.BlockSpec((1, MAIN), lambda i: (0, 0))
    in_specs = [r_mix, r_main, r_mem] + _z_specs(tm, zb)
    args = [dy, main, ymem] + [p] * GATE_NZ
    out_specs = [r_main, r_mem, r_mix]
    out_shape = [jax.ShapeDtypeStruct((t, MAIN), F32), jax.ShapeDtypeStruct((t, MEMW), F32),
                 jax.ShapeDtypeStruct((t, MIX), BF16)]
    if ln:
        in_specs += [vec, vec]
        args += [ln_g.reshape(1, MAIN), ln_b.reshape(1, MAIN)]
        out_specs += [vec, vec]
        out_shape += [jax.ShapeDtypeStruct((1, MAIN), F32)] * 2
    return pl.pallas_call(
        functools.partial(_gate_bwd_kernel, ln=ln),
        name=name,
        grid=(t // tm,),
        in_specs=in_specs,
        out_specs=out_specs,
        out_shape=out_shape,
        compiler_params=_cparams(("arbitrary",)),
    )(*args)


def _dot_nt(a, b):
    return lax.dot_general(a, b, (((1,), (1,)), ((), ())), preferred_element_type=F32)


def _dot_tn(a, b):
    return lax.dot_general(a, b, (((0,), (0,)), ((), ())), preferred_element_type=F32)


def _dot(a, b):
    return jnp.dot(a, b, preferred_element_type=F32)


def _mem_probs(q, k):
    s = _dot_nt(q, k) * MEM_SCALE
    p = jnp.exp(s - jnp.max(s, axis=-1, keepdims=True))
    return p / jnp.sum(p, axis=-1, keepdims=True)


def _mem_fwd_kernel(q_ref, kv_ref, o_ref):
    for h in range(MEM_HEADS):
        c = slice(h * HEAD, (h + 1) * HEAD)
        cv = slice(MEMW + h * HEAD, MEMW + (h + 1) * HEAD)
        p = _mem_probs(q_ref[0, :, c].astype(BF16), kv_ref[0, :, c])
        o_ref[0, :, c] = _dot(p.astype(BF16), kv_ref[0, :, cv])


def _mem_fwd(p, kvm, col_block, bsz, seq, *, name, tq=512):
    tq = _tile(seq, tq)
    p3 = p.reshape(bsz, seq, p.shape[1])
    mlen = kvm.shape[1]
    return pl.pallas_call(
        functools.partial(_mem_fwd_kernel),
        name=name,
        grid=(bsz, seq // tq),
        in_specs=[pl.BlockSpec((1, tq, MEMW), lambda b, i: (b, i, col_block)),
                  pl.BlockSpec((1, mlen, 2 * MEMW), lambda b, i: (b, 0, 0))],
        out_specs=pl.BlockSpec((1, tq, MEMW), lambda b, i: (b, i, 0)),
        out_shape=jax.ShapeDtypeStruct((bsz, seq, MEMW), F32),
        compiler_params=_cparams(("parallel", "parallel")),
    )(p3, kvm)


def _mem_bwd_kernel(q_ref, kv_ref, do_ref, dq_ref, dkv_ref):
    @pl.when(pl.program_id(1) == 0)
    def _():
        dkv_ref[...] = jnp.zeros(dkv_ref.shape, F32)

    for h in range(MEM_HEADS):
        c = slice(h * HEAD, (h + 1) * HEAD)
        cv = slice(MEMW + h * HEAD, MEMW + (h + 1) * HEAD)
        q = q_ref[0, :, c].astype(BF16)
        k = kv_ref[0, :, c]
        v = kv_ref[0, :, cv]
        do = do_ref[0, :, c].astype(BF16)
        p = _mem_probs(q, k)
        dp = _dot_nt(do, v)
        ds = (p * (dp - jnp.sum(p * dp, axis=-1, keepdims=True)) * MEM_SCALE).astype(BF16)
        dq_ref[0, :, c] = _dot(ds, k).astype(dq_ref.dtype)
        dkv_ref[0, :, c] += _dot_tn(ds, q)
        dkv_ref[0, :, cv] += _dot_tn(p.astype(BF16), do)


def _mem_bwd(p, kvm, dymem, col_block, bsz, seq, *, name, tq=512):
    tq = _tile(seq, tq)
    p3 = p.reshape(bsz, seq, p.shape[1])
    mlen = kvm.shape[1]
    return pl.pallas_call(
        functools.partial(_mem_bwd_kernel),
        name=name,
        grid=(bsz, seq // tq),
        in_specs=[pl.BlockSpec((1, tq, MEMW), lambda b, i: (b, i, col_block)),
                  pl.BlockSpec((1, mlen, 2 * MEMW), lambda b, i: (b, 0, 0)),
                  pl.BlockSpec((1, tq, MEMW), lambda b, i: (b, i, 0))],
        out_specs=[pl.BlockSpec((1, tq, MEMW), lambda b, i: (b, i, 0)),
                   pl.BlockSpec((1, mlen, 2 * MEMW), lambda b, i: (b, 0, 0))],
        out_shape=[jax.ShapeDtypeStruct((bsz, seq, MEMW), BF16),
                   jax.ShapeDtypeStruct((bsz, mlen, 2 * MEMW), F32)],
        compiler_params=_cparams(("parallel", "arbitrary")),
    )(p3, kvm, dymem.reshape(bsz, seq, MEMW))


def _swap32(x):
    lane = lax.broadcasted_iota(jnp.int32, x.shape, 1)
    return jnp.where(lane < 32, pltpu.roll(x, 96, 1), pltpu.roll(x, 32, 1))


def _rope(x, cs, sn):
    return x * cs + _swap32(x) * sn


def _rope_t(d, cs, sn):
    return d * cs + _swap32(d * sn)


UP_HEADS = 2


def _q_up_kernel(a_ref, b_ref, cs_ref, sn_ref, o_ref):
    acc = _dot(a_ref[...], b_ref[...])
    cs = cs_ref[...]
    sn = sn_ref[...]
    for h in range(UP_HEADS):
        c0 = slice(h * QK_PAD, h * QK_PAD + HEAD)
        c1 = slice(h * QK_PAD + HEAD, (h + 1) * QK_PAD)
        o_ref[:, c0] = acc[:, c0].astype(o_ref.dtype)
        o_ref[:, c1] = _rope(acc[:, c1], cs, sn).astype(o_ref.dtype)


def _q_up(cqn, w_uq, cs, sn, *, tm=512):
    t, kd = cqn.shape
    tm = _tile(t, tm)
    tn = UP_HEADS * QK_PAD
    tab = pl.BlockSpec((tm, 128), lambda i, j: (i, 0))
    return pl.pallas_call(
        functools.partial(_q_up_kernel),
        name="q_up_rope",
        grid=(t // tm, MLA_HEADS // UP_HEADS),
        in_specs=[pl.BlockSpec((tm, kd), lambda i, j: (i, 0)), pl.BlockSpec((kd, tn), lambda i, j: (0, j)), tab, tab],
        out_specs=pl.BlockSpec((tm, tn), lambda i, j: (i, j)),
        out_shape=jax.ShapeDtypeStruct((t, MLA_HEADS * QK_PAD), BF16),
        compiler_params=_cparams(("parallel", "parallel")),
    )(cqn, w_uq, cs, sn)


def _kv_up_kernel(a_ref, b_ref, kr_ref, cs_ref, sn_ref, k_ref, v_ref):
    acc = _dot(a_ref[...], b_ref[...])
    krr = _rope(kr_ref[...], cs_ref[...], sn_ref[...]).astype(k_ref.dtype)
    for h in range(UP_HEADS):
        k_ref[:, h * QK_PAD:h * QK_PAD + HEAD] = acc[:, h * 2 * HEAD:h * 2 * HEAD + HEAD].astype(k_ref.dtype)
        k_ref[:, h * QK_PAD + HEAD:(h + 1) * QK_PAD] = krr
        v_ref[:, h * HEAD:(h + 1) * HEAD] = acc[:, h * 2 * HEAD + HEAD:(h + 1) * 2 * HEAD].astype(v_ref.dtype)


def _kv_up(ckvn, w_ukv, p1, cs, sn, *, tm=512):
    t, kd = ckvn.shape
    tm = _tile(t, tm)
    tab = pl.BlockSpec((tm, 128), lambda i, j: (i, 0))
    return pl.pallas_call(
        functools.partial(_kv_up_kernel),
        name="kv_up_pack",
        grid=(t // tm, MLA_HEADS // UP_HEADS),
        in_specs=[pl.BlockSpec((tm, kd), lambda i, j: (i, 0)),
                  pl.BlockSpec((kd, UP_HEADS * 2 * HEAD), lambda i, j: (0, j)),
                  pl.BlockSpec((tm, 128), lambda i, j: (i, P1_KR // 128)), tab, tab],
        out_specs=[pl.BlockSpec((tm, UP_HEADS * QK_PAD), lambda i, j: (i, j)),
                   pl.BlockSpec((tm, UP_HEADS * HEAD), lambda i, j: (i, j))],
        out_shape=[jax.ShapeDtypeStruct((t, MLA_HEADS * QK_PAD), BF16), jax.ShapeDtypeStruct((t, MAIN), BF16)],
        compiler_params=_cparams(("parallel", "parallel")),
    )(ckvn, w_ukv, p1, cs, sn)


def _kr_bwd_kernel(d_ref, cs_ref, sn_ref, o_ref):
    acc = d_ref[:, :HEAD]
    for h in range(1, MLA_HEADS):
        acc = acc + d_ref[:, h * HEAD:(h + 1) * HEAD]
    o_ref[...] = _rope_t(acc, cs_ref[...], sn_ref[...]).astype(o_ref.dtype)


def _kr_bwd(dkrr, cs, sn, *, tm=512):
    t = dkrr.shape[0]
    tm = _tile(t, tm)
    tab = pl.BlockSpec((tm, 128), lambda i: (i, 0))
    return pl.pallas_call(
        functools.partial(_kr_bwd_kernel),
        name="kr_bwd",
        grid=(t // tm,),
        in_specs=[pl.BlockSpec((tm, MAIN), lambda i: (i, 0)), tab, tab],
        out_specs=tab,
        out_shape=jax.ShapeDtypeStruct((t, 128), BF16),
        compiler_params=_cparams(("parallel",)),
    )(dkrr, cs, sn)


ATT_T = 256


def _causal(s, t):
    row = lax.broadcasted_iota(jnp.int32, (t, t), 0)
    col = lax.broadcasted_iota(jnp.int32, (t, t), 1)
    return jnp.where(col <= row, s, NEG)


def _attn_fwd_kernel(q_ref, k_ref, v_ref, o_ref, lse_ref, *, seq, t):
    for i in range(seq // t):
        own = slice(i * t, (i + 1) * t)
        q = q_ref[0, own, :]
        sd = _causal(_dot_nt(q, k_ref[0, own, :]) * MLA_SCALE, t)
        m = jnp.max(sd, axis=-1, keepdims=True)
        if i:
            so = _dot_nt(q, k_ref[0, :i * t, :]) * MLA_SCALE
            m = jnp.maximum(m, jnp.max(so, axis=-1, keepdims=True))
        pd = jnp.exp(sd - m)
        l = jnp.sum(pd, axis=-1, keepdims=True)
        acc = _dot(pd.astype(BF16), v_ref[0, own, :])
        if i:
            po = jnp.exp(so - m)
            l = l + jnp.sum(po, axis=-1, keepdims=True)
            acc = acc + _dot(po.astype(BF16), v_ref[0, :i * t, :])
        o_ref[0, own, :] = acc / l
        lse_ref[0, 0, own, :] = m + jnp.log(l)


def _attn_fwd(qb, kb, vb, bsz, seq):
    t = min(ATT_T, seq)
    q3 = qb.reshape(bsz, seq, MLA_HEADS * QK_PAD)
    k3 = kb.reshape(bsz, seq, MLA_HEADS * QK_PAD)
    v3 = vb.reshape(bsz, seq, MAIN)
    qk = pl.BlockSpec((1, seq, QK_PAD), lambda b, h: (b, 0, h))
    vv = pl.BlockSpec((1, seq, HEAD), lambda b, h: (b, 0, h))
    return pl.pallas_call(
        functools.partial(_attn_fwd_kernel, seq=seq, t=t),
        name="attn_fwd",
        grid=(bsz, MLA_HEADS),
        in_specs=[qk, qk, vv],
        out_specs=[vv, pl.BlockSpec((1, 1, seq, 1), lambda b, h: (b, h, 0, 0))],
        out_shape=[jax.ShapeDtypeStruct((bsz, seq, MAIN), F32),
                   jax.ShapeDtypeStruct((bsz, MLA_HEADS, seq, 1), F32)],
        compiler_params=_cparams(("parallel", "parallel")),
    )(q3, k3, v3)


def _attn_bwd_kernel(q_ref, k_ref, v_ref, o_ref, do_ref, lse_ref, cs_ref, sn_ref, dq_ref, dkv_ref, dkr_ref,
                     delta_ref, dqacc_ref, *, seq, t):
    for r in range(0, seq, t):
        rows = slice(r, r + t)
        delta_ref[rows, :] = jnp.sum(do_ref[0, rows, :] * o_ref[0, rows, :], axis=-1, keepdims=True)
    dqacc_ref[...] = jnp.zeros(dqacc_ref.shape, F32)

    def piece(rows, k, v, masked):
        q = q_ref[0, rows, :]
        do = do_ref[0, rows, :].astype(BF16)
        s = _dot_nt(q, k) * MLA_SCALE
        if masked:
            s = _causal(s, t)
        p = jnp.exp(s - lse_ref[0, 0, rows, :])
        dp = _dot_nt(do, v)
        ds = (p * (dp - delta_ref[rows, :]) * MLA_SCALE).astype(BF16)
        dqacc_ref[rows, :] += _dot(ds, k)
        return _dot_tn(ds, q), _dot_tn(p.astype(BF16), do)

    for j in range(seq // t):
        own = slice(j * t, (j + 1) * t)
        k = k_ref[0, own, :]
        v = v_ref[0, own, :]
        dk, dv = piece(own, k, v, True)
        if (j + 1) * t < seq:
            dk2, dv2 = piece(slice((j + 1) * t, seq), k, v, False)
            dk, dv = dk + dk2, dv + dv2
        dkv_ref[0, own, :HEAD] = dk[:, :HEAD].astype(dkv_ref.dtype)
        dkv_ref[0, own, HEAD:] = dv.astype(dkv_ref.dtype)
        dkr_ref[0, own, :] = dk[:, HEAD:]
    for r in range(0, seq, t):
        rows = slice(r, r + t)
        dq = dqacc_ref[rows, :]
        dq_ref[0, rows, :HEAD] = dq[:, :HEAD].astype(dq_ref.dtype)
        dq_ref[0, rows, HEAD:] = _rope_t(dq[:, HEAD:], cs_ref[0, rows, :], sn_ref[0, rows, :]).astype(dq_ref.dtype)


def _attn_bwd(qb, kb, vb, o, do, lse, cs, sn, bsz, seq):
    t = min(ATT_T, seq)
    q3 = qb.reshape(bsz, seq, MLA_HEADS * QK_PAD)
    k3 = kb.reshape(bsz, seq, MLA_HEADS * QK_PAD)
    v3 = vb.reshape(bsz, seq, MAIN)
    qk = pl.BlockSpec((1, seq, QK_PAD), lambda b, h: (b, 0, h))
    vv = pl.BlockSpec((1, seq, HEAD), lambda b, h: (b, 0, h))
    tab = pl.BlockSpec((1, seq, 128), lambda b, h: (b, 0, 0))
    dq, dkv, dkr = pl.pallas_call(
        functools.partial(_attn_bwd_kernel, seq=seq, t=t),
        name="attn_bwd",
        grid=(bsz, MLA_HEADS),
        in_specs=[qk, qk, vv, vv, vv, pl.BlockSpec((1, 1, seq, 1), lambda b, h: (b, h, 0, 0)), tab, tab],
        out_specs=[qk, qk, vv],
        out_shape=[jax.ShapeDtypeStruct((bsz, seq, MLA_HEADS * QK_PAD), BF16),
                   jax.ShapeDtypeStruct((bsz, seq, MLA_HEADS * 2 * HEAD), BF16),
                   jax.ShapeDtypeStruct((bsz, seq, MAIN), F32)],
        scratch_shapes=[pltpu.VMEM((seq, 1), F32), pltpu.VMEM((seq, QK_PAD), F32)],
        compiler_params=_cparams(("parallel", "parallel")),
    )(q3, k3, v3, o, do.reshape(bsz, seq, MAIN), lse, cs.reshape(bsz, seq, 128), sn.reshape(bsz, seq, 128))
    n = bsz * seq
    return dq.reshape(n, -1), dkv.reshape(n, -1), dkr.reshape(n, -1)


def _adamw_kernel(w_ref, g_ref, m_ref, v_ref, d_ref, nm_ref, nv_ref):
    g = g_ref[...]
    m = ADAM_B1 * m_ref[...] + (1.0 - ADAM_B1) * g
    v = ADAM_B2 * v_ref[...] + (1.0 - ADAM_B2) * (g * g)
    m_hat = m / (1.0 - ADAM_B1 ** ADAM_STEP)
    v_hat = v / (1.0 - ADAM_B2 ** ADAM_STEP)
    d_ref[...] = -ADAM_LR * (m_hat / (jnp.sqrt(v_hat) + ADAM_EPS) + ADAM_WD * w_ref[...])
    nm_ref[...] = m
    nv_ref[...] = v


def _adamw(w, g, m, v, *, name):
    shape = w.shape
    c = shape[-1]
    r = w.size // c
    tr = r
    for cand in (512, 256, 128, 64, 32, 16, 8):
        if r % cand == 0 and cand * c * 4 <= 2 * 1024 * 1024:
            tr = cand
            break
    blk = pl.BlockSpec((tr, c), lambda i: (i, 0))
    outs = pl.pallas_call(
        functools.partial(_adamw_kernel),
        name=name,
        grid=(r // tr,),
        in_specs=[blk] * 4,
        out_specs=[blk] * 3,
        out_shape=[jax.ShapeDtypeStruct((r, c), F32)] * 3,
        compiler_params=_cparams(("parallel",)),
    )(w.reshape(r, c), g.reshape(r, c), m.reshape(r, c), v.reshape(r, c))
    return tuple(o.reshape(shape) for o in outs)


def _place():
    return lax.axis_index("x"), lax.axis_index("y"), lax.axis_index("c")


def _other_chips(x, y):
    return [(1 - x, y), (x, 1 - y), (1 - x, 1 - y)]


class _Geom:
    def __init__(self, kind, rows, cols):
        self.kind, self.rows, self.cols, self.hr = kind, rows, cols, rows // 2
        self.full_shape = {"rows": (4 * rows, cols), "cols": (rows, 4 * cols), "chips": (4, rows, cols)}[kind]
        self.nt = self.hr // ADD_TILE

    def view(self, ref, s, h):
        if self.kind == "rows":
            return ref.at[pl.ds(s * self.rows + h * self.hr, self.hr), :]
        if self.kind == "cols":
            return ref.at[pl.ds(h * self.hr, self.hr), pl.ds(s * self.cols, self.cols)]
        return ref.at[s, pl.ds(h * self.hr, self.hr), :]

    def shard_half(self, ref, h):
        return ref.at[pl.ds(h * self.hr, self.hr), :]

    def rows_view(self, ref, s, r0, nr):
        if self.kind == "rows":
            return ref.at[pl.ds(s * self.rows + r0, nr), :]
        if self.kind == "cols":
            return ref.at[pl.ds(r0, nr), pl.ds(s * self.cols, self.cols)]
        return ref.at[s, pl.ds(r0, nr), :]

    def tile_spec(self, chip_half_of):
        if self.kind == "rows":
            def index(*a):
                s, h, i = chip_half_of(*a)
                return (s * (self.rows // ADD_TILE) + h * self.nt + i, 0)
            return pl.BlockSpec((ADD_TILE, self.cols), index)
        if self.kind == "cols":
            def index(*a):
                s, h, i = chip_half_of(*a)
                return (h * self.nt + i, s)
            return pl.BlockSpec((ADD_TILE, self.cols), index)

        def index(*a):
            s, h, i = chip_half_of(*a)
            return (s, h * self.nt + i, 0)
        return pl.BlockSpec((None, ADD_TILE, self.cols), index)


GEOMS = (("w_mem_kv0", _Geom("rows", 256, 1024)), ("w_mem_kv1", _Geom("rows", 256, 1024)),
         ("w_out0", _Geom("rows", 512, 1024)), ("w_out1", _Geom("rows", 512, 1024)),
         ("conv_w_in", _Geom("cols", 1024, 1408)), ("mla_w_ukv", _Geom("cols", 256, 768)),
         ("mla_w_in", _Geom("chips", 1024, 848)), ("mla_w_uq", _Geom("chips", 512, 576)))
N_BIG = len(GEOMS)


def _remote(k, src_ref, dst_ref, to, ssem, rsem):
    return pltpu.make_async_remote_copy(src_ref=src_ref, dst_ref=dst_ref, send_sem=ssem.at[k], recv_sem=rsem.at[k],
                                        device_id=to, device_id_type=MESH)


GROUP_A = ("w_mem_kv0", "w_out0", "conv_w_in")
GROUP_B = ("w_mem_kv1", "w_out1", "mla_w_ukv", "mla_w_in", "mla_w_uq")
GATHER_0 = ("conv_w_in",)
GATHER_1 = ("w_mem_kv0", "w_out0", "mla_w_ukv", "mla_w_uq")
GATHER_2 = ("w_mem_kv1", "mla_w_in")
GATHER_3 = ("w_out1",)
GEOM = dict(GEOMS)


def _gather_over_ici(geoms, srcs, outs, ssem, rsem, act, base=0):
    x, y, c = _place()
    s = 2 * x + y
    n = len(geoms)
    locals_ = []
    for w, g in enumerate(geoms):
        nr = g.rows // LOCAL_CHUNKS
        for q in range(LOCAL_CHUNKS):
            locals_.append(pltpu.make_async_copy(srcs[w].at[pl.ds(q * nr, nr), :], g.rows_view(outs[w], s, q * nr, nr),
                                                 ssem.at[base + 3 * n + LOCAL_CHUNKS * w + q]))
    sends = [_remote(base + 3 * w + j, g.shard_half(srcs[w], c), g.view(outs[w], s, c), (*chip, c), ssem, rsem)
             for w, g in enumerate(geoms) for j, chip in enumerate(_other_chips(x, y))]
    if act == "start":
        for cp in sends + locals_:
            cp.start()
        return
    for w, g in enumerate(geoms):
        for j, chip in enumerate(_other_chips(x, y)):
            blk = g.view(outs[w], 2 * chip[0] + chip[1], c)
            _remote(base + 3 * w + j, blk, blk, (*chip, c), ssem, rsem).wait_recv()
    for cp in sends:
        cp.wait_send()
    for cp in locals_:
        cp.wait()


def _forward_to_sibling(geoms, fulls, ssem, rsem, act, base=0):
    x, y, c = _place()
    for w, g in enumerate(geoms):
        for j, chip in enumerate(_other_chips(x, y)):
            s = 2 * chip[0] + chip[1]
            mine, theirs = g.view(fulls[w], s, c), g.view(fulls[w], s, 1 - c)
            if act == "start":
                _remote(base + 3 * w + j, mine, mine, (x, y, 1 - c), ssem, rsem).start()
            else:
                _remote(base + 3 * w + j, theirs, theirs, (x, y, 1 - c), ssem, rsem).wait_recv()
                _remote(base + 3 * w + j, mine, mine, (x, y, 1 - c), ssem, rsem).wait_send()


def _gather_comm(forward_names, fulls, ici_names, shards):
    fwd = [GEOM[nm] for nm in forward_names]
    ici = [GEOM[nm] for nm in ici_names]
    nf, base = len(fwd), 3 * len(fwd)

    def run(act):
        def go(i, o, ss, rs):
            _forward_to_sibling(fwd, o[:nf], ss, rs, act)
            _gather_over_ici(ici, i[nf:], o[nf:], ss, rs, act, base)
        return go

    return _Comm(list(fulls) + list(shards),
                 [jax.ShapeDtypeStruct(f.shape, f.dtype) for f in fulls]
                 + [jax.ShapeDtypeStruct(g.full_shape, BF16) for g in ici],
                 base + (3 + LOCAL_CHUNKS) * len(ici), run("start"), run("wait"), aliases={w: w for w in range(nf)})


def _allgather_kernel(*refs, geoms):
    n = len(geoms)
    srcs, outs, (ssem, rsem, fsem, gsem) = refs[:n], refs[n:2 * n], refs[2 * n:]
    _gather_over_ici(geoms, srcs, outs, ssem, rsem, "start")
    _gather_over_ici(geoms, srcs, outs, ssem, rsem, "wait")
    _forward_to_sibling(geoms, outs, fsem, gsem, "start")
    _forward_to_sibling(geoms, outs, fsem, gsem, "wait")


def _allgather_weights(names, shards):
    geoms = [GEOM[nm] for nm in names]
    n = len(geoms)
    return pl.pallas_call(
        functools.partial(_allgather_kernel, geoms=geoms),
        name="allgather_weights",
        in_specs=[HBM] * n,
        out_specs=[HBM] * n,
        out_shape=[jax.ShapeDtypeStruct(g.full_shape, BF16) for g in geoms],
        scratch_shapes=[pltpu.SemaphoreType.DMA(((3 + LOCAL_CHUNKS) * n,)), pltpu.SemaphoreType.DMA((3 * n,)),
                        pltpu.SemaphoreType.DMA((3 * n,)), pltpu.SemaphoreType.DMA((3 * n,))],
    )(*shards)


def _swap_halves_with_sibling(geoms, srcs, dsts, ssem, rsem, act):
    x, y, c = _place()
    for w, g in enumerate(geoms):
        for s in range(4):
            cp = _remote(4 * w + s, g.view(srcs[w], s, 1 - c), dsts[w].at[s], (x, y, 1 - c), ssem, rsem)
            if act == "start":
                cp.start()
            else:
                cp.wait()


def _swap_comm(names, gb):
    geoms = [GEOM[nm] for nm in names]
    return _Comm(gb, [jax.ShapeDtypeStruct((4, g.hr, g.cols), BF16) for g in geoms], 4 * len(geoms),
                 lambda i, o, ss, rs: _swap_halves_with_sibling(geoms, i, o, ss, rs, "start"),
                 lambda i, o, ss, rs: _swap_halves_with_sibling(geoms, i, o, ss, rs, "wait"))


def _swap_halves_kernel(*refs, geoms):
    n = len(geoms)
    srcs, dsts, (ssem, rsem) = refs[:n], refs[n:2 * n], refs[2 * n:]
    _swap_halves_with_sibling(geoms, srcs, dsts, ssem, rsem, "start")
    _swap_halves_with_sibling(geoms, srcs, dsts, ssem, rsem, "wait")


def _swap_halves(names, gb, *, name):
    geoms = [GEOM[nm] for nm in names]
    n = len(geoms)
    return pl.pallas_call(
        functools.partial(_swap_halves_kernel, geoms=geoms),
        name=name,
        in_specs=[HBM] * n,
        out_specs=[HBM] * n,
        out_shape=[jax.ShapeDtypeStruct((4, g.hr, g.cols), BF16) for g in geoms],
        scratch_shapes=[pltpu.SemaphoreType.DMA((4 * n,)), pltpu.SemaphoreType.DMA((4 * n,))],
    )(*gb)


def _exchange_with_chips(srcs, dsts, ssem, rsem, act):
    x, y, c = _place()
    for w in range(len(srcs)):
        for j, chip in enumerate(_other_chips(x, y)):
            cp = _remote(3 * w + j, srcs[w].at[2 * chip[0] + chip[1]], dsts[w].at[j], (*chip, c), ssem, rsem)
            if act == "start":
                cp.start()
            else:
                cp.wait()


def _exchange_comm(pairs):
    return _Comm(pairs, [jax.ShapeDtypeStruct((3,) + p.shape[1:], p.dtype) for p in pairs], 3 * len(pairs),
                 lambda i, o, ss, rs: _exchange_with_chips(i, o, ss, rs, "start"),
                 lambda i, o, ss, rs: _exchange_with_chips(i, o, ss, rs, "wait"))


def _share_kernel(*refs):
    srcs, dsts, (ssem, rsem) = refs[:N_BIG], refs[N_BIG:2 * N_BIG], refs[2 * N_BIG:]
    x, y, c = _place()
    cps = [_remote(w, srcs[w], dsts[w], (x, y, 1 - c), ssem, rsem) for w in range(N_BIG)]
    for cp in cps:
        cp.start()
    for cp in cps:
        cp.wait()


def _share_with_sibling(halves):
    return pl.pallas_call(
        functools.partial(_share_kernel),
        name="rs_share_halves",
        in_specs=[HBM] * N_BIG,
        out_specs=[HBM] * N_BIG,
        out_shape=[jax.ShapeDtypeStruct(h.shape, h.dtype) for h in halves],
        scratch_shapes=[pltpu.SemaphoreType.DMA((N_BIG,)), pltpu.SemaphoreType.DMA((N_BIG,))],
    )(*halves)


def _gather_sum_kernel(src, gat, tot, ssem, rsem):
    x, y, c = _place()
    me = 4 * x + 2 * y + c
    gat[me] = src[...]
    flips = [(dx, dy, dc) for dx in (0, 1) for dy in (0, 1) for dc in (0, 1)][1:]
    cps = []
    for k, (dx, dy, dc) in enumerate(flips):
        peer = (1 - x if dx else x, 1 - y if dy else y, 1 - c if dc else c)
        cp = pltpu.make_async_remote_copy(src_ref=src, dst_ref=gat.at[me], send_sem=ssem.at[k], recv_sem=rsem.at[k],
                                          device_id=peer, device_id_type=MESH)
        cp.start()
        cps.append((cp, 4 * peer[0] + 2 * peer[1] + peer[2], peer))
    for k, (cp, idx, peer) in enumerate(cps):
        pltpu.make_async_remote_copy(src_ref=src, dst_ref=gat.at[idx], send_sem=ssem.at[k], recv_sem=rsem.at[k],
                                     device_id=peer, device_id_type=MESH).wait_recv()
    for cp, _, _ in cps:
        cp.wait_send()
    acc = gat[0]
    for d in range(1, 8):
        acc = acc + gat[d]
    tot[...] = acc


def _gather_sum_small(a, *, name):
    vm = pl.BlockSpec(memory_space=pltpu.VMEM)
    return pl.pallas_call(
        functools.partial(_gather_sum_kernel),
        name=name,
        in_specs=[vm],
        out_specs=[vm, vm],
        out_shape=[jax.ShapeDtypeStruct((8,) + a.shape, a.dtype), jax.ShapeDtypeStruct(a.shape, a.dtype)],
        scratch_shapes=[pltpu.SemaphoreType.DMA((7,)), pltpu.SemaphoreType.DMA((7,))],
    )(a)


def _add_pairs_kernel(c_ref, g_ref, r_ref, o_ref):
    o_ref[...] = (g_ref[...] + r_ref[...].astype(F32)).astype(o_ref.dtype)


def _add_pairs(geom, core, g, recv, *, name):
    half = pl.BlockSpec((None, ADD_TILE, geom.cols), lambda s, i, cr: (s, i, 0))
    return pl.pallas_call(
        functools.partial(_add_pairs_kernel),
        name=name,
        grid_spec=pltpu.PrefetchScalarGridSpec(
            num_scalar_prefetch=1,
            grid=(4, geom.nt),
            in_specs=[geom.tile_spec(lambda s, i, cr: (s, cr[0], i)), half],
            out_specs=half,
        ),
        out_shape=jax.ShapeDtypeStruct(recv.shape, BF16),
        compiler_params=_cparams(("parallel", "parallel")),
    )(core, g, recv)


def _add_final_kernel(sc_ref, g_ref, r_ref, e_ref, o_ref):
    acc = g_ref[...] + r_ref[...].astype(F32)
    for j in range(3):
        acc = acc + e_ref[j].astype(F32)
    o_ref[...] = acc


def _add_final(geom, chip_core, g, recv, exch, *, name):
    return pl.pallas_call(
        functools.partial(_add_final_kernel),
        name=name,
        grid_spec=pltpu.PrefetchScalarGridSpec(
            num_scalar_prefetch=1,
            grid=(geom.nt,),
            in_specs=[geom.tile_spec(lambda i, sc: (sc[0], sc[1], i)),
                      pl.BlockSpec((None, ADD_TILE, geom.cols), lambda i, sc: (sc[0], i, 0)),
                      pl.BlockSpec((3, ADD_TILE, geom.cols), lambda i, sc: (0, i, 0))],
            out_specs=pl.BlockSpec((ADD_TILE, geom.cols), lambda i, sc: (i, 0)),
        ),
        out_shape=jax.ShapeDtypeStruct((geom.hr, geom.cols), F32),
        compiler_params=_cparams(("parallel",)),
    )(chip_core, g, recv, exch)


def _chip_major(w):
    return w.reshape(w.shape[0], 4, w.shape[1] // 4).transpose(1, 0, 2)


def _from_chip_major(w):
    return w.transpose(1, 0, 2).reshape(w.shape[1], 4 * w.shape[2])


def _mla_in_to_internal(w):
    return jnp.concatenate([w[:, 1344:], w[:, :512], w[:, 832:1344], w[:, 512:768], w[:, 768:832],
                            jnp.zeros((w.shape[0], 64), w.dtype)], axis=1)


def _mla_in_from_internal(w):
    return jnp.concatenate([w[:, P1_CQ:P1_QM], w[:, P1_CKV:P1_KR], w[:, P1_KR:P1_KR + 64], w[:, P1_QM:P1_CKV],
                            w[:, :MIX]], axis=1)


def _uq_to_internal(w):
    w = w.reshape(w.shape[0], MLA_HEADS, HEAD + ROPE)
    return jnp.pad(w, ((0, 0), (0, 0), (0, QK_PAD - HEAD - ROPE))).reshape(w.shape[0], MLA_HEADS * QK_PAD)


def _uq_from_internal(w):
    return w.reshape(w.shape[0], MLA_HEADS, QK_PAD)[:, :, :HEAD + ROPE].reshape(w.shape[0], MLA_HEADS * (HEAD + ROPE))


def _rows128(a, rows):
    flat = a.reshape(-1)
    return jnp.pad(flat, (0, rows * 128 - flat.shape[0])).reshape(rows, 128)


def kernel(x, mem, positions, norm_g, mem_norm_g, w_mem_kv, w_out, conv_w_in, conv_dw, conv_dw_b, conv_ln_g, conv_ln_b, mla_w_in, mla_q_norm_g, mla_w_uq, mla_kv_norm_g, mla_w_ukv, final_norm_g, loss_target, m_norm_g, m_mem_norm_g, m_w_mem_kv, m_w_out, m_conv_w_in, m_conv_dw, m_conv_dw_b, m_conv_ln_g, m_conv_ln_b, m_mla_w_in, m_mla_q_norm_g, m_mla_w_uq, m_mla_kv_norm_g, m_mla_w_ukv, m_final_norm_g, v_norm_g, v_mem_norm_g, v_w_mem_kv, v_w_out, v_conv_w_in, v_conv_dw, v_conv_dw_b, v_conv_ln_g, v_conv_ln_b, v_mla_w_in, v_mla_q_norm_g, v_mla_w_uq, v_mla_kv_norm_g, v_mla_w_ukv, v_final_norm_g):
    bsz, seq, d = x.shape
    n = bsz * seq
    mlen = mem.shape[1]
    ax, ay, ac = _place()
    chip = 2 * ax + ay

    shards = dict(w_mem_kv0=w_mem_kv[0], w_mem_kv1=w_mem_kv[1], w_out0=w_out[0], w_out1=w_out[1],
                  conv_w_in=conv_w_in[0], mla_w_ukv=mla_w_ukv[0], mla_w_in=mla_w_in[0], mla_w_uq=mla_w_uq[0])
    bshard = {nm: sh.astype(BF16) for nm, sh in shards.items()}
    wf = dict(zip(GATHER_0, _allgather_weights(GATHER_0, [bshard[nm] for nm in GATHER_0])))
    w_conv_in = wf["conv_w_in"]
    w_conv_in_t = w_conv_in.T

    small_in = jnp.concatenate([_rows128(conv_dw[0], 96), _rows128(mla_q_norm_g, 8), _rows128(mla_kv_norm_g, 8)],
                               axis=0)
    small_all, _ = _gather_sum_small(small_in, name="gather_small_params")
    small_all = small_all[0::2]
    dw_full = small_all[:, :93].reshape(4, -1)[:, :CONV_K * 384].reshape(4, CONV_K, 384)
    dw_full = dw_full.transpose(1, 0, 2).reshape(CONV_K, MAIN)
    qg_full = small_all[:, 96].reshape(Q_RANK)
    kvg_full = small_all[:, 104, :64].reshape(KV_RANK)

    inv_freq = 1.0 / (ROPE_THETA ** (jnp.arange(0, ROPE, 2, dtype=F32) / ROPE))
    ang = positions.astype(F32).reshape(n, 1) * inv_freq
    cos, sin, zer = jnp.cos(ang), jnp.sin(ang), jnp.zeros((n, 64), F32)
    rope_c = jnp.concatenate([cos, cos, zer], axis=1)
    rope_s = jnp.concatenate([-sin, sin, zer], axis=1)

    x2 = x.reshape(n, d)
    mem2 = mem.reshape(bsz * mlen, d)
    tgt2 = loss_target.reshape(n, d)

    memn = [_rms_fwd(mem2, mem_norm_g[i], name=f"mem_norm{i}") for i in range(2)]
    u0 = _rms_fwd(x2, norm_g[0], name="norm0")
    p0, landed1 = _matmul(u0, w_conv_in, name="conv_in_proj",
                          comm=_gather_comm((), (), GATHER_1, [bshard[nm] for nm in GATHER_1]))
    hc, carried = _conv_fwd(p0, dw_full, conv_dw_b, bsz, seq,
                            comm=_gather_comm(GATHER_1, landed1, GATHER_2, [bshard[nm] for nm in GATHER_2]))
    hc = hc.reshape(n, MAIN)
    wf.update(zip(GATHER_1, carried[:len(GATHER_1)]))
    kvm = [_matmul(memn[0], wf["w_mem_kv0"], out_dtype=BF16, name="mem_kv0").reshape(bsz, mlen, 2 * MEMW)]
    ymem0 = _mem_fwd(p0, kvm[0], P0_QM // MEMW, bsz, seq, name="mem_attn0").reshape(n, MEMW)
    y0 = _gate_fwd(hc, ymem0, p0, P0_Z // GATE_CHUNK, conv_ln_g, conv_ln_b, name="gate0")
    h1, gathered2 = _matmul(y0, wf["w_out0"], x2, name="out_proj0",
                            comm=_gather_comm(GATHER_2, carried[len(GATHER_1):], (), ()))
    wf.update(zip(GATHER_2, gathered2))
    w_mla_in = _mla_in_to_internal(_from_chip_major(wf["mla_w_in"]))
    w_uq = _uq_to_internal(_from_chip_major(wf["mla_w_uq"]))
    w_ukv = wf["mla_w_ukv"]
    w_mla_in_t, w_uq_t, w_ukv_t = w_mla_in.T, w_uq.T, w_ukv.T

    u1 = _rms_fwd(h1, norm_g[1], name="norm1")
    p1, landed3 = _matmul(u1, w_mla_in, name="mla_in_proj",
                          comm=_gather_comm((), (), GATHER_3, [bshard[nm] for nm in GATHER_3]))
    kvm1, gathered3 = _matmul(memn[1], wf["w_mem_kv1"], out_dtype=BF16, name="mem_kv1",
                              comm=_gather_comm(GATHER_3, landed3, (), ()))
    kvm.append(kvm1.reshape(bsz, mlen, 2 * MEMW))
    wf.update(zip(GATHER_3, gathered3))
    w_memkv = [wf["w_mem_kv0"], wf["w_mem_kv1"]]
    w_o = [wf["w_out0"], wf["w_out1"]]
    w_memkv_t = [w.T for w in w_memkv]
    w_o_t = [w.T for w in w_o]
    cqn = _rms_fwd(p1, qg_full, width=Q_RANK, col_block=P1_CQ // Q_RANK, name="q_norm")
    ckvn = _rms_fwd(p1, kvg_full, width=KV_RANK, col_block=P1_CKV // KV_RANK, name="kv_norm")
    qb = _q_up(cqn, w_uq, rope_c, rope_s)
    kb, vb = _kv_up(ckvn, w_ukv, p1, rope_c, rope_s)
    o1, lse = _attn_fwd(qb, kb, vb, bsz, seq)
    o1 = o1.reshape(n, MAIN)
    ymem1 = _mem_fwd(p1, kvm[1], P1_QM // MEMW, bsz, seq, name="mem_attn1").reshape(n, MEMW)
    y1 = _gate_fwd(o1, ymem1, p1, P1_Z // GATE_CHUNK, name="gate1")
    h2 = _matmul(y1, w_o[1], h1, name="out_proj1")

    dh2, dh2b, d_final_g, loss_part = _final_loss(h2, final_norm_g, tgt2)
    loss = lax.psum(loss_part[0, 0], ("x", "y", "c"))

    gbig = {}
    gbig["w_out1"] = _matmul(y1, dh2b, ta=True, also_bf16=True, name="d_w_out1")
    dy1 = _matmul(dh2b, w_o_t[1], name="d_y1")
    do1, dymem1, dz1 = _gate_bwd(dy1, o1, ymem1, p1, P1_Z // GATE_CHUNK, name="gate1_bwd")
    dqm1, dkvm1 = _mem_bwd(p1, kvm[1], dymem1, P1_QM // MEMW, bsz, seq, name="mem_attn1_bwd")
    dqb, dkv, dkrr = _attn_bwd(qb, kb, vb, o1.reshape(bsz, seq, MAIN), do1, lse, rope_c, rope_s, bsz, seq)
    dkr = _kr_bwd(dkrr, rope_c, rope_s)
    g_w_uq = _matmul(cqn, dqb, ta=True, name="d_w_uq")
    dcqn = _matmul(dqb, w_uq_t, name="d_cqn")
    gbig["mla_w_ukv"] = _matmul(ckvn, dkv, ta=True, also_bf16=True, name="d_w_ukv")
    dckvn = _matmul(dkv, w_ukv_t, name="d_ckvn")
    dcq, g_qg = _rms_bwd(p1, qg_full, dcqn, width=Q_RANK, col_block=P1_CQ // Q_RANK, out_dtype=BF16,
                         name="q_norm_bwd")
    dckv, g_kvg = _rms_bwd(p1, kvg_full, dckvn, width=KV_RANK, col_block=P1_CKV // KV_RANK, out_dtype=BF16,
                           name="kv_norm_bwd")
    dp1 = jnp.concatenate([dz1, dcq, dqm1.reshape(n, MEMW), dckv, dkr], axis=1)
    g_w_mla_in = _matmul(u1, dp1, ta=True, name="d_w_mla_in")
    du1 = _matmul(dp1, w_mla_in_t, name="d_u1")
    dh1, dh1b, g_norm1 = _rms_bwd(h1, norm_g[1], du1, dh2, also_bf16=True, name="norm1_bwd")

    def mem_kv_bwd(i, dkvm):
        dk2 = dkvm.reshape(bsz * mlen, 2 * MEMW)
        gbig[f"w_mem_kv{i}"] = _matmul(memn[i], dk2, ta=True, also_bf16=True, name=f"d_w_mem_kv{i}")
        dmemn = _matmul(dk2, w_memkv_t[i], name=f"d_memn{i}")
        return _rms_bwd(mem2, mem_norm_g[i], dmemn, name=f"mem_norm{i}_bwd")[1]

    g_mem_g1 = mem_kv_bwd(1, dkvm1)
    for nm, g_int, back in (("mla_w_in", g_w_mla_in, _mla_in_from_internal), ("mla_w_uq", g_w_uq, _uq_from_internal)):
        g_cm = _chip_major(back(g_int))
        gbig[nm] = (g_cm, g_cm.astype(BF16))

    core = jnp.reshape(ac, (1,)).astype(jnp.int32)
    chip_core = jnp.stack([chip, ac]).astype(jnp.int32)

    def add_pairs(names, from_sib):
        return [_add_pairs(GEOM[nm], core, gbig[nm][0], fs, name=f"rs_add_pairs_{nm}")
                for nm, fs in zip(names, from_sib)]

    def pair_sums(names, tag):
        from_sib = _swap_halves(names, [gbig[nm][1] for nm in names], name=f"rs_sibling_swap_{tag}")
        return from_sib, add_pairs(names, from_sib)

    gbig["w_out0"], from_sib_b = _matmul(y0, dh1b, ta=True, also_bf16=True, name="d_w_out0",
                                         comm=_swap_comm(GROUP_B, [gbig[nm][1] for nm in GROUP_B]))
    pairs_b = add_pairs(GROUP_B, from_sib_b)
    dy0 = _matmul(dh1b, w_o_t[0], name="d_y0")
    dhc, dymem0, dz0, g_ln_g, g_ln_b = _gate_bwd(dy0, hc, ymem0, p0, P0_Z // GATE_CHUNK, conv_ln_g, conv_ln_b,
                                                 name="gate0_bwd")
    dqm0, dkvm0 = _mem_bwd(p0, kvm[0], dymem0, P0_QM // MEMW, bsz, seq, name="mem_attn0_bwd")
    (da, dg, g_dw32), exch_b = _conv_bwd(p0, dhc.reshape(bsz, seq, MAIN), dw_full, bsz, seq,
                                         comm=_exchange_comm(pairs_b))
    dp0 = [da.reshape(n, MAIN), dg.reshape(n, MAIN), dqm0.reshape(n, MEMW), dz0]
    gbig["conv_w_in"] = _matmul(u0, dp0, ta=True, also_bf16=True, name="d_w_conv_in")
    g_mem_g = [mem_kv_bwd(0, dkvm0), g_mem_g1]

    from_sib_a, pairs_a = pair_sums(GROUP_A, "a")
    du0, exch_a = _matmul(dp0, w_conv_in_t, name="d_u0", comm=_exchange_comm(pairs_a))
    grad_x, g_norm0 = _rms_bwd(x2, norm_g[0], du0, dh1, name="norm0_bwd")

    from_sibling = dict(zip(GROUP_A + GROUP_B, list(from_sib_a) + list(from_sib_b)))
    exch = dict(zip(GROUP_A + GROUP_B, list(exch_a) + list(exch_b)))
    mine = [_add_final(g, chip_core, gbig[nm][0], from_sibling[nm], exch[nm], name=f"rs_add_final_{nm}")
            for nm, g in GEOMS]
    theirs = _share_with_sibling(mine)
    red = {}
    for (nm, _), a, b in zip(GEOMS, mine, theirs):
        red[nm] = jnp.concatenate([jnp.where(ac == 0, a, b), jnp.where(ac == 0, b, a)], axis=0)
    red["w_mem_kv"] = jnp.stack([red["w_mem_kv0"], red["w_mem_kv1"]])
    red["w_out"] = jnp.stack([red["w_out0"], red["w_out1"]])

    small_g = jnp.concatenate([
        _rows128(jnp.concatenate([g_norm0, g_norm1], axis=0), 16), _rows128(jnp.concatenate(g_mem_g, axis=0), 16),
        _rows128(g_dw32[CONV_K], 16), _rows128(g_ln_g, 16), _rows128(g_ln_b, 16), _rows128(d_final_g, 8),
        _rows128(g_dw32[:CONV_K], 376), _rows128(g_qg, 8), _rows128(g_kvg, 8)], axis=0)
    _, small_sum = _gather_sum_small(small_g, name="allreduce_small_grads")
    flat = small_sum.reshape(-1)

    def take(off, shape):
        size = 1
        for s_ in shape:
            size *= s_
        return flat[off * 128:off * 128 + size].reshape(shape)

    grads = dict(red)
    grads["conv_w_in"] = red["conv_w_in"][None]
    grads["mla_w_in"] = red["mla_w_in"][None]
    grads["mla_w_uq"] = red["mla_w_uq"][None]
    grads["mla_w_ukv"] = red["mla_w_ukv"][None]
    grads["norm_g"] = take(0, (2, D_MODEL))
    grads["mem_norm_g"] = take(16, (2, D_MODEL))
    grads["conv_dw_b"] = take(32, (1, MAIN))
    grads["conv_ln_g"] = take(48, (1, MAIN))
    grads["conv_ln_b"] = take(64, (1, MAIN))
    grads["final_norm_g"] = take(80, (D_MODEL,))
    grads["conv_dw"] = lax.dynamic_slice_in_dim(take(88, (CONV_K, MAIN)), chip * 384, 384, axis=1)[None]
    grads["mla_q_norm_g"] = lax.dynamic_slice_in_dim(take(464, (Q_RANK,)), chip * 128, 128, axis=0)[None]
    grads["mla_kv_norm_g"] = lax.dynamic_slice_in_dim(take(472, (KV_RANK,)), chip * 64, 64, axis=0)[None]

    params = dict(norm_g=norm_g, mem_norm_g=mem_norm_g, w_mem_kv=w_mem_kv, w_out=w_out, conv_w_in=conv_w_in,
                  conv_dw=conv_dw, conv_dw_b=conv_dw_b, conv_ln_g=conv_ln_g, conv_ln_b=conv_ln_b, mla_w_in=mla_w_in,
                  mla_q_norm_g=mla_q_norm_g, mla_w_uq=mla_w_uq, mla_kv_norm_g=mla_kv_norm_g, mla_w_ukv=mla_w_ukv,
                  final_norm_g=final_norm_g)
    mom1 = dict(norm_g=m_norm_g, mem_norm_g=m_mem_norm_g, w_mem_kv=m_w_mem_kv, w_out=m_w_out, conv_w_in=m_conv_w_in,
                conv_dw=m_conv_dw, conv_dw_b=m_conv_dw_b, conv_ln_g=m_conv_ln_g, conv_ln_b=m_conv_ln_b,
                mla_w_in=m_mla_w_in, mla_q_norm_g=m_mla_q_norm_g, mla_w_uq=m_mla_w_uq,
                mla_kv_norm_g=m_mla_kv_norm_g, mla_w_ukv=m_mla_w_ukv, final_norm_g=m_final_norm_g)
    mom2 = dict(norm_g=v_norm_g, mem_norm_g=v_mem_norm_g, w_mem_kv=v_w_mem_kv, w_out=v_w_out, conv_w_in=v_conv_w_in,
                conv_dw=v_conv_dw, conv_dw_b=v_conv_dw_b, conv_ln_g=v_conv_ln_g, conv_ln_b=v_conv_ln_b,
                mla_w_in=v_mla_w_in, mla_q_norm_g=v_mla_q_norm_g, mla_w_uq=v_mla_w_uq,
                mla_kv_norm_g=v_mla_kv_norm_g, mla_w_ukv=v_mla_w_ukv, final_norm_g=v_final_norm_g)
    names = list(params)
    g_out, deltas, new_m, new_v = [], [], [], []
    for nm in names:
        w = params[nm]
        g = grads[nm].reshape(w.shape)
        w2 = w.reshape(1, -1) if w.ndim == 1 else w
        dlt, m_new, v_new = _adamw(w2, g.reshape(w2.shape), mom1[nm].reshape(w2.shape), mom2[nm].reshape(w2.shape),
                                   name=f"adamw_{nm}")
        g_out.append(g)
        deltas.append(dlt.reshape(w.shape))
        new_m.append(m_new.reshape(w.shape))
        new_v.append(v_new.reshape(w.shape))

    return (loss, grad_x.reshape(bsz, seq, d), *g_out, *deltas, *new_m, *new_v)
```
